```python
import jax, jax.numpy as jnp
from jax import lax
import numpy as np

D_MODEL = 2048
BATCH = 8
SEQ = 8192
DEPTH = 1

CHUNK = 64
GDN_HEADS = 16
GDN_DK = 128
GDN_DV = 128
GDN_CONV = 4
QK_W = GDN_HEADS * GDN_DK
V_W = GDN_HEADS * GDN_DV
QKV_W = 2 * QK_W + V_W
POOL_WINDOWS = (2, 4, 8, 16)
POOL_WIDTH = D_MODEL // 2
POOL_GROUP = POOL_WIDTH // 4
XA_HEADS = 4
XA_HEAD_DIM = D_MODEL // XA_HEADS
MEM_LEN = 256
D_FF = 5504
FFN_CONV = 3
EPS = 1e-6
IN_SIZES = (QKV_W, V_W, GDN_HEADS, GDN_HEADS, POOL_WIDTH, D_MODEL, D_MODEL)
D_IN = QKV_W + V_W + 2 * GDN_HEADS + POOL_WIDTH + 2 * D_MODEL

kernel_name = "hybrid_gdn_pool_xattn_convffn_block"


def rms_norm(x, w):
    xf = x.astype(jnp.float32)
    y = xf * lax.rsqrt(jnp.mean(xf * xf, axis=-1, keepdims=True) + EPS)
    return (y * w.astype(jnp.float32)).astype(x.dtype)


def l2_norm(x):
    return x * lax.rsqrt(jnp.sum(x * x, axis=-1, keepdims=True) + EPS)


def causal_dwconv(x, w):
    K = w.shape[0]
    S = x.shape[1]
    xp = jnp.pad(x, ((0, 0), (K - 1, 0), (0, 0)))
    y = xp[:, K - 1:K - 1 + S] * w[K - 1]
    for j in range(K - 1):
        y = y + xp[:, j:j + S] * w[j]
    return y


def gated_delta_rule(q, k, v, g, beta):
    B, S, H, DK = q.shape
    DV = v.shape[-1]
    N = S // CHUNK

    def blocks(t):
        return t.reshape(B, N, CHUNK, H, -1).transpose(0, 3, 1, 2, 4)

    q, k, v = blocks(q), blocks(k), blocks(v)
    g = blocks(g[..., None])[..., 0]
    beta = blocks(beta[..., None])[..., 0]
    G = jnp.cumsum(g, axis=-1)
    pos = jnp.arange(CHUNK)
    incl = pos[:, None] >= pos[None, :]
    strict = pos[:, None] > pos[None, :]
    gap = G[..., :, None] - G[..., None, :]
    decay = jnp.where(incl, jnp.exp(jnp.where(incl, gap, 0.0)), 0.0)
    kk = jnp.einsum('bhnid,bhnjd->bhnij', k, k)
    lower = jnp.where(strict, beta[..., :, None] * decay * kk, 0.0)
    system = lower + jnp.eye(CHUNK, dtype=lower.dtype)
    rhs = jnp.concatenate([beta[..., None] * v, (beta * jnp.exp(G))[..., None] * k], axis=-1)
    sol = lax.linalg.triangular_solve(system, rhs, left_side=True, lower=True,
                                      unit_diagonal=True)
    u_v, w_k = sol[..., :DV], sol[..., DV:]
    attn = decay * jnp.einsum('bhnid,bhnjd->bhnij', q, k)
    q_dec = q * jnp.exp(G)[..., None]
    k_dec = k * jnp.exp(G[..., -1:] - G)[..., None]
    chunk_decay = jnp.exp(G[..., -1])

    def step(state, xs):
        u_c, w_c, a_c, qd_c, kd_c, cd_c = xs
        u = u_c - jnp.einsum('bhck,bhkv->bhcv', w_c, state)
        o = jnp.einsum('bhck,bhkv->bhcv', qd_c, state) + jnp.einsum('bhij,bhjv->bhiv', a_c, u)
        state = cd_c[..., None, None] * state + jnp.einsum('bhck,bhcv->bhkv', kd_c, u)
        return state, o

    xs = tuple(jnp.moveaxis(t, 2, 0) for t in (u_v, w_k, attn, q_dec, k_dec, chunk_decay))
    state0 = jnp.zeros((B, H, DK, DV), jnp.float32)
    _, o = lax.scan(step, state0, xs)
    return o.transpose(1, 0, 3, 2, 4).reshape(B, S, H, DV)


def multi_scale_pool(p, pool_w, pool_scale):
    B, S, _ = p.shape
    pf = p.astype(jnp.float32)
    csum = jnp.pad(jnp.cumsum(pf, axis=1), ((0, 0), (1, 0), (0, 0)))
    xs = jnp.split(pf, len(POOL_WINDOWS), axis=-1)
    cs = jnp.split(csum, len(POOL_WINDOWS), axis=-1)
    t = jnp.arange(S)
    outs = []
    for gi, win in enumerate(POOL_WINDOWS):
        lo = jnp.maximum(t + 1 - win, 0)
        cnt = jnp.minimum(t + 1, win).astype(jnp.float32)
        mean = (cs[gi][:, 1:] - cs[gi][:, lo]) / cnt[None, :, None]
        outs.append(mean - xs[gi])
    y = jnp.stack(outs, axis=2)
    y = jnp.einsum('bsgc,gcd->bsgd', y, pool_w.astype(jnp.float32)).reshape(B, S, POOL_WIDTH)
    return (y * pool_scale.astype(jnp.float32)).astype(p.dtype)


def hybrid_mixer(h, w_in, conv_qkv, a_log, dt_bias, gdn_norm, pool_w, pool_scale,
                 w_branch_a, w_branch_b, w_mix_out):
    B, S, _ = h.shape
    splits = [int(i) for i in np.cumsum(IN_SIZES)[:-1]]
    qkv, z, b_raw, a_raw, p, gate_a, gate_b = jnp.split(h @ w_in, splits, axis=-1)
    qkv = jax.nn.silu(causal_dwconv(qkv, conv_qkv))
    q, k, v = jnp.split(qkv, [QK_W, 2 * QK_W], axis=-1)
    q = l2_norm(q.reshape(B, S, GDN_HEADS, GDN_DK).astype(jnp.float32)) * (GDN_DK ** -0.5)
    k = l2_norm(k.reshape(B, S, GDN_HEADS, GDN_DK).astype(jnp.float32))
    v = v.reshape(B, S, GDN_HEADS, GDN_DV).astype(jnp.float32)
    beta = jax.nn.sigmoid(b_raw.astype(jnp.float32))
    g = -jnp.exp(a_log.astype(jnp.float32)) * jax.nn.softplus(
        a_raw.astype(jnp.float32) + dt_bias.astype(jnp.float32))
    o = gated_delta_rule(q, k, v, g, beta)
    o = rms_norm(o, gdn_norm) * jax.nn.silu(z.reshape(B, S, GDN_HEADS, GDN_DV).astype(jnp.float32))
    y_a = o.reshape(B, S, V_W).astype(h.dtype) @ w_branch_a
    y_b = multi_scale_pool(p, pool_w, pool_scale) @ w_branch_b
    merged = jax.nn.sigmoid(gate_a) * y_a + jax.nn.sigmoid(gate_b) * y_b
    return merged @ w_mix_out


def memory_cross_attention(h, m, w_xq, w_xkv, w_xo):
    B, S, _ = h.shape
    M = m.shape[1]
    q = (h @ w_xq).reshape(B, S, XA_HEADS, XA_HEAD_DIM)
    k, v = jnp.split(m @ w_xkv, 2, axis=-1)
    k = k.reshape(B, M, XA_HEADS, XA_HEAD_DIM)
    v = v.reshape(B, M, XA_HEADS, XA_HEAD_DIM)
    s = jnp.einsum('bshd,bmhd->bhsm', q, k).astype(jnp.float32) * (XA_HEAD_DIM ** -0.5)
    pr = jax.nn.softmax(s, axis=-1).astype(v.dtype)
    o = jnp.einsum('bhsm,bmhd->bshd', pr, v).reshape(B, S, D_MODEL)
    return o @ w_xo


def conv_glu_ffn(h, w_up, ffn_conv_w, ffn_conv_b, w_down):
    u = causal_dwconv(h @ w_up, ffn_conv_w) + ffn_conv_b
    a, b = jnp.split(u, 2, axis=-1)
    return (jax.nn.silu(a) * b) @ w_down


def _fwd_setup_inputs(seed: int = 0) -> dict:
    key = jax.random.key(seed)
    ks = iter(jax.random.split(key, 32))

    def dense(shape, fan_in):
        return jax.random.normal(next(ks), shape, jnp.float32) * (fan_in ** -0.5)

    def gain(shape):
        return 1.0 + 0.05 * jax.random.normal(next(ks), shape, jnp.float32)

    L = DEPTH
    x = jax.random.normal(next(ks), (BATCH, SEQ, D_MODEL), jnp.float32)
    mem = jax.random.normal(next(ks), (BATCH, MEM_LEN, D_MODEL), jnp.float32)
    a_log = jnp.log(jax.random.uniform(next(ks), (L, GDN_HEADS), jnp.float32, 1.0, 16.0))
    dt = jnp.exp(jax.random.uniform(next(ks), (L, GDN_HEADS), jnp.float32,
                                    np.log(1e-3), np.log(1e-1)))
    dt_bias = dt + jnp.log(-jnp.expm1(-dt))
    return {
        "x": x,
        "mem": mem,
        "mix_pre_norm": gain((L, D_MODEL)),
        "w_in": dense((L, D_MODEL, D_IN), D_MODEL),
        "conv_qkv": dense((L, GDN_CONV, QKV_W), GDN_CONV),
        "a_log": a_log,
        "dt_bias": dt_bias,
        "gdn_norm": gain((L, GDN_DV)),
        "pool_w": dense((L, len(POOL_WINDOWS), POOL_GROUP, POOL_GROUP), POOL_GROUP),
        "pool_scale": gain((L, POOL_WIDTH)),
        "w_branch_a": dense((L, V_W, D_MODEL), V_W),
        "w_branch_b": dense((L, POOL_WIDTH, D_MODEL), POOL_WIDTH),
        "w_mix_out": dense((L, D_MODEL, D_MODEL), D_MODEL),
        "mix_post_norm": gain((L, D_MODEL)),
        "xa_pre_norm": gain((L, D_MODEL)),
        "mem_norm": gain((L, D_MODEL)),
        "w_xq": dense((L, D_MODEL, D_MODEL), D_MODEL),
        "w_xkv": dense((L, D_MODEL, 2 * D_MODEL), D_MODEL),
        "w_xo": dense((L, D_MODEL, D_MODEL), D_MODEL),
        "xa_post_norm": gain((L, D_MODEL)),
        "ffn_pre_norm": gain((L, D_MODEL)),
        "w_up": dense((L, D_MODEL, 2 * D_FF), D_MODEL),
        "ffn_conv_w": dense((L, FFN_CONV, 2 * D_FF), FFN_CONV),
        "ffn_conv_b": 0.01 * jax.random.normal(next(ks), (L, 2 * D_FF), jnp.float32),
        "w_down": dense((L, D_FF, D_MODEL), D_FF),
        "ffn_post_norm": gain((L, D_MODEL)),
    }


def _fwd_reference(x, mem, mix_pre_norm, w_in, conv_qkv, a_log, dt_bias, gdn_norm, pool_w,
              pool_scale, w_branch_a, w_branch_b, w_mix_out, mix_post_norm, xa_pre_norm,
              mem_norm, w_xq, w_xkv, w_xo, xa_post_norm, ffn_pre_norm, w_up, ffn_conv_w,
              ffn_conv_b, w_down, ffn_post_norm):
    for l in range(DEPTH):
        h = rms_norm(x, mix_pre_norm[l])
        y = hybrid_mixer(h, w_in[l], conv_qkv[l], a_log[l], dt_bias[l], gdn_norm[l], pool_w[l],
                         pool_scale[l], w_branch_a[l], w_branch_b[l], w_mix_out[l])
        x = x + rms_norm(y, mix_post_norm[l])
        h = rms_norm(x, xa_pre_norm[l])
        m = rms_norm(mem, mem_norm[l])
        y = memory_cross_attention(h, m, w_xq[l], w_xkv[l], w_xo[l])
        x = x + rms_norm(y, xa_post_norm[l])
        h = rms_norm(x, ffn_pre_norm[l])
        y = conv_glu_ffn(h, w_up[l], ffn_conv_w[l], ffn_conv_b[l], w_down[l])
        x = x + rms_norm(y, ffn_post_norm[l])
    return x


import jax as _jax
import jax.numpy as _jnp

TWIN_FORMAT = 'train_step'
FWD_PARAMS = ['x', 'mem', 'mix_pre_norm', 'w_in', 'conv_qkv', 'a_log', 'dt_bias', 'gdn_norm', 'pool_w', 'pool_scale', 'w_branch_a', 'w_branch_b', 'w_mix_out', 'mix_post_norm', 'xa_pre_norm', 'mem_norm', 'w_xq', 'w_xkv', 'w_xo', 'xa_post_norm', 'ffn_pre_norm', 'w_up', 'ffn_conv_w', 'ffn_conv_b', 'w_down', 'ffn_post_norm']
TWIN_WEIGHTS = ['mix_pre_norm', 'w_in', 'conv_qkv', 'a_log', 'dt_bias', 'gdn_norm', 'pool_w', 'pool_scale', 'w_branch_a', 'w_branch_b', 'w_mix_out', 'mix_post_norm', 'xa_pre_norm', 'mem_norm', 'w_xq', 'w_xkv', 'w_xo', 'xa_post_norm', 'ffn_pre_norm', 'w_up', 'ffn_conv_w', 'ffn_conv_b', 'w_down', 'ffn_post_norm']
TWIN_DIFF_INPUT = 'x'
TWIN_INPUTS = ['x', 'mem', 'mix_pre_norm', 'w_in', 'conv_qkv', 'a_log', 'dt_bias', 'gdn_norm', 'pool_w', 'pool_scale', 'w_branch_a', 'w_branch_b', 'w_mix_out', 'mix_post_norm', 'xa_pre_norm', 'mem_norm', 'w_xq', 'w_xkv', 'w_xo', 'xa_post_norm', 'ffn_pre_norm', 'w_up', 'ffn_conv_w', 'ffn_conv_b', 'w_down', 'ffn_post_norm', 'loss_target', 'm_mix_pre_norm', 'm_w_in', 'm_conv_qkv', 'm_a_log', 'm_dt_bias', 'm_gdn_norm', 'm_pool_w', 'm_pool_scale', 'm_w_branch_a', 'm_w_branch_b', 'm_w_mix_out', 'm_mix_post_norm', 'm_xa_pre_norm', 'm_mem_norm', 'm_w_xq', 'm_w_xkv', 'm_w_xo', 'm_xa_post_norm', 'm_ffn_pre_norm', 'm_w_up', 'm_ffn_conv_w', 'm_ffn_conv_b', 'm_w_down', 'm_ffn_post_norm', 'v_mix_pre_norm', 'v_w_in', 'v_conv_qkv', 'v_a_log', 'v_dt_bias', 'v_gdn_norm', 'v_pool_w', 'v_pool_scale', 'v_w_branch_a', 'v_w_branch_b', 'v_w_mix_out', 'v_mix_post_norm', 'v_xa_pre_norm', 'v_mem_norm', 'v_w_xq', 'v_w_xkv', 'v_w_xo', 'v_xa_post_norm', 'v_ffn_pre_norm', 'v_w_up', 'v_ffn_conv_w', 'v_ffn_conv_b', 'v_w_down', 'v_ffn_post_norm']
TWIN_OUTPUTS = ['loss', 'grad_x', 'grad_mix_pre_norm', 'grad_w_in', 'grad_conv_qkv', 'grad_a_log', 'grad_dt_bias', 'grad_gdn_norm', 'grad_pool_w', 'grad_pool_scale', 'grad_w_branch_a', 'grad_w_branch_b', 'grad_w_mix_out', 'grad_mix_post_norm', 'grad_xa_pre_norm', 'grad_mem_norm', 'grad_w_xq', 'grad_w_xkv', 'grad_w_xo', 'grad_xa_post_norm', 'grad_ffn_pre_norm', 'grad_w_up', 'grad_ffn_conv_w', 'grad_ffn_conv_b', 'grad_w_down', 'grad_ffn_post_norm', 'delta_mix_pre_norm', 'delta_w_in', 'delta_conv_qkv', 'delta_a_log', 'delta_dt_bias', 'delta_gdn_norm', 'delta_pool_w', 'delta_pool_scale', 'delta_w_branch_a', 'delta_w_branch_b', 'delta_w_mix_out', 'delta_mix_post_norm', 'delta_xa_pre_norm', 'delta_mem_norm', 'delta_w_xq', 'delta_w_xkv', 'delta_w_xo', 'delta_xa_post_norm', 'delta_ffn_pre_norm', 'delta_w_up', 'delta_ffn_conv_w', 'delta_ffn_conv_b', 'delta_w_down', 'delta_ffn_post_norm', 'new_m_mix_pre_norm', 'new_m_w_in', 'new_m_conv_qkv', 'new_m_a_log', 'new_m_dt_bias', 'new_m_gdn_norm', 'new_m_pool_w', 'new_m_pool_scale', 'new_m_w_branch_a', 'new_m_w_branch_b', 'new_m_w_mix_out', 'new_m_mix_post_norm', 'new_m_xa_pre_norm', 'new_m_mem_norm', 'new_m_w_xq', 'new_m_w_xkv', 'new_m_w_xo', 'new_m_xa_post_norm', 'new_m_ffn_pre_norm', 'new_m_w_up', 'new_m_ffn_conv_w', 'new_m_ffn_conv_b', 'new_m_w_down', 'new_m_ffn_post_norm', 'new_v_mix_pre_norm', 'new_v_w_in', 'new_v_conv_qkv', 'new_v_a_log', 'new_v_dt_bias', 'new_v_gdn_norm', 'new_v_pool_w', 'new_v_pool_scale', 'new_v_w_branch_a', 'new_v_w_branch_b', 'new_v_w_mix_out', 'new_v_mix_post_norm', 'new_v_xa_pre_norm', 'new_v_mem_norm', 'new_v_w_xq', 'new_v_w_xkv', 'new_v_w_xo', 'new_v_xa_post_norm', 'new_v_ffn_pre_norm', 'new_v_w_up', 'new_v_ffn_conv_w', 'new_v_ffn_conv_b', 'new_v_w_down', 'new_v_ffn_post_norm']
TWIN_LEAF_KINDS = {'loss': 'loss', 'grad_x': 'grad_x', 'grad_mix_pre_norm': 'grad_w', 'grad_w_in': 'grad_w', 'grad_conv_qkv': 'grad_w', 'grad_a_log': 'grad_w', 'grad_dt_bias': 'grad_w', 'grad_gdn_norm': 'grad_w', 'grad_pool_w': 'grad_w', 'grad_pool_scale': 'grad_w', 'grad_w_branch_a': 'grad_w', 'grad_w_branch_b': 'grad_w', 'grad_w_mix_out': 'grad_w', 'grad_mix_post_norm': 'grad_w', 'grad_xa_pre_norm': 'grad_w', 'grad_mem_norm': 'grad_w', 'grad_w_xq': 'grad_w', 'grad_w_xkv': 'grad_w', 'grad_w_xo': 'grad_w', 'grad_xa_post_norm': 'grad_w', 'grad_ffn_pre_norm': 'grad_w', 'grad_w_up': 'grad_w', 'grad_ffn_conv_w': 'grad_w', 'grad_ffn_conv_b': 'grad_w', 'grad_w_down': 'grad_w', 'grad_ffn_post_norm': 'grad_w', 'delta_mix_pre_norm': 'delta_w', 'delta_w_in': 'delta_w', 'delta_conv_qkv': 'delta_w', 'delta_a_log': 'delta_w', 'delta_dt_bias': 'delta_w', 'delta_gdn_norm': 'delta_w', 'delta_pool_w': 'delta_w', 'delta_pool_scale': 'delta_w', 'delta_w_branch_a': 'delta_w', 'delta_w_branch_b': 'delta_w', 'delta_w_mix_out': 'delta_w', 'delta_mix_post_norm': 'delta_w', 'delta_xa_pre_norm': 'delta_w', 'delta_mem_norm': 'delta_w', 'delta_w_xq': 'delta_w', 'delta_w_xkv': 'delta_w', 'delta_w_xo': 'delta_w', 'delta_xa_post_norm': 'delta_w', 'delta_ffn_pre_norm': 'delta_w', 'delta_w_up': 'delta_w', 'delta_ffn_conv_w': 'delta_w', 'delta_ffn_conv_b': 'delta_w', 'delta_w_down': 'delta_w', 'delta_ffn_post_norm': 'delta_w', 'new_m_mix_pre_norm': 'new_m', 'new_m_w_in': 'new_m', 'new_m_conv_qkv': 'new_m', 'new_m_a_log': 'new_m', 'new_m_dt_bias': 'new_m', 'new_m_gdn_norm': 'new_m', 'new_m_pool_w': 'new_m', 'new_m_pool_scale': 'new_m', 'new_m_w_branch_a': 'new_m', 'new_m_w_branch_b': 'new_m', 'new_m_w_mix_out': 'new_m', 'new_m_mix_post_norm': 'new_m', 'new_m_xa_pre_norm': 'new_m', 'new_m_mem_norm': 'new_m', 'new_m_w_xq': 'new_m', 'new_m_w_xkv': 'new_m', 'new_m_w_xo': 'new_m', 'new_m_xa_post_norm': 'new_m', 'new_m_ffn_pre_norm': 'new_m', 'new_m_w_up': 'new_m', 'new_m_ffn_conv_w': 'new_m', 'new_m_ffn_conv_b': 'new_m', 'new_m_w_down': 'new_m', 'new_m_ffn_post_norm': 'new_m', 'new_v_mix_pre_norm': 'new_v', 'new_v_w_in': 'new_v', 'new_v_conv_qkv': 'new_v', 'new_v_a_log': 'new_v', 'new_v_dt_bias': 'new_v', 'new_v_gdn_norm': 'new_v', 'new_v_pool_w': 'new_v', 'new_v_pool_scale': 'new_v', 'new_v_w_branch_a': 'new_v', 'new_v_w_branch_b': 'new_v', 'new_v_w_mix_out': 'new_v', 'new_v_mix_post_norm': 'new_v', 'new_v_xa_pre_norm': 'new_v', 'new_v_mem_norm': 'new_v', 'new_v_w_xq': 'new_v', 'new_v_w_xkv': 'new_v', 'new_v_w_xo': 'new_v', 'new_v_xa_post_norm': 'new_v', 'new_v_ffn_pre_norm': 'new_v', 'new_v_w_up': 'new_v', 'new_v_ffn_conv_w': 'new_v', 'new_v_ffn_conv_b': 'new_v', 'new_v_w_down': 'new_v', 'new_v_ffn_post_norm': 'new_v'}


def _forward(args):
    return _fwd_reference(*[args[k] for k in FWD_PARAMS])


def _output_shape():
    def fwd():
        inp = _fwd_setup_inputs(0)
        return _fwd_reference(*[inp[k] for k in FWD_PARAMS])
    out = _jax.eval_shape(fwd)
    return out.shape, out.dtype

N_MICROBATCH = 1
ADAM_LR = 0.001
ADAM_B1 = 0.9
ADAM_B2 = 0.999
ADAM_EPS = 1e-08
ADAM_WD = 0.01
ADAM_STEP = 10
PER_EXAMPLE_BATCH_AXIS = {'x': 0, 'mem': 0, 'loss_target': 0}
SHARED_INPUTS = []
_WEIGHT_DTYPES = {'mix_pre_norm': _jnp.float32, 'w_in': _jnp.float32, 'conv_qkv': _jnp.float32, 'a_log': _jnp.float32, 'dt_bias': _jnp.float32, 'gdn_norm': _jnp.float32, 'pool_w': _jnp.float32, 'pool_scale': _jnp.float32, 'w_branch_a': _jnp.float32, 'w_branch_b': _jnp.float32, 'w_mix_out': _jnp.float32, 'mix_post_norm': _jnp.float32, 'xa_pre_norm': _jnp.float32, 'mem_norm': _jnp.float32, 'w_xq': _jnp.float32, 'w_xkv': _jnp.float32, 'w_xo': _jnp.float32, 'xa_post_norm': _jnp.float32, 'ffn_pre_norm': _jnp.float32, 'w_up': _jnp.float32, 'ffn_conv_w': _jnp.float32, 'ffn_conv_b': _jnp.float32, 'w_down': _jnp.float32, 'ffn_post_norm': _jnp.float32}
MOMENT_SCALE = {'mix_pre_norm': 6.553426e-01, 'w_in': 2.387590e-01, 'conv_qkv': 2.684321e-01, 'a_log': 1.255352e+00, 'dt_bias': 1.231303e+00, 'gdn_norm': 2.226953e+00, 'pool_w': 6.631875e-01, 'pool_scale': 7.052142e-01, 'w_branch_a': 6.387657e-01, 'w_branch_b': 4.904294e-01, 'w_mix_out': 7.984704e-01, 'mix_post_norm': 3.202975e+01, 'xa_pre_norm': 4.236544e-01, 'mem_norm': 1.661896e+00, 'w_xq': 4.074604e-01, 'w_xkv': 1.131848e+00, 'w_xo': 1.714533e+00, 'xa_post_norm': 3.275890e+01, 'ffn_pre_norm': 1.171185e+00, 'w_up': 4.847140e-01, 'ffn_conv_w': 5.576514e-01, 'ffn_conv_b': 1.763541e+00, 'w_down': 1.001771e+00, 'ffn_post_norm': 3.198169e+01}


def _to_microbatches(a, axis):
    t = _jnp.moveaxis(a, axis, 0)
    t = t.reshape((N_MICROBATCH, t.shape[0] // N_MICROBATCH) + t.shape[1:])
    return _jnp.moveaxis(t, 1, axis + 1)


def setup_inputs(seed: int = 0) -> dict:
    inp = _fwd_setup_inputs(seed)
    key = _jax.random.fold_in(_jax.random.key(seed), 7919)
    shape, _ = _output_shape()
    out = dict(inp)
    out["loss_target"] = _jax.random.normal(_jax.random.fold_in(key, 0), shape, _jnp.float32)
    for i, name in enumerate(TWIN_WEIGHTS):
        w = inp[name].astype(_jnp.float32)
        if MOMENT_SCALE is None:
            s = _jnp.sqrt(_jnp.mean(_jnp.square(w)) + 1e-30)
        else:
            s = MOMENT_SCALE[name]
        km, kv = _jax.random.split(_jax.random.fold_in(key, i + 1))
        out[name] = w
        out["m_" + name] = s * _jax.random.normal(km, w.shape, _jnp.float32)
        out["v_" + name] = (s * s) * _jax.random.uniform(kv, w.shape, _jnp.float32, 0.5, 1.5)
    if N_MICROBATCH > 1:
        for name, axis in PER_EXAMPLE_BATCH_AXIS.items():
            out[name] = _to_microbatches(out[name], axis)
    return {'x': out['x'], 'mem': out['mem'], 'mix_pre_norm': out['mix_pre_norm'], 'w_in': out['w_in'], 'conv_qkv': out['conv_qkv'], 'a_log': out['a_log'], 'dt_bias': out['dt_bias'], 'gdn_norm': out['gdn_norm'], 'pool_w': out['pool_w'], 'pool_scale': out['pool_scale'], 'w_branch_a': out['w_branch_a'], 'w_branch_b': out['w_branch_b'], 'w_mix_out': out['w_mix_out'], 'mix_post_norm': out['mix_post_norm'], 'xa_pre_norm': out['xa_pre_norm'], 'mem_norm': out['mem_norm'], 'w_xq': out['w_xq'], 'w_xkv': out['w_xkv'], 'w_xo': out['w_xo'], 'xa_post_norm': out['xa_post_norm'], 'ffn_pre_norm': out['ffn_pre_norm'], 'w_up': out['w_up'], 'ffn_conv_w': out['ffn_conv_w'], 'ffn_conv_b': out['ffn_conv_b'], 'w_down': out['w_down'], 'ffn_post_norm': out['ffn_post_norm'], 'loss_target': out['loss_target'], 'm_mix_pre_norm': out['m_mix_pre_norm'], 'm_w_in': out['m_w_in'], 'm_conv_qkv': out['m_conv_qkv'], 'm_a_log': out['m_a_log'], 'm_dt_bias': out['m_dt_bias'], 'm_gdn_norm': out['m_gdn_norm'], 'm_pool_w': out['m_pool_w'], 'm_pool_scale': out['m_pool_scale'], 'm_w_branch_a': out['m_w_branch_a'], 'm_w_branch_b': out['m_w_branch_b'], 'm_w_mix_out': out['m_w_mix_out'], 'm_mix_post_norm': out['m_mix_post_norm'], 'm_xa_pre_norm': out['m_xa_pre_norm'], 'm_mem_norm': out['m_mem_norm'], 'm_w_xq': out['m_w_xq'], 'm_w_xkv': out['m_w_xkv'], 'm_w_xo': out['m_w_xo'], 'm_xa_post_norm': out['m_xa_post_norm'], 'm_ffn_pre_norm': out['m_ffn_pre_norm'], 'm_w_up': out['m_w_up'], 'm_ffn_conv_w': out['m_ffn_conv_w'], 'm_ffn_conv_b': out['m_ffn_conv_b'], 'm_w_down': out['m_w_down'], 'm_ffn_post_norm': out['m_ffn_post_norm'], 'v_mix_pre_norm': out['v_mix_pre_norm'], 'v_w_in': out['v_w_in'], 'v_conv_qkv': out['v_conv_qkv'], 'v_a_log': out['v_a_log'], 'v_dt_bias': out['v_dt_bias'], 'v_gdn_norm': out['v_gdn_norm'], 'v_pool_w': out['v_pool_w'], 'v_pool_scale': out['v_pool_scale'], 'v_w_branch_a': out['v_w_branch_a'], 'v_w_branch_b': out['v_w_branch_b'], 'v_w_mix_out': out['v_w_mix_out'], 'v_mix_post_norm': out['v_mix_post_norm'], 'v_xa_pre_norm': out['v_xa_pre_norm'], 'v_mem_norm': out['v_mem_norm'], 'v_w_xq': out['v_w_xq'], 'v_w_xkv': out['v_w_xkv'], 'v_w_xo': out['v_w_xo'], 'v_xa_post_norm': out['v_xa_post_norm'], 'v_ffn_pre_norm': out['v_ffn_pre_norm'], 'v_w_up': out['v_w_up'], 'v_ffn_conv_w': out['v_ffn_conv_w'], 'v_ffn_conv_b': out['v_ffn_conv_b'], 'v_w_down': out['v_w_down'], 'v_ffn_post_norm': out['v_ffn_post_norm']}


def _loss(weights, diff, rest, loss_target):
    with _jax.named_scope("forward"):
        args = {**rest, TWIN_DIFF_INPUT: diff, **{k: w.astype(_WEIGHT_DTYPES[k]) for k, w in weights.items()}}
        y = _forward(args)
    with _jax.named_scope("loss_head"):
        err = _jnp.square(y.astype(_jnp.float32) - loss_target)
        return 0.5 * _jnp.sum(_jnp.mean(err, axis=-1)) if err.ndim else 0.5 * err


def _adamw(w, g, m, v):
    m = ADAM_B1 * m + (1.0 - ADAM_B1) * g
    v = ADAM_B2 * v + (1.0 - ADAM_B2) * _jnp.square(g)
    m_hat = m / (1.0 - ADAM_B1 ** ADAM_STEP)
    v_hat = v / (1.0 - ADAM_B2 ** ADAM_STEP)
    delta = -ADAM_LR * (m_hat / (_jnp.sqrt(v_hat) + ADAM_EPS) + ADAM_WD * w)
    return delta, m, v


def reference(x, mem, mix_pre_norm, w_in, conv_qkv, a_log, dt_bias, gdn_norm, pool_w, pool_scale, w_branch_a, w_branch_b, w_mix_out, mix_post_norm, xa_pre_norm, mem_norm, w_xq, w_xkv, w_xo, xa_post_norm, ffn_pre_norm, w_up, ffn_conv_w, ffn_conv_b, w_down, ffn_post_norm, loss_target, m_mix_pre_norm, m_w_in, m_conv_qkv, m_a_log, m_dt_bias, m_gdn_norm, m_pool_w, m_pool_scale, m_w_branch_a, m_w_branch_b, m_w_mix_out, m_mix_post_norm, m_xa_pre_norm, m_mem_norm, m_w_xq, m_w_xkv, m_w_xo, m_xa_post_norm, m_ffn_pre_norm, m_w_up, m_ffn_conv_w, m_ffn_conv_b, m_w_down, m_ffn_post_norm, v_mix_pre_norm, v_w_in, v_conv_qkv, v_a_log, v_dt_bias, v_gdn_norm, v_pool_w, v_pool_scale, v_w_branch_a, v_w_branch_b, v_w_mix_out, v_mix_post_norm, v_xa_pre_norm, v_mem_norm, v_w_xq, v_w_xkv, v_w_xo, v_xa_post_norm, v_ffn_pre_norm, v_w_up, v_ffn_conv_w, v_ffn_conv_b, v_w_down, v_ffn_post_norm):
    given = dict(x=x, mem=mem, mix_pre_norm=mix_pre_norm, w_in=w_in, conv_qkv=conv_qkv, a_log=a_log, dt_bias=dt_bias, gdn_norm=gdn_norm, pool_w=pool_w, pool_scale=pool_scale, w_branch_a=w_branch_a, w_branch_b=w_branch_b, w_mix_out=w_mix_out, mix_post_norm=mix_post_norm, xa_pre_norm=xa_pre_norm, mem_norm=mem_norm, w_xq=w_xq, w_xkv=w_xkv, w_xo=w_xo, xa_post_norm=xa_post_norm, ffn_pre_norm=ffn_pre_norm, w_up=w_up, ffn_conv_w=ffn_conv_w, ffn_conv_b=ffn_conv_b, w_down=w_down, ffn_post_norm=ffn_post_norm, loss_target=loss_target, m_mix_pre_norm=m_mix_pre_norm, m_w_in=m_w_in, m_conv_qkv=m_conv_qkv, m_a_log=m_a_log, m_dt_bias=m_dt_bias, m_gdn_norm=m_gdn_norm, m_pool_w=m_pool_w, m_pool_scale=m_pool_scale, m_w_branch_a=m_w_branch_a, m_w_branch_b=m_w_branch_b, m_w_mix_out=m_w_mix_out, m_mix_post_norm=m_mix_post_norm, m_xa_pre_norm=m_xa_pre_norm, m_mem_norm=m_mem_norm, m_w_xq=m_w_xq, m_w_xkv=m_w_xkv, m_w_xo=m_w_xo, m_xa_post_norm=m_xa_post_norm, m_ffn_pre_norm=m_ffn_pre_norm, m_w_up=m_w_up, m_ffn_conv_w=m_ffn_conv_w, m_ffn_conv_b=m_ffn_conv_b, m_w_down=m_w_down, m_ffn_post_norm=m_ffn_post_norm, v_mix_pre_norm=v_mix_pre_norm, v_w_in=v_w_in, v_conv_qkv=v_conv_qkv, v_a_log=v_a_log, v_dt_bias=v_dt_bias, v_gdn_norm=v_gdn_norm, v_pool_w=v_pool_w, v_pool_scale=v_pool_scale, v_w_branch_a=v_w_branch_a, v_w_branch_b=v_w_branch_b, v_w_mix_out=v_w_mix_out, v_mix_post_norm=v_mix_post_norm, v_xa_pre_norm=v_xa_pre_norm, v_mem_norm=v_mem_norm, v_w_xq=v_w_xq, v_w_xkv=v_w_xkv, v_w_xo=v_w_xo, v_xa_post_norm=v_xa_post_norm, v_ffn_pre_norm=v_ffn_pre_norm, v_w_up=v_w_up, v_ffn_conv_w=v_ffn_conv_w, v_ffn_conv_b=v_ffn_conv_b, v_w_down=v_w_down, v_ffn_post_norm=v_ffn_post_norm)
    weights = {n: given[n] for n in TWIN_WEIGHTS}
    shared = {n: given[n] for n in SHARED_INPUTS}
    per_example = {n: given[n] for n in ['x', 'mem']}
    grad_fn = _jax.value_and_grad(_loss, argnums=(0, 1))

    def one_microbatch(ex, loss_target):
        ex = dict(ex)
        diff = ex.pop(TWIN_DIFF_INPUT)
        return grad_fn(weights, diff, {**shared, **ex}, loss_target)

    if N_MICROBATCH == 1:
        loss, (grad_w, grad_x) = one_microbatch(per_example, given["loss_target"])
    else:
        def body(carry, xs):
            loss_sum, grad_sum = carry
            l_k, (gw_k, gx_k) = one_microbatch(xs[0], xs[1])
            with _jax.named_scope("update"):
                return (loss_sum + l_k, _jax.tree.map(_jnp.add, grad_sum, gw_k)), gx_k

        init = (_jnp.zeros((), _jnp.float32), _jax.tree.map(_jnp.zeros_like, weights))
        (loss, grad_w), grad_x = _jax.lax.scan(body, init, (per_example, given["loss_target"]))
    with _jax.named_scope("update"):
        delta_w, new_m, new_v = {}, {}, {}
        for n in TWIN_WEIGHTS:
            delta_w[n], new_m[n], new_v[n] = _adamw(weights[n], grad_w[n], given["m_" + n], given["v_" + n])
    return (loss, grad_x, *[grad_w[n] for n in TWIN_WEIGHTS], *[delta_w[n] for n in TWIN_WEIGHTS],
            *[new_m[n] for n in TWIN_WEIGHTS], *[new_v[n] for n in TWIN_WEIGHTS])
```

```python
import functools

import jax
import jax.numpy as jnp
from jax import lax
from jax.experimental import pallas as pl
from jax.experimental.pallas import tpu as pltpu

F32, BF16 = jnp.float32, jnp.bfloat16
HI = lax.Precision.HIGHEST
MESH = pl.DeviceIdType.MESH
ANY = pl.BlockSpec(memory_space=pl.ANY)

N_DEV = 8
EPS = 1e-6
CHUNK = 64
HD = 128
GDN_CONV = 4
FFN_CONV = 3
POOL_WINDOWS = (2, 4, 8, 16)
XA_HEADS = 4
HALO = 16
LANE = 128
VMEM_LIMIT = 48 * 1024 * 1024

ADAM_LR, ADAM_B1, ADAM_B2, ADAM_EPS, ADAM_WD, ADAM_STEP = 0.001, 0.9, 0.999, 1e-08, 0.01, 10


def _tile(n, pref, align=LANE):
    best = None
    t = align
    while t <= min(n, pref):
        if n % t == 0:
            best = t
        t += align
    return best if best is not None else n


def _params(**kw):
    return pltpu.CompilerParams(vmem_limit_bytes=VMEM_LIMIT, **kw)


def _sigmoid(x):
    return 1.0 / (1.0 + jnp.exp(-x))


def _silu(x):
    return x * _sigmoid(x)


def _dsilu(x):
    s = _sigmoid(x)
    return s * (1.0 + x * (1.0 - s))


def _colsum8(t):
    return t.reshape(t.shape[0] // 8, 8, t.shape[1]).sum(axis=0)


def _matmul(a, b, mode, out_dtype, name, tm=1024, tn=1024, tk=1024):
    if mode == "nn":
        (M, K), (K2, N) = a.shape, b.shape
    elif mode == "nt":
        (M, K), (N, K2) = a.shape, b.shape
    else:
        (K, M), (K2, N) = a.shape, b.shape
    assert K == K2, (name, a.shape, b.shape)
    tm, tn = _tile(M, tm), _tile(N, tn)
    tk = K if K <= 2048 else _tile(K, tk)
    nk = K // tk
    if mode == "nn":
        a_spec = pl.BlockSpec((tm, tk), lambda i, j, k: (i, k))
        b_spec = pl.BlockSpec((tk, tn), lambda i, j, k: (k, j))
        dims = (((1,), (0,)), ((), ()))
    elif mode == "nt":
        a_spec = pl.BlockSpec((tm, tk), lambda i, j, k: (i, k))
        b_spec = pl.BlockSpec((tn, tk), lambda i, j, k: (j, k))
        dims = (((1,), (1,)), ((), ()))
    else:
        a_spec = pl.BlockSpec((tk, tm), lambda i, j, k: (k, i))
        b_spec = pl.BlockSpec((tk, tn), lambda i, j, k: (k, j))
        dims = (((0,), (0,)), ((), ()))

    def body(a_ref, b_ref, o_ref, acc):
        part = lax.dot_general(a_ref[...], b_ref[...], dims, preferred_element_type=F32)
        if nk == 1:
            o_ref[...] = part.astype(o_ref.dtype)
        else:
            k = pl.program_id(2)

            @pl.when(k == 0)
            def _():
                acc[...] = part

            @pl.when(k > 0)
            def _():
                acc[...] += part

            @pl.when(k == nk - 1)
            def _():
                o_ref[...] = acc[...].astype(o_ref.dtype)

    return pl.pallas_call(
        body, name=name, out_shape=jax.ShapeDtypeStruct((M, N), out_dtype),
        grid=(M // tm, N // tn, nk), in_specs=[a_spec, b_spec],
        out_specs=pl.BlockSpec((tm, tn), lambda i, j, k: (i, j)),
        scratch_shapes=[pltpu.VMEM((tm, tn) if nk > 1 else (8, LANE), F32)],
        compiler_params=_params(dimension_semantics=("parallel", "parallel", "arbitrary")),
    )(a, b)


def _rstd(xf):
    return lax.rsqrt(jnp.mean(xf * xf, axis=-1, keepdims=True) + EPS)


def _rms_bwd(xf, w, dy):
    r = _rstd(xf)
    g = dy * w
    dx = r * g - xf * (r * r * r) * jnp.mean(g * xf, axis=-1, keepdims=True)
    return dx, dy * xf * r


def _row_tile(rows):
    return _tile(rows, 256, 8)


def _prenorm(x, w, name):
    rows, d = x.shape
    ts = _row_tile(rows)

    def body(x_ref, w_ref, h_ref):
        xf = x_ref[...]
        h_ref[...] = (xf * _rstd(xf) * w_ref[...]).astype(BF16)

    return pl.pallas_call(
        body, name=name, out_shape=jax.ShapeDtypeStruct((rows, d), BF16), grid=(rows // ts,),
        in_specs=[pl.BlockSpec((ts, d), lambda i: (i, 0)), pl.BlockSpec((1, d), lambda i: (0, 0))],
        out_specs=pl.BlockSpec((ts, d), lambda i: (i, 0)), compiler_params=_params(),
    )(x, w)


def _post_pre(xres, y, w_post, w_pre, name):
    rows, d = xres.shape
    ts = _row_tile(rows)

    def body(x_ref, y_ref, wp_ref, wn_ref, xo_ref, h_ref):
        yf = y_ref[...]
        xn = x_ref[...] + yf * _rstd(yf) * wp_ref[...]
        xo_ref[...] = xn
        h_ref[...] = (xn * _rstd(xn) * wn_ref[...]).astype(BF16)

    row = pl.BlockSpec((ts, d), lambda i: (i, 0))
    vec = pl.BlockSpec((1, d), lambda i: (0, 0))
    return pl.pallas_call(
        body, name=name, grid=(rows // ts,),
        out_shape=(jax.ShapeDtypeStruct((rows, d), F32), jax.ShapeDtypeStruct((rows, d), BF16)),
        in_specs=[row, row, vec, vec], out_specs=(row, row), compiler_params=_params(),
    )(xres, y, w_post, w_pre)


def _post_loss(xres, y, w_post, target, name):
    rows, d = xres.shape
    ts = _row_tile(rows)
    n = rows // ts

    def body(x_ref, y_ref, wp_ref, t_ref, dx_ref, loss_ref, acc):
        i = pl.program_id(0)
        yf = y_ref[...]
        diff = x_ref[...] + yf * _rstd(yf) * wp_ref[...] - t_ref[...]
        dx_ref[...] = diff * (1.0 / d)

        @pl.when(i == 0)
        def _():
            acc[...] = jnp.zeros_like(acc)

        acc[...] += _colsum8(diff * diff)

        @pl.when(i == n - 1)
        def _():
            loss_ref[...] = jnp.broadcast_to((0.5 / d) * jnp.sum(acc[...]), loss_ref.shape)

    row = pl.BlockSpec((ts, d), lambda i: (i, 0))
    vec = pl.BlockSpec((1, d), lambda i: (0, 0))
    return pl.pallas_call(
        body, name=name, grid=(n,),
        out_shape=(jax.ShapeDtypeStruct((rows, d), F32), jax.ShapeDtypeStruct((1, LANE), F32)),
        in_specs=[row, row, vec, row], out_specs=(row, pl.BlockSpec((1, LANE), lambda i: (0, 0))),
        scratch_shapes=[pltpu.VMEM((8, d), F32)], compiler_params=_params(),
    )(xres, y, w_post, target)


def _post_bwd(y, w_post, dxn, name):
    rows, d = y.shape
    ts = _row_tile(rows)
    n = rows // ts

    def body(y_ref, w_ref, d_ref, dy_ref, dw_ref, acc):
        i = pl.program_id(0)
        dy, dwr = _rms_bwd(y_ref[...], w_ref[...], d_ref[...])
        dy_ref[...] = dy.astype(BF16)

        @pl.when(i == 0)
        def _():
            acc[...] = jnp.zeros_like(acc)

        acc[...] += _colsum8(dwr)

        @pl.when(i == n - 1)
        def _():
            dw_ref[...] = jnp.sum(acc[...], axis=0, keepdims=True)

    row = pl.BlockSpec((ts, d), lambda i: (i, 0))
    vec = pl.BlockSpec((1, d), lambda i: (0, 0))
    return pl.pallas_call(
        body, name=name, grid=(n,),
        out_shape=(jax.ShapeDtypeStruct((rows, d), BF16), jax.ShapeDtypeStruct((1, d), F32)),
        in_specs=[row, vec, row], out_specs=(row, vec),
        scratch_shapes=[pltpu.VMEM((8, d), F32)], compiler_params=_params(),
    )(y, w_post, dxn)


def _pre_bwd(x, w_pre, dh, dres, name):
    rows, d = x.shape
    ts = _row_tile(rows)
    n = rows // ts

    def body(x_ref, w_ref, dh_ref, dr_ref, dx_ref, dw_ref, acc):
        i = pl.program_id(0)
        dx, dwr = _rms_bwd(x_ref[...], w_ref[...], dh_ref[...].astype(F32))
        dx_ref[...] = dr_ref[...] + dx

        @pl.when(i == 0)
        def _():
            acc[...] = jnp.zeros_like(acc)

        acc[...] += _colsum8(dwr)

        @pl.when(i == n - 1)
        def _():
            dw_ref[...] = jnp.sum(acc[...], axis=0, keepdims=True)

    row = pl.BlockSpec((ts, d), lambda i: (i, 0))
    vec = pl.BlockSpec((1, d), lambda i: (0, 0))
    return pl.pallas_call(
        body, name=name, grid=(n,),
        out_shape=(jax.ShapeDtypeStruct((rows, d), F32), jax.ShapeDtypeStruct((1, d), F32)),
        in_specs=[row, vec, row, row], out_specs=(row, vec),
        scratch_shapes=[pltpu.VMEM((8, d), F32)], compiler_params=_params(),
    )(x, w_pre, dh, dres)


def _prev_halo_spec(ts, cw, col0=0):
    return pl.BlockSpec((HALO, cw), lambda i, j: (jnp.maximum(i * (ts // HALO) - 1, 0), j + col0))


def _fill_causal(ext, tile_f32, halo_f32, i):
    ext[pl.ds(0, HALO), :] = jnp.where(i > 0, halo_f32, 0.0)
    ext[pl.ds(HALO, tile_f32.shape[0]), :] = tile_f32


def _delayed(ext, j, ts):
    return ext[pl.ds(HALO - j, ts), :]


def _conv_t(dc, w, name, col_tile=512):
    S, C = dc.shape
    K = w.shape[0]
    ts, cw = _tile(S, 512, HALO), _tile(C, col_tile)
    n = S // ts

    def body(d_ref, nx_ref, w_ref, o_ref, ext):
        i = pl.program_id(0)
        ext[pl.ds(0, ts), :] = d_ref[...].astype(F32)
        ext[pl.ds(ts, HALO), :] = jnp.where(i < n - 1, nx_ref[...].astype(F32), 0.0)
        wv = w_ref[...]
        acc = wv[K - 1:K, :] * ext[pl.ds(0, ts), :]
        for j in range(K - 1):
            acc = acc + wv[j:j + 1, :] * ext[pl.ds(K - 1 - j, ts), :]
        o_ref[...] = acc.astype(o_ref.dtype)

    return pl.pallas_call(
        body, name=name, out_shape=jax.ShapeDtypeStruct((S, C), BF16), grid=(n, C // cw),
        in_specs=[pl.BlockSpec((ts, cw), lambda i, j: (i, j)),
                  pl.BlockSpec((HALO, cw), lambda i, j: (jnp.minimum((i + 1) * (ts // HALO), S // HALO - 1), j)),
                  pl.BlockSpec((K, cw), lambda i, j: (0, j))],
        out_specs=pl.BlockSpec((ts, cw), lambda i, j: (i, j)),
        scratch_shapes=[pltpu.VMEM((ts + HALO, cw), F32)], compiler_params=_params(),
    )(dc, dc, w)


def _qkv_conv(proj, conv_w, qkv_w, name):
    S = proj.shape[0]
    H3 = qkv_w // HD
    H = H3 // 3
    hb = 4 if H % 4 == 0 else 1
    cw = hb * HD
    ts = _tile(S, 512, HALO)
    per_kind = H // hb

    def body(x_ref, h_ref, w_ref, o_ref, ext):
        i, j = pl.program_id(0), pl.program_id(1)
        _fill_causal(ext, x_ref[...].astype(F32), h_ref[...].astype(F32), i)
        wv = w_ref[...]
        c = wv[GDN_CONV - 1:GDN_CONV, :] * _delayed(ext, 0, ts)
        for t in range(GDN_CONV - 1):
            c = c + wv[t:t + 1, :] * _delayed(ext, GDN_CONV - 1 - t, ts)
        s = _silu(c)
        kind = j // per_kind
        scale = jnp.where(kind == 0, HD ** -0.5, 1.0)
        for a in range(hb):
            sa = s[:, HD * a:HD * (a + 1)]
            r = lax.rsqrt(jnp.sum(sa * sa, axis=-1, keepdims=True) + EPS)
            o_ref[a] = jnp.where(kind == 2, sa, sa * r * scale)

    return pl.pallas_call(
        body, name=name, out_shape=jax.ShapeDtypeStruct((H3, S, HD), F32), grid=(S // ts, qkv_w // cw),
        in_specs=[pl.BlockSpec((ts, cw), lambda i, j: (i, j)), _prev_halo_spec(ts, cw),
                  pl.BlockSpec((GDN_CONV, cw), lambda i, j: (0, j))],
        out_specs=pl.BlockSpec((hb, ts, HD), lambda i, j: (j, i, 0)),
        scratch_shapes=[pltpu.VMEM((ts + HALO, cw), F32)], compiler_params=_params(),
    )(proj, proj, conv_w)


def _qkv_conv_bwd(proj, conv_w, dqkv_hm, qkv_w, name):
    S = proj.shape[0]
    H = qkv_w // HD // 3
    hb = 4 if H % 4 == 0 else 1
    cw = hb * HD
    ts = _tile(S, 512, HALO)
    n = S // ts
    per_kind = H // hb

    def body(x_ref, h_ref, w_ref, d_ref, dc_ref, dw_ref, ext, acc):
        j, i = pl.program_id(0), pl.program_id(1)
        _fill_causal(ext, x_ref[...].astype(F32), h_ref[...].astype(F32), i)
        wv = w_ref[...]
        c = wv[GDN_CONV - 1:GDN_CONV, :] * _delayed(ext, 0, ts)
        for t in range(GDN_CONV - 1):
            c = c + wv[t:t + 1, :] * _delayed(ext, GDN_CONV - 1 - t, ts)
        s = _silu(c)
        kind = j // per_kind
        scale = jnp.where(kind == 0, HD ** -0.5, 1.0)
        parts = []
        for a in range(hb):
            sa = s[:, HD * a:HD * (a + 1)]
            dy = d_ref[a]
            r = lax.rsqrt(jnp.sum(sa * sa, axis=-1, keepdims=True) + EPS)
            dn = scale * (r * dy - sa * (r * r * r) * jnp.sum(dy * sa, axis=-1, keepdims=True))
            parts.append(jnp.where(kind == 2, dy, dn))
        dc = jnp.concatenate(parts, axis=1) * _dsilu(c)
        dc_ref[...] = dc.astype(BF16)

        @pl.when(i == 0)
        def _():
            acc[...] = jnp.zeros_like(acc)

        for t in range(GDN_CONV):
            acc[t] += _colsum8(dc * _delayed(ext, GDN_CONV - 1 - t, ts))

        @pl.when(i == n - 1)
        def _():
            dw_ref[...] = jnp.sum(acc[...], axis=1)

    return pl.pallas_call(
        body, name=name, grid=(qkv_w // cw, n),
        out_shape=(jax.ShapeDtypeStruct((S, qkv_w), BF16), jax.ShapeDtypeStruct((GDN_CONV, qkv_w), F32)),
        in_specs=[pl.BlockSpec((ts, cw), lambda j, i: (i, j)),
                  pl.BlockSpec((HALO, cw), lambda j, i: (jnp.maximum(i * (ts // HALO) - 1, 0), j)),
                  pl.BlockSpec((GDN_CONV, cw), lambda j, i: (0, j)),
                  pl.BlockSpec((hb, ts, HD), lambda j, i: (j, i, 0))],
        out_specs=(pl.BlockSpec((ts, cw), lambda j, i: (i, j)), pl.BlockSpec((GDN_CONV, cw), lambda j, i: (0, j))),
        scratch_shapes=[pltpu.VMEM((ts + HALO, cw), F32), pltpu.VMEM((GDN_CONV, 8, cw), F32)],
        compiler_params=_params(),
    )(proj, proj, conv_w, dqkv_hm)


def _chunk_cumsum(x):
    row = lax.broadcasted_iota(jnp.int32, x.shape, 0) & (CHUNK - 1)
    s = 1
    while s < CHUNK:
        x = x + jnp.where(row >= s, pltpu.roll(x, s, axis=0), 0.0)
        s *= 2
    return x


def _chunk_rev_cumsum(x):
    rows = x.shape[0]
    row = lax.broadcasted_iota(jnp.int32, x.shape, 0) & (CHUNK - 1)
    s = 1
    while s < CHUNK:
        x = x + jnp.where(row < CHUNK - s, pltpu.roll(x, rows - s, axis=0), 0.0)
        s *= 2
    return x


def _softplus(x):
    return jnp.maximum(x, 0.0) + jnp.log1p(jnp.exp(-jnp.abs(x)))


def _gates(proj, ba_col, a_log_l, dt_bias_l, H, name):
    S = proj.shape[0]
    ts = _tile(S, 512, CHUNK)

    def body(x_ref, al_ref, dt_ref, o_ref):
        x = x_ref[...].astype(F32)
        lane = lax.broadcasted_iota(jnp.int32, x.shape, 1)
        g = -jnp.exp(al_ref[...]) * _softplus(x + dt_ref[...])
        G = _chunk_cumsum(jnp.where((lane >= H) & (lane < 2 * H), g, 0.0))
        o_ref[...] = jnp.where(lane < H, _sigmoid(x), G)

    return pl.pallas_call(
        body, name=name, out_shape=jax.ShapeDtypeStruct((S, LANE), F32), grid=(S // ts,),
        in_specs=[pl.BlockSpec((ts, LANE), lambda i: (i, ba_col)), pl.BlockSpec((1, LANE), lambda i: (0, 0)),
                  pl.BlockSpec((1, LANE), lambda i: (0, 0))],
        out_specs=pl.BlockSpec((ts, LANE), lambda i: (i, 0)), compiler_params=_params(),
    )(proj, a_log_l, dt_bias_l)


def _gates_bwd(proj, ba_col, ba_w, a_log_l, dt_bias_l, dbg, H, name):
    S = proj.shape[0]
    ts = _tile(S, 512, CHUNK)
    n = S // ts

    def body(x_ref, al_ref, dt_ref, d_ref, o_ref, dal_ref, ddt_ref, acc):
        i = pl.program_id(0)
        x = x_ref[...].astype(F32)
        d = d_ref[...]
        lane = lax.broadcasted_iota(jnp.int32, x.shape, 1)
        is_a = (lane >= H) & (lane < 2 * H)
        beta = _sigmoid(x)
        nea = -jnp.exp(al_ref[...])
        z = x + dt_ref[...]
        dg = _chunk_rev_cumsum(jnp.where(is_a, d, 0.0))
        da_raw = dg * nea * _sigmoid(z)
        o = jnp.where(lane < H, d * beta * (1.0 - beta), jnp.where(is_a, da_raw, 0.0))
        if ba_w > LANE:
            o = jnp.concatenate([o, jnp.zeros((ts, ba_w - LANE), F32)], axis=1)
        o_ref[...] = o.astype(BF16)

        @pl.when(i == 0)
        def _():
            acc[...] = jnp.zeros_like(acc)

        acc[0] += _colsum8(jnp.where(is_a, dg * nea * _softplus(z), 0.0))
        acc[1] += _colsum8(jnp.where(is_a, da_raw, 0.0))

        @pl.when(i == n - 1)
        def _():
            dal_ref[...] = jnp.sum(acc[0], axis=0, keepdims=True)
            ddt_ref[...] = jnp.sum(acc[1], axis=0, keepdims=True)

    vec = pl.BlockSpec((1, LANE), lambda i: (0, 0))
    return pl.pallas_call(
        body, name=name, grid=(n,),
        out_shape=(jax.ShapeDtypeStruct((S, ba_w), BF16), jax.ShapeDtypeStruct((1, LANE), F32),
                   jax.ShapeDtypeStruct((1, LANE), F32)),
        in_specs=[pl.BlockSpec((ts, LANE), lambda i: (i, ba_col)), vec, vec, pl.BlockSpec((ts, LANE), lambda i: (i, 0))],
        out_specs=(pl.BlockSpec((ts, ba_w), lambda i: (i, 0)), vec, vec),
        scratch_shapes=[pltpu.VMEM((2, 8, LANE), F32)], compiler_params=_params(),
    )(proj, a_log_l, dt_bias_l, dbg)


def _bmm(a, b):
    return jnp.einsum("hik,hkj->hij", a, b, precision=HI, preferred_element_type=F32)


def _bmm_nt(a, b):
    return jnp.einsum("hik,hjk->hij", a, b, precision=HI, preferred_element_type=F32)


def _bmm_tn(a, b):
    return jnp.einsum("hki,hkj->hij", a, b, precision=HI, preferred_element_type=F32)


def _unit_lower_inverse(L, r, c):
    eye = (r == c).astype(F32)
    m = jnp.where((r >> 3) == (c >> 3), -L, 0.0)
    m2 = _bmm(m, m)
    m4 = _bmm(m2, m2)
    x = eye + m
    x = x + _bmm(x, m2)
    x = x + _bmm(x, m4)
    for sh in (3, 4, 5):
        off = ((r >> (sh + 1)) == (c >> (sh + 1))) & ((r >> sh) != (c >> sh))
        x = x - _bmm(x, _bmm(jnp.where(off, L, 0.0), x))
    return x


def _to_row(col, eye):
    return jnp.sum(jnp.where(eye, jnp.broadcast_to(col, eye.shape), 0.0), axis=1, keepdims=True)


def _to_col(rowv, eye):
    return jnp.sum(jnp.where(eye, jnp.broadcast_to(rowv, eye.shape), 0.0), axis=2, keepdims=True)


def _gdn_chunk(q, k, v, bg, H):
    shape = (H, CHUNK, CHUNK)
    r = lax.broadcasted_iota(jnp.int32, shape, 1)
    c = lax.broadcasted_iota(jnp.int32, shape, 2)
    eye, incl, strict = r == c, r >= c, r > c
    beta = jnp.stack([bg[:, h:h + 1] for h in range(H)], axis=0)
    G = jnp.stack([bg[:, H + h:H + h + 1] for h in range(H)], axis=0)
    gap = jnp.broadcast_to(G, shape) - _to_row(G, eye)
    decay = jnp.where(incl, jnp.exp(jnp.where(incl, gap, 0.0)), 0.0)
    kk = _bmm_nt(k, k)
    L = jnp.where(strict, beta * decay * kk, 0.0)
    ainv = _unit_lower_inverse(L, r, c)
    eG = jnp.exp(G)
    u_v = _bmm(ainv, beta * v)
    w_k = _bmm(ainv, (beta * eG) * k)
    qk = _bmm_nt(q, k)
    GL = G[:, CHUNK - 1:CHUNK, :]
    ek = jnp.exp(GL - G)
    return dict(eye=eye, strict=strict, r=r, c=c, beta=beta, G=G, decay=decay, kk=kk, ainv=ainv, eG=eG,
                u_v=u_v, w_k=w_k, qk=qk, attn=decay * qk, GL=GL, ek=ek, cd=jnp.exp(GL))


def _gdn_fwd(qkv_hm, bg, H, name):
    S = qkv_hm.shape[1]
    N = S // CHUNK

    def body(q_ref, k_ref, v_ref, bg_ref, o_ref, st_ref, state):
        n = pl.program_id(0)

        @pl.when(n == 0)
        def _():
            state[...] = jnp.zeros_like(state)

        q, k, v = q_ref[...], k_ref[...], v_ref[...]
        t = _gdn_chunk(q, k, v, bg_ref[...], H)
        s0 = state[...]
        st_ref[0] = s0
        u = t["u_v"] - _bmm(t["w_k"], s0)
        o_ref[...] = _bmm(q * t["eG"], s0) + _bmm(t["attn"], u)
        state[...] = t["cd"] * s0 + _bmm_tn(k * t["ek"], u)

    blk = lambda kind: pl.BlockSpec((H, CHUNK, HD), lambda n: (kind, n, 0))
    return pl.pallas_call(
        body, name=name, grid=(N,),
        out_shape=(jax.ShapeDtypeStruct((H, S, HD), F32), jax.ShapeDtypeStruct((N, H, HD, HD), F32)),
        in_specs=[blk(0), blk(1), blk(2), pl.BlockSpec((CHUNK, LANE), lambda n: (n, 0))],
        out_specs=(pl.BlockSpec((H, CHUNK, HD), lambda n: (0, n, 0)),
                   pl.BlockSpec((1, H, HD, HD), lambda n: (n, 0, 0, 0))),
        scratch_shapes=[pltpu.VMEM((H, HD, HD), F32)], compiler_params=_params(),
    )(qkv_hm, qkv_hm, qkv_hm, bg)


def _gdn_bwd(qkv_hm, bg, states, do_hm, H, name):
    S = qkv_hm.shape[1]
    N = S // CHUNK

    def body(q_ref, k_ref, v_ref, bg_ref, st_ref, do_ref, dqkv_ref, dbg_ref, dstate):
        n = pl.program_id(0)

        @pl.when(n == 0)
        def _():
            dstate[...] = jnp.zeros_like(dstate)

        q, k, v, do = q_ref[...], k_ref[...], v_ref[...], do_ref[...]
        t = _gdn_chunk(q, k, v, bg_ref[...], H)
        eye, beta, eG, decay, kk, ainv = t["eye"], t["beta"], t["eG"], t["decay"], t["kk"], t["ainv"]
        s0 = st_ref[0]
        ds1 = dstate[...]
        u = t["u_v"] - _bmm(t["w_k"], s0)
        qdec, kdec = q * eG, k * t["ek"]
        d_qdec = _bmm_nt(do, s0)
        d_attn = _bmm_nt(do, u)
        du = _bmm_tn(t["attn"], do) + _bmm(kdec, ds1)
        d_cd = jnp.sum(jnp.sum(ds1 * s0, axis=2, keepdims=True), axis=1, keepdims=True)
        d_kdec = _bmm_nt(u, ds1)
        d_wk = -_bmm_nt(du, s0)
        dstate[...] = t["cd"] * ds1 + _bmm_tn(qdec, do) - _bmm_tn(t["w_k"], du)
        d_rv = _bmm_tn(ainv, du)
        d_rk = _bmm_tn(ainv, d_wk)
        dL = jnp.where(t["strict"], -(_bmm_nt(d_rv, t["u_v"]) + _bmm_nt(d_rk, t["w_k"])), 0.0)
        rk_k = jnp.sum(d_rk * k, axis=2, keepdims=True)
        d_beta = (jnp.sum(dL * decay * kk, axis=2, keepdims=True) + jnp.sum(d_rv * v, axis=2, keepdims=True)
                  + rk_k * eG)
        d_decay = dL * beta * kk + d_attn * t["qk"]
        d_kk = dL * beta * decay
        d_qk = d_attn * decay
        dqkv_ref[pl.ds(2 * H, H)] = beta * d_rv
        dqkv_ref[pl.ds(0, H)] = _bmm(d_qk, k) + d_qdec * eG
        dqkv_ref[pl.ds(H, H)] = ((beta * eG) * d_rk + _bmm(d_kk, k) + _bmm_tn(d_kk, k) + _bmm_tn(d_qk, q)
                       + d_kdec * t["ek"])
        d_eG = rk_k * beta + jnp.sum(d_qdec * q, axis=2, keepdims=True)
        e = jnp.sum(d_kdec * kdec, axis=2, keepdims=True)
        T = d_decay * decay
        dG = d_eG * eG - e + jnp.sum(T, axis=2, keepdims=True) - _to_col(jnp.sum(T, axis=1, keepdims=True), eye)
        dGL = jnp.sum(e, axis=1, keepdims=True) + d_cd * t["cd"]
        row1 = lax.broadcasted_iota(jnp.int32, (H, CHUNK, 1), 1)
        dG = dG + jnp.where(row1 == CHUNK - 1, dGL, 0.0)
        lane = lax.broadcasted_iota(jnp.int32, (CHUNK, LANE), 1)
        out = jnp.zeros((CHUNK, LANE), F32)
        for h in range(H):
            out = out + jnp.where(lane == h, d_beta[h], 0.0) + jnp.where(lane == H + h, dG[h], 0.0)
        dbg_ref[...] = out

    blk = lambda kind: pl.BlockSpec((H, CHUNK, HD), lambda n: (kind, N - 1 - n, 0))
    return pl.pallas_call(
        body, name=name, grid=(N,),
        out_shape=(jax.ShapeDtypeStruct((3 * H, S, HD), F32), jax.ShapeDtypeStruct((S, LANE), F32)),
        in_specs=[blk(0), blk(1), blk(2), pl.BlockSpec((CHUNK, LANE), lambda n: (N - 1 - n, 0)),
                  pl.BlockSpec((1, H, HD, HD), lambda n: (N - 1 - n, 0, 0, 0)), blk(0)],
        out_specs=(pl.BlockSpec((3 * H, CHUNK, HD), lambda n: (0, N - 1 - n, 0)),
                   pl.BlockSpec((CHUNK, LANE), lambda n: (N - 1 - n, 0))),
        scratch_shapes=[pltpu.VMEM((H, HD, HD), F32)], compiler_params=_params(),
    )(qkv_hm, qkv_hm, qkv_hm, bg, states, do_hm)


def _gdn_out(o_hm, proj, z_col, gdn_w, name):
    H, S, _ = o_hm.shape
    vw = H * HD
    ts = _tile(S, 256, 8)

    def body(o_ref, z_ref, w_ref, y_ref):
        z = z_ref[...].astype(F32)
        w = w_ref[...]
        parts = []
        for h in range(H):
            o = o_ref[h]
            parts.append(o * _rstd(o) * w)
        y_ref[...] = (jnp.concatenate(parts, axis=1) * _silu(z)).astype(BF16)

    return pl.pallas_call(
        body, name=name, out_shape=jax.ShapeDtypeStruct((S, vw), BF16), grid=(S // ts,),
        in_specs=[pl.BlockSpec((H, ts, HD), lambda i: (0, i, 0)), pl.BlockSpec((ts, vw), lambda i: (i, z_col)),
                  pl.BlockSpec((1, HD), lambda i: (0, 0))],
        out_specs=pl.BlockSpec((ts, vw), lambda i: (i, 0)), compiler_params=_params(),
    )(o_hm, proj, gdn_w)


def _gdn_out_bwd(o_hm, proj, z_col, gdn_w, dy, name):
    H, S, _ = o_hm.shape
    vw = H * HD
    ts = _tile(S, 256, 8)
    n = S // ts

    def body(o_ref, z_ref, w_ref, dy_ref, do_ref, dz_ref, dw_ref, acc):
        i = pl.program_id(0)
        z = z_ref[...].astype(F32)
        dy = dy_ref[...].astype(F32)
        w = w_ref[...]
        gz = dy * _silu(z)
        normed, dwr = [], jnp.zeros((ts, HD), F32)
        for h in range(H):
            o = o_ref[h]
            dxo, dwh = _rms_bwd(o, w, gz[:, HD * h:HD * (h + 1)])
            do_ref[h] = dxo
            dwr = dwr + dwh
            normed.append(o * _rstd(o) * w)
        dz_ref[...] = (dy * jnp.concatenate(normed, axis=1) * _dsilu(z)).astype(BF16)

        @pl.when(i == 0)
        def _():
            acc[...] = jnp.zeros_like(acc)

        acc[...] += _colsum8(dwr)

        @pl.when(i == n - 1)
        def _():
            dw_ref[...] = jnp.sum(acc[...], axis=0, keepdims=True)

    return pl.pallas_call(
        body, name=name, grid=(n,),
        out_shape=(jax.ShapeDtypeStruct((H, S, HD), F32), jax.ShapeDtypeStruct((S, vw), BF16),
                   jax.ShapeDtypeStruct((1, HD), F32)),
        in_specs=[pl.BlockSpec((H, ts, HD), lambda i: (0, i, 0)), pl.BlockSpec((ts, vw), lambda i: (i, z_col)),
                  pl.BlockSpec((1, HD), lambda i: (0, 0)), pl.BlockSpec((ts, vw), lambda i: (i, 0))],
        out_specs=(pl.BlockSpec((H, ts, HD), lambda i: (0, i, 0)), pl.BlockSpec((ts, vw), lambda i: (i, 0)),
                   pl.BlockSpec((1, HD), lambda i: (0, 0))),
        scratch_shapes=[pltpu.VMEM((8, HD), F32)], compiler_params=_params(),
    )(o_hm, proj, gdn_w, dy)


def _pool_trailing(ext, ts, pg, row0):
    outs, inv_cnts = [], []
    t_abs = row0 + lax.broadcasted_iota(jnp.int32, (ts, 1), 0)
    for gi, win in enumerate(POOL_WINDOWS):
        cols = pl.ds(gi * pg, pg)
        cur = ext[pl.ds(HALO, ts), cols]
        acc = cur
        for j in range(1, win):
            acc = acc + ext[pl.ds(HALO - j, ts), cols]
        inv = 1.0 / jnp.minimum(t_abs + 1, win).astype(F32)
        outs.append(acc * inv - cur)
    return outs


def _pool_fwd(proj, p_col, pool_w, pool_scale, pw, name):
    S = proj.shape[0]
    pg = pw // len(POOL_WINDOWS)
    ts = _tile(S, 512, HALO)

    def body(x_ref, h_ref, w_ref, sc_ref, o_ref, ext):
        i = pl.program_id(0)
        _fill_causal(ext, x_ref[...].astype(F32), h_ref[...].astype(F32), i)
        ys = _pool_trailing(ext, ts, pg, i * ts)
        outs = [jnp.dot(ys[gi].astype(BF16), w_ref[gi], preferred_element_type=F32) for gi in range(len(ys))]
        o_ref[...] = (jnp.concatenate(outs, axis=1) * sc_ref[...]).astype(BF16)

    return pl.pallas_call(
        body, name=name, out_shape=jax.ShapeDtypeStruct((S, pw), BF16), grid=(S // ts,),
        in_specs=[pl.BlockSpec((ts, pw), lambda i: (i, p_col)),
                  pl.BlockSpec((HALO, pw), lambda i: (jnp.maximum(i * (ts // HALO) - 1, 0), p_col)),
                  pl.BlockSpec((len(POOL_WINDOWS), pg, pg), lambda i: (0, 0, 0)), pl.BlockSpec((1, pw), lambda i: (0, 0))],
        out_specs=pl.BlockSpec((ts, pw), lambda i: (i, 0)),
        scratch_shapes=[pltpu.VMEM((ts + HALO, pw), F32)], compiler_params=_params(),
    )(proj, proj, pool_w, pool_scale)


def _pool_bwd(proj, p_col, pool_w, pool_scale, dpb, pw, name):
    S = proj.shape[0]
    G = len(POOL_WINDOWS)
    pg = pw // G
    ts = _tile(S, 512, HALO)
    n = S // ts

    def body(x_ref, h_ref, w_ref, sc_ref, d_ref, dn_ref, dp_ref, dw_ref, dsc_ref, ext, zext, wacc, sacc):
        i = pl.program_id(0)
        _fill_causal(ext, x_ref[...].astype(F32), h_ref[...].astype(F32), i)
        ys = _pool_trailing(ext, ts, pg, i * ts)
        d_ext = jnp.concatenate([d_ref[...].astype(F32), jnp.where(i < n - 1, dn_ref[...].astype(F32), 0.0)], axis=0)
        dt = d_ext * sc_ref[...]
        t_abs = i * ts + lax.broadcasted_iota(jnp.int32, (ts + HALO, 1), 0)

        @pl.when(i == 0)
        def _():
            wacc[...] = jnp.zeros_like(wacc)
            sacc[...] = jnp.zeros_like(sacc)

        dps, tfs = [], []
        for gi, win in enumerate(POOL_WINDOWS):
            cols = slice(gi * pg, (gi + 1) * pg)
            w = w_ref[gi]
            dt_g = dt[:, cols].astype(BF16)
            y_g = ys[gi].astype(BF16)
            tfs.append(jnp.dot(y_g, w, preferred_element_type=F32))
            wacc[gi] += lax.dot_general(y_g, dt_g[:ts], (((0,), (0,)), ((), ())), preferred_element_type=F32)
            dyp = lax.dot_general(dt_g, w, (((1,), (1,)), ((), ())), preferred_element_type=F32)
            zext[:, pl.ds(gi * pg, pg)] = dyp * (1.0 / jnp.minimum(t_abs + 1, win).astype(F32))
            acc = -dyp[:ts]
            for j in range(win):
                acc = acc + zext[pl.ds(j, ts), pl.ds(gi * pg, pg)]
            dps.append(acc)
        dp_ref[...] = jnp.concatenate(dps, axis=1).astype(BF16)
        sacc[...] += _colsum8(d_ext[:ts] * jnp.concatenate(tfs, axis=1))

        @pl.when(i == n - 1)
        def _():
            dw_ref[...] = wacc[...]
            dsc_ref[...] = jnp.sum(sacc[...], axis=0, keepdims=True)

    return pl.pallas_call(
        body, name=name, grid=(n,),
        out_shape=(jax.ShapeDtypeStruct((S, pw), BF16), jax.ShapeDtypeStruct((G, pg, pg), F32),
                   jax.ShapeDtypeStruct((1, pw), F32)),
        in_specs=[pl.BlockSpec((ts, pw), lambda i: (i, p_col)),
                  pl.BlockSpec((HALO, pw), lambda i: (jnp.maximum(i * (ts // HALO) - 1, 0), p_col)),
                  pl.BlockSpec((G, pg, pg), lambda i: (0, 0, 0)), pl.BlockSpec((1, pw), lambda i: (0, 0)),
                  pl.BlockSpec((ts, pw), lambda i: (i, 0)),
                  pl.BlockSpec((HALO, pw), lambda i: (jnp.minimum((i + 1) * (ts // HALO), S // HALO - 1), 0))],
        out_specs=(pl.BlockSpec((ts, pw), lambda i: (i, 0)), pl.BlockSpec((G, pg, pg), lambda i: (0, 0, 0)),
                   pl.BlockSpec((1, pw), lambda i: (0, 0))),
        scratch_shapes=[pltpu.VMEM((ts + HALO, pw), F32), pltpu.VMEM((ts + HALO, pw), F32),
                        pltpu.VMEM((G, pg, pg), F32), pltpu.VMEM((8, pw), F32)],
        compiler_params=_params(),
    )(proj, proj, pool_w, pool_scale, dpb, dpb)


def _merge(proj, ga_col, gb_col, ya, yb, name):
    S, d = ya.shape
    ts = _tile(S, 512, 16)

    def body(ga_ref, gb_ref, ya_ref, yb_ref, o_ref):
        o_ref[...] = (_sigmoid(ga_ref[...].astype(F32)) * ya_ref[...].astype(F32)
                      + _sigmoid(gb_ref[...].astype(F32)) * yb_ref[...].astype(F32)).astype(BF16)

    row = pl.BlockSpec((ts, d), lambda i: (i, 0))
    return pl.pallas_call(
        body, name=name, out_shape=jax.ShapeDtypeStruct((S, d), BF16), grid=(S // ts,),
        in_specs=[pl.BlockSpec((ts, d), lambda i: (i, ga_col)), pl.BlockSpec((ts, d), lambda i: (i, gb_col)), row, row],
        out_specs=row, compiler_params=_params(),
    )(proj, proj, ya, yb)


def _merge_bwd(proj, ga_col, gb_col, ya, yb, dm, name):
    S, d = ya.shape
    ts = _tile(S, 512, 16)

    def body(ga_ref, gb_ref, ya_ref, yb_ref, dm_ref, dya_ref, dyb_ref, dga_ref, dgb_ref):
        dmv = dm_ref[...].astype(F32)
        sa, sb = _sigmoid(ga_ref[...].astype(F32)), _sigmoid(gb_ref[...].astype(F32))
        dya_ref[...] = (dmv * sa).astype(BF16)
        dyb_ref[...] = (dmv * sb).astype(BF16)
        dga_ref[...] = (dmv * ya_ref[...].astype(F32) * sa * (1.0 - sa)).astype(BF16)
        dgb_ref[...] = (dmv * yb_ref[...].astype(F32) * sb * (1.0 - sb)).astype(BF16)

    row = pl.BlockSpec((ts, d), lambda i: (i, 0))
    o = jax.ShapeDtypeStruct((S, d), BF16)
    return pl.pallas_call(
        body, name=name, out_shape=(o, o, o, o), grid=(S // ts,),
        in_specs=[pl.BlockSpec((ts, d), lambda i: (i, ga_col)), pl.BlockSpec((ts, d), lambda i: (i, gb_col)), row, row, row],
        out_specs=(row, row, row, row), compiler_params=_params(),
    )(proj, proj, ya, yb, dm)


def _xattn_fwd(q, kv, name):
    S, d = q.shape
    M = kv.shape[0]
    hd = d // XA_HEADS
    ts = _tile(S, 512, 16)
    scale = hd ** -0.5

    def body(q_ref, k_ref, v_ref, o_ref):
        s = lax.dot_general(q_ref[...], k_ref[...], (((1,), (1,)), ((), ())), preferred_element_type=F32) * scale
        p = jnp.exp(s - jnp.max(s, axis=-1, keepdims=True))
        p = p / jnp.sum(p, axis=-1, keepdims=True)
        o_ref[...] = jnp.dot(p.astype(BF16), v_ref[...], preferred_element_type=F32).astype(BF16)

    return pl.pallas_call(
        body, name=name, out_shape=jax.ShapeDtypeStruct((S, d), BF16), grid=(S // ts, XA_HEADS),
        in_specs=[pl.BlockSpec((ts, hd), lambda i, h: (i, h)), pl.BlockSpec((M, hd), lambda i, h: (0, h)),
                  pl.BlockSpec((M, hd), lambda i, h: (0, XA_HEADS + h))],
        out_specs=pl.BlockSpec((ts, hd), lambda i, h: (i, h)), compiler_params=_params(),
    )(q, kv, kv)


def _xattn_bwd(q, kv, do, name):
    S, d = q.shape
    M = kv.shape[0]
    hd = d // XA_HEADS
    ts = _tile(S, 512, 16)
    n = S // ts
    scale = hd ** -0.5

    def body(q_ref, k_ref, v_ref, do_ref, dq_ref, dk_ref, dv_ref, kacc, vacc):
        i = pl.program_id(1)
        qv, kv_, vv, dov = q_ref[...], k_ref[...], v_ref[...], do_ref[...]
        s = lax.dot_general(qv, kv_, (((1,), (1,)), ((), ())), preferred_element_type=F32) * scale
        p = jnp.exp(s - jnp.max(s, axis=-1, keepdims=True))
        p = p / jnp.sum(p, axis=-1, keepdims=True)
        dp = lax.dot_general(dov, vv, (((1,), (1,)), ((), ())), preferred_element_type=F32)
        ds = (p * (dp - jnp.sum(p * dp, axis=-1, keepdims=True)) * scale).astype(BF16)
        dq_ref[...] = jnp.dot(ds, kv_, preferred_element_type=F32).astype(BF16)

        @pl.when(i == 0)
        def _():
            kacc[...] = jnp.zeros_like(kacc)
            vacc[...] = jnp.zeros_like(vacc)

        kacc[...] += lax.dot_general(ds, qv, (((0,), (0,)), ((), ())), preferred_element_type=F32)
        vacc[...] += lax.dot_general(p.astype(BF16), dov, (((0,), (0,)), ((), ())), preferred_element_type=F32)

        @pl.when(i == n - 1)
        def _():
            dk_ref[...] = kacc[...]
            dv_ref[...] = vacc[...]

    dq, dk, dv = pl.pallas_call(
        body, name=name, grid=(XA_HEADS, n),
        out_shape=(jax.ShapeDtypeStruct((S, d), BF16), jax.ShapeDtypeStruct((M, d), F32), jax.ShapeDtypeStruct((M, d), F32)),
        in_specs=[pl.BlockSpec((ts, hd), lambda h, i: (i, h)), pl.BlockSpec((M, hd), lambda h, i: (0, h)),
                  pl.BlockSpec((M, hd), lambda h, i: (0, XA_HEADS + h)), pl.BlockSpec((ts, hd), lambda h, i: (i, h))],
        out_specs=(pl.BlockSpec((ts, hd), lambda h, i: (i, h)), pl.BlockSpec((M, hd), lambda h, i: (0, h)),
                   pl.BlockSpec((M, hd), lambda h, i: (0, h))),
        scratch_shapes=[pltpu.VMEM((M, hd), F32), pltpu.VMEM((M, hd), F32)], compiler_params=_params(),
    )(q, kv, kv, do)
    return dq, jnp.concatenate([dk, dv], axis=1)


def _ffn_u(ext, wv, bias, ts):
    u = wv[FFN_CONV - 1:FFN_CONV, :] * _delayed(ext, 0, ts) + bias
    for t in range(FFN_CONV - 1):
        u = u + wv[t:t + 1, :] * _delayed(ext, FFN_CONV - 1 - t, ts)
    return u


def _ffn_act(up, conv_w, bias, name):
    S, F2 = up.shape
    F = F2 // 2
    ts, cw = _tile(S, 512, HALO), _tile(F, 512)
    nb = F // cw

    def body(a_ref, ah_ref, b_ref, bh_ref, wa_ref, wb_ref, ba_ref, bb_ref, o_ref, ea, eb):
        i = pl.program_id(0)
        _fill_causal(ea, a_ref[...].astype(F32), ah_ref[...].astype(F32), i)
        _fill_causal(eb, b_ref[...].astype(F32), bh_ref[...].astype(F32), i)
        ua = _ffn_u(ea, wa_ref[...], ba_ref[...], ts)
        ub = _ffn_u(eb, wb_ref[...], bb_ref[...], ts)
        o_ref[...] = (_silu(ua) * ub).astype(BF16)

    tile = lambda c0: pl.BlockSpec((ts, cw), lambda i, j: (i, j + c0))
    vec = lambda rows, c0: pl.BlockSpec((rows, cw), lambda i, j: (0, j + c0))
    return pl.pallas_call(
        body, name=name, out_shape=jax.ShapeDtypeStruct((S, F), BF16), grid=(S // ts, nb),
        in_specs=[tile(0), _prev_halo_spec(ts, cw), tile(nb), _prev_halo_spec(ts, cw, nb),
                  vec(FFN_CONV, 0), vec(FFN_CONV, nb), vec(1, 0), vec(1, nb)],
        out_specs=pl.BlockSpec((ts, cw), lambda i, j: (i, j)),
        scratch_shapes=[pltpu.VMEM((ts + HALO, cw), F32), pltpu.VMEM((ts + HALO, cw), F32)],
        compiler_params=_params(),
    )(up, up, up, up, conv_w, conv_w, bias, bias)


def _ffn_act_bwd(up, conv_w, bias, dact, name):
    S, F2 = up.shape
    F = F2 // 2
    ts, cw = _tile(S, 512, HALO), _tile(F, 512)
    nb = F // cw
    n = S // ts

    def body(a_ref, ah_ref, b_ref, bh_ref, wa_ref, wb_ref, ba_ref, bb_ref, d_ref,
             dua_ref, dub_ref, dwa_ref, dwb_ref, dba_ref, dbb_ref, ea, eb, wacc, bacc):
        i = pl.program_id(1)
        _fill_causal(ea, a_ref[...].astype(F32), ah_ref[...].astype(F32), i)
        _fill_causal(eb, b_ref[...].astype(F32), bh_ref[...].astype(F32), i)
        ua = _ffn_u(ea, wa_ref[...], ba_ref[...], ts)
        ub = _ffn_u(eb, wb_ref[...], bb_ref[...], ts)
        d = d_ref[...].astype(F32)
        dua = d * ub * _dsilu(ua)
        dub = d * _silu(ua)
        dua_ref[...] = dua.astype(BF16)
        dub_ref[...] = dub.astype(BF16)

        @pl.when(i == 0)
        def _():
            wacc[...] = jnp.zeros_like(wacc)
            bacc[...] = jnp.zeros_like(bacc)

        for t in range(FFN_CONV):
            wacc[0, t] += _colsum8(dua * _delayed(ea, FFN_CONV - 1 - t, ts))
            wacc[1, t] += _colsum8(dub * _delayed(eb, FFN_CONV - 1 - t, ts))
        bacc[0] += _colsum8(dua)
        bacc[1] += _colsum8(dub)

        @pl.when(i == n - 1)
        def _():
            dwa_ref[...] = jnp.sum(wacc[0], axis=1)
            dwb_ref[...] = jnp.sum(wacc[1], axis=1)
            dba_ref[...] = jnp.sum(bacc[0], axis=0, keepdims=True)
            dbb_ref[...] = jnp.sum(bacc[1], axis=0, keepdims=True)

    tile = lambda c0: pl.BlockSpec((ts, cw), lambda j, i: (i, j + c0))
    halo = lambda c0: pl.BlockSpec((HALO, cw), lambda j, i: (jnp.maximum(i * (ts // HALO) - 1, 0), j + c0))
    vec = lambda rows, c0: pl.BlockSpec((rows, cw), lambda j, i: (0, j + c0))
    dua, dub, dwa, dwb, dba, dbb = pl.pallas_call(
        body, name=name, grid=(nb, n),
        out_shape=(jax.ShapeDtypeStruct((S, F), BF16), jax.ShapeDtypeStruct((S, F), BF16),
                   jax.ShapeDtypeStruct((FFN_CONV, F), F32), jax.ShapeDtypeStruct((FFN_CONV, F), F32),
                   jax.ShapeDtypeStruct((1, F), F32), jax.ShapeDtypeStruct((1, F), F32)),
        in_specs=[tile(0), halo(0), tile(nb), halo(nb), vec(FFN_CONV, 0), vec(FFN_CONV, nb), vec(1, 0), vec(1, nb), tile(0)],
        out_specs=(tile(0), tile(0), vec(FFN_CONV, 0), vec(FFN_CONV, 0), vec(1, 0), vec(1, 0)),
        scratch_shapes=[pltpu.VMEM((ts + HALO, cw), F32), pltpu.VMEM((ts + HALO, cw), F32),
                        pltpu.VMEM((2, FFN_CONV, 8, cw), F32), pltpu.VMEM((2, 8, cw), F32)],
        compiler_params=_params(),
    )(up, up, up, up, conv_w, conv_w, bias, bias, dact)
    return (jnp.concatenate([dua, dub], axis=1), jnp.concatenate([dwa, dwb], axis=1),
            jnp.concatenate([dba, dbb], axis=1))


def _adamw(gparts, w, m, v, name):
    rows = w.shape[0]
    tr = _tile(rows, 1024, 8)

    def body(g_ref, w_ref, m_ref, v_ref, go_ref, d_ref, mo_ref, vo_ref):
        g = g_ref[0]
        for s in range(1, N_DEV):
            g = g + g_ref[s]
        mn = ADAM_B1 * m_ref[...] + (1.0 - ADAM_B1) * g
        vn = ADAM_B2 * v_ref[...] + (1.0 - ADAM_B2) * (g * g)
        m_hat = mn / (1.0 - ADAM_B1 ** ADAM_STEP)
        v_hat = vn / (1.0 - ADAM_B2 ** ADAM_STEP)
        go_ref[...] = g
        d_ref[...] = -ADAM_LR * (m_hat / (jnp.sqrt(v_hat) + ADAM_EPS) + ADAM_WD * w_ref[...])
        mo_ref[...] = mn
        vo_ref[...] = vn

    row = pl.BlockSpec((tr, LANE), lambda i: (i, 0))
    o = jax.ShapeDtypeStruct((rows, LANE), F32)
    return pl.pallas_call(
        body, name=name, out_shape=(o, o, o, o), grid=(rows // tr,),
        in_specs=[pl.BlockSpec((N_DEV, tr, LANE), lambda i: (0, i, 0)), row, row, row],
        out_specs=(row, row, row, row), compiler_params=_params(),
    )(gparts, w, m, v)


def _position():
    return lax.axis_index("x"), lax.axis_index("y"), lax.axis_index("c")


def _all_gather(block, name):
    rows, C = block.shape

    def body(x_ref, out_ref, send_sems, recv_sems, local_sem):
        x, y, c = _position()
        me, sibling = (x, y, c), (x, y, 1 - c)
        chips = [(1 - x, y), (x, 1 - y), (1 - x, 1 - y)]

        def slot(px, py, pc):
            return out_ref.at[4 * px + 2 * py + pc]

        def copy(k, blk, to, src=None):
            return pltpu.make_async_remote_copy(
                src_ref=slot(*blk) if src is None else src, dst_ref=slot(*blk),
                send_sem=send_sems.at[k], recv_sem=recv_sems.at[k], device_id=to, device_id_type=MESH)

        mine = pltpu.make_async_copy(x_ref, slot(*me), local_sem)
        mine.start()
        first = [copy(0, me, sibling, src=x_ref)]
        first += [copy(1 + j, me, (*chip, c), src=x_ref) for j, chip in enumerate(chips)]
        for cp in first:
            cp.start()
        passed = [copy(4 + j, (*chip, c), sibling) for j, chip in enumerate(chips)]
        for j, chip in enumerate(chips):
            copy(1 + j, (*chip, c), me).wait_recv()
            passed[j].start()
        copy(0, sibling, me).wait_recv()
        for j, chip in enumerate(chips):
            copy(4 + j, (*chip, 1 - c), me).wait_recv()
        for cp in first + passed:
            cp.wait_send()
        mine.wait()

    return pl.pallas_call(
        body, name=name, out_shape=jax.ShapeDtypeStruct((N_DEV, rows, C), block.dtype),
        in_specs=[ANY], out_specs=ANY,
        scratch_shapes=[pltpu.SemaphoreType.DMA((7,)), pltpu.SemaphoreType.DMA((7,)), pltpu.SemaphoreType.DMA],
    )(block)


def _slice_exchange(parts, name):
    _, rows, C = parts.shape

    def body(p_ref, out_ref, send_sems, recv_sems, local_sem):
        x, y, c = _position()
        my_slot = 4 * x + 2 * y + c
        mine = pltpu.make_async_copy(p_ref.at[my_slot], out_ref.at[my_slot], local_sem)
        mine.start()
        copies = []
        for k in range(1, N_DEV):
            px, py, pc = x ^ (k >> 2), y ^ ((k >> 1) & 1), c ^ (k & 1)
            copies.append(pltpu.make_async_remote_copy(
                src_ref=p_ref.at[4 * px + 2 * py + pc], dst_ref=out_ref.at[my_slot],
                send_sem=send_sems.at[k - 1], recv_sem=recv_sems.at[k - 1],
                device_id=(px, py, pc), device_id_type=MESH))
        for cp in copies:
            cp.start()
        for cp in copies:
            cp.wait()
        mine.wait()

    return pl.pallas_call(
        body, name=name, out_shape=jax.ShapeDtypeStruct(parts.shape, parts.dtype),
        in_specs=[ANY], out_specs=ANY,
        scratch_shapes=[pltpu.SemaphoreType.DMA((7,)), pltpu.SemaphoreType.DMA((7,)), pltpu.SemaphoreType.DMA],
    )(parts)


def _pack(arrays, dtype, row_multiple):
    flat, layout, off = [], [], 0
    for a in arrays:
        n = a.size
        padded = -(-n // LANE) * LANE
        f = a.reshape(-1).astype(dtype)
        if padded != n:
            f = jnp.pad(f, (0, padded - n))
        flat.append(f)
        layout.append((off, n, a.shape))
        off += padded
    total = -(-off // (LANE * row_multiple)) * (LANE * row_multiple)
    if total != off:
        flat.append(jnp.zeros((total - off,), dtype))
    return jnp.concatenate(flat).reshape(total // LANE, LANE), layout


def _unpack(buf, layout):
    flat = buf.reshape(-1)
    return [flat[off:off + n].reshape(shape) for off, n, shape in layout]


def _unpack_gathered(buf, layout):
    flat = buf.reshape(N_DEV, -1)
    return [flat[:, off:off + n].reshape((N_DEV,) + shape) for off, n, shape in layout]


def _cols_to_full(g):
    return jnp.transpose(g, (1, 0, 2)).reshape(g.shape[1], N_DEV * g.shape[2])


def _full_to_cols(a):
    return jnp.transpose(a.reshape(a.shape[0], N_DEV, a.shape[1] // N_DEV), (1, 0, 2))


def _rows_to_full(g):
    return g.reshape(N_DEV * g.shape[1], g.shape[2])


def _pad_cols(a, width):
    return a if a.shape[-1] == width else jnp.pad(a, [(0, 0)] * (a.ndim - 1) + [(0, width - a.shape[-1])])


SHARDED = ("w_in", "conv_qkv", "pool_w", "w_branch_a", "w_branch_b", "w_mix_out", "w_xq", "w_xkv", "w_xo", "w_up",
           "ffn_conv_w", "w_down")
REPLICATED = ("mix_pre_norm", "a_log", "dt_bias", "gdn_norm", "pool_scale", "mix_post_norm", "xa_pre_norm", "mem_norm",
              "xa_post_norm", "ffn_pre_norm", "ffn_conv_b", "ffn_post_norm")
WEIGHTS = ("mix_pre_norm", "w_in", "conv_qkv", "a_log", "dt_bias", "gdn_norm", "pool_w", "pool_scale", "w_branch_a",
           "w_branch_b", "w_mix_out", "mix_post_norm", "xa_pre_norm", "mem_norm", "w_xq", "w_xkv", "w_xo", "xa_post_norm",
           "ffn_pre_norm", "w_up", "ffn_conv_w", "ffn_conv_b", "w_down", "ffn_post_norm")
BIG_BF16 = ("w_in", "w_branch_a", "w_branch_b", "w_mix_out", "w_xq", "w_xkv", "w_xo", "w_up", "w_down")
SMALL_F32 = ("conv_qkv", "pool_w", "ffn_conv_w")


def _local_step(x, mem, target, W):
    S, D = x.shape
    H = W["a_log"].shape[-1]
    qkv_w, vw = 3 * H * HD, H * HD
    pw = W["pool_scale"].shape[-1]
    F = W["w_down"].shape[0]
    Fp = -(-F // 512) * 512 if F >= 512 else F
    ba_w = 512 if D >= 2048 else LANE
    seg = dict(qkv=(0, qkv_w), z=(qkv_w, vw), ga=(qkv_w + vw, D), gb=(qkv_w + vw + D, D),
               p=(qkv_w + vw + 2 * D, pw), ba=(qkv_w + vw + 2 * D + pw, ba_w))
    w_in = W["w_in"]
    o_qkv, o_z, o_b, o_a, o_p, o_ga, o_gb = 0, qkv_w, qkv_w + vw, qkv_w + vw + H, qkv_w + vw + 2 * H, \
        qkv_w + vw + 2 * H + pw, qkv_w + vw + 2 * H + pw + D
    win_p = jnp.concatenate([
        w_in[:, o_qkv:o_z], w_in[:, o_z:o_b], w_in[:, o_ga:o_gb], w_in[:, o_gb:o_gb + D], w_in[:, o_p:o_ga],
        _pad_cols(w_in[:, o_b:o_p], ba_w)], axis=1).astype(BF16)
    lanes = lambda vec: jnp.pad(vec.reshape(1, H).astype(F32), ((0, 0), (H, LANE - 2 * H)))
    a_log_l, dt_bias_l = lanes(W["a_log"]), lanes(W["dt_bias"])
    w_up = W["w_up"]
    wup_p = jnp.concatenate([_pad_cols(w_up[:, :F], Fp), _pad_cols(w_up[:, F:], Fp)], axis=1).astype(BF16)
    cw3 = W["ffn_conv_w"].astype(F32)
    cw3_p = jnp.concatenate([_pad_cols(cw3[:, :F], Fp), _pad_cols(cw3[:, F:], Fp)], axis=1)
    fb = W["ffn_conv_b"].reshape(1, 2 * F).astype(F32)
    fb_p = jnp.concatenate([_pad_cols(fb[:, :F], Fp), _pad_cols(fb[:, F:], Fp)], axis=1)
    wdown_p = jnp.pad(W["w_down"], ((0, Fp - F), (0, 0))).astype(BF16)
    bf = lambda name: W[name].astype(BF16)
    vecf = lambda name: W[name].reshape(1, -1).astype(F32)
    conv_qkv = W["conv_qkv"].astype(F32)
    pool_w = W["pool_w"].astype(BF16)
    col = lambda name, width: seg[name][0] // width

    h1 = _prenorm(x, vecf("mix_pre_norm"), "mix_prenorm")
    proj = _matmul(h1, win_p, "nn", BF16, "in_proj", tn=768)
    qkv_hm = _qkv_conv(proj, conv_qkv, qkv_w, "qkv_conv")
    bg = _gates(proj, col("ba", LANE), a_log_l, dt_bias_l, H, "gates")
    o_hm, states = _gdn_fwd(qkv_hm, bg, H, "gdn_fwd")
    oa = _gdn_out(o_hm, proj, col("z", vw), vecf("gdn_norm"), "gdn_out")
    ya = _matmul(oa, bf("w_branch_a"), "nn", BF16, "branch_a")
    pb = _pool_fwd(proj, col("p", pw), pool_w, vecf("pool_scale"), pw, "pool_fwd")
    yb = _matmul(pb, bf("w_branch_b"), "nn", BF16, "branch_b")
    merged = _merge(proj, col("ga", D), col("gb", D), ya, yb, "merge")
    y1 = _matmul(merged, bf("w_mix_out"), "nn", F32, "mix_out")
    x1, h2 = _post_pre(x, y1, vecf("mix_post_norm"), vecf("xa_pre_norm"), "mix_post")
    mn = _prenorm(mem, vecf("mem_norm"), "mem_norm")
    qx = _matmul(h2, bf("w_xq"), "nn", BF16, "xq")
    kv = _matmul(mn, bf("w_xkv"), "nn", BF16, "xkv")
    ox = _xattn_fwd(qx, kv, "xattn_fwd")
    y2 = _matmul(ox, bf("w_xo"), "nn", F32, "xo")
    x2, h3 = _post_pre(x1, y2, vecf("xa_post_norm"), vecf("ffn_pre_norm"), "xa_post")
    up = _matmul(h3, wup_p, "nn", BF16, "ffn_up")
    act = _ffn_act(up, cw3_p, fb_p, "ffn_act")
    y3 = _matmul(act, wdown_p, "nn", F32, "ffn_down")
    dx3, loss = _post_loss(x2, y3, vecf("ffn_post_norm"), target, "ffn_post_loss")

    g = {}
    dy3, g["ffn_post_norm"] = _post_bwd(y3, vecf("ffn_post_norm"), dx3, "ffn_post_bwd")
    dact = _matmul(dy3, wdown_p, "nt", BF16, "ffn_down_dx")
    g["w_down"] = _matmul(act, dy3, "tn", F32, "ffn_down_dw")[:F]
    du, dcw3, dfb = _ffn_act_bwd(up, cw3_p, fb_p, dact, "ffn_act_bwd")
    halves = lambda a: jnp.concatenate([a[:, :F], a[:, Fp:Fp + F]], axis=1)
    g["ffn_conv_w"], g["ffn_conv_b"] = halves(dcw3), halves(dfb)
    dup = _conv_t(du, cw3_p, "ffn_conv_t")
    dh3 = _matmul(dup, wup_p, "nt", F32, "ffn_up_dx")
    g["w_up"] = halves(_matmul(h3, dup, "tn", F32, "ffn_up_dw"))
    dx2, g["ffn_pre_norm"] = _pre_bwd(x2, vecf("ffn_pre_norm"), dh3, dx3, "ffn_pre_bwd")
    dy2, g["xa_post_norm"] = _post_bwd(y2, vecf("xa_post_norm"), dx2, "xa_post_bwd")
    dox = _matmul(dy2, bf("w_xo"), "nt", BF16, "xo_dx")
    g["w_xo"] = _matmul(ox, dy2, "tn", F32, "xo_dw")
    dqx, dkv = _xattn_bwd(qx, kv, dox, "xattn_bwd")
    dkv_b = dkv.astype(BF16)
    dh2 = _matmul(dqx, bf("w_xq"), "nt", F32, "xq_dx")
    g["w_xq"] = _matmul(h2, dqx, "tn", F32, "xq_dw")
    dmn = _matmul(dkv_b, bf("w_xkv"), "nt", F32, "xkv_dx")
    g["w_xkv"] = _matmul(mn, dkv_b, "tn", F32, "xkv_dw")
    _, g["mem_norm"] = _pre_bwd(mem, vecf("mem_norm"), dmn, jnp.zeros_like(mem), "mem_norm_bwd")
    dx1, g["xa_pre_norm"] = _pre_bwd(x1, vecf("xa_pre_norm"), dh2, dx2, "xa_pre_bwd")
    dy1, g["mix_post_norm"] = _post_bwd(y1, vecf("mix_post_norm"), dx1, "mix_post_bwd")
    dmerged = _matmul(dy1, bf("w_mix_out"), "nt", BF16, "mix_out_dx")
    g["w_mix_out"] = _matmul(merged, dy1, "tn", F32, "mix_out_dw")
    dya, dyb, dga, dgb = _merge_bwd(proj, col("ga", D), col("gb", D), ya, yb, dmerged, "merge_bwd")
    doa = _matmul(dya, bf("w_branch_a"), "nt", BF16, "branch_a_dx")
    g["w_branch_a"] = _matmul(oa, dya, "tn", F32, "branch_a_dw")
    dpb = _matmul(dyb, bf("w_branch_b"), "nt", BF16, "branch_b_dx")
    g["w_branch_b"] = _matmul(pb, dyb, "tn", F32, "branch_b_dw")
    dp, g["pool_w"], g["pool_scale"] = _pool_bwd(proj, col("p", pw), pool_w, vecf("pool_scale"), dpb, pw, "pool_bwd")
    do_hm, dz, g["gdn_norm"] = _gdn_out_bwd(o_hm, proj, col("z", vw), vecf("gdn_norm"), doa, "gdn_out_bwd")
    dqkv_hm, dbg = _gdn_bwd(qkv_hm, bg, states, do_hm, H, "gdn_bwd")
    dba, dal, ddt = _gates_bwd(proj, col("ba", LANE), ba_w, a_log_l, dt_bias_l, dbg, H, "gates_bwd")
    g["a_log"], g["dt_bias"] = dal[:, H:2 * H], ddt[:, H:2 * H]
    dc, g["conv_qkv"] = _qkv_conv_bwd(proj, conv_qkv, dqkv_hm, qkv_w, "qkv_conv_bwd")
    dqkv = _conv_t(dc, conv_qkv, "qkv_conv_t")
    dproj = jnp.concatenate([dqkv, dz, dga, dgb, dp, dba], axis=1)
    dh1 = _matmul(dproj, win_p, "nt", F32, "in_proj_dx")
    dwin_p = _matmul(h1, dproj, "tn", F32, "in_proj_dw", tn=768)
    sl = lambda name, n=None: dwin_p[:, seg[name][0]:seg[name][0] + (seg[name][1] if n is None else n)]
    g["w_in"] = jnp.concatenate([sl("qkv"), sl("z"), sl("ba", 2 * H), sl("p"), sl("ga"), sl("gb")], axis=1)
    grad_x, g["mix_pre_norm"] = _pre_bwd(x, vecf("mix_pre_norm"), dh1, dx1, "mix_pre_bwd")
    return loss, grad_x, g


def kernel(x, mem, mix_pre_norm, w_in, conv_qkv, a_log, dt_bias, gdn_norm, pool_w, pool_scale, w_branch_a, w_branch_b, w_mix_out, mix_post_norm, xa_pre_norm, mem_norm, w_xq, w_xkv, w_xo, xa_post_norm, ffn_pre_norm, w_up, ffn_conv_w, ffn_conv_b, w_down, ffn_post_norm, loss_target, m_mix_pre_norm, m_w_in, m_conv_qkv, m_a_log, m_dt_bias, m_gdn_norm, m_pool_w, m_pool_scale, m_w_branch_a, m_w_branch_b, m_w_mix_out, m_mix_post_norm, m_xa_pre_norm, m_mem_norm, m_w_xq, m_w_xkv, m_w_xo, m_xa_post_norm, m_ffn_pre_norm, m_w_up, m_ffn_conv_w, m_ffn_conv_b, m_w_down, m_ffn_post_norm, v_mix_pre_norm, v_w_in, v_conv_qkv, v_a_log, v_dt_bias, v_gdn_norm, v_pool_w, v_pool_scale, v_w_branch_a, v_w_branch_b, v_w_mix_out, v_mix_post_norm, v_xa_pre_norm, v_mem_norm, v_w_xq, v_w_xkv, v_w_xo, v_xa_post_norm, v_ffn_pre_norm, v_w_up, v_ffn_conv_w, v_ffn_conv_b, v_w_down, v_ffn_post_norm):
    given = dict(locals())
    w = {n: given[n][0] for n in WEIGHTS}
    m = {n: given["m_" + n][0] for n in WEIGHTS}
    v = {n: given["v_" + n][0] for n in WEIGHTS}

    big, big_layout = _pack([w[n] for n in BIG_BF16], BF16, 16)
    small, small_layout = _pack([w[n] for n in SMALL_F32], F32, 8)
    big_g = dict(zip(BIG_BF16, _unpack_gathered(_all_gather(big, "gather_weights"), big_layout)))
    small_g = dict(zip(SMALL_F32, _unpack_gathered(_all_gather(small, "gather_filters"), small_layout)))
    full = {n: w[n] for n in REPLICATED}
    for n in ("w_in", "w_branch_b", "w_xkv", "w_up"):
        full[n] = _cols_to_full(big_g[n])
    for n in ("w_branch_a", "w_mix_out", "w_xq", "w_xo", "w_down"):
        full[n] = _rows_to_full(big_g[n])
    full["conv_qkv"] = _cols_to_full(small_g["conv_qkv"])
    full["ffn_conv_w"] = _cols_to_full(small_g["ffn_conv_w"])
    full["pool_w"] = jnp.transpose(small_g["pool_w"], (1, 0, 2, 3)).reshape(
        pool_w.shape[1], N_DEV * pool_w.shape[2], pool_w.shape[3])

    loss, grad_x, g = _local_step(x[0], mem[0], loss_target[0], full)

    def slices(n):
        a = g[n]
        if n in ("w_in", "w_branch_b", "w_xkv", "w_up", "conv_qkv", "ffn_conv_w"):
            return _full_to_cols(a)
        if n == "pool_w":
            return jnp.transpose(a.reshape(a.shape[0], N_DEV, a.shape[1] // N_DEV, a.shape[2]), (1, 0, 2, 3))
        return a.reshape(N_DEV, a.shape[0] // N_DEV, a.shape[1])

    ROWS = 1024
    sliced = {n: slices(n) for n in SHARDED}
    per_dev = [_pack([sliced[n][j] for n in SHARDED], F32, ROWS)[0] for j in range(N_DEV)]
    received = _slice_exchange(jnp.stack(per_dev, axis=0), "exchange_grads")
    wp, layout = _pack([w[n] for n in SHARDED], F32, ROWS)
    mp, _ = _pack([m[n] for n in SHARDED], F32, ROWS)
    vp, _ = _pack([v[n] for n in SHARDED], F32, ROWS)
    outs_sh = [dict(zip(SHARDED, _unpack(o, layout))) for o in _adamw(received, wp, mp, vp, "adamw_sharded")]

    rep_parts, rep_layout = _pack([g[n].reshape(w[n].shape) for n in REPLICATED] + [loss], F32, 8)
    rep_all = _all_gather(rep_parts, "gather_small_grads")
    zero_loss = jnp.zeros_like(loss)
    wr, _ = _pack([w[n] for n in REPLICATED] + [zero_loss], F32, 8)
    mr, _ = _pack([m[n] for n in REPLICATED] + [zero_loss], F32, 8)
    vr, _ = _pack([v[n] for n in REPLICATED] + [zero_loss], F32, 8)
    outs_rep = [_unpack(o, rep_layout) for o in _adamw(rep_all, wr, mr, vr, "adamw_replicated")]
    loss_total = outs_rep[0][-1][0, 0]
    outs_rp = [dict(zip(REPLICATED, o[:-1])) for o in outs_rep]

    result = [loss_total, grad_x[None]]
    for k in range(4):
        for n in WEIGHTS:
            src = outs_sh[k] if n in SHARDED else outs_rp[k]
            result.append(src[n][None])
    return tuple(result)
```

```python
import functools

import jax
import jax.numpy as jnp
from jax import lax
from jax.experimental import pallas as pl
from jax.experimental.pallas import tpu as pltpu

F32, BF16 = jnp.float32, jnp.bfloat16
MESH = pl.DeviceIdType.MESH
ANY = pl.BlockSpec(memory_space=pl.ANY)

N_DEV = 8
EPS = 1e-6
CHUNK = 64
HD = 128
GDN_CONV = 4
FFN_CONV = 3
POOL_WINDOWS = (2, 4, 8, 16)
XA_HEADS = 4
HALO = 16
LANE = 128
VMEM_LIMIT = 48 * 1024 * 1024

ADAM_LR, ADAM_B1, ADAM_B2, ADAM_EPS, ADAM_WD, ADAM_STEP = 0.001, 0.9, 0.999, 1e-08, 0.01, 10


def _tile(n, pref, align=LANE):
    best = None
    t = align
    while t <= min(n, pref):
        if n % t == 0:
            best = t
        t += align
    return best if best is not None else n


def _params(**kw):
    return pltpu.CompilerParams(vmem_limit_bytes=VMEM_LIMIT, **kw)


def _sigmoid(x):
    return 1.0 / (1.0 + jnp.exp(-x))


def _silu(x):
    return x * _sigmoid(x)


def _dsilu(x):
    s = _sigmoid(x)
    return s * (1.0 + x * (1.0 - s))


def _colsum8(t):
    return t.reshape(t.shape[0] // 8, 8, t.shape[1]).sum(axis=0)


def _matmul(a, b, mode, out_dtype, name, tm=1024, tn=1024, tk=1024):
    if mode == "nn":
        (M, K), (K2, N) = a.shape, b.shape
    elif mode == "nt":
        (M, K), (N, K2) = a.shape, b.shape
    else:
        (K, M), (K2, N) = a.shape, b.shape
    assert K == K2, (name, a.shape, b.shape)
    tm, tn = _tile(M, tm), _tile(N, tn)
    tk = K if K <= 2048 else _tile(K, tk)
    nk = K // tk
    if mode == "nn":
        a_spec = pl.BlockSpec((tm, tk), lambda i, j, k: (i, k))
        b_spec = pl.BlockSpec((tk, tn), lambda i, j, k: (k, j))
        dims = (((1,), (0,)), ((), ()))
    elif mode == "nt":
        a_spec = pl.BlockSpec((tm, tk), lambda i, j, k: (i, k))
        b_spec = pl.BlockSpec((tn, tk), lambda i, j, k: (j, k))
        dims = (((1,), (1,)), ((), ()))
    else:
        a_spec = pl.BlockSpec((tk, tm), lambda i, j, k: (k, i))
        b_spec = pl.BlockSpec((tk, tn), lambda i, j, k: (k, j))
        dims = (((0,), (0,)), ((), ()))

    def body(a_ref, b_ref, o_ref, acc):
        part = lax.dot_general(a_ref[...], b_ref[...], dims, preferred_element_type=F32)
        if nk == 1:
            o_ref[...] = part.astype(o_ref.dtype)
        else:
            k = pl.program_id(2)

            @pl.when(k == 0)
            def _():
                acc[...] = part

            @pl.when(k > 0)
            def _():
                acc[...] += part

            @pl.when(k == nk - 1)
            def _():
                o_ref[...] = acc[...].astype(o_ref.dtype)

    return pl.pallas_call(
        body, name=name, out_shape=jax.ShapeDtypeStruct((M, N), out_dtype),
        grid=(M // tm, N // tn, nk), in_specs=[a_spec, b_spec],
        out_specs=pl.BlockSpec((tm, tn), lambda i, j, k: (i, j)),
        scratch_shapes=[pltpu.VMEM((tm, tn) if nk > 1 else (8, LANE), F32)],
        compiler_params=_params(dimension_semantics=("parallel", "parallel", "arbitrary")),
    )(a, b)


def _rstd(xf):
    return lax.rsqrt(jnp.mean(xf * xf, axis=-1, keepdims=True) + EPS)


def _rms_bwd(xf, w, dy):
    r = _rstd(xf)
    g = dy * w
    dx = r * g - xf * (r * r * r) * jnp.mean(g * xf, axis=-1, keepdims=True)
    return dx, dy * xf * r


def _row_tile(rows):
    return _tile(rows, 256, 8)


def _prenorm(x, w, name):
    rows, d = x.shape
    ts = _row_tile(rows)

    def body(x_ref, w_ref, h_ref):
        xf = x_ref[...]
        h_ref[...] = (xf * _rstd(xf) * w_ref[...]).astype(BF16)

    return pl.pallas_call(
        body, name=name, out_shape=jax.ShapeDtypeStruct((rows, d), BF16), grid=(rows // ts,),
        in_specs=[pl.BlockSpec((ts, d), lambda i: (i, 0)), pl.BlockSpec((1, d), lambda i: (0, 0))],
        out_specs=pl.BlockSpec((ts, d), lambda i: (i, 0)), compiler_params=_params(),
    )(x, w)


def _post_pre(xres, y, w_post, w_pre, name):
    rows, d = xres.shape
    ts = _row_tile(rows)

    def body(x_ref, y_ref, wp_ref, wn_ref, xo_ref, h_ref):
        yf = y_ref[...]
        xn = x_ref[...] + yf * _rstd(yf) * wp_ref[...]
        xo_ref[...] = xn
        h_ref[...] = (xn * _rstd(xn) * wn_ref[...]).astype(BF16)

    row = pl.BlockSpec((ts, d), lambda i: (i, 0))
    vec = pl.BlockSpec((1, d), lambda i: (0, 0))
    return pl.pallas_call(
        body, name=name, grid=(rows // ts,),
        out_shape=(jax.ShapeDtypeStruct((rows, d), F32), jax.ShapeDtypeStruct((rows, d), BF16)),
        in_specs=[row, row, vec, vec], out_specs=(row, row), compiler_params=_params(),
    )(xres, y, w_post, w_pre)


def _post_loss(xres, y, w_post, target, name):
    rows, d = xres.shape
    ts = _row_tile(rows)
    n = rows // ts

    def body(x_ref, y_ref, wp_ref, t_ref, dx_ref, loss_ref, acc):
        i = pl.program_id(0)
        yf = y_ref[...]
        diff = x_ref[...] + yf * _rstd(yf) * wp_ref[...] - t_ref[...]
        dx_ref[...] = diff * (1.0 / d)

        @pl.when(i == 0)
        def _():
            acc[...] = jnp.zeros_like(acc)

        acc[...] += _colsum8(diff * diff)

        @pl.when(i == n - 1)
        def _():
            loss_ref[...] = jnp.broadcast_to((0.5 / d) * jnp.sum(acc[...]), loss_ref.shape)

    row = pl.BlockSpec((ts, d), lambda i: (i, 0))
    vec = pl.BlockSpec((1, d), lambda i: (0, 0))
    return pl.pallas_call(
        body, name=name, grid=(n,),
        out_shape=(jax.ShapeDtypeStruct((rows, d), F32), jax.ShapeDtypeStruct((1, LANE), F32)),
        in_specs=[row, row, vec, row], out_specs=(row, pl.BlockSpec((1, LANE), lambda i: (0, 0))),
        scratch_shapes=[pltpu.VMEM((8, d), F32)], compiler_params=_params(),
    )(xres, y, w_post, target)


def _post_bwd(y, w_post, dxn, name):
    rows, d = y.shape
    ts = _row_tile(rows)
    n = rows // ts

    def body(y_ref, w_ref, d_ref, dy_ref, dw_ref, acc):
        i = pl.program_id(0)
        dy, dwr = _rms_bwd(y_ref[...], w_ref[...], d_ref[...])
        dy_ref[...] = dy.astype(BF16)

        @pl.when(i == 0)
        def _():
            acc[...] = jnp.zeros_like(acc)

        acc[...] += _colsum8(dwr)

        @pl.when(i == n - 1)
        def _():
            dw_ref[...] = jnp.sum(acc[...], axis=0, keepdims=True)

    row = pl.BlockSpec((ts, d), lambda i: (i, 0))
    vec = pl.BlockSpec((1, d), lambda i: (0, 0))
    return pl.pallas_call(
        body, name=name, grid=(n,),
        out_shape=(jax.ShapeDtypeStruct((rows, d), BF16), jax.ShapeDtypeStruct((1, d), F32)),
        in_specs=[row, vec, row], out_specs=(row, vec),
        scratch_shapes=[pltpu.VMEM((8, d), F32)], compiler_params=_params(),
    )(y, w_post, dxn)


def _pre_bwd(x, w_pre, dh, dres, name):
    rows, d = x.shape
    ts = _row_tile(rows)
    n = rows // ts

    def body(x_ref, w_ref, dh_ref, dr_ref, dx_ref, dw_ref, acc):
        i = pl.program_id(0)
        dx, dwr = _rms_bwd(x_ref[...], w_ref[...], dh_ref[...].astype(F32))
        dx_ref[...] = dr_ref[...] + dx

        @pl.when(i == 0)
        def _():
            acc[...] = jnp.zeros_like(acc)

        acc[...] += _colsum8(dwr)

        @pl.when(i == n - 1)
        def _():
            dw_ref[...] = jnp.sum(acc[...], axis=0, keepdims=True)

    row = pl.BlockSpec((ts, d), lambda i: (i, 0))
    vec = pl.BlockSpec((1, d), lambda i: (0, 0))
    return pl.pallas_call(
        body, name=name, grid=(n,),
        out_shape=(jax.ShapeDtypeStruct((rows, d), F32), jax.ShapeDtypeStruct((1, d), F32)),
        in_specs=[row, vec, row, row], out_specs=(row, vec),
        scratch_shapes=[pltpu.VMEM((8, d), F32)], compiler_params=_params(),
    )(x, w_pre, dh, dres)


def _prev_halo_spec(ts, cw, col0=0):
    return pl.BlockSpec((HALO, cw), lambda i, j: (jnp.maximum(i * (ts // HALO) - 1, 0), j + col0))


def _fill_causal(ext, tile_f32, halo_f32, i):
    ext[pl.ds(0, HALO), :] = jnp.where(i > 0, halo_f32, 0.0)
    ext[pl.ds(HALO, tile_f32.shape[0]), :] = tile_f32


def _delayed(ext, j, ts):
    return ext[pl.ds(HALO - j, ts), :]


def _conv_t(dc, w, name, col_tile=512):
    S, C = dc.shape
    K = w.shape[0]
    ts, cw = _tile(S, 512, HALO), _tile(C, col_tile)
    n = S // ts

    def body(d_ref, nx_ref, w_ref, o_ref, ext):
        i = pl.program_id(0)
        ext[pl.ds(0, ts), :] = d_ref[...].astype(F32)
        ext[pl.ds(ts, HALO), :] = jnp.where(i < n - 1, nx_ref[...].astype(F32), 0.0)
        wv = w_ref[...]
        acc = wv[K - 1:K, :] * ext[pl.ds(0, ts), :]
        for j in range(K - 1):
            acc = acc + wv[j:j + 1, :] * ext[pl.ds(K - 1 - j, ts), :]
        o_ref[...] = acc.astype(o_ref.dtype)

    return pl.pallas_call(
        body, name=name, out_shape=jax.ShapeDtypeStruct((S, C), BF16), grid=(n, C // cw),
        in_specs=[pl.BlockSpec((ts, cw), lambda i, j: (i, j)),
                  pl.BlockSpec((HALO, cw), lambda i, j: (jnp.minimum((i + 1) * (ts // HALO), S // HALO - 1), j)),
                  pl.BlockSpec((K, cw), lambda i, j: (0, j))],
        out_specs=pl.BlockSpec((ts, cw), lambda i, j: (i, j)),
        scratch_shapes=[pltpu.VMEM((ts + HALO, cw), F32)], compiler_params=_params(),
    )(dc, dc, w)


def _qkv_conv(proj, conv_w, qkv_w, name):
    S = proj.shape[0]
    H3 = qkv_w // HD
    H = H3 // 3
    hb = 4 if H % 4 == 0 else 1
    cw = hb * HD
    ts = _tile(S, 512, HALO)
    per_kind = H // hb

    def body(x_ref, h_ref, w_ref, o_ref, ext):
        i, j = pl.program_id(0), pl.program_id(1)
        _fill_causal(ext, x_ref[...].astype(F32), h_ref[...].astype(F32), i)
        wv = w_ref[...]
        c = wv[GDN_CONV - 1:GDN_CONV, :] * _delayed(ext, 0, ts)
        for t in range(GDN_CONV - 1):
            c = c + wv[t:t + 1, :] * _delayed(ext, GDN_CONV - 1 - t, ts)
        s = _silu(c)
        kind = j // per_kind
        scale = jnp.where(kind == 0, HD ** -0.5, 1.0)
        for a in range(hb):
            sa = s[:, HD * a:HD * (a + 1)]
            r = lax.rsqrt(jnp.sum(sa * sa, axis=-1, keepdims=True) + EPS)
            o_ref[a] = jnp.where(kind == 2, sa, sa * r * scale)

    return pl.pallas_call(
        body, name=name, out_shape=jax.ShapeDtypeStruct((H3, S, HD), F32), grid=(S // ts, qkv_w // cw),
        in_specs=[pl.BlockSpec((ts, cw), lambda i, j: (i, j)), _prev_halo_spec(ts, cw),
                  pl.BlockSpec((GDN_CONV, cw), lambda i, j: (0, j))],
        out_specs=pl.BlockSpec((hb, ts, HD), lambda i, j: (j, i, 0)),
        scratch_shapes=[pltpu.VMEM((ts + HALO, cw), F32)], compiler_params=_params(),
    )(proj, proj, conv_w)


def _qkv_conv_bwd(proj, conv_w, dqkv_hm, qkv_w, name):
    S = proj.shape[0]
    H = qkv_w // HD // 3
    hb = 4 if H % 4 == 0 else 1
    cw = hb * HD
    ts = _tile(S, 512, HALO)
    n = S // ts
    per_kind = H // hb

    def body(x_ref, h_ref, w_ref, d_ref, dc_ref, dw_ref, ext, acc):
        j, i = pl.program_id(0), pl.program_id(1)
        _fill_causal(ext, x_ref[...].astype(F32), h_ref[...].astype(F32), i)
        wv = w_ref[...]
        c = wv[GDN_CONV - 1:GDN_CONV, :] * _delayed(ext, 0, ts)
        for t in range(GDN_CONV - 1):
            c = c + wv[t:t + 1, :] * _delayed(ext, GDN_CONV - 1 - t, ts)
        s = _silu(c)
        kind = j // per_kind
        scale = jnp.where(kind == 0, HD ** -0.5, 1.0)
        parts = []
        for a in range(hb):
            sa = s[:, HD * a:HD * (a + 1)]
            dy = d_ref[a]
            r = lax.rsqrt(jnp.sum(sa * sa, axis=-1, keepdims=True) + EPS)
            dn = scale * (r * dy - sa * (r * r * r) * jnp.sum(dy * sa, axis=-1, keepdims=True))
            parts.append(jnp.where(kind == 2, dy, dn))
        dc = jnp.concatenate(parts, axis=1) * _dsilu(c)
        dc_ref[...] = dc.astype(BF16)

        @pl.when(i == 0)
        def _():
            acc[...] = jnp.zeros_like(acc)

        for t in range(GDN_CONV):
            acc[t] += _colsum8(dc * _delayed(ext, GDN_CONV - 1 - t, ts))

        @pl.when(i == n - 1)
        def _():
            dw_ref[...] = jnp.sum(acc[...], axis=1)

    return pl.pallas_call(
        body, name=name, grid=(qkv_w // cw, n),
        out_shape=(jax.ShapeDtypeStruct((S, qkv_w), BF16), jax.ShapeDtypeStruct((GDN_CONV, qkv_w), F32)),
        in_specs=[pl.BlockSpec((ts, cw), lambda j, i: (i, j)),
                  pl.BlockSpec((HALO, cw), lambda j, i: (jnp.maximum(i * (ts // HALO) - 1, 0), j)),
                  pl.BlockSpec((GDN_CONV, cw), lambda j, i: (0, j)),
                  pl.BlockSpec((hb, ts, HD), lambda j, i: (j, i, 0))],
        out_specs=(pl.BlockSpec((ts, cw), lambda j, i: (i, j)), pl.BlockSpec((GDN_CONV, cw), lambda j, i: (0, j))),
        scratch_shapes=[pltpu.VMEM((ts + HALO, cw), F32), pltpu.VMEM((GDN_CONV, 8, cw), F32)],
        compiler_params=_params(),
    )(proj, proj, conv_w, dqkv_hm)


def _chunk_cumsum(x):
    row = lax.broadcasted_iota(jnp.int32, x.shape, 0) & (CHUNK - 1)
    s = 1
    while s < CHUNK:
        x = x + jnp.where(row >= s, pltpu.roll(x, s, axis=0), 0.0)
        s *= 2
    return x


def _chunk_rev_cumsum(x):
    rows = x.shape[0]
    row = lax.broadcasted_iota(jnp.int32, x.shape, 0) & (CHUNK - 1)
    s = 1
    while s < CHUNK:
        x = x + jnp.where(row < CHUNK - s, pltpu.roll(x, rows - s, axis=0), 0.0)
        s *= 2
    return x


def _softplus(x):
    return jnp.maximum(x, 0.0) + jnp.log1p(jnp.exp(-jnp.abs(x)))


def _gates(proj, ba_col, a_log_l, dt_bias_l, H, name):
    S = proj.shape[0]
    ts = _tile(S, 512, CHUNK)

    def body(x_ref, al_ref, dt_ref, o_ref):
        x = x_ref[...].astype(F32)
        lane = lax.broadcasted_iota(jnp.int32, x.shape, 1)
        g = -jnp.exp(al_ref[...]) * _softplus(x + dt_ref[...])
        G = _chunk_cumsum(jnp.where((lane >= H) & (lane < 2 * H), g, 0.0))
        o_ref[...] = jnp.where(lane < H, _sigmoid(x), G)

    return pl.pallas_call(
        body, name=name, out_shape=jax.ShapeDtypeStruct((S, LANE), F32), grid=(S // ts,),
        in_specs=[pl.BlockSpec((ts, LANE), lambda i: (i, ba_col)), pl.BlockSpec((1, LANE), lambda i: (0, 0)),
                  pl.BlockSpec((1, LANE), lambda i: (0, 0))],
        out_specs=pl.BlockSpec((ts, LANE), lambda i: (i, 0)), compiler_params=_params(),
    )(proj, a_log_l, dt_bias_l)


def _gates_bwd(proj, ba_col, ba_w, a_log_l, dt_bias_l, dbg, H, name):
    S = proj.shape[0]
    ts = _tile(S, 512, CHUNK)
    n = S // ts

    def body(x_ref, al_ref, dt_ref, d_ref, o_ref, dal_ref, ddt_ref, acc):
        i = pl.program_id(0)
        x = x_ref[...].astype(F32)
        d = d_ref[...]
        lane = lax.broadcasted_iota(jnp.int32, x.shape, 1)
        is_a = (lane >= H) & (lane < 2 * H)
        beta = _sigmoid(x)
        nea = -jnp.exp(al_ref[...])
        z = x + dt_ref[...]
        dg = _chunk_rev_cumsum(jnp.where(is_a, d, 0.0))
        da_raw = dg * nea * _sigmoid(z)
        o = jnp.where(lane < H, d * beta * (1.0 - beta), jnp.where(is_a, da_raw, 0.0))
        if ba_w > LANE:
            o = jnp.concatenate([o, jnp.zeros((ts, ba_w - LANE), F32)], axis=1)
        o_ref[...] = o.astype(BF16)

        @pl.when(i == 0)
        def _():
            acc[...] = jnp.zeros_like(acc)

        acc[0] += _colsum8(jnp.where(is_a, dg * nea * _softplus(z), 0.0))
        acc[1] += _colsum8(jnp.where(is_a, da_raw, 0.0))

        @pl.when(i == n - 1)
        def _():
            dal_ref[...] = jnp.sum(acc[0], axis=0, keepdims=True)
            ddt_ref[...] = jnp.sum(acc[1], axis=0, keepdims=True)

    vec = pl.BlockSpec((1, LANE), lambda i: (0, 0))
    return pl.pallas_call(
        body, name=name, grid=(n,),
        out_shape=(jax.ShapeDtypeStruct((S, ba_w), BF16), jax.ShapeDtypeStruct((1, LANE), F32),
                   jax.ShapeDtypeStruct((1, LANE), F32)),
        in_specs=[pl.BlockSpec((ts, LANE), lambda i: (i, ba_col)), vec, vec, pl.BlockSpec((ts, LANE), lambda i: (i, 0))],
        out_specs=(pl.BlockSpec((ts, ba_w), lambda i: (i, 0)), vec, vec),
        scratch_shapes=[pltpu.VMEM((2, 8, LANE), F32)], compiler_params=_params(),
    )(proj, a_log_l, dt_bias_l, dbg)


_BMM_FORMS = {"nn": "hik,hkj->hij", "nt": "hik,hjk->hij", "tn": "hki,hkj->hij"}


def _split_bf16(a):
    hi = a.astype(BF16)
    return hi, (a - hi.astype(F32)).astype(BF16)


def _bmm(a, b, form="nn", exact=False):
    e = lambda x, y: jnp.einsum(_BMM_FORMS[form], x, y, preferred_element_type=F32)
    if not exact:
        return e(a.astype(BF16), b.astype(BF16))
    (ah, al), (bh, bl) = _split_bf16(a), _split_bf16(b)
    return (e(ah, bl) + e(al, bh)) + e(ah, bh)


def _unit_lower_inverse(L, r, c):
    eye = (r == c).astype(F32)
    m = jnp.where((r >> 3) == (c >> 3), -L, 0.0)
    m2 = _bmm(m, m, exact=True)
    m4 = _bmm(m2, m2, exact=True)
    x = eye + m
    x = x + _bmm(x, m2, exact=True)
    x = x + _bmm(x, m4, exact=True)
    for sh in (3, 4, 5):
        off = ((r >> (sh + 1)) == (c >> (sh + 1))) & ((r >> sh) != (c >> sh))
        x = x - _bmm(x, _bmm(jnp.where(off, L, 0.0), x, exact=True), exact=True)
    return x


def _to_row(col, eye):
    return jnp.sum(jnp.where(eye, jnp.broadcast_to(col, eye.shape), 0.0), axis=1, keepdims=True)


def _to_col(rowv, eye):
    return jnp.sum(jnp.where(eye, jnp.broadcast_to(rowv, eye.shape), 0.0), axis=2, keepdims=True)


def _gdn_chunk(q, k, v, bg, H):
    shape = (H, CHUNK, CHUNK)
    r = lax.broadcasted_iota(jnp.int32, shape, 1)
    c = lax.broadcasted_iota(jnp.int32, shape, 2)
    eye, incl, strict = r == c, r >= c, r > c
    beta = jnp.stack([bg[:, h:h + 1] for h in range(H)], axis=0)
    G = jnp.stack([bg[:, H + h:H + h + 1] for h in range(H)], axis=0)
    gap = jnp.broadcast_to(G, shape) - _to_row(G, eye)
    decay = jnp.where(incl, jnp.exp(jnp.where(incl, gap, 0.0)), 0.0)
    kk = _bmm(k, k, "nt", exact=True)
    L = jnp.where(strict, beta * decay * kk, 0.0)
    ainv = _unit_lower_inverse(L, r, c)
    eG = jnp.exp(G)
    u_v = _bmm(ainv, beta * v, exact=True)
    w_k = _bmm(ainv, (beta * eG) * k, exact=True)
    qk = _bmm(q, k, "nt", exact=True)
    GL = G[:, CHUNK - 1:CHUNK, :]
    ek = jnp.exp(GL - G)
    return dict(eye=eye, strict=strict, r=r, c=c, beta=beta, G=G, decay=decay, kk=kk, ainv=ainv, eG=eG,
                u_v=u_v, w_k=w_k, qk=qk, attn=decay * qk, GL=GL, ek=ek, cd=jnp.exp(GL))


def _gdn_fwd(qkv_hm, bg, H, name):
    S = qkv_hm.shape[1]
    N = S // CHUNK

    def body(q_ref, k_ref, v_ref, bg_ref, o_ref, st_ref, state):
        n = pl.program_id(0)

        @pl.when(n == 0)
        def _():
            state[...] = jnp.zeros_like(state)

        q, k, v = q_ref[...], k_ref[...], v_ref[...]
        t = _gdn_chunk(q, k, v, bg_ref[...], H)
        s0 = state[...]
        st_ref[0] = s0
        u = t["u_v"] - _bmm(t["w_k"], s0)
        o_ref[...] = _bmm(q * t["eG"], s0) + _bmm(t["attn"], u)
        state[...] = t["cd"] * s0 + _bmm(k * t["ek"], u, "tn")

    blk = lambda kind: pl.BlockSpec((H, CHUNK, HD), lambda n: (kind, n, 0))
    return pl.pallas_call(
        body, name=name, grid=(N,),
        out_shape=(jax.ShapeDtypeStruct((H, S, HD), F32), jax.ShapeDtypeStruct((N, H, HD, HD), F32)),
        in_specs=[blk(0), blk(1), blk(2), pl.BlockSpec((CHUNK, LANE), lambda n: (n, 0))],
        out_specs=(pl.BlockSpec((H, CHUNK, HD), lambda n: (0, n, 0)),
                   pl.BlockSpec((1, H, HD, HD), lambda n: (n, 0, 0, 0))),
        scratch_shapes=[pltpu.VMEM((H, HD, HD), F32)], compiler_params=_params(),
    )(qkv_hm, qkv_hm, qkv_hm, bg)


def _gdn_bwd(qkv_hm, bg, states, do_hm, H, name):
    S = qkv_hm.shape[1]
    N = S // CHUNK

    def body(q_ref, k_ref, v_ref, bg_ref, st_ref, do_ref, dqkv_ref, dbg_ref, dstate):
        n = pl.program_id(0)

        @pl.when(n == 0)
        def _():
            dstate[...] = jnp.zeros_like(dstate)

        q, k, v, do = q_ref[...], k_ref[...], v_ref[...], do_ref[...]
        t = _gdn_chunk(q, k, v, bg_ref[...], H)
        eye, beta, eG, decay, kk, ainv = t["eye"], t["beta"], t["eG"], t["decay"], t["kk"], t["ainv"]
        s0 = st_ref[0]
        ds1 = dstate[...]
        u = t["u_v"] - _bmm(t["w_k"], s0)
        qdec, kdec = q * eG, k * t["ek"]
        d_qdec = _bmm(do, s0, "nt")
        d_attn = _bmm(do, u, "nt")
        du = _bmm(t["attn"], do, "tn") + _bmm(kdec, ds1)
        d_cd = jnp.sum(jnp.sum(ds1 * s0, axis=2, keepdims=True), axis=1, keepdims=True)
        d_kdec = _bmm(u, ds1, "nt")
        d_wk = -_bmm(du, s0, "nt")
        dstate[...] = t["cd"] * ds1 + _bmm(qdec, do, "tn") - _bmm(t["w_k"], du, "tn")
        d_rv = _bmm(ainv, du, "tn", exact=True)
        d_rk = _bmm(ainv, d_wk, "tn", exact=True)
        dL = jnp.where(t["strict"], -(_bmm(d_rv, t["u_v"], "nt") + _bmm(d_rk, t["w_k"], "nt")), 0.0)
        rk_k = jnp.sum(d_rk * k, axis=2, keepdims=True)
        d_beta = (jnp.sum(dL * decay * kk, axis=2, keepdims=True) + jnp.sum(d_rv * v, axis=2, keepdims=True)
                  + rk_k * eG)
        d_decay = dL * beta * kk + d_attn * t["qk"]
        d_kk = dL * beta * decay
        d_qk = d_attn * decay
        dqkv_ref[pl.ds(2 * H, H)] = beta * d_rv
        dqkv_ref[pl.ds(0, H)] = _bmm(d_qk, k) + d_qdec * eG
        dqkv_ref[pl.ds(H, H)] = ((beta * eG) * d_rk + _bmm(d_kk, k) + _bmm(d_kk, k, "tn") + _bmm(d_qk, q, "tn")
                       + d_kdec * t["ek"])
        d_eG = rk_k * beta + jnp.sum(d_qdec * q, axis=2, keepdims=True)
        e = jnp.sum(d_kdec * kdec, axis=2, keepdims=True)
        T = d_decay * decay
        dG = d_eG * eG - e + jnp.sum(T, axis=2, keepdims=True) - _to_col(jnp.sum(T, axis=1, keepdims=True), eye)
        dGL = jnp.sum(e, axis=1, keepdims=True) + d_cd * t["cd"]
        row1 = lax.broadcasted_iota(jnp.int32, (H, CHUNK, 1), 1)
        dG = dG + jnp.where(row1 == CHUNK - 1, dGL, 0.0)
        lane = lax.broadcasted_iota(jnp.int32, (CHUNK, LANE), 1)
        out = jnp.zeros((CHUNK, LANE), F32)
        for h in range(H):
            out = out + jnp.where(lane == h, d_beta[h], 0.0) + jnp.where(lane == H + h, dG[h], 0.0)
        dbg_ref[...] = out

    blk = lambda kind: pl.BlockSpec((H, CHUNK, HD), lambda n: (kind, N - 1 - n, 0))
    return pl.pallas_call(
        body, name=name, grid=(N,),
        out_shape=(jax.ShapeDtypeStruct((3 * H, S, HD), F32), jax.ShapeDtypeStruct((S, LANE), F32)),
        in_specs=[blk(0), blk(1), blk(2), pl.BlockSpec((CHUNK, LANE), lambda n: (N - 1 - n, 0)),
                  pl.BlockSpec((1, H, HD, HD), lambda n: (N - 1 - n, 0, 0, 0)), blk(0)],
        out_specs=(pl.BlockSpec((3 * H, CHUNK, HD), lambda n: (0, N - 1 - n, 0)),
                   pl.BlockSpec((CHUNK, LANE), lambda n: (N - 1 - n, 0))),
        scratch_shapes=[pltpu.VMEM((H, HD, HD), F32)], compiler_params=_params(),
    )(qkv_hm, qkv_hm, qkv_hm, bg, states, do_hm)


def _gdn_out(o_hm, proj, z_col, gdn_w, name):
    H, S, _ = o_hm.shape
    vw = H * HD
    ts = _tile(S, 256, 8)

    def body(o_ref, z_ref, w_ref, y_ref):
        z = z_ref[...].astype(F32)
        w = w_ref[...]
        parts = []
        for h in range(H):
            o = o_ref[h]
            parts.append(o * _rstd(o) * w)
        y_ref[...] = (jnp.concatenate(parts, axis=1) * _silu(z)).astype(BF16)

    return pl.pallas_call(
        body, name=name, out_shape=jax.ShapeDtypeStruct((S, vw), BF16), grid=(S // ts,),
        in_specs=[pl.BlockSpec((H, ts, HD), lambda i: (0, i, 0)), pl.BlockSpec((ts, vw), lambda i: (i, z_col)),
                  pl.BlockSpec((1, HD), lambda i: (0, 0))],
        out_specs=pl.BlockSpec((ts, vw), lambda i: (i, 0)), compiler_params=_params(),
    )(o_hm, proj, gdn_w)


def _gdn_out_bwd(o_hm, proj, z_col, gdn_w, dy, name):
    H, S, _ = o_hm.shape
    vw = H * HD
    ts = _tile(S, 256, 8)
    n = S // ts

    def body(o_ref, z_ref, w_ref, dy_ref, do_ref, dz_ref, dw_ref, acc):
        i = pl.program_id(0)
        z = z_ref[...].astype(F32)
        dy = dy_ref[...].astype(F32)
        w = w_ref[...]
        gz = dy * _silu(z)
        normed, dwr = [], jnp.zeros((ts, HD), F32)
        for h in range(H):
            o = o_ref[h]
            dxo, dwh = _rms_bwd(o, w, gz[:, HD * h:HD * (h + 1)])
            do_ref[h] = dxo
            dwr = dwr + dwh
            normed.append(o * _rstd(o) * w)
        dz_ref[...] = (dy * jnp.concatenate(normed, axis=1) * _dsilu(z)).astype(BF16)

        @pl.when(i == 0)
        def _():
            acc[...] = jnp.zeros_like(acc)

        acc[...] += _colsum8(dwr)

        @pl.when(i == n - 1)
        def _():
            dw_ref[...] = jnp.sum(acc[...], axis=0, keepdims=True)

    return pl.pallas_call(
        body, name=name, grid=(n,),
        out_shape=(jax.ShapeDtypeStruct((H, S, HD), F32), jax.ShapeDtypeStruct((S, vw), BF16),
                   jax.ShapeDtypeStruct((1, HD), F32)),
        in_specs=[pl.BlockSpec((H, ts, HD), lambda i: (0, i, 0)), pl.BlockSpec((ts, vw), lambda i: (i, z_col)),
                  pl.BlockSpec((1, HD), lambda i: (0, 0)), pl.BlockSpec((ts, vw), lambda i: (i, 0))],
        out_specs=(pl.BlockSpec((H, ts, HD), lambda i: (0, i, 0)), pl.BlockSpec((ts, vw), lambda i: (i, 0)),
                   pl.BlockSpec((1, HD), lambda i: (0, 0))),
        scratch_shapes=[pltpu.VMEM((8, HD), F32)], compiler_params=_params(),
    )(o_hm, proj, gdn_w, dy)


def _pool_trailing(ext, ts, pg, row0):
    outs, inv_cnts = [], []
    t_abs = row0 + lax.broadcasted_iota(jnp.int32, (ts, 1), 0)
    for gi, win in enumerate(POOL_WINDOWS):
        cols = pl.ds(gi * pg, pg)
        cur = ext[pl.ds(HALO, ts), cols]
        acc = cur
        for j in range(1, win):
            acc = acc + ext[pl.ds(HALO - j, ts), cols]
        inv = 1.0 / jnp.minimum(t_abs + 1, win).astype(F32)
        outs.append(acc * inv - cur)
    return outs


def _pool_fwd(proj, p_col, pool_w, pool_scale, pw, name):
    S = proj.shape[0]
    pg = pw // len(POOL_WINDOWS)
    ts = _tile(S, 512, HALO)

    def body(x_ref, h_ref, w_ref, sc_ref, o_ref, ext):
        i = pl.program_id(0)
        _fill_causal(ext, x_ref[...].astype(F32), h_ref[...].astype(F32), i)
        ys = _pool_trailing(ext, ts, pg, i * ts)
        outs = [jnp.dot(ys[gi].astype(BF16), w_ref[gi], preferred_element_type=F32) for gi in range(len(ys))]
        o_ref[...] = (jnp.concatenate(outs, axis=1) * sc_ref[...]).astype(BF16)

    return pl.pallas_call(
        body, name=name, out_shape=jax.ShapeDtypeStruct((S, pw), BF16), grid=(S // ts,),
        in_specs=[pl.BlockSpec((ts, pw), lambda i: (i, p_col)),
                  pl.BlockSpec((HALO, pw), lambda i: (jnp.maximum(i * (ts // HALO) - 1, 0), p_col)),
                  pl.BlockSpec((len(POOL_WINDOWS), pg, pg), lambda i: (0, 0, 0)), pl.BlockSpec((1, pw), lambda i: (0, 0))],
        out_specs=pl.BlockSpec((ts, pw), lambda i: (i, 0)),
        scratch_shapes=[pltpu.VMEM((ts + HALO, pw), F32)], compiler_params=_params(),
    )(proj, proj, pool_w, pool_scale)


def _pool_bwd(proj, p_col, pool_w, pool_scale, dpb, pw, name):
    S = proj.shape[0]
    G = len(POOL_WINDOWS)
    pg = pw // G
    ts = _tile(S, 512, HALO)
    n = S // ts

    def body(x_ref, h_ref, w_ref, sc_ref, d_ref, dn_ref, dp_ref, dw_ref, dsc_ref, ext, zext, wacc, sacc):
        i = pl.program_id(0)
        _fill_causal(ext, x_ref[...].astype(F32), h_ref[...].astype(F32), i)
        ys = _pool_trailing(ext, ts, pg, i * ts)
        d_ext = jnp.concatenate([d_ref[...].astype(F32), jnp.where(i < n - 1, dn_ref[...].astype(F32), 0.0)], axis=0)
        dt = d_ext * sc_ref[...]
        t_abs = i * ts + lax.broadcasted_iota(jnp.int32, (ts + HALO, 1), 0)

        @pl.when(i == 0)
        def _():
            wacc[...] = jnp.zeros_like(wacc)
            sacc[...] = jnp.zeros_like(sacc)

        dps, tfs = [], []
        for gi, win in enumerate(POOL_WINDOWS):
            cols = slice(gi * pg, (gi + 1) * pg)
            w = w_ref[gi]
            dt_g = dt[:, cols].astype(BF16)
            y_g = ys[gi].astype(BF16)
            tfs.append(jnp.dot(y_g, w, preferred_element_type=F32))
            wacc[gi] += lax.dot_general(y_g, dt_g[:ts], (((0,), (0,)), ((), ())), preferred_element_type=F32)
            dyp = lax.dot_general(dt_g, w, (((1,), (1,)), ((), ())), preferred_element_type=F32)
            zext[:, pl.ds(gi * pg, pg)] = dyp * (1.0 / jnp.minimum(t_abs + 1, win).astype(F32))
            acc = -dyp[:ts]
            for j in range(win):
                acc = acc + zext[pl.ds(j, ts), pl.ds(gi * pg, pg)]
            dps.append(acc)
        dp_ref[...] = jnp.concatenate(dps, axis=1).astype(BF16)
        sacc[...] += _colsum8(d_ext[:ts] * jnp.concatenate(tfs, axis=1))

        @pl.when(i == n - 1)
        def _():
            dw_ref[...] = wacc[...]
            dsc_ref[...] = jnp.sum(sacc[...], axis=0, keepdims=True)

    return pl.pallas_call(
        body, name=name, grid=(n,),
        out_shape=(jax.ShapeDtypeStruct((S, pw), BF16), jax.ShapeDtypeStruct((G, pg, pg), F32),
                   jax.ShapeDtypeStruct((1, pw), F32)),
        in_specs=[pl.BlockSpec((ts, pw), lambda i: (i, p_col)),
                  pl.BlockSpec((HALO, pw), lambda i: (jnp.maximum(i * (ts // HALO) - 1, 0), p_col)),
                  pl.BlockSpec((G, pg, pg), lambda i: (0, 0, 0)), pl.BlockSpec((1, pw), lambda i: (0, 0)),
                  pl.BlockSpec((ts, pw), lambda i: (i, 0)),
                  pl.BlockSpec((HALO, pw), lambda i: (jnp.minimum((i + 1) * (ts // HALO), S // HALO - 1), 0))],
        out_specs=(pl.BlockSpec((ts, pw), lambda i: (i, 0)), pl.BlockSpec((G, pg, pg), lambda i: (0, 0, 0)),
                   pl.BlockSpec((1, pw), lambda i: (0, 0))),
        scratch_shapes=[pltpu.VMEM((ts + HALO, pw), F32), pltpu.VMEM((ts + HALO, pw), F32),
                        pltpu.VMEM((G, pg, pg), F32), pltpu.VMEM((8, pw), F32)],
        compiler_params=_params(),
    )(proj, proj, pool_w, pool_scale, dpb, dpb)


def _merge(proj, ga_col, gb_col, ya, yb, name):
    S, d = ya.shape
    ts = _tile(S, 512, 16)

    def body(ga_ref, gb_ref, ya_ref, yb_ref, o_ref):
        o_ref[...] = (_sigmoid(ga_ref[...].astype(F32)) * ya_ref[...].astype(F32)
                      + _sigmoid(gb_ref[...].astype(F32)) * yb_ref[...].astype(F32)).astype(BF16)

    row = pl.BlockSpec((ts, d), lambda i: (i, 0))
    return pl.pallas_call(
        body, name=name, out_shape=jax.ShapeDtypeStruct((S, d), BF16), grid=(S // ts,),
        in_specs=[pl.BlockSpec((ts, d), lambda i: (i, ga_col)), pl.BlockSpec((ts, d), lambda i: (i, gb_col)), row, row],
        out_specs=row, compiler_params=_params(),
    )(proj, proj, ya, yb)


def _merge_bwd(proj, ga_col, gb_col, ya, yb, dm, name):
    S, d = ya.shape
    ts = _tile(S, 512, 16)

    def body(ga_ref, gb_ref, ya_ref, yb_ref, dm_ref, dya_ref, dyb_ref, dga_ref, dgb_ref):
        dmv = dm_ref[...].astype(F32)
        sa, sb = _sigmoid(ga_ref[...].astype(F32)), _sigmoid(gb_ref[...].astype(F32))
        dya_ref[...] = (dmv * sa).astype(BF16)
        dyb_ref[...] = (dmv * sb).astype(BF16)
        dga_ref[...] = (dmv * ya_ref[...].astype(F32) * sa * (1.0 - sa)).astype(BF16)
        dgb_ref[...] = (dmv * yb_ref[...].astype(F32) * sb * (1.0 - sb)).astype(BF16)

    row = pl.BlockSpec((ts, d), lambda i: (i, 0))
    o = jax.ShapeDtypeStruct((S, d), BF16)
    return pl.pallas_call(
        body, name=name, out_shape=(o, o, o, o), grid=(S // ts,),
        in_specs=[pl.BlockSpec((ts, d), lambda i: (i, ga_col)), pl.BlockSpec((ts, d), lambda i: (i, gb_col)), row, row, row],
        out_specs=(row, row, row, row), compiler_params=_params(),
    )(proj, proj, ya, yb, dm)


def _xattn_fwd(q, kv, name):
    S, d = q.shape
    M = kv.shape[0]
    hd = d // XA_HEADS
    ts = _tile(S, 512, 16)
    scale = hd ** -0.5

    def body(q_ref, k_ref, v_ref, o_ref):
        s = lax.dot_general(q_ref[...], k_ref[...], (((1,), (1,)), ((), ())), preferred_element_type=F32) * scale
        p = jnp.exp(s - jnp.max(s, axis=-1, keepdims=True))
        p = p / jnp.sum(p, axis=-1, keepdims=True)
        o_ref[...] = jnp.dot(p.astype(BF16), v_ref[...], preferred_element_type=F32).astype(BF16)

    return pl.pallas_call(
        body, name=name, out_shape=jax.ShapeDtypeStruct((S, d), BF16), grid=(S // ts, XA_HEADS),
        in_specs=[pl.BlockSpec((ts, hd), lambda i, h: (i, h)), pl.BlockSpec((M, hd), lambda i, h: (0, h)),
                  pl.BlockSpec((M, hd), lambda i, h: (0, XA_HEADS + h))],
        out_specs=pl.BlockSpec((ts, hd), lambda i, h: (i, h)), compiler_params=_params(),
    )(q, kv, kv)


def _xattn_bwd(q, kv, do, name):
    S, d = q.shape
    M = kv.shape[0]
    hd = d // XA_HEADS
    ts = _tile(S, 512, 16)
    n = S // ts
    scale = hd ** -0.5

    def body(q_ref, k_ref, v_ref, do_ref, dq_ref, dk_ref, dv_ref, kacc, vacc):
        i = pl.program_id(1)
        qv, kv_, vv, dov = q_ref[...], k_ref[...], v_ref[...], do_ref[...]
        s = lax.dot_general(qv, kv_, (((1,), (1,)), ((), ())), preferred_element_type=F32) * scale
        p = jnp.exp(s - jnp.max(s, axis=-1, keepdims=True))
        p = p / jnp.sum(p, axis=-1, keepdims=True)
        dp = lax.dot_general(dov, vv, (((1,), (1,)), ((), ())), preferred_element_type=F32)
        ds = (p * (dp - jnp.sum(p * dp, axis=-1, keepdims=True)) * scale).astype(BF16)
        dq_ref[...] = jnp.dot(ds, kv_, preferred_element_type=F32).astype(BF16)

        @pl.when(i == 0)
        def _():
            kacc[...] = jnp.zeros_like(kacc)
            vacc[...] = jnp.zeros_like(vacc)

        kacc[...] += lax.dot_general(ds, qv, (((0,), (0,)), ((), ())), preferred_element_type=F32)
        vacc[...] += lax.dot_general(p.astype(BF16), dov, (((0,), (0,)), ((), ())), preferred_element_type=F32)

        @pl.when(i == n - 1)
        def _():
            dk_ref[...] = kacc[...]
            dv_ref[...] = vacc[...]

    dq, dk, dv = pl.pallas_call(
        body, name=name, grid=(XA_HEADS, n),
        out_shape=(jax.ShapeDtypeStruct((S, d), BF16), jax.ShapeDtypeStruct((M, d), F32), jax.ShapeDtypeStruct((M, d), F32)),
        in_specs=[pl.BlockSpec((ts, hd), lambda h, i: (i, h)), pl.BlockSpec((M, hd), lambda h, i: (0, h)),
                  pl.BlockSpec((M, hd), lambda h, i: (0, XA_HEADS + h)), pl.BlockSpec((ts, hd), lambda h, i: (i, h))],
        out_specs=(pl.BlockSpec((ts, hd), lambda h, i: (i, h)), pl.BlockSpec((M, hd), lambda h, i: (0, h)),
                   pl.BlockSpec((M, hd), lambda h, i: (0, h))),
        scratch_shapes=[pltpu.VMEM((M, hd), F32), pltpu.VMEM((M, hd), F32)], compiler_params=_params(),
    )(q, kv, kv, do)
    return dq, jnp.concatenate([dk, dv], axis=1)


def _ffn_u(ext, wv, bias, ts):
    u = wv[FFN_CONV - 1:FFN_CONV, :] * _delayed(ext, 0, ts) + bias
    for t in range(FFN_CONV - 1):
        u = u + wv[t:t + 1, :] * _delayed(ext, FFN_CONV - 1 - t, ts)
    return u


def _ffn_act(up, conv_w, bias, name):
    S, F2 = up.shape
    F = F2 // 2
    ts, cw = _tile(S, 512, HALO), _tile(F, 512)
    nb = F // cw

    def body(a_ref, ah_ref, b_ref, bh_ref, wa_ref, wb_ref, ba_ref, bb_ref, o_ref, ea, eb):
        i = pl.program_id(0)
        _fill_causal(ea, a_ref[...].astype(F32), ah_ref[...].astype(F32), i)
        _fill_causal(eb, b_ref[...].astype(F32), bh_ref[...].astype(F32), i)
        ua = _ffn_u(ea, wa_ref[...], ba_ref[...], ts)
        ub = _ffn_u(eb, wb_ref[...], bb_ref[...], ts)
        o_ref[...] = (_silu(ua) * ub).astype(BF16)

    tile = lambda c0: pl.BlockSpec((ts, cw), lambda i, j: (i, j + c0))
    vec = lambda rows, c0: pl.BlockSpec((rows, cw), lambda i, j: (0, j + c0))
    return pl.pallas_call(
        body, name=name, out_shape=jax.ShapeDtypeStruct((S, F), BF16), grid=(S // ts, nb),
        in_specs=[tile(0), _prev_halo_spec(ts, cw), tile(nb), _prev_halo_spec(ts, cw, nb),
                  vec(FFN_CONV, 0), vec(FFN_CONV, nb), vec(1, 0), vec(1, nb)],
        out_specs=pl.BlockSpec((ts, cw), lambda i, j: (i, j)),
        scratch_shapes=[pltpu.VMEM((ts + HALO, cw), F32), pltpu.VMEM((ts + HALO, cw), F32)],
        compiler_params=_params(),
    )(up, up, up, up, conv_w, conv_w, bias, bias)


def _ffn_act_bwd(up, conv_w, bias, dact, name):
    S, F2 = up.shape
    F = F2 // 2
    ts, cw = _tile(S, 512, HALO), _tile(F, 512)
    nb = F // cw
    n = S // ts

    def body(a_ref, ah_ref, b_ref, bh_ref, wa_ref, wb_ref, ba_ref, bb_ref, d_ref,
             dua_ref, dub_ref, dwa_ref, dwb_ref, dba_ref, dbb_ref, ea, eb, wacc, bacc):
        i = pl.program_id(1)
        _fill_causal(ea, a_ref[...].astype(F32), ah_ref[...].astype(F32), i)
        _fill_causal(eb, b_ref[...].astype(F32), bh_ref[...].astype(F32), i)
        ua = _ffn_u(ea, wa_ref[...], ba_ref[...], ts)
        ub = _ffn_u(eb, wb_ref[...], bb_ref[...], ts)
        d = d_ref[...].astype(F32)
        dua = d * ub * _dsilu(ua)
        dub = d * _silu(ua)
        dua_ref[...] = dua.astype(BF16)
        dub_ref[...] = dub.astype(BF16)

        @pl.when(i == 0)
        def _():
            wacc[...] = jnp.zeros_like(wacc)
            bacc[...] = jnp.zeros_like(bacc)

        for t in range(FFN_CONV):
            wacc[0, t] += _colsum8(dua * _delayed(ea, FFN_CONV - 1 - t, ts))
            wacc[1, t] += _colsum8(dub * _delayed(eb, FFN_CONV - 1 - t, ts))
        bacc[0] += _colsum8(dua)
        bacc[1] += _colsum8(dub)

        @pl.when(i == n - 1)
        def _():
            dwa_ref[...] = jnp.sum(wacc[0], axis=1)
            dwb_ref[...] = jnp.sum(wacc[1], axis=1)
            dba_ref[...] = jnp.sum(bacc[0], axis=0, keepdims=True)
            dbb_ref[...] = jnp.sum(bacc[1], axis=0, keepdims=True)

    tile = lambda c0: pl.BlockSpec((ts, cw), lambda j, i: (i, j + c0))
    halo = lambda c0: pl.BlockSpec((HALO, cw), lambda j, i: (jnp.maximum(i * (ts // HALO) - 1, 0), j + c0))
    vec = lambda rows, c0: pl.BlockSpec((rows, cw), lambda j, i: (0, j + c0))
    dua, dub, dwa, dwb, dba, dbb = pl.pallas_call(
        body, name=name, grid=(nb, n),
        out_shape=(jax.ShapeDtypeStruct((S, F), BF16), jax.ShapeDtypeStruct((S, F), BF16),
                   jax.ShapeDtypeStruct((FFN_CONV, F), F32), jax.ShapeDtypeStruct((FFN_CONV, F), F32),
                   jax.ShapeDtypeStruct((1, F), F32), jax.ShapeDtypeStruct((1, F), F32)),
        in_specs=[tile(0), halo(0), tile(nb), halo(nb), vec(FFN_CONV, 0), vec(FFN_CONV, nb), vec(1, 0), vec(1, nb), tile(0)],
        out_specs=(tile(0), tile(0), vec(FFN_CONV, 0), vec(FFN_CONV, 0), vec(1, 0), vec(1, 0)),
        scratch_shapes=[pltpu.VMEM((ts + HALO, cw), F32), pltpu.VMEM((ts + HALO, cw), F32),
                        pltpu.VMEM((2, FFN_CONV, 8, cw), F32), pltpu.VMEM((2, 8, cw), F32)],
        compiler_params=_params(),
    )(up, up, up, up, conv_w, conv_w, bias, bias, dact)
    return (jnp.concatenate([dua, dub], axis=1), jnp.concatenate([dwa, dwb], axis=1),
            jnp.concatenate([dba, dbb], axis=1))


def _adamw(gparts, w, m, v, name):
    R, C = w.shape
    tr = _tile(R, max(16, (256 * 1024) // C), 16)

    def body(g_ref, w_ref, m_ref, v_ref, go_ref, d_ref, mo_ref, vo_ref):
        g = g_ref[0].astype(F32)
        for s in range(1, N_DEV):
            g = g + g_ref[s].astype(F32)
        mn = ADAM_B1 * m_ref[...] + (1.0 - ADAM_B1) * g
        vn = ADAM_B2 * v_ref[...] + (1.0 - ADAM_B2) * (g * g)
        m_hat = mn / (1.0 - ADAM_B1 ** ADAM_STEP)
        v_hat = vn / (1.0 - ADAM_B2 ** ADAM_STEP)
        go_ref[...] = g
        d_ref[...] = -ADAM_LR * (m_hat / (jnp.sqrt(v_hat) + ADAM_EPS) + ADAM_WD * w_ref[...])
        mo_ref[...] = mn
        vo_ref[...] = vn

    row = pl.BlockSpec((tr, C), lambda i: (i, 0))
    o = jax.ShapeDtypeStruct((R, C), F32)
    return pl.pallas_call(
        body, name=name, out_shape=(o, o, o, o), grid=(R // tr,),
        in_specs=[pl.BlockSpec((N_DEV, tr, C), lambda i: (0, i, 0)), row, row, row],
        out_specs=(row, row, row, row), compiler_params=_params(),
    )(gparts, w, m, v)


def _position():
    return lax.axis_index("x"), lax.axis_index("y"), lax.axis_index("c")


def _gather_many(blocks, name):
    n = len(blocks)

    def body(*refs):
        x_refs, out_refs = refs[:n], refs[n:2 * n]
        send_sems, recv_sems, local_sems = refs[2 * n:]
        x, y, c = _position()
        me, sibling = (x, y, c), (x, y, 1 - c)
        chips = [(1 - x, y), (x, 1 - y), (1 - x, 1 - y)]

        def copy(a, k, blk, to, src=None):
            slot = out_refs[a].at[4 * blk[0] + 2 * blk[1] + blk[2]]
            return pltpu.make_async_remote_copy(
                src_ref=slot if src is None else src, dst_ref=slot,
                send_sem=send_sems.at[7 * a + k], recv_sem=recv_sems.at[7 * a + k], device_id=to, device_id_type=MESH)

        mine = [pltpu.make_async_copy(x_refs[a], out_refs[a].at[4 * x + 2 * y + c], local_sems.at[a]) for a in range(n)]
        for cp in mine:
            cp.start()
        first = []
        for a in range(n):
            first.append(copy(a, 0, me, sibling, src=x_refs[a]))
            first += [copy(a, 1 + j, me, (*chip, c), src=x_refs[a]) for j, chip in enumerate(chips)]
        for cp in first:
            cp.start()
        passed = []
        for j, chip in enumerate(chips):
            for a in range(n):
                copy(a, 1 + j, (*chip, c), me).wait_recv()
                fwd = copy(a, 4 + j, (*chip, c), sibling)
                fwd.start()
                passed.append(fwd)
        for a in range(n):
            copy(a, 0, sibling, me).wait_recv()
        for j, chip in enumerate(chips):
            for a in range(n):
                copy(a, 4 + j, (*chip, 1 - c), me).wait_recv()
        for cp in first + passed:
            cp.wait_send()
        for cp in mine:
            cp.wait()

    return pl.pallas_call(
        body, name=name, out_shape=[jax.ShapeDtypeStruct((N_DEV,) + b.shape, b.dtype) for b in blocks],
        in_specs=[ANY] * n, out_specs=[ANY] * n,
        scratch_shapes=[pltpu.SemaphoreType.DMA((7 * n,)), pltpu.SemaphoreType.DMA((7 * n,)), pltpu.SemaphoreType.DMA((n,))],
    )(*blocks)


def _exchange_many(parts, name):
    n = len(parts)

    def body(*refs):
        p_refs, out_refs = refs[:n], refs[n:2 * n]
        send_sems, recv_sems, local_sems = refs[2 * n:]
        x, y, c = _position()
        my_slot = 4 * x + 2 * y + c
        mine = [pltpu.make_async_copy(p_refs[a].at[my_slot], out_refs[a].at[my_slot], local_sems.at[a]) for a in range(n)]
        for cp in mine:
            cp.start()
        copies = []
        for k in range(1, N_DEV):
            px, py, pc = x ^ (k >> 2), y ^ ((k >> 1) & 1), c ^ (k & 1)
            for a in range(n):
                copies.append(pltpu.make_async_remote_copy(
                    src_ref=p_refs[a].at[4 * px + 2 * py + pc], dst_ref=out_refs[a].at[my_slot],
                    send_sem=send_sems.at[7 * a + k - 1], recv_sem=recv_sems.at[7 * a + k - 1],
                    device_id=(px, py, pc), device_id_type=MESH))
        for cp in copies:
            cp.start()
        for cp in copies:
            cp.wait()
        for cp in mine:
            cp.wait()

    return pl.pallas_call(
        body, name=name, out_shape=[jax.ShapeDtypeStruct(p.shape, p.dtype) for p in parts],
        in_specs=[ANY] * n, out_specs=[ANY] * n,
        scratch_shapes=[pltpu.SemaphoreType.DMA((7 * n,)), pltpu.SemaphoreType.DMA((7 * n,)), pltpu.SemaphoreType.DMA((n,))],
    )(*parts)


def _col_pieces(col_map, shard_w):
    pieces = []
    for lo, hi, dst in col_map:
        c = lo
        while c < hi:
            j = c // shard_w
            end = min(hi, (j + 1) * shard_w)
            pieces.append((j, c - j * shard_w, end - c, dst + (c - lo)))
            c = end
    return pieces


def _assemble_cols(shards, pieces, width, name):
    _, R, Cs = shards.shape
    tr = _tile(R, 128, 16)

    def body(s_ref, o_ref):
        o_ref[...] = jnp.zeros(o_ref.shape, o_ref.dtype)
        for j, lo, n, dst in pieces:
            o_ref[:, dst:dst + n] = s_ref[j, :, lo:lo + n]

    return pl.pallas_call(
        body, name=name, out_shape=jax.ShapeDtypeStruct((R, width), shards.dtype), grid=(R // tr,),
        in_specs=[pl.BlockSpec((N_DEV, tr, Cs), lambda i: (0, i, 0))],
        out_specs=pl.BlockSpec((tr, width), lambda i: (i, 0)), compiler_params=_params(),
    )(shards)


def _split_cols(full, pieces, shard_w, name):
    R, width = full.shape
    tr = _tile(R, 128, 16)

    def body(f_ref, o_ref):
        for j, lo, n, dst in pieces:
            o_ref[j, :, lo:lo + n] = f_ref[:, dst:dst + n]

    return pl.pallas_call(
        body, name=name, out_shape=jax.ShapeDtypeStruct((N_DEV, R, shard_w), full.dtype), grid=(R // tr,),
        in_specs=[pl.BlockSpec((tr, width), lambda i: (i, 0))],
        out_specs=pl.BlockSpec((N_DEV, tr, shard_w), lambda i: (0, i, 0)), compiler_params=_params(),
    )(full)


def _pack(arrays, row_multiple=8):
    flat, layout, off = [], [], 0
    for a in arrays:
        n = a.size
        padded = -(-n // LANE) * LANE
        f = a.reshape(-1).astype(F32)
        if padded != n:
            f = jnp.pad(f, (0, padded - n))
        flat.append(f)
        layout.append((off, n, a.shape))
        off += padded
    total = -(-off // (LANE * row_multiple)) * (LANE * row_multiple)
    if total != off:
        flat.append(jnp.zeros((total - off,), F32))
    return jnp.concatenate(flat).reshape(total // LANE, LANE), layout


def _unpack(buf, layout):
    flat = buf.reshape(-1)
    return [flat[off:off + n].reshape(shape) for off, n, shape in layout]


def _cols_to_full(g):
    return jnp.transpose(g, (1, 0, 2)).reshape(g.shape[1], N_DEV * g.shape[2])


def _full_to_cols(a):
    return jnp.transpose(a.reshape(a.shape[0], N_DEV, a.shape[1] // N_DEV), (1, 0, 2))


def _rows_to_full(g):
    return g.reshape(N_DEV * g.shape[1], g.shape[2])


def _full_to_rows(a):
    return a.reshape(N_DEV, a.shape[0] // N_DEV, a.shape[1])


def _pad_cols(a, width):
    return a if a.shape[-1] == width else jnp.pad(a, [(0, 0)] * (a.ndim - 1) + [(0, width - a.shape[-1])])


SHARDED = ("w_in", "conv_qkv", "pool_w", "w_branch_a", "w_branch_b", "w_mix_out", "w_xq", "w_xkv", "w_xo", "w_up",
           "ffn_conv_w", "w_down")
REPLICATED = ("mix_pre_norm", "a_log", "dt_bias", "gdn_norm", "pool_scale", "mix_post_norm", "xa_pre_norm", "mem_norm",
              "xa_post_norm", "ffn_pre_norm", "ffn_conv_b", "ffn_post_norm")
WEIGHTS = ("mix_pre_norm", "w_in", "conv_qkv", "a_log", "dt_bias", "gdn_norm", "pool_w", "pool_scale", "w_branch_a",
           "w_branch_b", "w_mix_out", "mix_post_norm", "xa_pre_norm", "mem_norm", "w_xq", "w_xkv", "w_xo", "xa_post_norm",
           "ffn_pre_norm", "w_up", "ffn_conv_w", "ffn_conv_b", "w_down", "ffn_post_norm")
MATMUL_WEIGHTS = ("w_in", "w_branch_a", "w_branch_b", "w_mix_out", "w_xq", "w_xkv", "w_xo", "w_up", "w_down")
COL_SHARDED = ("w_in", "w_branch_b", "w_xkv", "w_up", "conv_qkv", "ffn_conv_w")
ROW_SHARDED = ("w_branch_a", "w_mix_out", "w_xq", "w_xo", "w_down")


class _Layout:
    def __init__(self, D, H, pw, F):
        self.D, self.H, self.pw, self.F = D, H, pw, F
        self.qkv_w, self.vw = 3 * H * HD, H * HD
        self.ba_w = 512 if D >= 2048 else LANE
        self.Fp = -(-F // 512) * 512 if F >= 512 else F
        q, vw = self.qkv_w, self.vw
        self.seg = dict(qkv=(0, q), z=(q, vw), ga=(q + vw, D), gb=(q + vw + D, D), p=(q + vw + 2 * D, pw),
                        ba=(q + vw + 2 * D + pw, self.ba_w))
        self.in_w = q + vw + 2 * D + pw + self.ba_w
        o_z, o_b = q, q + vw
        o_p = o_b + 2 * H
        o_ga = o_p + pw
        o_gb = o_ga + D
        self.d_in = o_gb + D
        self.in_map = [(0, o_z, self.seg["qkv"][0]), (o_z, o_b, self.seg["z"][0]), (o_b, o_p, self.seg["ba"][0]),
                       (o_p, o_ga, self.seg["p"][0]), (o_ga, o_gb, self.seg["ga"][0]), (o_gb, self.d_in, self.seg["gb"][0])]
        self.up_map = [(0, F, 0), (F, 2 * F, self.Fp)]

    def col(self, name, width):
        return self.seg[name][0] // width


def _local_step(x, mem, target, P, L):
    D, H, pw, F, Fp = L.D, L.H, L.pw, L.F, L.Fp
    qkv_w, vw, ba_w = L.qkv_w, L.vw, L.ba_w
    col = L.col
    win_p, wup_p, wdown_p, cw3_p, fb_p = P["win_p"], P["wup_p"], P["wdown_p"], P["cw3_p"], P["fb_p"]
    conv_qkv, pool_w = P["conv_qkv"], P["pool_w"]
    lanes = lambda vec: jnp.pad(vec.reshape(1, H).astype(F32), ((0, 0), (H, LANE - 2 * H)))
    a_log_l, dt_bias_l = lanes(P["a_log"]), lanes(P["dt_bias"])
    bf = lambda name: P[name]
    vecf = lambda name: P[name].reshape(1, -1).astype(F32)

    h1 = _prenorm(x, vecf("mix_pre_norm"), "mix_prenorm")
    proj = _matmul(h1, win_p, "nn", BF16, "in_proj", tn=768)
    qkv_hm = _qkv_conv(proj, conv_qkv, qkv_w, "qkv_conv")
    bg = _gates(proj, col("ba", LANE), a_log_l, dt_bias_l, H, "gates")
    o_hm, states = _gdn_fwd(qkv_hm, bg, H, "gdn_fwd")
    oa = _gdn_out(o_hm, proj, col("z", vw), vecf("gdn_norm"), "gdn_out")
    ya = _matmul(oa, bf("w_branch_a"), "nn", BF16, "branch_a")
    pb = _pool_fwd(proj, col("p", pw), pool_w, vecf("pool_scale"), pw, "pool_fwd")
    yb = _matmul(pb, bf("w_branch_b"), "nn", BF16, "branch_b")
    merged = _merge(proj, col("ga", D), col("gb", D), ya, yb, "merge")
    y1 = _matmul(merged, bf("w_mix_out"), "nn", F32, "mix_out")
    x1, h2 = _post_pre(x, y1, vecf("mix_post_norm"), vecf("xa_pre_norm"), "mix_post")
    mn = _prenorm(mem, vecf("mem_norm"), "mem_norm")
    qx = _matmul(h2, bf("w_xq"), "nn", BF16, "xq")
    kv = _matmul(mn, bf("w_xkv"), "nn", BF16, "xkv")
    ox = _xattn_fwd(qx, kv, "xattn_fwd")
    y2 = _matmul(ox, bf("w_xo"), "nn", F32, "xo")
    x2, h3 = _post_pre(x1, y2, vecf("xa_post_norm"), vecf("ffn_pre_norm"), "xa_post")
    up = _matmul(h3, wup_p, "nn", BF16, "ffn_up")
    act = _ffn_act(up, cw3_p, fb_p, "ffn_act")
    y3 = _matmul(act, wdown_p, "nn", F32, "ffn_down")
    dx3, loss = _post_loss(x2, y3, vecf("ffn_post_norm"), target, "ffn_post_loss")

    g = {}
    dy3, g["ffn_post_norm"] = _post_bwd(y3, vecf("ffn_post_norm"), dx3, "ffn_post_bwd")
    dact = _matmul(dy3, wdown_p, "nt", BF16, "ffn_down_dx")
    g["w_down_p"] = _matmul(act, dy3, "tn", BF16, "ffn_down_dw")
    du, g["ffn_conv_w_p"], g["ffn_conv_b_p"] = _ffn_act_bwd(up, cw3_p, fb_p, dact, "ffn_act_bwd")
    dup = _conv_t(du, cw3_p, "ffn_conv_t")
    dh3 = _matmul(dup, wup_p, "nt", F32, "ffn_up_dx")
    g["w_up_p"] = _matmul(h3, dup, "tn", BF16, "ffn_up_dw")
    dx2, g["ffn_pre_norm"] = _pre_bwd(x2, vecf("ffn_pre_norm"), dh3, dx3, "ffn_pre_bwd")
    dy2, g["xa_post_norm"] = _post_bwd(y2, vecf("xa_post_norm"), dx2, "xa_post_bwd")
    dox = _matmul(dy2, bf("w_xo"), "nt", BF16, "xo_dx")
    g["w_xo"] = _matmul(ox, dy2, "tn", BF16, "xo_dw")
    dqx, dkv = _xattn_bwd(qx, kv, dox, "xattn_bwd")
    dkv_b = dkv.astype(BF16)
    dh2 = _matmul(dqx, bf("w_xq"), "nt", F32, "xq_dx")
    g["w_xq"] = _matmul(h2, dqx, "tn", BF16, "xq_dw")
    dmn = _matmul(dkv_b, bf("w_xkv"), "nt", F32, "xkv_dx")
    g["w_xkv"] = _matmul(mn, dkv_b, "tn", BF16, "xkv_dw")
    _, g["mem_norm"] = _pre_bwd(mem, vecf("mem_norm"), dmn, jnp.zeros_like(mem), "mem_norm_bwd")
    dx1, g["xa_pre_norm"] = _pre_bwd(x1, vecf("xa_pre_norm"), dh2, dx2, "xa_pre_bwd")
    dy1, g["mix_post_norm"] = _post_bwd(y1, vecf("mix_post_norm"), dx1, "mix_post_bwd")
    dmerged = _matmul(dy1, bf("w_mix_out"), "nt", BF16, "mix_out_dx")
    g["w_mix_out"] = _matmul(merged, dy1, "tn", BF16, "mix_out_dw")
    dya, dyb, dga, dgb = _merge_bwd(proj, col("ga", D), col("gb", D), ya, yb, dmerged, "merge_bwd")
    doa = _matmul(dya, bf("w_branch_a"), "nt", BF16, "branch_a_dx")
    g["w_branch_a"] = _matmul(oa, dya, "tn", BF16, "branch_a_dw")
    dpb = _matmul(dyb, bf("w_branch_b"), "nt", BF16, "branch_b_dx")
    g["w_branch_b"] = _matmul(pb, dyb, "tn", BF16, "branch_b_dw")
    dp, g["pool_w"], g["pool_scale"] = _pool_bwd(proj, col("p", pw), pool_w, vecf("pool_scale"), dpb, pw, "pool_bwd")
    do_hm, dz, g["gdn_norm"] = _gdn_out_bwd(o_hm, proj, col("z", vw), vecf("gdn_norm"), doa, "gdn_out_bwd")
    dqkv_hm, dbg = _gdn_bwd(qkv_hm, bg, states, do_hm, H, "gdn_bwd")
    dba, dal, ddt = _gates_bwd(proj, col("ba", LANE), ba_w, a_log_l, dt_bias_l, dbg, H, "gates_bwd")
    g["a_log"], g["dt_bias"] = dal[:, H:2 * H], ddt[:, H:2 * H]
    dc, g["conv_qkv"] = _qkv_conv_bwd(proj, conv_qkv, dqkv_hm, qkv_w, "qkv_conv_bwd")
    dqkv = _conv_t(dc, conv_qkv, "qkv_conv_t")
    dproj = jnp.concatenate([dqkv, dz, dga, dgb, dp, dba], axis=1)
    dh1 = _matmul(dproj, win_p, "nt", F32, "in_proj_dx")
    g["w_in_p"] = _matmul(h1, dproj, "tn", BF16, "in_proj_dw", tn=768)
    grad_x, g["mix_pre_norm"] = _pre_bwd(x, vecf("mix_pre_norm"), dh1, dx1, "mix_pre_bwd")
    return loss, grad_x, g


def _two_halves(a, F, Fp):
    return jnp.concatenate([_pad_cols(a[..., :F], Fp), _pad_cols(a[..., F:], Fp)], axis=-1)


def _from_halves(a, F, Fp):
    return jnp.concatenate([a[..., :F], a[..., Fp:Fp + F]], axis=-1)


def kernel(x, mem, mix_pre_norm, w_in, conv_qkv, a_log, dt_bias, gdn_norm, pool_w, pool_scale, w_branch_a, w_branch_b, w_mix_out, mix_post_norm, xa_pre_norm, mem_norm, w_xq, w_xkv, w_xo, xa_post_norm, ffn_pre_norm, w_up, ffn_conv_w, ffn_conv_b, w_down, ffn_post_norm, loss_target, m_mix_pre_norm, m_w_in, m_conv_qkv, m_a_log, m_dt_bias, m_gdn_norm, m_pool_w, m_pool_scale, m_w_branch_a, m_w_branch_b, m_w_mix_out, m_mix_post_norm, m_xa_pre_norm, m_mem_norm, m_w_xq, m_w_xkv, m_w_xo, m_xa_post_norm, m_ffn_pre_norm, m_w_up, m_ffn_conv_w, m_ffn_conv_b, m_w_down, m_ffn_post_norm, v_mix_pre_norm, v_w_in, v_conv_qkv, v_a_log, v_dt_bias, v_gdn_norm, v_pool_w, v_pool_scale, v_w_branch_a, v_w_branch_b, v_w_mix_out, v_mix_post_norm, v_xa_pre_norm, v_mem_norm, v_w_xq, v_w_xkv, v_w_xo, v_xa_post_norm, v_ffn_pre_norm, v_w_up, v_ffn_conv_w, v_ffn_conv_b, v_w_down, v_ffn_post_norm):
    given = dict(locals())
    w = {n: given[n][0] for n in WEIGHTS}
    m = {n: given["m_" + n][0] for n in WEIGHTS}
    v = {n: given["v_" + n][0] for n in WEIGHTS}
    D = x.shape[-1]
    F = w["w_down"].shape[0] * N_DEV
    L = _Layout(D, w["a_log"].shape[-1], w["pool_scale"].shape[-1], F)
    Fp = L.Fp
    in_pieces = _col_pieces(L.in_map, w["w_in"].shape[1])
    up_pieces = _col_pieces(L.up_map, w["w_up"].shape[1])

    shards = [w[n].astype(BF16) if n in MATMUL_WEIGHTS else w[n] for n in SHARDED]
    G = dict(zip(SHARDED, _gather_many(shards, "gather_weights")))
    P = {n: w[n] for n in REPLICATED}
    P["win_p"] = _assemble_cols(G["w_in"], in_pieces, L.in_w, "assemble_w_in")
    P["wup_p"] = _assemble_cols(G["w_up"], up_pieces, 2 * Fp, "assemble_w_up")
    P["wdown_p"] = jnp.pad(_rows_to_full(G["w_down"]), ((0, Fp - F), (0, 0)))
    for n in ("w_branch_a", "w_mix_out", "w_xq", "w_xo"):
        P[n] = _rows_to_full(G[n])
    for n in ("w_branch_b", "w_xkv", "conv_qkv"):
        P[n] = _cols_to_full(G[n])
    P["cw3_p"] = _two_halves(_cols_to_full(G["ffn_conv_w"]), F, Fp)
    P["fb_p"] = _two_halves(w["ffn_conv_b"].reshape(1, 2 * F), F, Fp)
    P["pool_w"] = jnp.transpose(G["pool_w"], (1, 0, 2, 3)).reshape(
        pool_w.shape[1], N_DEV * pool_w.shape[2], pool_w.shape[3]).astype(BF16)

    loss, grad_x, g = _local_step(x[0], mem[0], loss_target[0], P, L)

    parts = {"w_in": _split_cols(g["w_in_p"], in_pieces, w["w_in"].shape[1], "split_w_in"),
             "w_up": _split_cols(g["w_up_p"], up_pieces, w["w_up"].shape[1], "split_w_up"),
             "w_down": _full_to_rows(g["w_down_p"][:F]),
             "ffn_conv_w": _full_to_cols(_from_halves(g["ffn_conv_w_p"], F, Fp)),
             "pool_w": jnp.transpose(g["pool_w"].reshape(pool_w.shape[1], N_DEV, pool_w.shape[2], pool_w.shape[3]),
                                     (1, 0, 2, 3)).reshape(N_DEV, pool_w.shape[1] * pool_w.shape[2], pool_w.shape[3])}
    for n in ("w_branch_a", "w_mix_out", "w_xq", "w_xo"):
        parts[n] = _full_to_rows(g[n])
    for n in ("w_branch_b", "w_xkv", "conv_qkv"):
        parts[n] = _full_to_cols(g[n])
    received = dict(zip(SHARDED, _exchange_many([parts[n] for n in SHARDED], "exchange_grads")))
    outs = {}
    for n in SHARDED:
        as2d = lambda a: a.reshape(-1, a.shape[-1])
        res = _adamw(received[n], as2d(w[n]), as2d(m[n]), as2d(v[n]), "adamw_" + n)
        outs[n] = [r.reshape(w[n].shape) for r in res]

    g["ffn_conv_b"] = _from_halves(g["ffn_conv_b_p"], F, Fp)
    rep_parts, rep_layout = _pack([g[n].reshape(w[n].shape) for n in REPLICATED] + [loss])
    rep_all, = _gather_many([rep_parts], "gather_small_grads")
    zero_loss = jnp.zeros_like(loss)
    wr, _ = _pack([w[n] for n in REPLICATED] + [zero_loss])
    mr, _ = _pack([m[n] for n in REPLICATED] + [zero_loss])
    vr, _ = _pack([v[n] for n in REPLICATED] + [zero_loss])
    outs_rep = [_unpack(o, rep_layout) for o in _adamw(rep_all, wr, mr, vr, "adamw_replicated")]
    loss_total = outs_rep[0][-1][0, 0]
    for i, n in enumerate(REPLICATED):
        outs[n] = [outs_rep[k][i] for k in range(4)]

    result = [loss_total, grad_x[None]]
    for k in range(4):
        for n in WEIGHTS:
            result.append(outs[n][k][None])
    return tuple(result)
```

```python
import functools

import jax
import jax.numpy as jnp
from jax import lax
from jax.experimental import pallas as pl
from jax.experimental.pallas import tpu as pltpu

F32, BF16 = jnp.float32, jnp.bfloat16
MESH = pl.DeviceIdType.MESH
ANY = pl.BlockSpec(memory_space=pl.ANY)

N_DEV = 8
EPS = 1e-6
CHUNK = 64
HD = 128
GDN_CONV = 4
FFN_CONV = 3
POOL_WINDOWS = (2, 4, 8, 16)
XA_HEADS = 4
HALO = 16
LANE = 128
VMEM_LIMIT = 48 * 1024 * 1024

ADAM_LR, ADAM_B1, ADAM_B2, ADAM_EPS, ADAM_WD, ADAM_STEP = 0.001, 0.9, 0.999, 1e-08, 0.01, 10


def _tile(n, pref, align=LANE):
    best = None
    t = align
    while t <= min(n, pref):
        if n % t == 0:
            best = t
        t += align
    return best if best is not None else n


def _params(**kw):
    return pltpu.CompilerParams(vmem_limit_bytes=VMEM_LIMIT, **kw)


def _sigmoid(x):
    return 1.0 / (1.0 + jnp.exp(-x))


def _silu(x):
    return x * _sigmoid(x)


def _dsilu(x):
    s = _sigmoid(x)
    return s * (1.0 + x * (1.0 - s))


def _colsum8(t):
    return t.reshape(t.shape[0] // 8, 8, t.shape[1]).sum(axis=0)


def _ride(body, n_in, n_out, rider, first, middle, last):
    if rider is None:
        return body
    n = rider.n

    def wrapped(*refs):
        ins, r_in = refs[:n_in], refs[n_in:n_in + n]
        outs, r_out = refs[n_in + n:n_in + n + n_out], refs[n_in + n + n_out:n_in + 2 * n + n_out]
        rest = refs[n_in + 2 * n + n_out:]
        start, hand_on, finish = rider.phases(r_in, r_out, *rest[-3:])
        pl.when(first())(start)
        body(*ins, *outs, *rest[:-3])
        if hand_on is not None:
            pl.when(middle())(hand_on)
        pl.when(last())(finish)

    return wrapped


def _matmul(a, b, mode, out_dtype, name, tm=1024, tn=1024, tk=2816, rider=None):
    if mode == "nn":
        (M, K), (K2, N) = a.shape, b.shape
    elif mode == "nt":
        (M, K), (N, K2) = a.shape, b.shape
    else:
        (K, M), (K2, N) = a.shape, b.shape
    assert K == K2, (name, a.shape, b.shape)
    tm, tn = _tile(M, tm), _tile(N, tn)
    tk = K if K <= tk else _tile(K, tk)
    nk = K // tk
    if mode == "nn":
        a_spec = pl.BlockSpec((tm, tk), lambda i, j, k: (i, k))
        b_spec = pl.BlockSpec((tk, tn), lambda i, j, k: (k, j))
        dims = (((1,), (0,)), ((), ()))
    elif mode == "nt":
        a_spec = pl.BlockSpec((tm, tk), lambda i, j, k: (i, k))
        b_spec = pl.BlockSpec((tn, tk), lambda i, j, k: (j, k))
        dims = (((1,), (1,)), ((), ()))
    else:
        a_spec = pl.BlockSpec((tk, tm), lambda i, j, k: (k, i))
        b_spec = pl.BlockSpec((tk, tn), lambda i, j, k: (k, j))
        dims = (((0,), (0,)), ((), ()))

    def body(a_ref, b_ref, o_ref, acc):
        part = lax.dot_general(a_ref[...], b_ref[...], dims, preferred_element_type=F32)
        if nk == 1:
            o_ref[...] = part.astype(o_ref.dtype)
        else:
            k = pl.program_id(2)

            @pl.when(k == 0)
            def _():
                acc[...] = part

            @pl.when(k > 0)
            def _():
                acc[...] += part

            @pl.when(k == nk - 1)
            def _():
                o_ref[...] = acc[...].astype(o_ref.dtype)

    grid = (M // tm, N // tn, nk)
    at = lambda step: lambda: ((pl.program_id(0) == step[0]) & (pl.program_id(1) == step[1])
                               & (pl.program_id(2) == step[2]))
    extra = rider.n if rider is not None else 0
    res = pl.pallas_call(
        _ride(body, 2, 1, rider, at((0, 0, 0)), at((grid[0] // 2, 0, 0)), at((grid[0] - 1, grid[1] - 1, nk - 1))),
        name=name, out_shape=[jax.ShapeDtypeStruct((M, N), out_dtype)] + (rider.out_shape if rider else []),
        grid=grid, in_specs=[a_spec, b_spec] + [ANY] * extra,
        out_specs=[pl.BlockSpec((tm, tn), lambda i, j, k: (i, j))] + [ANY] * extra,
        scratch_shapes=[pltpu.VMEM((tm, tn) if nk > 1 else (8, LANE), F32)] + (rider.scratch if rider else []),
        compiler_params=_params(dimension_semantics=("arbitrary",) * 3 if rider else ("parallel", "parallel", "arbitrary")),
    )(a, b, *(rider.arrays if rider else []))
    return (res[0], res[1:]) if rider else res[0]


def _rstd(xf):
    return lax.rsqrt(jnp.mean(xf * xf, axis=-1, keepdims=True) + EPS)


def _rms_bwd(xf, w, dy):
    r = _rstd(xf)
    g = dy * w
    dx = r * g - xf * (r * r * r) * jnp.mean(g * xf, axis=-1, keepdims=True)
    return dx, dy * xf * r


def _row_tile(rows):
    return _tile(rows, 256, 8)


def _prenorm(x, w, name):
    rows, d = x.shape
    ts = _row_tile(rows)

    def body(x_ref, w_ref, h_ref):
        xf = x_ref[...]
        h_ref[...] = (xf * _rstd(xf) * w_ref[...]).astype(BF16)

    return pl.pallas_call(
        body, name=name, out_shape=jax.ShapeDtypeStruct((rows, d), BF16), grid=(rows // ts,),
        in_specs=[pl.BlockSpec((ts, d), lambda i: (i, 0)), pl.BlockSpec((1, d), lambda i: (0, 0))],
        out_specs=pl.BlockSpec((ts, d), lambda i: (i, 0)), compiler_params=_params(),
    )(x, w)


def _post_pre(xres, y, w_post, w_pre, name):
    rows, d = xres.shape
    ts = _row_tile(rows)

    def body(x_ref, y_ref, wp_ref, wn_ref, xo_ref, h_ref):
        yf = y_ref[...]
        xn = x_ref[...] + yf * _rstd(yf) * wp_ref[...]
        xo_ref[...] = xn
        h_ref[...] = (xn * _rstd(xn) * wn_ref[...]).astype(BF16)

    row = pl.BlockSpec((ts, d), lambda i: (i, 0))
    vec = pl.BlockSpec((1, d), lambda i: (0, 0))
    return pl.pallas_call(
        body, name=name, grid=(rows // ts,),
        out_shape=(jax.ShapeDtypeStruct((rows, d), F32), jax.ShapeDtypeStruct((rows, d), BF16)),
        in_specs=[row, row, vec, vec], out_specs=(row, row), compiler_params=_params(),
    )(xres, y, w_post, w_pre)


def _post_loss(xres, y, w_post, target, name):
    rows, d = xres.shape
    ts = _row_tile(rows)
    n = rows // ts

    def body(x_ref, y_ref, wp_ref, t_ref, dx_ref, loss_ref, acc):
        i = pl.program_id(0)
        yf = y_ref[...]
        diff = x_ref[...] + yf * _rstd(yf) * wp_ref[...] - t_ref[...]
        dx_ref[...] = diff * (1.0 / d)

        @pl.when(i == 0)
        def _():
            acc[...] = jnp.zeros_like(acc)

        acc[...] += _colsum8(diff * diff)

        @pl.when(i == n - 1)
        def _():
            loss_ref[...] = jnp.broadcast_to((0.5 / d) * jnp.sum(acc[...]), loss_ref.shape)

    row = pl.BlockSpec((ts, d), lambda i: (i, 0))
    vec = pl.BlockSpec((1, d), lambda i: (0, 0))
    return pl.pallas_call(
        body, name=name, grid=(n,),
        out_shape=(jax.ShapeDtypeStruct((rows, d), F32), jax.ShapeDtypeStruct((1, LANE), F32)),
        in_specs=[row, row, vec, row], out_specs=(row, pl.BlockSpec((1, LANE), lambda i: (0, 0))),
        scratch_shapes=[pltpu.VMEM((8, d), F32)], compiler_params=_params(),
    )(xres, y, w_post, target)


def _post_bwd(y, w_post, dxn, name):
    rows, d = y.shape
    ts = _row_tile(rows)
    n = rows // ts

    def body(y_ref, w_ref, d_ref, dy_ref, dw_ref, acc):
        i = pl.program_id(0)
        dy, dwr = _rms_bwd(y_ref[...], w_ref[...], d_ref[...])
        dy_ref[...] = dy.astype(BF16)

        @pl.when(i == 0)
        def _():
            acc[...] = jnp.zeros_like(acc)

        acc[...] += _colsum8(dwr)

        @pl.when(i == n - 1)
        def _():
            dw_ref[...] = jnp.sum(acc[...], axis=0, keepdims=True)

    row = pl.BlockSpec((ts, d), lambda i: (i, 0))
    vec = pl.BlockSpec((1, d), lambda i: (0, 0))
    return pl.pallas_call(
        body, name=name, grid=(n,),
        out_shape=(jax.ShapeDtypeStruct((rows, d), BF16), jax.ShapeDtypeStruct((1, d), F32)),
        in_specs=[row, vec, row], out_specs=(row, vec),
        scratch_shapes=[pltpu.VMEM((8, d), F32)], compiler_params=_params(),
    )(y, w_post, dxn)


def _pre_bwd(x, w_pre, dh, dres, name):
    rows, d = x.shape
    ts = _row_tile(rows)
    n = rows // ts

    def body(x_ref, w_ref, dh_ref, dr_ref, dx_ref, dw_ref, acc):
        i = pl.program_id(0)
        dx, dwr = _rms_bwd(x_ref[...], w_ref[...], dh_ref[...].astype(F32))
        dx_ref[...] = dr_ref[...] + dx

        @pl.when(i == 0)
        def _():
            acc[...] = jnp.zeros_like(acc)

        acc[...] += _colsum8(dwr)

        @pl.when(i == n - 1)
        def _():
            dw_ref[...] = jnp.sum(acc[...], axis=0, keepdims=True)

    row = pl.BlockSpec((ts, d), lambda i: (i, 0))
    vec = pl.BlockSpec((1, d), lambda i: (0, 0))
    return pl.pallas_call(
        body, name=name, grid=(n,),
        out_shape=(jax.ShapeDtypeStruct((rows, d), F32), jax.ShapeDtypeStruct((1, d), F32)),
        in_specs=[row, vec, row, row], out_specs=(row, vec),
        scratch_shapes=[pltpu.VMEM((8, d), F32)], compiler_params=_params(),
    )(x, w_pre, dh, dres)


def _prev_halo_spec(ts, cw, col0=0):
    return pl.BlockSpec((HALO, cw), lambda i, j: (jnp.maximum(i * (ts // HALO) - 1, 0), j + col0))


def _fill_causal(ext, tile_f32, halo_f32, i):
    ext[pl.ds(0, HALO), :] = jnp.where(i > 0, halo_f32, 0.0)
    ext[pl.ds(HALO, tile_f32.shape[0]), :] = tile_f32


def _delayed(ext, j, ts):
    return ext[pl.ds(HALO - j, ts), :]


def _conv_t(dc, w, name, col_tile=512):
    S, C = dc.shape
    K = w.shape[0]
    ts, cw = _tile(S, 512, HALO), _tile(C, col_tile)
    n = S // ts

    def body(d_ref, nx_ref, w_ref, o_ref, ext):
        i = pl.program_id(0)
        ext[pl.ds(0, ts), :] = d_ref[...].astype(F32)
        ext[pl.ds(ts, HALO), :] = jnp.where(i < n - 1, nx_ref[...].astype(F32), 0.0)
        wv = w_ref[...]
        acc = wv[K - 1:K, :] * ext[pl.ds(0, ts), :]
        for j in range(K - 1):
            acc = acc + wv[j:j + 1, :] * ext[pl.ds(K - 1 - j, ts), :]
        o_ref[...] = acc.astype(o_ref.dtype)

    return pl.pallas_call(
        body, name=name, out_shape=jax.ShapeDtypeStruct((S, C), BF16), grid=(n, C // cw),
        in_specs=[pl.BlockSpec((ts, cw), lambda i, j: (i, j)),
                  pl.BlockSpec((HALO, cw), lambda i, j: (jnp.minimum((i + 1) * (ts // HALO), S // HALO - 1), j)),
                  pl.BlockSpec((K, cw), lambda i, j: (0, j))],
        out_specs=pl.BlockSpec((ts, cw), lambda i, j: (i, j)),
        scratch_shapes=[pltpu.VMEM((ts + HALO, cw), F32)], compiler_params=_params(),
    )(dc, dc, w)


def _qkv_conv(proj, conv_w, qkv_w, name):
    S = proj.shape[0]
    H3 = qkv_w // HD
    H = H3 // 3
    hb = 4 if H % 4 == 0 else 1
    cw = hb * HD
    ts = _tile(S, 512, HALO)
    per_kind = H // hb

    def body(x_ref, h_ref, w_ref, o_ref, ext):
        i, j = pl.program_id(0), pl.program_id(1)
        _fill_causal(ext, x_ref[...].astype(F32), h_ref[...].astype(F32), i)
        wv = w_ref[...]
        c = wv[GDN_CONV - 1:GDN_CONV, :] * _delayed(ext, 0, ts)
        for t in range(GDN_CONV - 1):
            c = c + wv[t:t + 1, :] * _delayed(ext, GDN_CONV - 1 - t, ts)
        s = _silu(c)
        kind = j // per_kind
        scale = jnp.where(kind == 0, HD ** -0.5, 1.0)
        for a in range(hb):
            sa = s[:, HD * a:HD * (a + 1)]
            r = lax.rsqrt(jnp.sum(sa * sa, axis=-1, keepdims=True) + EPS)
            o_ref[a] = jnp.where(kind == 2, sa, sa * r * scale)

    return pl.pallas_call(
        body, name=name, out_shape=jax.ShapeDtypeStruct((H3, S, HD), F32), grid=(S // ts, qkv_w // cw),
        in_specs=[pl.BlockSpec((ts, cw), lambda i, j: (i, j)), _prev_halo_spec(ts, cw),
                  pl.BlockSpec((GDN_CONV, cw), lambda i, j: (0, j))],
        out_specs=pl.BlockSpec((hb, ts, HD), lambda i, j: (j, i, 0)),
        scratch_shapes=[pltpu.VMEM((ts + HALO, cw), F32)], compiler_params=_params(),
    )(proj, proj, conv_w)


def _qkv_conv_bwd(proj, conv_w, dqkv_hm, qkv_w, name):
    S = proj.shape[0]
    H = qkv_w // HD // 3
    hb = 4 if H % 4 == 0 else 1
    cw = hb * HD
    ts = _tile(S, 512, HALO)
    n = S // ts
    per_kind = H // hb

    def body(x_ref, h_ref, w_ref, d_ref, dc_ref, dw_ref, ext, acc):
        j, i = pl.program_id(0), pl.program_id(1)
        _fill_causal(ext, x_ref[...].astype(F32), h_ref[...].astype(F32), i)
        wv = w_ref[...]
        c = wv[GDN_CONV - 1:GDN_CONV, :] * _delayed(ext, 0, ts)
        for t in range(GDN_CONV - 1):
            c = c + wv[t:t + 1, :] * _delayed(ext, GDN_CONV - 1 - t, ts)
        s = _silu(c)
        kind = j // per_kind
        scale = jnp.where(kind == 0, HD ** -0.5, 1.0)
        parts = []
        for a in range(hb):
            sa = s[:, HD * a:HD * (a + 1)]
            dy = d_ref[a]
            r = lax.rsqrt(jnp.sum(sa * sa, axis=-1, keepdims=True) + EPS)
            dn = scale * (r * dy - sa * (r * r * r) * jnp.sum(dy * sa, axis=-1, keepdims=True))
            parts.append(jnp.where(kind == 2, dy, dn))
        dc = jnp.concatenate(parts, axis=1) * _dsilu(c)
        dc_ref[...] = dc.astype(BF16)

        @pl.when(i == 0)
        def _():
            acc[...] = jnp.zeros_like(acc)

        for t in range(GDN_CONV):
            acc[t] += _colsum8(dc * _delayed(ext, GDN_CONV - 1 - t, ts))

        @pl.when(i == n - 1)
        def _():
            dw_ref[...] = jnp.sum(acc[...], axis=1)

    return pl.pallas_call(
        body, name=name, grid=(qkv_w // cw, n),
        out_shape=(jax.ShapeDtypeStruct((S, qkv_w), BF16), jax.ShapeDtypeStruct((GDN_CONV, qkv_w), F32)),
        in_specs=[pl.BlockSpec((ts, cw), lambda j, i: (i, j)),
                  pl.BlockSpec((HALO, cw), lambda j, i: (jnp.maximum(i * (ts // HALO) - 1, 0), j)),
                  pl.BlockSpec((GDN_CONV, cw), lambda j, i: (0, j)),
                  pl.BlockSpec((hb, ts, HD), lambda j, i: (j, i, 0))],
        out_specs=(pl.BlockSpec((ts, cw), lambda j, i: (i, j)), pl.BlockSpec((GDN_CONV, cw), lambda j, i: (0, j))),
        scratch_shapes=[pltpu.VMEM((ts + HALO, cw), F32), pltpu.VMEM((GDN_CONV, 8, cw), F32)],
        compiler_params=_params(),
    )(proj, proj, conv_w, dqkv_hm)


def _chunk_cumsum(x):
    row = lax.broadcasted_iota(jnp.int32, x.shape, 0) & (CHUNK - 1)
    s = 1
    while s < CHUNK:
        x = x + jnp.where(row >= s, pltpu.roll(x, s, axis=0), 0.0)
        s *= 2
    return x


def _chunk_rev_cumsum(x):
    rows = x.shape[0]
    row = lax.broadcasted_iota(jnp.int32, x.shape, 0) & (CHUNK - 1)
    s = 1
    while s < CHUNK:
        x = x + jnp.where(row < CHUNK - s, pltpu.roll(x, rows - s, axis=0), 0.0)
        s *= 2
    return x


def _softplus(x):
    return jnp.maximum(x, 0.0) + jnp.log1p(jnp.exp(-jnp.abs(x)))


def _gates(proj, ba_col, a_log_l, dt_bias_l, H, name):
    S = proj.shape[0]
    ts = _tile(S, 512, CHUNK)

    def body(x_ref, al_ref, dt_ref, o_ref):
        x = x_ref[...].astype(F32)
        lane = lax.broadcasted_iota(jnp.int32, x.shape, 1)
        g = -jnp.exp(al_ref[...]) * _softplus(x + dt_ref[...])
        G = _chunk_cumsum(jnp.where((lane >= H) & (lane < 2 * H), g, 0.0))
        o_ref[...] = jnp.where(lane < H, _sigmoid(x), G)

    return pl.pallas_call(
        body, name=name, out_shape=jax.ShapeDtypeStruct((S, LANE), F32), grid=(S // ts,),
        in_specs=[pl.BlockSpec((ts, LANE), lambda i: (i, ba_col)), pl.BlockSpec((1, LANE), lambda i: (0, 0)),
                  pl.BlockSpec((1, LANE), lambda i: (0, 0))],
        out_specs=pl.BlockSpec((ts, LANE), lambda i: (i, 0)), compiler_params=_params(),
    )(proj, a_log_l, dt_bias_l)


def _gates_bwd(proj, ba_col, ba_w, a_log_l, dt_bias_l, dbg, H, name):
    S = proj.shape[0]
    ts = _tile(S, 512, CHUNK)
    n = S // ts

    def body(x_ref, al_ref, dt_ref, d_ref, o_ref, dal_ref, ddt_ref, acc):
        i = pl.program_id(0)
        x = x_ref[...].astype(F32)
        d = d_ref[...]
        lane = lax.broadcasted_iota(jnp.int32, x.shape, 1)
        is_a = (lane >= H) & (lane < 2 * H)
        beta = _sigmoid(x)
        nea = -jnp.exp(al_ref[...])
        z = x + dt_ref[...]
        dg = _chunk_rev_cumsum(jnp.where(is_a, d, 0.0))
        da_raw = dg * nea * _sigmoid(z)
        o = jnp.where(lane < H, d * beta * (1.0 - beta), jnp.where(is_a, da_raw, 0.0))
        if ba_w > LANE:
            o = jnp.concatenate([o, jnp.zeros((ts, ba_w - LANE), F32)], axis=1)
        o_ref[...] = o.astype(BF16)

        @pl.when(i == 0)
        def _():
            acc[...] = jnp.zeros_like(acc)

        acc[0] += _colsum8(jnp.where(is_a, dg * nea * _softplus(z), 0.0))
        acc[1] += _colsum8(jnp.where(is_a, da_raw, 0.0))

        @pl.when(i == n - 1)
        def _():
            dal_ref[...] = jnp.sum(acc[0], axis=0, keepdims=True)
            ddt_ref[...] = jnp.sum(acc[1], axis=0, keepdims=True)

    vec = pl.BlockSpec((1, LANE), lambda i: (0, 0))
    return pl.pallas_call(
        body, name=name, grid=(n,),
        out_shape=(jax.ShapeDtypeStruct((S, ba_w), BF16), jax.ShapeDtypeStruct((1, LANE), F32),
                   jax.ShapeDtypeStruct((1, LANE), F32)),
        in_specs=[pl.BlockSpec((ts, LANE), lambda i: (i, ba_col)), vec, vec, pl.BlockSpec((ts, LANE), lambda i: (i, 0))],
        out_specs=(pl.BlockSpec((ts, ba_w), lambda i: (i, 0)), vec, vec),
        scratch_shapes=[pltpu.VMEM((2, 8, LANE), F32)], compiler_params=_params(),
    )(proj, a_log_l, dt_bias_l, dbg)


_BMM_FORMS = {"nn": "hik,hkj->hij", "nt": "hik,hjk->hij", "tn": "hki,hkj->hij"}


def _split_bf16(a):
    hi = a.astype(BF16)
    return hi, (a - hi.astype(F32)).astype(BF16)


def _bmm(a, b, form="nn", exact=False):
    e = lambda x, y: jnp.einsum(_BMM_FORMS[form], x, y, preferred_element_type=F32)
    if not exact:
        return e(a.astype(BF16), b.astype(BF16))
    (ah, al), (bh, bl) = _split_bf16(a), _split_bf16(b)
    return (e(ah, bl) + e(al, bh)) + e(ah, bh)


def _unit_lower_inverse(L, r, c):
    eye = (r == c).astype(F32)
    m = jnp.where((r >> 3) == (c >> 3), -L, 0.0)
    m2 = _bmm(m, m, exact=True)
    m4 = _bmm(m2, m2, exact=True)
    x = eye + m
    x = x + _bmm(x, m2, exact=True)
    x = x + _bmm(x, m4, exact=True)
    for sh in (3, 4, 5):
        off = ((r >> (sh + 1)) == (c >> (sh + 1))) & ((r >> sh) != (c >> sh))
        x = x - _bmm(x, _bmm(jnp.where(off, L, 0.0), x, exact=True), exact=True)
    return x


def _to_row(col, eye):
    return jnp.sum(jnp.where(eye, jnp.broadcast_to(col, eye.shape), 0.0), axis=1, keepdims=True)


def _to_col(rowv, eye):
    return jnp.sum(jnp.where(eye, jnp.broadcast_to(rowv, eye.shape), 0.0), axis=2, keepdims=True)


def _gdn_chunk(q, k, v, bg, H):
    shape = (H, CHUNK, CHUNK)
    r = lax.broadcasted_iota(jnp.int32, shape, 1)
    c = lax.broadcasted_iota(jnp.int32, shape, 2)
    eye, incl, strict = r == c, r >= c, r > c
    beta = jnp.stack([bg[:, h:h + 1] for h in range(H)], axis=0)
    G = jnp.stack([bg[:, H + h:H + h + 1] for h in range(H)], axis=0)
    gap = jnp.broadcast_to(G, shape) - _to_row(G, eye)
    decay = jnp.where(incl, jnp.exp(jnp.where(incl, gap, 0.0)), 0.0)
    kk = _bmm(k, k, "nt", exact=True)
    L = jnp.where(strict, beta * decay * kk, 0.0)
    ainv = _unit_lower_inverse(L, r, c)
    eG = jnp.exp(G)
    u_v = _bmm(ainv, beta * v, exact=True)
    w_k = _bmm(ainv, (beta * eG) * k, exact=True)
    qk = _bmm(q, k, "nt", exact=True)
    GL = G[:, CHUNK - 1:CHUNK, :]
    ek = jnp.exp(GL - G)
    return dict(eye=eye, strict=strict, r=r, c=c, beta=beta, G=G, decay=decay, kk=kk, ainv=ainv, eG=eG,
                u_v=u_v, w_k=w_k, qk=qk, attn=decay * qk, GL=GL, ek=ek, cd=jnp.exp(GL))


def _chunk_steps(N):
    at = lambda step: lambda: pl.program_id(0) == step
    return at(0), at(N - max(N // 8, 1)), at(N - 1)


def _gdn_fwd(qkv_hm, bg, H, name, rider=None):
    S = qkv_hm.shape[1]
    N = S // CHUNK
    extra = rider.n if rider is not None else 0

    def body(q_ref, k_ref, v_ref, bg_ref, o_ref, st_ref, state):
        n = pl.program_id(0)

        @pl.when(n == 0)
        def _():
            state[...] = jnp.zeros_like(state)

        q, k, v = q_ref[...], k_ref[...], v_ref[...]
        t = _gdn_chunk(q, k, v, bg_ref[...], H)
        s0 = state[...]
        st_ref[0] = s0
        u = t["u_v"] - _bmm(t["w_k"], s0)
        o_ref[...] = _bmm(q * t["eG"], s0) + _bmm(t["attn"], u)
        state[...] = t["cd"] * s0 + _bmm(k * t["ek"], u, "tn")

    blk = lambda kind: pl.BlockSpec((H, CHUNK, HD), lambda n: (kind, n, 0))
    res = pl.pallas_call(
        _ride(body, 4, 2, rider, *_chunk_steps(N)), name=name, grid=(N,),
        out_shape=[jax.ShapeDtypeStruct((H, S, HD), F32), jax.ShapeDtypeStruct((N, H, HD, HD), F32)]
        + (rider.out_shape if rider else []),
        in_specs=[blk(0), blk(1), blk(2), pl.BlockSpec((CHUNK, LANE), lambda n: (n, 0))] + [ANY] * extra,
        out_specs=[pl.BlockSpec((H, CHUNK, HD), lambda n: (0, n, 0)),
                   pl.BlockSpec((1, H, HD, HD), lambda n: (n, 0, 0, 0))] + [ANY] * extra,
        scratch_shapes=[pltpu.VMEM((H, HD, HD), F32)] + (rider.scratch if rider else []), compiler_params=_params(),
    )(qkv_hm, qkv_hm, qkv_hm, bg, *(rider.arrays if rider else []))
    return res[0], res[1], res[2:]


def _gdn_bwd(qkv_hm, bg, states, do_hm, H, name, rider=None):
    S = qkv_hm.shape[1]
    N = S // CHUNK
    extra = rider.n if rider is not None else 0

    def body(q_ref, k_ref, v_ref, bg_ref, st_ref, do_ref, dqkv_ref, dbg_ref, dstate):
        n = pl.program_id(0)

        @pl.when(n == 0)
        def _():
            dstate[...] = jnp.zeros_like(dstate)

        q, k, v, do = q_ref[...], k_ref[...], v_ref[...], do_ref[...]
        t = _gdn_chunk(q, k, v, bg_ref[...], H)
        eye, beta, eG, decay, kk, ainv = t["eye"], t["beta"], t["eG"], t["decay"], t["kk"], t["ainv"]
        s0 = st_ref[0]
        ds1 = dstate[...]
        u = t["u_v"] - _bmm(t["w_k"], s0)
        qdec, kdec = q * eG, k * t["ek"]
        d_qdec = _bmm(do, s0, "nt")
        d_attn = _bmm(do, u, "nt")
        du = _bmm(t["attn"], do, "tn") + _bmm(kdec, ds1)
        d_cd = jnp.sum(jnp.sum(ds1 * s0, axis=2, keepdims=True), axis=1, keepdims=True)
        d_kdec = _bmm(u, ds1, "nt")
        d_wk = -_bmm(du, s0, "nt")
        dstate[...] = t["cd"] * ds1 + _bmm(qdec, do, "tn") - _bmm(t["w_k"], du, "tn")
        d_rv = _bmm(ainv, du, "tn", exact=True)
        d_rk = _bmm(ainv, d_wk, "tn", exact=True)
        dL = jnp.where(t["strict"], -(_bmm(d_rv, t["u_v"], "nt") + _bmm(d_rk, t["w_k"], "nt")), 0.0)
        rk_k = jnp.sum(d_rk * k, axis=2, keepdims=True)
        d_beta = (jnp.sum(dL * decay * kk, axis=2, keepdims=True) + jnp.sum(d_rv * v, axis=2, keepdims=True)
                  + rk_k * eG)
        d_decay = dL * beta * kk + d_attn * t["qk"]
        d_kk = dL * beta * decay
        d_qk = d_attn * decay
        dqkv_ref[pl.ds(2 * H, H)] = beta * d_rv
        dqkv_ref[pl.ds(0, H)] = _bmm(d_qk, k) + d_qdec * eG
        dqkv_ref[pl.ds(H, H)] = ((beta * eG) * d_rk + _bmm(d_kk, k) + _bmm(d_kk, k, "tn") + _bmm(d_qk, q, "tn")
                       + d_kdec * t["ek"])
        d_eG = rk_k * beta + jnp.sum(d_qdec * q, axis=2, keepdims=True)
        e = jnp.sum(d_kdec * kdec, axis=2, keepdims=True)
        T = d_decay * decay
        dG = d_eG * eG - e + jnp.sum(T, axis=2, keepdims=True) - _to_col(jnp.sum(T, axis=1, keepdims=True), eye)
        dGL = jnp.sum(e, axis=1, keepdims=True) + d_cd * t["cd"]
        row1 = lax.broadcasted_iota(jnp.int32, (H, CHUNK, 1), 1)
        dG = dG + jnp.where(row1 == CHUNK - 1, dGL, 0.0)
        lane = lax.broadcasted_iota(jnp.int32, (CHUNK, LANE), 1)
        out = jnp.zeros((CHUNK, LANE), F32)
        for h in range(H):
            out = out + jnp.where(lane == h, d_beta[h], 0.0) + jnp.where(lane == H + h, dG[h], 0.0)
        dbg_ref[...] = out

    blk = lambda kind: pl.BlockSpec((H, CHUNK, HD), lambda n: (kind, N - 1 - n, 0))
    res = pl.pallas_call(
        _ride(body, 6, 2, rider, *_chunk_steps(N)), name=name, grid=(N,),
        out_shape=[jax.ShapeDtypeStruct((3 * H, S, HD), F32), jax.ShapeDtypeStruct((S, LANE), F32)]
        + (rider.out_shape if rider else []),
        in_specs=[blk(0), blk(1), blk(2), pl.BlockSpec((CHUNK, LANE), lambda n: (N - 1 - n, 0)),
                  pl.BlockSpec((1, H, HD, HD), lambda n: (N - 1 - n, 0, 0, 0)), blk(0)] + [ANY] * extra,
        out_specs=[pl.BlockSpec((3 * H, CHUNK, HD), lambda n: (0, N - 1 - n, 0)),
                   pl.BlockSpec((CHUNK, LANE), lambda n: (N - 1 - n, 0))] + [ANY] * extra,
        scratch_shapes=[pltpu.VMEM((H, HD, HD), F32)] + (rider.scratch if rider else []), compiler_params=_params(),
    )(qkv_hm, qkv_hm, qkv_hm, bg, states, do_hm, *(rider.arrays if rider else []))
    return res[0], res[1], res[2:]


def _gdn_out(o_hm, proj, z_col, gdn_w, name):
    H, S, _ = o_hm.shape
    vw = H * HD
    ts = _tile(S, 256, 8)

    def body(o_ref, z_ref, w_ref, y_ref):
        z = z_ref[...].astype(F32)
        w = w_ref[...]
        parts = []
        for h in range(H):
            o = o_ref[h]
            parts.append(o * _rstd(o) * w)
        y_ref[...] = (jnp.concatenate(parts, axis=1) * _silu(z)).astype(BF16)

    return pl.pallas_call(
        body, name=name, out_shape=jax.ShapeDtypeStruct((S, vw), BF16), grid=(S // ts,),
        in_specs=[pl.BlockSpec((H, ts, HD), lambda i: (0, i, 0)), pl.BlockSpec((ts, vw), lambda i: (i, z_col)),
                  pl.BlockSpec((1, HD), lambda i: (0, 0))],
        out_specs=pl.BlockSpec((ts, vw), lambda i: (i, 0)), compiler_params=_params(),
    )(o_hm, proj, gdn_w)


def _gdn_out_bwd(o_hm, proj, z_col, gdn_w, dy, name):
    H, S, _ = o_hm.shape
    vw = H * HD
    ts = _tile(S, 256, 8)
    n = S // ts

    def body(o_ref, z_ref, w_ref, dy_ref, do_ref, dz_ref, dw_ref, acc):
        i = pl.program_id(0)
        z = z_ref[...].astype(F32)
        dy = dy_ref[...].astype(F32)
        w = w_ref[...]
        gz = dy * _silu(z)
        normed, dwr = [], jnp.zeros((ts, HD), F32)
        for h in range(H):
            o = o_ref[h]
            dxo, dwh = _rms_bwd(o, w, gz[:, HD * h:HD * (h + 1)])
            do_ref[h] = dxo
            dwr = dwr + dwh
            normed.append(o * _rstd(o) * w)
        dz_ref[...] = (dy * jnp.concatenate(normed, axis=1) * _dsilu(z)).astype(BF16)

        @pl.when(i == 0)
        def _():
            acc[...] = jnp.zeros_like(acc)

        acc[...] += _colsum8(dwr)

        @pl.when(i == n - 1)
        def _():
            dw_ref[...] = jnp.sum(acc[...], axis=0, keepdims=True)

    return pl.pallas_call(
        body, name=name, grid=(n,),
        out_shape=(jax.ShapeDtypeStruct((H, S, HD), F32), jax.ShapeDtypeStruct((S, vw), BF16),
                   jax.ShapeDtypeStruct((1, HD), F32)),
        in_specs=[pl.BlockSpec((H, ts, HD), lambda i: (0, i, 0)), pl.BlockSpec((ts, vw), lambda i: (i, z_col)),
                  pl.BlockSpec((1, HD), lambda i: (0, 0)), pl.BlockSpec((ts, vw), lambda i: (i, 0))],
        out_specs=(pl.BlockSpec((H, ts, HD), lambda i: (0, i, 0)), pl.BlockSpec((ts, vw), lambda i: (i, 0)),
                   pl.BlockSpec((1, HD), lambda i: (0, 0))),
        scratch_shapes=[pltpu.VMEM((8, HD), F32)], compiler_params=_params(),
    )(o_hm, proj, gdn_w, dy)


def _pool_trailing(ext, ts, pg, row0):
    outs, inv_cnts = [], []
    t_abs = row0 + lax.broadcasted_iota(jnp.int32, (ts, 1), 0)
    for gi, win in enumerate(POOL_WINDOWS):
        cols = pl.ds(gi * pg, pg)
        cur = ext[pl.ds(HALO, ts), cols]
        acc = cur
        for j in range(1, win):
            acc = acc + ext[pl.ds(HALO - j, ts), cols]
        inv = 1.0 / jnp.minimum(t_abs + 1, win).astype(F32)
        outs.append(acc * inv - cur)
    return outs


def _pool_fwd(proj, p_col, pool_w, pool_scale, pw, name):
    S = proj.shape[0]
    pg = pw // len(POOL_WINDOWS)
    ts = _tile(S, 512, HALO)

    def body(x_ref, h_ref, w_ref, sc_ref, o_ref, ext):
        i = pl.program_id(0)
        _fill_causal(ext, x_ref[...].astype(F32), h_ref[...].astype(F32), i)
        ys = _pool_trailing(ext, ts, pg, i * ts)
        outs = [jnp.dot(ys[gi].astype(BF16), w_ref[gi], preferred_element_type=F32) for gi in range(len(ys))]
        o_ref[...] = (jnp.concatenate(outs, axis=1) * sc_ref[...]).astype(BF16)

    return pl.pallas_call(
        body, name=name, out_shape=jax.ShapeDtypeStruct((S, pw), BF16), grid=(S // ts,),
        in_specs=[pl.BlockSpec((ts, pw), lambda i: (i, p_col)),
                  pl.BlockSpec((HALO, pw), lambda i: (jnp.maximum(i * (ts // HALO) - 1, 0), p_col)),
                  pl.BlockSpec((len(POOL_WINDOWS), pg, pg), lambda i: (0, 0, 0)), pl.BlockSpec((1, pw), lambda i: (0, 0))],
        out_specs=pl.BlockSpec((ts, pw), lambda i: (i, 0)),
        scratch_shapes=[pltpu.VMEM((ts + HALO, pw), F32)], compiler_params=_params(),
    )(proj, proj, pool_w, pool_scale)


def _pool_bwd(proj, p_col, pool_w, pool_scale, dpb, pw, name):
    S = proj.shape[0]
    G = len(POOL_WINDOWS)
    pg = pw // G
    ts = _tile(S, 512, HALO)
    n = S // ts

    def body(x_ref, h_ref, w_ref, sc_ref, d_ref, dn_ref, dp_ref, dw_ref, dsc_ref, ext, zext, wacc, sacc):
        i = pl.program_id(0)
        _fill_causal(ext, x_ref[...].astype(F32), h_ref[...].astype(F32), i)
        ys = _pool_trailing(ext, ts, pg, i * ts)
        d_ext = jnp.concatenate([d_ref[...].astype(F32), jnp.where(i < n - 1, dn_ref[...].astype(F32), 0.0)], axis=0)
        dt = d_ext * sc_ref[...]
        t_abs = i * ts + lax.broadcasted_iota(jnp.int32, (ts + HALO, 1), 0)

        @pl.when(i == 0)
        def _():
            wacc[...] = jnp.zeros_like(wacc)
            sacc[...] = jnp.zeros_like(sacc)

        dps, tfs = [], []
        for gi, win in enumerate(POOL_WINDOWS):
            cols = slice(gi * pg, (gi + 1) * pg)
            w = w_ref[gi]
            dt_g = dt[:, cols].astype(BF16)
            y_g = ys[gi].astype(BF16)
            tfs.append(jnp.dot(y_g, w, preferred_element_type=F32))
            wacc[gi] += lax.dot_general(y_g, dt_g[:ts], (((0,), (0,)), ((), ())), preferred_element_type=F32)
            dyp = lax.dot_general(dt_g, w, (((1,), (1,)), ((), ())), preferred_element_type=F32)
            zext[:, pl.ds(gi * pg, pg)] = dyp * (1.0 / jnp.minimum(t_abs + 1, win).astype(F32))
            acc = -dyp[:ts]
            for j in range(win):
                acc = acc + zext[pl.ds(j, ts), pl.ds(gi * pg, pg)]
            dps.append(acc)
        dp_ref[...] = jnp.concatenate(dps, axis=1).astype(BF16)
        sacc[...] += _colsum8(d_ext[:ts] * jnp.concatenate(tfs, axis=1))

        @pl.when(i == n - 1)
        def _():
            dw_ref[...] = wacc[...]
            dsc_ref[...] = jnp.sum(sacc[...], axis=0, keepdims=True)

    return pl.pallas_call(
        body, name=name, grid=(n,),
        out_shape=(jax.ShapeDtypeStruct((S, pw), BF16), jax.ShapeDtypeStruct((G, pg, pg), F32),
                   jax.ShapeDtypeStruct((1, pw), F32)),
        in_specs=[pl.BlockSpec((ts, pw), lambda i: (i, p_col)),
                  pl.BlockSpec((HALO, pw), lambda i: (jnp.maximum(i * (ts // HALO) - 1, 0), p_col)),
                  pl.BlockSpec((G, pg, pg), lambda i: (0, 0, 0)), pl.BlockSpec((1, pw), lambda i: (0, 0)),
                  pl.BlockSpec((ts, pw), lambda i: (i, 0)),
                  pl.BlockSpec((HALO, pw), lambda i: (jnp.minimum((i + 1) * (ts // HALO), S // HALO - 1), 0))],
        out_specs=(pl.BlockSpec((ts, pw), lambda i: (i, 0)), pl.BlockSpec((G, pg, pg), lambda i: (0, 0, 0)),
                   pl.BlockSpec((1, pw), lambda i: (0, 0))),
        scratch_shapes=[pltpu.VMEM((ts + HALO, pw), F32), pltpu.VMEM((ts + HALO, pw), F32),
                        pltpu.VMEM((G, pg, pg), F32), pltpu.VMEM((8, pw), F32)],
        compiler_params=_params(),
    )(proj, proj, pool_w, pool_scale, dpb, dpb)


def _merge(proj, ga_col, gb_col, ya, yb, name):
    S, d = ya.shape
    ts = _tile(S, 512, 16)

    def body(ga_ref, gb_ref, ya_ref, yb_ref, o_ref):
        o_ref[...] = (_sigmoid(ga_ref[...].astype(F32)) * ya_ref[...].astype(F32)
                      + _sigmoid(gb_ref[...].astype(F32)) * yb_ref[...].astype(F32)).astype(BF16)

    row = pl.BlockSpec((ts, d), lambda i: (i, 0))
    return pl.pallas_call(
        body, name=name, out_shape=jax.ShapeDtypeStruct((S, d), BF16), grid=(S // ts,),
        in_specs=[pl.BlockSpec((ts, d), lambda i: (i, ga_col)), pl.BlockSpec((ts, d), lambda i: (i, gb_col)), row, row],
        out_specs=row, compiler_params=_params(),
    )(proj, proj, ya, yb)


def _merge_bwd(proj, ga_col, gb_col, ya, yb, dm, name):
    S, d = ya.shape
    ts = _tile(S, 512, 16)

    def body(ga_ref, gb_ref, ya_ref, yb_ref, dm_ref, dya_ref, dyb_ref, dga_ref, dgb_ref):
        dmv = dm_ref[...].astype(F32)
        sa, sb = _sigmoid(ga_ref[...].astype(F32)), _sigmoid(gb_ref[...].astype(F32))
        dya_ref[...] = (dmv * sa).astype(BF16)
        dyb_ref[...] = (dmv * sb).astype(BF16)
        dga_ref[...] = (dmv * ya_ref[...].astype(F32) * sa * (1.0 - sa)).astype(BF16)
        dgb_ref[...] = (dmv * yb_ref[...].astype(F32) * sb * (1.0 - sb)).astype(BF16)

    row = pl.BlockSpec((ts, d), lambda i: (i, 0))
    o = jax.ShapeDtypeStruct((S, d), BF16)
    return pl.pallas_call(
        body, name=name, out_shape=(o, o, o, o), grid=(S // ts,),
        in_specs=[pl.BlockSpec((ts, d), lambda i: (i, ga_col)), pl.BlockSpec((ts, d), lambda i: (i, gb_col)), row, row, row],
        out_specs=(row, row, row, row), compiler_params=_params(),
    )(proj, proj, ya, yb, dm)


def _xattn_fwd(q, kv, name):
    S, d = q.shape
    M = kv.shape[0]
    hd = d // XA_HEADS
    ts = _tile(S, 512, 16)
    scale = hd ** -0.5

    def body(q_ref, k_ref, v_ref, o_ref):
        s = lax.dot_general(q_ref[...], k_ref[...], (((1,), (1,)), ((), ())), preferred_element_type=F32) * scale
        p = jnp.exp(s - jnp.max(s, axis=-1, keepdims=True))
        p = p / jnp.sum(p, axis=-1, keepdims=True)
        o_ref[...] = jnp.dot(p.astype(BF16), v_ref[...], preferred_element_type=F32).astype(BF16)

    return pl.pallas_call(
        body, name=name, out_shape=jax.ShapeDtypeStruct((S, d), BF16), grid=(S // ts, XA_HEADS),
        in_specs=[pl.BlockSpec((ts, hd), lambda i, h: (i, h)), pl.BlockSpec((M, hd), lambda i, h: (0, h)),
                  pl.BlockSpec((M, hd), lambda i, h: (0, XA_HEADS + h))],
        out_specs=pl.BlockSpec((ts, hd), lambda i, h: (i, h)), compiler_params=_params(),
    )(q, kv, kv)


def _xattn_bwd(q, kv, do, name):
    S, d = q.shape
    M = kv.shape[0]
    hd = d // XA_HEADS
    ts = _tile(S, 512, 16)
    n = S // ts
    scale = hd ** -0.5

    def body(q_ref, k_ref, v_ref, do_ref, dq_ref, dk_ref, dv_ref, kacc, vacc):
        i = pl.program_id(1)
        qv, kv_, vv, dov = q_ref[...], k_ref[...], v_ref[...], do_ref[...]
        s = lax.dot_general(qv, kv_, (((1,), (1,)), ((), ())), preferred_element_type=F32) * scale
        p = jnp.exp(s - jnp.max(s, axis=-1, keepdims=True))
        p = p / jnp.sum(p, axis=-1, keepdims=True)
        dp = lax.dot_general(dov, vv, (((1,), (1,)), ((), ())), preferred_element_type=F32)
        ds = (p * (dp - jnp.sum(p * dp, axis=-1, keepdims=True)) * scale).astype(BF16)
        dq_ref[...] = jnp.dot(ds, kv_, preferred_element_type=F32).astype(BF16)

        @pl.when(i == 0)
        def _():
            kacc[...] = jnp.zeros_like(kacc)
            vacc[...] = jnp.zeros_like(vacc)

        kacc[...] += lax.dot_general(ds, qv, (((0,), (0,)), ((), ())), preferred_element_type=F32)
        vacc[...] += lax.dot_general(p.astype(BF16), dov, (((0,), (0,)), ((), ())), preferred_element_type=F32)

        @pl.when(i == n - 1)
        def _():
            dk_ref[...] = kacc[...]
            dv_ref[...] = vacc[...]

    dq, dk, dv = pl.pallas_call(
        body, name=name, grid=(XA_HEADS, n),
        out_shape=(jax.ShapeDtypeStruct((S, d), BF16), jax.ShapeDtypeStruct((M, d), F32), jax.ShapeDtypeStruct((M, d), F32)),
        in_specs=[pl.BlockSpec((ts, hd), lambda h, i: (i, h)), pl.BlockSpec((M, hd), lambda h, i: (0, h)),
                  pl.BlockSpec((M, hd), lambda h, i: (0, XA_HEADS + h)), pl.BlockSpec((ts, hd), lambda h, i: (i, h))],
        out_specs=(pl.BlockSpec((ts, hd), lambda h, i: (i, h)), pl.BlockSpec((M, hd), lambda h, i: (0, h)),
                   pl.BlockSpec((M, hd), lambda h, i: (0, h))),
        scratch_shapes=[pltpu.VMEM((M, hd), F32), pltpu.VMEM((M, hd), F32)], compiler_params=_params(),
    )(q, kv, kv, do)
    return dq, jnp.concatenate([dk, dv], axis=1)


def _ffn_u(ext, wv, bias, ts):
    u = wv[FFN_CONV - 1:FFN_CONV, :] * _delayed(ext, 0, ts) + bias
    for t in range(FFN_CONV - 1):
        u = u + wv[t:t + 1, :] * _delayed(ext, FFN_CONV - 1 - t, ts)
    return u


def _ffn_act(up, conv_w, bias, name):
    S, F2 = up.shape
    F = F2 // 2
    ts, cw = _tile(S, 512, HALO), _tile(F, 512)
    nb = F // cw

    def body(a_ref, ah_ref, b_ref, bh_ref, wa_ref, wb_ref, ba_ref, bb_ref, o_ref, ea, eb):
        i = pl.program_id(0)
        _fill_causal(ea, a_ref[...].astype(F32), ah_ref[...].astype(F32), i)
        _fill_causal(eb, b_ref[...].astype(F32), bh_ref[...].astype(F32), i)
        ua = _ffn_u(ea, wa_ref[...], ba_ref[...], ts)
        ub = _ffn_u(eb, wb_ref[...], bb_ref[...], ts)
        o_ref[...] = (_silu(ua) * ub).astype(BF16)

    tile = lambda c0: pl.BlockSpec((ts, cw), lambda i, j: (i, j + c0))
    vec = lambda rows, c0: pl.BlockSpec((rows, cw), lambda i, j: (0, j + c0))
    return pl.pallas_call(
        body, name=name, out_shape=jax.ShapeDtypeStruct((S, F), BF16), grid=(S // ts, nb),
        in_specs=[tile(0), _prev_halo_spec(ts, cw), tile(nb), _prev_halo_spec(ts, cw, nb),
                  vec(FFN_CONV, 0), vec(FFN_CONV, nb), vec(1, 0), vec(1, nb)],
        out_specs=pl.BlockSpec((ts, cw), lambda i, j: (i, j)),
        scratch_shapes=[pltpu.VMEM((ts + HALO, cw), F32), pltpu.VMEM((ts + HALO, cw), F32)],
        compiler_params=_params(),
    )(up, up, up, up, conv_w, conv_w, bias, bias)


def _ffn_act_bwd(up, conv_w, bias, dact, name):
    S, F2 = up.shape
    F = F2 // 2
    ts, cw = _tile(S, 512, HALO), _tile(F, 512)
    nb = F // cw
    n = S // ts

    def body(a_ref, ah_ref, b_ref, bh_ref, wa_ref, wb_ref, ba_ref, bb_ref, d_ref,
             dua_ref, dub_ref, dwa_ref, dwb_ref, dba_ref, dbb_ref, ea, eb, wacc, bacc):
        i = pl.program_id(1)
        _fill_causal(ea, a_ref[...].astype(F32), ah_ref[...].astype(F32), i)
        _fill_causal(eb, b_ref[...].astype(F32), bh_ref[...].astype(F32), i)
        ua = _ffn_u(ea, wa_ref[...], ba_ref[...], ts)
        ub = _ffn_u(eb, wb_ref[...], bb_ref[...], ts)
        d = d_ref[...].astype(F32)
        dua = d * ub * _dsilu(ua)
        dub = d * _silu(ua)
        dua_ref[...] = dua.astype(BF16)
        dub_ref[...] = dub.astype(BF16)

        @pl.when(i == 0)
        def _():
            wacc[...] = jnp.zeros_like(wacc)
            bacc[...] = jnp.zeros_like(bacc)

        for t in range(FFN_CONV):
            wacc[0, t] += _colsum8(dua * _delayed(ea, FFN_CONV - 1 - t, ts))
            wacc[1, t] += _colsum8(dub * _delayed(eb, FFN_CONV - 1 - t, ts))
        bacc[0] += _colsum8(dua)
        bacc[1] += _colsum8(dub)

        @pl.when(i == n - 1)
        def _():
            dwa_ref[...] = jnp.sum(wacc[0], axis=1)
            dwb_ref[...] = jnp.sum(wacc[1], axis=1)
            dba_ref[...] = jnp.sum(bacc[0], axis=0, keepdims=True)
            dbb_ref[...] = jnp.sum(bacc[1], axis=0, keepdims=True)

    tile = lambda c0: pl.BlockSpec((ts, cw), lambda j, i: (i, j + c0))
    halo = lambda c0: pl.BlockSpec((HALO, cw), lambda j, i: (jnp.maximum(i * (ts // HALO) - 1, 0), j + c0))
    vec = lambda rows, c0: pl.BlockSpec((rows, cw), lambda j, i: (0, j + c0))
    dua, dub, dwa, dwb, dba, dbb = pl.pallas_call(
        body, name=name, grid=(nb, n),
        out_shape=(jax.ShapeDtypeStruct((S, F), BF16), jax.ShapeDtypeStruct((S, F), BF16),
                   jax.ShapeDtypeStruct((FFN_CONV, F), F32), jax.ShapeDtypeStruct((FFN_CONV, F), F32),
                   jax.ShapeDtypeStruct((1, F), F32), jax.ShapeDtypeStruct((1, F), F32)),
        in_specs=[tile(0), halo(0), tile(nb), halo(nb), vec(FFN_CONV, 0), vec(FFN_CONV, nb), vec(1, 0), vec(1, nb), tile(0)],
        out_specs=(tile(0), tile(0), vec(FFN_CONV, 0), vec(FFN_CONV, 0), vec(1, 0), vec(1, 0)),
        scratch_shapes=[pltpu.VMEM((ts + HALO, cw), F32), pltpu.VMEM((ts + HALO, cw), F32),
                        pltpu.VMEM((2, FFN_CONV, 8, cw), F32), pltpu.VMEM((2, 8, cw), F32)],
        compiler_params=_params(),
    )(up, up, up, up, conv_w, conv_w, bias, bias, dact)
    return (jnp.concatenate([dua, dub], axis=1), jnp.concatenate([dwa, dwb], axis=1),
            jnp.concatenate([dba, dbb], axis=1))


def _adamw(gparts, w, m, v, name):
    R, C = w.shape
    tr = _tile(R, max(16, (256 * 1024) // C), 16)

    def body(g_ref, w_ref, m_ref, v_ref, go_ref, d_ref, mo_ref, vo_ref):
        g = g_ref[0].astype(F32)
        for s in range(1, N_DEV):
            g = g + g_ref[s].astype(F32)
        mn = ADAM_B1 * m_ref[...] + (1.0 - ADAM_B1) * g
        vn = ADAM_B2 * v_ref[...] + (1.0 - ADAM_B2) * (g * g)
        m_hat = mn / (1.0 - ADAM_B1 ** ADAM_STEP)
        v_hat = vn / (1.0 - ADAM_B2 ** ADAM_STEP)
        go_ref[...] = g
        d_ref[...] = -ADAM_LR * (m_hat / (jnp.sqrt(v_hat) + ADAM_EPS) + ADAM_WD * w_ref[...])
        mo_ref[...] = mn
        vo_ref[...] = vn

    row = pl.BlockSpec((tr, C), lambda i: (i, 0))
    o = jax.ShapeDtypeStruct((R, C), F32)
    return pl.pallas_call(
        body, name=name, out_shape=(o, o, o, o), grid=(R // tr,),
        in_specs=[pl.BlockSpec((N_DEV, tr, C), lambda i: (0, i, 0)), row, row, row],
        out_specs=(row, row, row, row), compiler_params=_params(),
    )(gparts, w, m, v)


def _position():
    return lax.axis_index("x"), lax.axis_index("y"), lax.axis_index("c")


class _Copies:
    def __init__(self, arrays):
        self.arrays, self.n = list(arrays), len(arrays)
        self.scratch = [pltpu.SemaphoreType.DMA((7 * self.n,)), pltpu.SemaphoreType.DMA((7 * self.n,)),
                        pltpu.SemaphoreType.DMA((self.n,))]


class _Gather(_Copies):
    def __init__(self, arrays):
        super().__init__(arrays)
        self.out_shape = [jax.ShapeDtypeStruct((N_DEV,) + b.shape, b.dtype) for b in self.arrays]

    def phases(self, x_refs, out_refs, send_sems, recv_sems, local_sems):
        n = self.n
        x, y, c = _position()
        me, sibling = (x, y, c), (x, y, 1 - c)
        chips = [(1 - x, y), (x, 1 - y), (1 - x, 1 - y)]

        def copy(a, k, blk, to, own=False):
            slot = out_refs[a].at[4 * blk[0] + 2 * blk[1] + blk[2]]
            return pltpu.make_async_remote_copy(
                src_ref=x_refs[a] if own else slot, dst_ref=slot,
                send_sem=send_sems.at[7 * a + k], recv_sem=recv_sems.at[7 * a + k], device_id=to, device_id_type=MESH)

        mine = [pltpu.make_async_copy(x_refs[a], out_refs[a].at[4 * x + 2 * y + c], local_sems.at[a]) for a in range(n)]
        first = [copy(a, 0, me, sibling, own=True) for a in range(n)]
        first += [copy(a, 1 + j, me, (*chip, c), own=True) for a in range(n) for j, chip in enumerate(chips)]
        passed = [copy(a, 4 + j, (*chip, c), sibling) for j, chip in enumerate(chips) for a in range(n)]

        def start():
            for cp in mine + first:
                cp.start()

        def hand_on():
            for j, chip in enumerate(chips):
                for a in range(n):
                    copy(a, 1 + j, (*chip, c), me).wait_recv()
                    passed[j * n + a].start()

        def finish():
            for a in range(n):
                copy(a, 0, sibling, me).wait_recv()
            for j, chip in enumerate(chips):
                for a in range(n):
                    copy(a, 4 + j, (*chip, 1 - c), me).wait_recv()
            for cp in first + passed:
                cp.wait_send()
            for cp in mine:
                cp.wait()

        return start, hand_on, finish


class _Exchange(_Copies):
    def __init__(self, arrays):
        super().__init__(arrays)
        self.out_shape = [jax.ShapeDtypeStruct(p.shape, p.dtype) for p in self.arrays]

    def phases(self, p_refs, out_refs, send_sems, recv_sems, local_sems):
        n = self.n
        x, y, c = _position()
        my_slot = 4 * x + 2 * y + c
        mine = [pltpu.make_async_copy(p_refs[a].at[my_slot], out_refs[a].at[my_slot], local_sems.at[a]) for a in range(n)]
        copies = []
        for k in range(1, N_DEV):
            px, py, pc = x ^ (k >> 2), y ^ ((k >> 1) & 1), c ^ (k & 1)
            for a in range(n):
                copies.append(pltpu.make_async_remote_copy(
                    src_ref=p_refs[a].at[4 * px + 2 * py + pc], dst_ref=out_refs[a].at[my_slot],
                    send_sem=send_sems.at[7 * a + k - 1], recv_sem=recv_sems.at[7 * a + k - 1],
                    device_id=(px, py, pc), device_id_type=MESH))

        def start():
            for cp in mine + copies:
                cp.start()

        def finish():
            for cp in copies + mine:
                cp.wait()

        return start, None, finish


def _communicate(copies, name):
    n = copies.n

    def body(*refs):
        start, hand_on, finish = copies.phases(refs[:n], refs[n:2 * n], *refs[2 * n:])
        start()
        if hand_on is not None:
            hand_on()
        finish()

    return pl.pallas_call(body, name=name, out_shape=copies.out_shape, in_specs=[ANY] * n, out_specs=[ANY] * n,
                          scratch_shapes=copies.scratch)(*copies.arrays)


def _col_pieces(col_map, shard_w):
    pieces = []
    for lo, hi, dst in col_map:
        c = lo
        while c < hi:
            j = c // shard_w
            end = min(hi, (j + 1) * shard_w)
            pieces.append((j, c - j * shard_w, end - c, dst + (c - lo)))
            c = end
    return pieces


def _assemble_cols(shards, pieces, width, name):
    _, R, Cs = shards.shape
    tr = _tile(R, 128, 16)

    def body(s_ref, o_ref):
        o_ref[...] = jnp.zeros(o_ref.shape, o_ref.dtype)
        for j, lo, n, dst in pieces:
            o_ref[:, dst:dst + n] = s_ref[j, :, lo:lo + n]

    return pl.pallas_call(
        body, name=name, out_shape=jax.ShapeDtypeStruct((R, width), shards.dtype), grid=(R // tr,),
        in_specs=[pl.BlockSpec((N_DEV, tr, Cs), lambda i: (0, i, 0))],
        out_specs=pl.BlockSpec((tr, width), lambda i: (i, 0)), compiler_params=_params(),
    )(shards)


def _split_cols(full, pieces, shard_w, name):
    R, width = full.shape
    tr = _tile(R, 128, 16)

    def body(f_ref, o_ref):
        for j, lo, n, dst in pieces:
            o_ref[j, :, lo:lo + n] = f_ref[:, dst:dst + n]

    return pl.pallas_call(
        body, name=name, out_shape=jax.ShapeDtypeStruct((N_DEV, R, shard_w), full.dtype), grid=(R // tr,),
        in_specs=[pl.BlockSpec((tr, width), lambda i: (i, 0))],
        out_specs=pl.BlockSpec((N_DEV, tr, shard_w), lambda i: (0, i, 0)), compiler_params=_params(),
    )(full)


def _pack(arrays, row_multiple=8):
    flat, layout, off = [], [], 0
    for a in arrays:
        n = a.size
        padded = -(-n // LANE) * LANE
        f = a.reshape(-1).astype(F32)
        if padded != n:
            f = jnp.pad(f, (0, padded - n))
        flat.append(f)
        layout.append((off, n, a.shape))
        off += padded
    total = -(-off // (LANE * row_multiple)) * (LANE * row_multiple)
    if total != off:
        flat.append(jnp.zeros((total - off,), F32))
    return jnp.concatenate(flat).reshape(total // LANE, LANE), layout


def _unpack(buf, layout):
    flat = buf.reshape(-1)
    return [flat[off:off + n].reshape(shape) for off, n, shape in layout]


def _cols_to_full(g):
    return jnp.transpose(g, (1, 0, 2)).reshape(g.shape[1], N_DEV * g.shape[2])


def _full_to_cols(a):
    return jnp.transpose(a.reshape(a.shape[0], N_DEV, a.shape[1] // N_DEV), (1, 0, 2))


def _rows_to_full(g):
    return g.reshape(N_DEV * g.shape[1], g.shape[2])


def _full_to_rows(a):
    return a.reshape(N_DEV, a.shape[0] // N_DEV, a.shape[1])


def _pad_cols(a, width):
    return a if a.shape[-1] == width else jnp.pad(a, [(0, 0)] * (a.ndim - 1) + [(0, width - a.shape[-1])])


SHARDED = ("w_in", "conv_qkv", "pool_w", "w_branch_a", "w_branch_b", "w_mix_out", "w_xq", "w_xkv", "w_xo", "w_up",
           "ffn_conv_w", "w_down")
REPLICATED = ("mix_pre_norm", "a_log", "dt_bias", "gdn_norm", "pool_scale", "mix_post_norm", "xa_pre_norm", "mem_norm",
              "xa_post_norm", "ffn_pre_norm", "ffn_conv_b", "ffn_post_norm")
WEIGHTS = ("mix_pre_norm", "w_in", "conv_qkv", "a_log", "dt_bias", "gdn_norm", "pool_w", "pool_scale", "w_branch_a",
           "w_branch_b", "w_mix_out", "mix_post_norm", "xa_pre_norm", "mem_norm", "w_xq", "w_xkv", "w_xo", "xa_post_norm",
           "ffn_pre_norm", "w_up", "ffn_conv_w", "ffn_conv_b", "w_down", "ffn_post_norm")
MATMUL_WEIGHTS = ("w_in", "w_branch_a", "w_branch_b", "w_mix_out", "w_xq", "w_xkv", "w_xo", "w_up", "w_down")
COL_SHARDED = ("w_in", "w_branch_b", "w_xkv", "w_up", "conv_qkv", "ffn_conv_w")
ROW_SHARDED = ("w_branch_a", "w_mix_out", "w_xq", "w_xo", "w_down")


class _Layout:
    def __init__(self, D, H, pw, F):
        self.D, self.H, self.pw, self.F = D, H, pw, F
        self.qkv_w, self.vw = 3 * H * HD, H * HD
        self.ba_w = 512 if D >= 2048 else LANE
        self.Fp = -(-F // 512) * 512 if F >= 512 else F
        q, vw = self.qkv_w, self.vw
        self.seg = dict(qkv=(0, q), z=(q, vw), ga=(q + vw, D), gb=(q + vw + D, D), p=(q + vw + 2 * D, pw),
                        ba=(q + vw + 2 * D + pw, self.ba_w))
        self.in_w = q + vw + 2 * D + pw + self.ba_w
        o_z, o_b = q, q + vw
        o_p = o_b + 2 * H
        o_ga = o_p + pw
        o_gb = o_ga + D
        self.d_in = o_gb + D
        self.in_map = [(0, o_z, self.seg["qkv"][0]), (o_z, o_b, self.seg["z"][0]), (o_b, o_p, self.seg["ba"][0]),
                       (o_p, o_ga, self.seg["p"][0]), (o_ga, o_gb, self.seg["ga"][0]), (o_gb, self.d_in, self.seg["gb"][0])]
        self.up_map = [(0, F, 0), (F, 2 * F, self.Fp)]

    def col(self, name, width):
        return self.seg[name][0] // width


def _local_step(x, mem, target, P, L, comm=None):
    D, H, pw, F, Fp = L.D, L.H, L.pw, L.F, L.Fp
    qkv_w, vw, ba_w = L.qkv_w, L.vw, L.ba_w
    col = L.col
    P = dict(P)
    win_p, cw3_p, fb_p = P["win_p"], P["cw3_p"], P["fb_p"]
    conv_qkv, pool_w = P["conv_qkv"], P["pool_w"]
    lanes = lambda vec: jnp.pad(vec.reshape(1, H).astype(F32), ((0, 0), (H, LANE - 2 * H)))
    a_log_l, dt_bias_l = lanes(P["a_log"]), lanes(P["dt_bias"])
    bf = lambda name: P[name]
    vecf = lambda name: P[name].reshape(1, -1).astype(F32)

    h1 = _prenorm(x, vecf("mix_pre_norm"), "mix_prenorm")
    proj = _matmul(h1, win_p, "nn", BF16, "in_proj", tn=768)
    qkv_hm = _qkv_conv(proj, conv_qkv, qkv_w, "qkv_conv")
    bg = _gates(proj, col("ba", LANE), a_log_l, dt_bias_l, H, "gates")
    o_hm, states, late = _gdn_fwd(qkv_hm, bg, H, "gdn_fwd", rider=comm.late_gather() if comm else None)
    if comm:
        P.update(comm.late_weights(late))
    wup_p, wdown_p = P["wup_p"], P["wdown_p"]
    oa = _gdn_out(o_hm, proj, col("z", vw), vecf("gdn_norm"), "gdn_out")
    ya = _matmul(oa, bf("w_branch_a"), "nn", BF16, "branch_a")
    pb = _pool_fwd(proj, col("p", pw), pool_w, vecf("pool_scale"), pw, "pool_fwd")
    yb = _matmul(pb, bf("w_branch_b"), "nn", BF16, "branch_b")
    merged = _merge(proj, col("ga", D), col("gb", D), ya, yb, "merge")
    y1 = _matmul(merged, bf("w_mix_out"), "nn", F32, "mix_out")
    x1, h2 = _post_pre(x, y1, vecf("mix_post_norm"), vecf("xa_pre_norm"), "mix_post")
    mn = _prenorm(mem, vecf("mem_norm"), "mem_norm")
    qx = _matmul(h2, bf("w_xq"), "nn", BF16, "xq")
    kv = _matmul(mn, bf("w_xkv"), "nn", BF16, "xkv")
    ox = _xattn_fwd(qx, kv, "xattn_fwd")
    y2 = _matmul(ox, bf("w_xo"), "nn", F32, "xo")
    x2, h3 = _post_pre(x1, y2, vecf("xa_post_norm"), vecf("ffn_pre_norm"), "xa_post")
    up = _matmul(h3, wup_p, "nn", BF16, "ffn_up")
    act = _ffn_act(up, cw3_p, fb_p, "ffn_act")
    y3 = _matmul(act, wdown_p, "nn", F32, "ffn_down")
    dx3, loss = _post_loss(x2, y3, vecf("ffn_post_norm"), target, "ffn_post_loss")

    g = {}
    dy3, g["ffn_post_norm"] = _post_bwd(y3, vecf("ffn_post_norm"), dx3, "ffn_post_bwd")
    dact = _matmul(dy3, wdown_p, "nt", BF16, "ffn_down_dx")
    g["w_down_p"] = _matmul(act, dy3, "tn", BF16, "ffn_down_dw")
    du, g["ffn_conv_w_p"], g["ffn_conv_b_p"] = _ffn_act_bwd(up, cw3_p, fb_p, dact, "ffn_act_bwd")
    dup = _conv_t(du, cw3_p, "ffn_conv_t")
    dh3 = _matmul(dup, wup_p, "nt", F32, "ffn_up_dx")
    g["w_up_p"] = _matmul(h3, dup, "tn", BF16, "ffn_up_dw")
    dx2, g["ffn_pre_norm"] = _pre_bwd(x2, vecf("ffn_pre_norm"), dh3, dx3, "ffn_pre_bwd")
    dy2, g["xa_post_norm"] = _post_bwd(y2, vecf("xa_post_norm"), dx2, "xa_post_bwd")
    dox = _matmul(dy2, bf("w_xo"), "nt", BF16, "xo_dx")
    g["w_xo"] = _matmul(ox, dy2, "tn", BF16, "xo_dw")
    dqx, dkv = _xattn_bwd(qx, kv, dox, "xattn_bwd")
    dkv_b = dkv.astype(BF16)
    dh2 = _matmul(dqx, bf("w_xq"), "nt", F32, "xq_dx")
    g["w_xq"] = _matmul(h2, dqx, "tn", BF16, "xq_dw")
    dmn = _matmul(dkv_b, bf("w_xkv"), "nt", F32, "xkv_dx")
    g["w_xkv"] = _matmul(mn, dkv_b, "tn", BF16, "xkv_dw")
    _, g["mem_norm"] = _pre_bwd(mem, vecf("mem_norm"), dmn, jnp.zeros_like(mem), "mem_norm_bwd")
    dx1, g["xa_pre_norm"] = _pre_bwd(x1, vecf("xa_pre_norm"), dh2, dx2, "xa_pre_bwd")
    dy1, g["mix_post_norm"] = _post_bwd(y1, vecf("mix_post_norm"), dx1, "mix_post_bwd")
    dmerged = _matmul(dy1, bf("w_mix_out"), "nt", BF16, "mix_out_dx")
    g["w_mix_out"] = _matmul(merged, dy1, "tn", BF16, "mix_out_dw")
    dya, dyb, dga, dgb = _merge_bwd(proj, col("ga", D), col("gb", D), ya, yb, dmerged, "merge_bwd")
    doa = _matmul(dya, bf("w_branch_a"), "nt", BF16, "branch_a_dx")
    g["w_branch_a"] = _matmul(oa, dya, "tn", BF16, "branch_a_dw")
    dpb = _matmul(dyb, bf("w_branch_b"), "nt", BF16, "branch_b_dx")
    g["w_branch_b"] = _matmul(pb, dyb, "tn", BF16, "branch_b_dw")
    dp, g["pool_w"], g["pool_scale"] = _pool_bwd(proj, col("p", pw), pool_w, vecf("pool_scale"), dpb, pw, "pool_bwd")
    do_hm, dz, g["gdn_norm"] = _gdn_out_bwd(o_hm, proj, col("z", vw), vecf("gdn_norm"), doa, "gdn_out_bwd")
    dqkv_hm, dbg, got = _gdn_bwd(qkv_hm, bg, states, do_hm, H, "gdn_bwd", rider=comm.early_exchange(g) if comm else None)
    if comm:
        comm.receive(comm.EARLY_GRADS, got)
    dba, dal, ddt = _gates_bwd(proj, col("ba", LANE), ba_w, a_log_l, dt_bias_l, dbg, H, "gates_bwd")
    g["a_log"], g["dt_bias"] = dal[:, H:2 * H], ddt[:, H:2 * H]
    dc, g["conv_qkv"] = _qkv_conv_bwd(proj, conv_qkv, dqkv_hm, qkv_w, "qkv_conv_bwd")
    dqkv = _conv_t(dc, conv_qkv, "qkv_conv_t")
    dproj = jnp.concatenate([dqkv, dz, dga, dgb, dp, dba], axis=1)
    g["w_in_p"] = _matmul(h1, dproj, "tn", BF16, "in_proj_dw", tn=768)
    if comm:
        dh1, got = _matmul(dproj, win_p, "nt", F32, "in_proj_dx", rider=comm.last_exchange(g))
        comm.receive(comm.LAST_GRADS, got)
    else:
        dh1 = _matmul(dproj, win_p, "nt", F32, "in_proj_dx")
    grad_x, g["mix_pre_norm"] = _pre_bwd(x, vecf("mix_pre_norm"), dh1, dx1, "mix_pre_bwd")
    return loss, grad_x, g


def _two_halves(a, F, Fp):
    return jnp.concatenate([_pad_cols(a[..., :F], Fp), _pad_cols(a[..., F:], Fp)], axis=-1)


def _from_halves(a, F, Fp):
    return jnp.concatenate([a[..., :F], a[..., Fp:Fp + F]], axis=-1)


class _StepComm:
    FIRST = ("w_in", "conv_qkv", "pool_w", "ffn_conv_w")
    LATE = ("w_branch_a", "w_branch_b", "w_mix_out", "w_xq", "w_xkv", "w_xo", "w_up", "w_down")
    EARLY_GRADS = ("pool_w", "w_branch_a", "w_branch_b", "w_mix_out", "w_xq", "w_xkv", "w_xo", "w_up", "ffn_conv_w",
                   "w_down")
    LAST_GRADS = ("w_in", "conv_qkv")

    def __init__(self, w, L):
        self.w, self.L = w, L
        self.in_pieces = _col_pieces(L.in_map, w["w_in"].shape[1])
        self.up_pieces = _col_pieces(L.up_map, w["w_up"].shape[1])
        self.received = {}

    def _shard(self, n):
        return self.w[n].astype(BF16) if n in MATMUL_WEIGHTS else self.w[n]

    def first_weights(self):
        L, (g, r, c) = self.L, self.w["pool_w"].shape
        G = dict(zip(self.FIRST, _communicate(_Gather([self._shard(n) for n in self.FIRST]), "gather_first")))
        return {"win_p": _assemble_cols(G["w_in"], self.in_pieces, L.in_w, "assemble_w_in"),
                "conv_qkv": _cols_to_full(G["conv_qkv"]),
                "cw3_p": _two_halves(_cols_to_full(G["ffn_conv_w"]), L.F, L.Fp),
                "pool_w": jnp.transpose(G["pool_w"], (1, 0, 2, 3)).reshape(g, N_DEV * r, c).astype(BF16)}

    def late_gather(self):
        return _Gather([self._shard(n) for n in self.LATE])

    def late_weights(self, results):
        L, G = self.L, dict(zip(self.LATE, results))
        P = {n: _rows_to_full(G[n]) for n in ("w_branch_a", "w_mix_out", "w_xq", "w_xo")}
        P.update({n: _cols_to_full(G[n]) for n in ("w_branch_b", "w_xkv")})
        P["wup_p"] = _assemble_cols(G["w_up"], self.up_pieces, 2 * L.Fp, "assemble_w_up")
        P["wdown_p"] = jnp.pad(_rows_to_full(G["w_down"]), ((0, L.Fp - L.F), (0, 0)))
        return P

    def _slices(self, g, n):
        L, w = self.L, self.w
        if n == "w_in":
            return _split_cols(g["w_in_p"], self.in_pieces, w["w_in"].shape[1], "split_w_in")
        if n == "w_up":
            return _split_cols(g["w_up_p"], self.up_pieces, w["w_up"].shape[1], "split_w_up")
        if n == "w_down":
            return _full_to_rows(g["w_down_p"][:L.F])
        if n == "ffn_conv_w":
            return _full_to_cols(_from_halves(g["ffn_conv_w_p"], L.F, L.Fp))
        if n == "pool_w":
            grp, r, c = w["pool_w"].shape
            return jnp.transpose(g[n].reshape(grp, N_DEV, r, c), (1, 0, 2, 3)).reshape(N_DEV, grp * r, c)
        return _full_to_rows(g[n]) if n in ROW_SHARDED else _full_to_cols(g[n])

    def early_exchange(self, g):
        return _Exchange([self._slices(g, n) for n in self.EARLY_GRADS])

    def last_exchange(self, g):
        return _Exchange([self._slices(g, n) for n in self.LAST_GRADS])

    def receive(self, names, results):
        self.received.update(zip(names, results))


def kernel(x, mem, mix_pre_norm, w_in, conv_qkv, a_log, dt_bias, gdn_norm, pool_w, pool_scale, w_branch_a, w_branch_b, w_mix_out, mix_post_norm, xa_pre_norm, mem_norm, w_xq, w_xkv, w_xo, xa_post_norm, ffn_pre_norm, w_up, ffn_conv_w, ffn_conv_b, w_down, ffn_post_norm, loss_target, m_mix_pre_norm, m_w_in, m_conv_qkv, m_a_log, m_dt_bias, m_gdn_norm, m_pool_w, m_pool_scale, m_w_branch_a, m_w_branch_b, m_w_mix_out, m_mix_post_norm, m_xa_pre_norm, m_mem_norm, m_w_xq, m_w_xkv, m_w_xo, m_xa_post_norm, m_ffn_pre_norm, m_w_up, m_ffn_conv_w, m_ffn_conv_b, m_w_down, m_ffn_post_norm, v_mix_pre_norm, v_w_in, v_conv_qkv, v_a_log, v_dt_bias, v_gdn_norm, v_pool_w, v_pool_scale, v_w_branch_a, v_w_branch_b, v_w_mix_out, v_mix_post_norm, v_xa_pre_norm, v_mem_norm, v_w_xq, v_w_xkv, v_w_xo, v_xa_post_norm, v_ffn_pre_norm, v_w_up, v_ffn_conv_w, v_ffn_conv_b, v_w_down, v_ffn_post_norm):
    given = dict(locals())
    w = {n: given[n][0] for n in WEIGHTS}
    m = {n: given["m_" + n][0] for n in WEIGHTS}
    v = {n: given["v_" + n][0] for n in WEIGHTS}
    D = x.shape[-1]
    F = w["w_down"].shape[0] * N_DEV
    L = _Layout(D, w["a_log"].shape[-1], w["pool_scale"].shape[-1], F)
    Fp = L.Fp

    comm = _StepComm(w, L)
    P = {n: w[n] for n in REPLICATED}
    P.update(comm.first_weights())
    P["fb_p"] = _two_halves(w["ffn_conv_b"].reshape(1, 2 * F), F, Fp)
    loss, grad_x, g = _local_step(x[0], mem[0], loss_target[0], P, L, comm)

    received = comm.received
    outs = {}
    for n in SHARDED:
        as2d = lambda a: a.reshape(-1, a.shape[-1])
        res = _adamw(received[n], as2d(w[n]), as2d(m[n]), as2d(v[n]), "adamw_" + n)
        outs[n] = [r.reshape(w[n].shape) for r in res]

    g["ffn_conv_b"] = _from_halves(g["ffn_conv_b_p"], F, Fp)
    rep_parts, rep_layout = _pack([g[n].reshape(w[n].shape) for n in REPLICATED] + [loss])
    rep_all, = _communicate(_Gather([rep_parts]), "gather_small_grads")
    zero_loss = jnp.zeros_like(loss)
    wr, _ = _pack([w[n] for n in REPLICATED] + [zero_loss])
    mr, _ = _pack([m[n] for n in REPLICATED] + [zero_loss])
    vr, _ = _pack([v[n] for n in REPLICATED] + [zero_loss])
    outs_rep = [_unpack(o, rep_layout) for o in _adamw(rep_all, wr, mr, vr, "adamw_replicated")]
    loss_total = outs_rep[0][-1][0, 0]
    for i, n in enumerate(REPLICATED):
        outs[n] = [outs_rep[k][i] for k in range(4)]

    result = [loss_total, grad_x[None]]
    for k in range(4):
        for n in WEIGHTS:
            result.append(outs[n][k][None])
    return tuple(result)
```

```python
import functools

import jax
import jax.numpy as jnp
from jax import lax
from jax.experimental import pallas as pl
from jax.experimental.pallas import tpu as pltpu

F32, BF16 = jnp.float32, jnp.bfloat16
MESH = pl.DeviceIdType.MESH
ANY = pl.BlockSpec(memory_space=pl.ANY)

N_DEV = 8
EPS = 1e-6
CHUNK = 64
HD = 128
GDN_CONV = 4
FFN_CONV = 3
POOL_WINDOWS = (2, 4, 8, 16)
XA_HEADS = 4
HALO = 16
RC = 16
LANE = 128
VMEM_LIMIT = 48 * 1024 * 1024

ADAM_LR, ADAM_B1, ADAM_B2, ADAM_EPS, ADAM_WD, ADAM_STEP = 0.001, 0.9, 0.999, 1e-08, 0.01, 10


def _tile(n, pref, align=LANE):
    best = None
    t = align
    while t <= min(n, pref):
        if n % t == 0:
            best = t
        t += align
    return best if best is not None else n


def _params(**kw):
    return pltpu.CompilerParams(vmem_limit_bytes=VMEM_LIMIT, **kw)


def _sigmoid(x):
    return 1.0 / (1.0 + jnp.exp(-x))


def _silu(x):
    return x * _sigmoid(x)


def _dsilu(x):
    s = _sigmoid(x)
    return s * (1.0 + x * (1.0 - s))


def _colsum8(t):
    return t.reshape(t.shape[0] // 8, 8, t.shape[1]).sum(axis=0)


def _ride(body, n_in, n_out, rider, first, middle, last):
    if rider is None:
        return body
    n = rider.n

    def wrapped(*refs):
        ins, r_in = refs[:n_in], refs[n_in:n_in + n]
        outs, r_out = refs[n_in + n:n_in + n + n_out], refs[n_in + n + n_out:n_in + 2 * n + n_out]
        rest = refs[n_in + 2 * n + n_out:]
        start, hand_on, finish = rider.phases(r_in, r_out, *rest[-3:])
        pl.when(first())(start)
        body(*ins, *outs, *rest[:-3])
        if hand_on is not None:
            pl.when(middle())(hand_on)
        pl.when(last())(finish)

    return wrapped


def _matmul(a, b, mode, out_dtype, name, tm=1024, tn=1024, tk=2816, rider=None):
    ga = a.shape[0] if (mode == "nt" and a.ndim == 3) else 1
    gb = b.shape[0] if (mode == "tn" and b.ndim == 3) else 1
    if mode == "nn":
        (M, K), (K2, N) = a.shape, b.shape
    elif mode == "nt":
        M, K = a.shape[-2], ga * a.shape[-1]
        N, K2 = b.shape
    else:
        K, M = a.shape
        K2, N = b.shape[-2], gb * b.shape[-1]
    assert K == K2, (name, a.shape, b.shape)
    tm, tn = _tile(M, tm), _tile(N // gb, tn)
    tk = K // ga if K // ga <= tk else _tile(K // ga, tk)
    nk = K // tk
    kpg, npg = K // ga // tk, N // gb // tn
    if mode == "nn":
        a_spec = pl.BlockSpec((tm, tk), lambda i, j, k: (i, k))
        b_spec = pl.BlockSpec((tk, tn), lambda i, j, k: (k, j))
        dims = (((1,), (0,)), ((), ()))
    elif mode == "nt":
        a_spec = (pl.BlockSpec((tm, tk), lambda i, j, k: (i, k)) if a.ndim == 2 else
                  pl.BlockSpec((None, tm, tk), lambda i, j, k: (k // kpg, i, k % kpg)))
        b_spec = pl.BlockSpec((tn, tk), lambda i, j, k: (j, k))
        dims = (((1,), (1,)), ((), ()))
    else:
        a_spec = pl.BlockSpec((tk, tm), lambda i, j, k: (k, i))
        b_spec = (pl.BlockSpec((tk, tn), lambda i, j, k: (k, j)) if b.ndim == 2 else
                  pl.BlockSpec((None, tk, tn), lambda i, j, k: (j // npg, k, j % npg)))
        dims = (((0,), (0,)), ((), ()))

    def body(a_ref, b_ref, o_ref, acc):
        part = lax.dot_general(a_ref[...], b_ref[...], dims, preferred_element_type=F32)
        if nk == 1:
            o_ref[...] = part.astype(o_ref.dtype)
        else:
            k = pl.program_id(2)

            @pl.when(k == 0)
            def _():
                acc[...] = part

            @pl.when(k > 0)
            def _():
                acc[...] += part

            @pl.when(k == nk - 1)
            def _():
                o_ref[...] = acc[...].astype(o_ref.dtype)

    grid = (M // tm, N // tn, nk)
    at = lambda step: lambda: ((pl.program_id(0) == step[0]) & (pl.program_id(1) == step[1])
                               & (pl.program_id(2) == step[2]))
    extra = rider.n if rider is not None else 0
    res = pl.pallas_call(
        _ride(body, 2, 1, rider, at((0, 0, 0)), at((grid[0] // 2, 0, 0)), at((grid[0] - 1, grid[1] - 1, nk - 1))),
        name=name, out_shape=[jax.ShapeDtypeStruct((M, N), out_dtype)] + (rider.out_shape if rider else []),
        grid=grid, in_specs=[a_spec, b_spec] + [ANY] * extra,
        out_specs=[pl.BlockSpec((tm, tn), lambda i, j, k: (i, j))] + [ANY] * extra,
        scratch_shapes=[pltpu.VMEM((tm, tn) if nk > 1 else (8, LANE), F32)] + (rider.scratch if rider else []),
        compiler_params=_params(dimension_semantics=("arbitrary",) * 3 if rider else ("parallel", "parallel", "arbitrary")),
    )(a, b, *(rider.arrays if rider else []))
    return (res[0], res[1:]) if rider else res[0]


def _rstd(xf):
    return lax.rsqrt(jnp.mean(xf * xf, axis=-1, keepdims=True) + EPS)


def _rms_bwd(xf, w, dy):
    r = _rstd(xf)
    g = dy * w
    dx = r * g - xf * (r * r * r) * jnp.mean(g * xf, axis=-1, keepdims=True)
    return dx, dy * xf * r


def _row_tile(rows):
    return _tile(rows, 256, 8)


def _prenorm(x, w, name):
    rows, d = x.shape
    ts = _row_tile(rows)

    def body(x_ref, w_ref, h_ref):
        xf = x_ref[...]
        h_ref[...] = (xf * _rstd(xf) * w_ref[...]).astype(BF16)

    return pl.pallas_call(
        body, name=name, out_shape=jax.ShapeDtypeStruct((rows, d), BF16), grid=(rows // ts,),
        in_specs=[pl.BlockSpec((ts, d), lambda i: (i, 0)), pl.BlockSpec((1, d), lambda i: (0, 0))],
        out_specs=pl.BlockSpec((ts, d), lambda i: (i, 0)), compiler_params=_params(),
    )(x, w)


def _post_pre(xres, y, w_post, w_pre, name):
    rows, d = xres.shape
    ts = _row_tile(rows)

    def body(x_ref, y_ref, wp_ref, wn_ref, xo_ref, h_ref):
        yf = y_ref[...]
        xn = x_ref[...] + yf * _rstd(yf) * wp_ref[...]
        xo_ref[...] = xn
        h_ref[...] = (xn * _rstd(xn) * wn_ref[...]).astype(BF16)

    row = pl.BlockSpec((ts, d), lambda i: (i, 0))
    vec = pl.BlockSpec((1, d), lambda i: (0, 0))
    return pl.pallas_call(
        body, name=name, grid=(rows // ts,),
        out_shape=(jax.ShapeDtypeStruct((rows, d), F32), jax.ShapeDtypeStruct((rows, d), BF16)),
        in_specs=[row, row, vec, vec], out_specs=(row, row), compiler_params=_params(),
    )(xres, y, w_post, w_pre)


def _post_loss(xres, y, w_post, target, name):
    rows, d = xres.shape
    ts = _row_tile(rows)
    n = rows // ts

    def body(x_ref, y_ref, wp_ref, t_ref, dx_ref, loss_ref, acc):
        i = pl.program_id(0)
        yf = y_ref[...]
        diff = x_ref[...] + yf * _rstd(yf) * wp_ref[...] - t_ref[...]
        dx_ref[...] = diff * (1.0 / d)

        @pl.when(i == 0)
        def _():
            acc[...] = jnp.zeros_like(acc)

        acc[...] += _colsum8(diff * diff)

        @pl.when(i == n - 1)
        def _():
            loss_ref[...] = jnp.broadcast_to((0.5 / d) * jnp.sum(acc[...]), loss_ref.shape)

    row = pl.BlockSpec((ts, d), lambda i: (i, 0))
    vec = pl.BlockSpec((1, d), lambda i: (0, 0))
    return pl.pallas_call(
        body, name=name, grid=(n,),
        out_shape=(jax.ShapeDtypeStruct((rows, d), F32), jax.ShapeDtypeStruct((1, LANE), F32)),
        in_specs=[row, row, vec, row], out_specs=(row, pl.BlockSpec((1, LANE), lambda i: (0, 0))),
        scratch_shapes=[pltpu.VMEM((8, d), F32)], compiler_params=_params(),
    )(xres, y, w_post, target)


def _post_bwd(y, w_post, dxn, name):
    rows, d = y.shape
    ts = _row_tile(rows)
    n = rows // ts

    def body(y_ref, w_ref, d_ref, dy_ref, dw_ref, acc):
        i = pl.program_id(0)
        dy, dwr = _rms_bwd(y_ref[...], w_ref[...], d_ref[...])
        dy_ref[...] = dy.astype(BF16)

        @pl.when(i == 0)
        def _():
            acc[...] = jnp.zeros_like(acc)

        acc[...] += _colsum8(dwr)

        @pl.when(i == n - 1)
        def _():
            dw_ref[...] = jnp.sum(acc[...], axis=0, keepdims=True)

    row = pl.BlockSpec((ts, d), lambda i: (i, 0))
    vec = pl.BlockSpec((1, d), lambda i: (0, 0))
    return pl.pallas_call(
        body, name=name, grid=(n,),
        out_shape=(jax.ShapeDtypeStruct((rows, d), BF16), jax.ShapeDtypeStruct((1, d), F32)),
        in_specs=[row, vec, row], out_specs=(row, vec),
        scratch_shapes=[pltpu.VMEM((8, d), F32)], compiler_params=_params(),
    )(y, w_post, dxn)


def _pre_bwd(x, w_pre, dh, dres, name):
    rows, d = x.shape
    ts = _row_tile(rows)
    n = rows // ts

    def body(x_ref, w_ref, dh_ref, dr_ref, dx_ref, dw_ref, acc):
        i = pl.program_id(0)
        dx, dwr = _rms_bwd(x_ref[...], w_ref[...], dh_ref[...].astype(F32))
        dx_ref[...] = dr_ref[...] + dx

        @pl.when(i == 0)
        def _():
            acc[...] = jnp.zeros_like(acc)

        acc[...] += _colsum8(dwr)

        @pl.when(i == n - 1)
        def _():
            dw_ref[...] = jnp.sum(acc[...], axis=0, keepdims=True)

    row = pl.BlockSpec((ts, d), lambda i: (i, 0))
    vec = pl.BlockSpec((1, d), lambda i: (0, 0))
    return pl.pallas_call(
        body, name=name, grid=(n,),
        out_shape=(jax.ShapeDtypeStruct((rows, d), F32), jax.ShapeDtypeStruct((1, d), F32)),
        in_specs=[row, vec, row, row], out_specs=(row, vec),
        scratch_shapes=[pltpu.VMEM((8, d), F32)], compiler_params=_params(),
    )(x, w_pre, dh, dres)


def _prev_halo_spec(ts, cw, col0=0):
    return pl.BlockSpec((HALO, cw), lambda i, j: (jnp.maximum(i * (ts // HALO) - 1, 0), j + col0))


def _fill_causal(ext, tile_f32, halo_f32, i):
    ext[pl.ds(0, HALO), :] = jnp.where(i > 0, halo_f32, 0.0)
    ext[pl.ds(HALO, tile_f32.shape[0]), :] = tile_f32


def _row_chunks(ts, chunk):
    def step(c, carry):
        chunk(pl.multiple_of(c * RC, RC))
        return carry

    lax.fori_loop(0, ts // RC, step, 0)


def _causal_taps(ext, r0, K):
    blk = ext[pl.ds(r0 + HALO - 8, RC + 8), :]
    return [blk[8 - j:8 - j + RC] for j in range(K)]


def _advanced_taps(ext, r0, K):
    blk = ext[pl.ds(r0, RC + 8), :]
    return [blk[j:j + RC] for j in range(K)]


def _filter(wv, taps):
    K = len(taps)
    acc = wv[K - 1:K, :] * taps[0]
    for t in range(K - 1):
        acc = acc + wv[t:t + 1, :] * taps[K - 1 - t]
    return acc


def _conv_t(dc, w, name, col_tile=512):
    G, S, C = dc.shape
    K = w.shape[1]
    ts, cw = _tile(S, 512, HALO), _tile(C, col_tile)
    n = S // ts

    def body(d_ref, nx_ref, w_ref, o_ref, ext):
        i = pl.program_id(1)
        ext[pl.ds(0, ts), :] = d_ref[...].astype(F32)
        ext[pl.ds(ts, HALO), :] = jnp.where(i < n - 1, nx_ref[...].astype(F32), 0.0)
        wv = w_ref[...]

        def chunk(r0):
            taps = _advanced_taps(ext, r0, K)
            acc = wv[K - 1:K, :] * taps[0]
            for j in range(K - 1):
                acc = acc + wv[j:j + 1, :] * taps[K - 1 - j]
            o_ref[pl.ds(r0, RC), :] = acc.astype(o_ref.dtype)

        _row_chunks(ts, chunk)

    return pl.pallas_call(
        body, name=name, out_shape=jax.ShapeDtypeStruct((G, S, C), BF16), grid=(G, n, C // cw),
        in_specs=[pl.BlockSpec((None, ts, cw), lambda g, i, j: (g, i, j)),
                  pl.BlockSpec((None, HALO, cw),
                               lambda g, i, j: (g, jnp.minimum((i + 1) * (ts // HALO), S // HALO - 1), j)),
                  pl.BlockSpec((None, K, cw), lambda g, i, j: (g, 0, j))],
        out_specs=pl.BlockSpec((None, ts, cw), lambda g, i, j: (g, i, j)),
        scratch_shapes=[pltpu.VMEM((ts + HALO, cw), F32)], compiler_params=_params(),
    )(dc, dc, w)


def _qkv_conv(proj, conv_w, qkv_w, name):
    S = proj.shape[0]
    H3 = qkv_w // HD
    H = H3 // 3
    hb = 4 if H % 4 == 0 else 1
    cw = hb * HD
    ts = _tile(S, 512, HALO)
    per_kind = H // hb

    def body(x_ref, h_ref, w_ref, o_ref, ext):
        i, j = pl.program_id(0), pl.program_id(1)
        _fill_causal(ext, x_ref[...].astype(F32), h_ref[...].astype(F32), i)
        wv = w_ref[...]
        kind = j // per_kind
        scale = jnp.where(kind == 0, HD ** -0.5, 1.0)

        def chunk(r0):
            s = _silu(_filter(wv, _causal_taps(ext, r0, GDN_CONV)))
            for a in range(hb):
                sa = s[:, HD * a:HD * (a + 1)]
                r = lax.rsqrt(jnp.sum(sa * sa, axis=-1, keepdims=True) + EPS)
                o_ref[a, pl.ds(r0, RC), :] = jnp.where(kind == 2, sa, sa * r * scale)

        _row_chunks(ts, chunk)

    return pl.pallas_call(
        body, name=name, out_shape=jax.ShapeDtypeStruct((H3, S, HD), F32), grid=(S // ts, qkv_w // cw),
        in_specs=[pl.BlockSpec((ts, cw), lambda i, j: (i, j)), _prev_halo_spec(ts, cw),
                  pl.BlockSpec((GDN_CONV, cw), lambda i, j: (0, j))],
        out_specs=pl.BlockSpec((hb, ts, HD), lambda i, j: (j, i, 0)),
        scratch_shapes=[pltpu.VMEM((ts + HALO, cw), F32)], compiler_params=_params(),
    )(proj, proj, conv_w)


def _qkv_conv_bwd(proj, conv_w, dqkv_hm, qkv_w, name):
    S = proj.shape[0]
    H = qkv_w // HD // 3
    hb = 4 if H % 4 == 0 else 1
    cw = hb * HD
    ts = _tile(S, 512, HALO)
    n = S // ts
    per_kind = H // hb

    def body(x_ref, h_ref, w_ref, d_ref, dc_ref, dw_ref, ext, acc):
        j, i = pl.program_id(0), pl.program_id(1)
        _fill_causal(ext, x_ref[...].astype(F32), h_ref[...].astype(F32), i)
        wv = w_ref[...]
        kind = j // per_kind
        scale = jnp.where(kind == 0, HD ** -0.5, 1.0)

        @pl.when(i == 0)
        def _():
            acc[...] = jnp.zeros_like(acc)

        def chunk(r0):
            taps = _causal_taps(ext, r0, GDN_CONV)
            c = _filter(wv, taps)
            s = _silu(c)
            parts = []
            for a in range(hb):
                sa = s[:, HD * a:HD * (a + 1)]
                dy = d_ref[a, pl.ds(r0, RC), :]
                r = lax.rsqrt(jnp.sum(sa * sa, axis=-1, keepdims=True) + EPS)
                dn = scale * (r * dy - sa * (r * r * r) * jnp.sum(dy * sa, axis=-1, keepdims=True))
                parts.append(jnp.where(kind == 2, dy, dn))
            dc = jnp.concatenate(parts, axis=1) * _dsilu(c)
            dc_ref[pl.ds(r0, RC), :] = dc.astype(BF16)
            for t in range(GDN_CONV):
                acc[t] += _colsum8(dc * taps[GDN_CONV - 1 - t])

        _row_chunks(ts, chunk)

        @pl.when(i == n - 1)
        def _():
            dw_ref[...] = jnp.sum(acc[...], axis=1)

    return pl.pallas_call(
        body, name=name, grid=(qkv_w // cw, n),
        out_shape=(jax.ShapeDtypeStruct((S, qkv_w), BF16), jax.ShapeDtypeStruct((GDN_CONV, qkv_w), F32)),
        in_specs=[pl.BlockSpec((ts, cw), lambda j, i: (i, j)),
                  pl.BlockSpec((HALO, cw), lambda j, i: (jnp.maximum(i * (ts // HALO) - 1, 0), j)),
                  pl.BlockSpec((GDN_CONV, cw), lambda j, i: (0, j)),
                  pl.BlockSpec((hb, ts, HD), lambda j, i: (j, i, 0))],
        out_specs=(pl.BlockSpec((ts, cw), lambda j, i: (i, j)), pl.BlockSpec((GDN_CONV, cw), lambda j, i: (0, j))),
        scratch_shapes=[pltpu.VMEM((ts + HALO, cw), F32), pltpu.VMEM((GDN_CONV, 8, cw), F32)],
        compiler_params=_params(),
    )(proj, proj, conv_w, dqkv_hm)


def _chunk_cumsum(x):
    row = lax.broadcasted_iota(jnp.int32, x.shape, 0) & (CHUNK - 1)
    s = 1
    while s < CHUNK:
        x = x + jnp.where(row >= s, pltpu.roll(x, s, axis=0), 0.0)
        s *= 2
    return x


def _chunk_rev_cumsum(x):
    rows = x.shape[0]
    row = lax.broadcasted_iota(jnp.int32, x.shape, 0) & (CHUNK - 1)
    s = 1
    while s < CHUNK:
        x = x + jnp.where(row < CHUNK - s, pltpu.roll(x, rows - s, axis=0), 0.0)
        s *= 2
    return x


def _softplus(x):
    return jnp.maximum(x, 0.0) + jnp.log1p(jnp.exp(-jnp.abs(x)))


def _gates(proj, ba_col, a_log_l, dt_bias_l, H, name):
    S = proj.shape[0]
    ts = _tile(S, 512, CHUNK)

    def body(x_ref, al_ref, dt_ref, o_ref):
        x = x_ref[...].astype(F32)
        lane = lax.broadcasted_iota(jnp.int32, x.shape, 1)
        g = -jnp.exp(al_ref[...]) * _softplus(x + dt_ref[...])
        G = _chunk_cumsum(jnp.where((lane >= H) & (lane < 2 * H), g, 0.0))
        o_ref[...] = jnp.where(lane < H, _sigmoid(x), G)

    return pl.pallas_call(
        body, name=name, out_shape=jax.ShapeDtypeStruct((S, LANE), F32), grid=(S // ts,),
        in_specs=[pl.BlockSpec((ts, LANE), lambda i: (i, ba_col)), pl.BlockSpec((1, LANE), lambda i: (0, 0)),
                  pl.BlockSpec((1, LANE), lambda i: (0, 0))],
        out_specs=pl.BlockSpec((ts, LANE), lambda i: (i, 0)), compiler_params=_params(),
    )(proj, a_log_l, dt_bias_l)


def _gates_bwd(proj, ba_col, ba_w, a_log_l, dt_bias_l, dbg, H, name):
    S = proj.shape[0]
    ts = _tile(S, 512, CHUNK)
    n = S // ts

    def body(x_ref, al_ref, dt_ref, d_ref, o_ref, dal_ref, ddt_ref, acc):
        i = pl.program_id(0)
        x = x_ref[...].astype(F32)
        d = d_ref[...]
        lane = lax.broadcasted_iota(jnp.int32, x.shape, 1)
        is_a = (lane >= H) & (lane < 2 * H)
        beta = _sigmoid(x)
        nea = -jnp.exp(al_ref[...])
        z = x + dt_ref[...]
        dg = _chunk_rev_cumsum(jnp.where(is_a, d, 0.0))
        da_raw = dg * nea * _sigmoid(z)
        o = jnp.where(lane < H, d * beta * (1.0 - beta), jnp.where(is_a, da_raw, 0.0))
        if ba_w > LANE:
            o = jnp.concatenate([o, jnp.zeros((ts, ba_w - LANE), F32)], axis=1)
        o_ref[...] = o.astype(BF16)

        @pl.when(i == 0)
        def _():
            acc[...] = jnp.zeros_like(acc)

        acc[0] += _colsum8(jnp.where(is_a, dg * nea * _softplus(z), 0.0))
        acc[1] += _colsum8(jnp.where(is_a, da_raw, 0.0))

        @pl.when(i == n - 1)
        def _():
            dal_ref[...] = jnp.sum(acc[0], axis=0, keepdims=True)
            ddt_ref[...] = jnp.sum(acc[1], axis=0, keepdims=True)

    vec = pl.BlockSpec((1, LANE), lambda i: (0, 0))
    return pl.pallas_call(
        body, name=name, grid=(n,),
        out_shape=(jax.ShapeDtypeStruct((S, ba_w), BF16), jax.ShapeDtypeStruct((1, LANE), F32),
                   jax.ShapeDtypeStruct((1, LANE), F32)),
        in_specs=[pl.BlockSpec((ts, LANE), lambda i: (i, ba_col)), vec, vec, pl.BlockSpec((ts, LANE), lambda i: (i, 0))],
        out_specs=(pl.BlockSpec((ts, ba_w), lambda i: (i, 0)), vec, vec),
        scratch_shapes=[pltpu.VMEM((2, 8, LANE), F32)], compiler_params=_params(),
    )(proj, a_log_l, dt_bias_l, dbg)


_BMM_FORMS = {"nn": "hik,hkj->hij", "nt": "hik,hjk->hij", "tn": "hki,hkj->hij"}


def _split_bf16(a):
    hi = a.astype(BF16)
    return hi, (a - hi.astype(F32)).astype(BF16)


def _bmm(a, b, form="nn", exact=False):
    e = lambda x, y: jnp.einsum(_BMM_FORMS[form], x, y, preferred_element_type=F32)
    if not exact:
        return e(a.astype(BF16), b.astype(BF16))
    (ah, al), (bh, bl) = _split_bf16(a), _split_bf16(b)
    return (e(ah, bl) + e(al, bh)) + e(ah, bh)


def _unit_lower_inverse(L, r, c):
    eye = (r == c).astype(F32)
    m = jnp.where((r >> 3) == (c >> 3), -L, 0.0)
    m2 = _bmm(m, m, exact=True)
    m4 = _bmm(m2, m2, exact=True)
    x = eye + m
    x = x + _bmm(x, m2, exact=True)
    x = x + _bmm(x, m4, exact=True)
    for sh in (3, 4, 5):
        off = ((r >> (sh + 1)) == (c >> (sh + 1))) & ((r >> sh) != (c >> sh))
        x = x - _bmm(x, _bmm(jnp.where(off, L, 0.0), x))
    return x


def _to_row(col, eye):
    return jnp.sum(jnp.where(eye, jnp.broadcast_to(col, eye.shape), 0.0), axis=1, keepdims=True)


def _to_col(rowv, eye):
    return jnp.sum(jnp.where(eye, jnp.broadcast_to(rowv, eye.shape), 0.0), axis=2, keepdims=True)


def _gdn_chunk(q, k, v, bg, H):
    shape = (H, CHUNK, CHUNK)
    r = lax.broadcasted_iota(jnp.int32, shape, 1)
    c = lax.broadcasted_iota(jnp.int32, shape, 2)
    eye, incl, strict = r == c, r >= c, r > c
    beta = jnp.stack([bg[:, h:h + 1] for h in range(H)], axis=0)
    G = jnp.stack([bg[:, H + h:H + h + 1] for h in range(H)], axis=0)
    gap = jnp.broadcast_to(G, shape) - _to_row(G, eye)
    decay = jnp.where(incl, jnp.exp(jnp.where(incl, gap, 0.0)), 0.0)
    kk = _bmm(k, k, "nt")
    L = jnp.where(strict, beta * decay * kk, 0.0)
    ainv = _unit_lower_inverse(L, r, c)
    eG = jnp.exp(G)
    u_v = _bmm(ainv, beta * v)
    w_k = _bmm(ainv, (beta * eG) * k)
    qk = _bmm(q, k, "nt", exact=True)
    GL = G[:, CHUNK - 1:CHUNK, :]
    ek = jnp.exp(GL - G)
    return dict(eye=eye, strict=strict, r=r, c=c, beta=beta, G=G, decay=decay, kk=kk, ainv=ainv, eG=eG,
                u_v=u_v, w_k=w_k, qk=qk, attn=decay * qk, GL=GL, ek=ek, cd=jnp.exp(GL))


def _chunk_steps(N):
    at = lambda step: lambda: pl.program_id(0) == step
    return at(0), at(N - max(N // 8, 1)), at(N - 1)


def _gdn_fwd(qkv_hm, bg, H, name, rider=None):
    S = qkv_hm.shape[1]
    N = S // CHUNK
    extra = rider.n if rider is not None else 0

    def body(q_ref, k_ref, v_ref, bg_ref, o_ref, st_ref, state):
        n = pl.program_id(0)

        @pl.when(n == 0)
        def _():
            state[...] = jnp.zeros_like(state)

        q, k, v = q_ref[...], k_ref[...], v_ref[...]
        t = _gdn_chunk(q, k, v, bg_ref[...], H)
        s0 = state[...]
        st_ref[0] = s0
        u = t["u_v"] - _bmm(t["w_k"], s0)
        o_ref[...] = _bmm(q * t["eG"], s0) + _bmm(t["attn"], u)
        state[...] = t["cd"] * s0 + _bmm(k * t["ek"], u, "tn")

    blk = lambda kind: pl.BlockSpec((H, CHUNK, HD), lambda n: (kind, n, 0))
    res = pl.pallas_call(
        _ride(body, 4, 2, rider, *_chunk_steps(N)), name=name, grid=(N,),
        out_shape=[jax.ShapeDtypeStruct((H, S, HD), F32), jax.ShapeDtypeStruct((N, H, HD, HD), F32)]
        + (rider.out_shape if rider else []),
        in_specs=[blk(0), blk(1), blk(2), pl.BlockSpec((CHUNK, LANE), lambda n: (n, 0))] + [ANY] * extra,
        out_specs=[pl.BlockSpec((H, CHUNK, HD), lambda n: (0, n, 0)),
                   pl.BlockSpec((1, H, HD, HD), lambda n: (n, 0, 0, 0))] + [ANY] * extra,
        scratch_shapes=[pltpu.VMEM((H, HD, HD), F32)] + (rider.scratch if rider else []), compiler_params=_params(),
    )(qkv_hm, qkv_hm, qkv_hm, bg, *(rider.arrays if rider else []))
    return res[0], res[1], res[2:]


def _gdn_bwd(qkv_hm, bg, states, do_hm, H, name, rider=None):
    S = qkv_hm.shape[1]
    N = S // CHUNK
    extra = rider.n if rider is not None else 0

    def body(q_ref, k_ref, v_ref, bg_ref, st_ref, do_ref, dqkv_ref, dbg_ref, dstate):
        n = pl.program_id(0)

        @pl.when(n == 0)
        def _():
            dstate[...] = jnp.zeros_like(dstate)

        q, k, v, do = q_ref[...], k_ref[...], v_ref[...], do_ref[...]
        t = _gdn_chunk(q, k, v, bg_ref[...], H)
        eye, beta, eG, decay, kk, ainv = t["eye"], t["beta"], t["eG"], t["decay"], t["kk"], t["ainv"]
        s0 = st_ref[0]
        ds1 = dstate[...]
        u = t["u_v"] - _bmm(t["w_k"], s0)
        qdec, kdec = q * eG, k * t["ek"]
        d_qdec = _bmm(do, s0, "nt")
        d_attn = _bmm(do, u, "nt")
        du = _bmm(t["attn"], do, "tn") + _bmm(kdec, ds1)
        d_cd = jnp.sum(jnp.sum(ds1 * s0, axis=2, keepdims=True), axis=1, keepdims=True)
        d_kdec = _bmm(u, ds1, "nt")
        d_wk = -_bmm(du, s0, "nt")
        dstate[...] = t["cd"] * ds1 + _bmm(qdec, do, "tn") - _bmm(t["w_k"], du, "tn")
        d_rv = _bmm(ainv, du, "tn")
        d_rk = _bmm(ainv, d_wk, "tn")
        dL = jnp.where(t["strict"], -(_bmm(d_rv, t["u_v"], "nt") + _bmm(d_rk, t["w_k"], "nt")), 0.0)
        rk_k = jnp.sum(d_rk * k, axis=2, keepdims=True)
        d_beta = (jnp.sum(dL * decay * kk, axis=2, keepdims=True) + jnp.sum(d_rv * v, axis=2, keepdims=True)
                  + rk_k * eG)
        d_decay = dL * beta * kk + d_attn * t["qk"]
        d_kk = dL * beta * decay
        d_qk = d_attn * decay
        dqkv_ref[pl.ds(2 * H, H)] = beta * d_rv
        dqkv_ref[pl.ds(0, H)] = _bmm(d_qk, k) + d_qdec * eG
        dqkv_ref[pl.ds(H, H)] = ((beta * eG) * d_rk + _bmm(d_kk, k) + _bmm(d_kk, k, "tn") + _bmm(d_qk, q, "tn")
                       + d_kdec * t["ek"])
        d_eG = rk_k * beta + jnp.sum(d_qdec * q, axis=2, keepdims=True)
        e = jnp.sum(d_kdec * kdec, axis=2, keepdims=True)
        T = d_decay * decay
        dG = d_eG * eG - e + jnp.sum(T, axis=2, keepdims=True) - _to_col(jnp.sum(T, axis=1, keepdims=True), eye)
        dGL = jnp.sum(e, axis=1, keepdims=True) + d_cd * t["cd"]
        row1 = lax.broadcasted_iota(jnp.int32, (H, CHUNK, 1), 1)
        dG = dG + jnp.where(row1 == CHUNK - 1, dGL, 0.0)
        lane = lax.broadcasted_iota(jnp.int32, (CHUNK, LANE), 1)
        out = jnp.zeros((CHUNK, LANE), F32)
        for h in range(H):
            out = out + jnp.where(lane == h, d_beta[h], 0.0) + jnp.where(lane == H + h, dG[h], 0.0)
        dbg_ref[...] = out

    blk = lambda kind: pl.BlockSpec((H, CHUNK, HD), lambda n: (kind, N - 1 - n, 0))
    res = pl.pallas_call(
        _ride(body, 6, 2, rider, *_chunk_steps(N)), name=name, grid=(N,),
        out_shape=[jax.ShapeDtypeStruct((3 * H, S, HD), F32), jax.ShapeDtypeStruct((S, LANE), F32)]
        + (rider.out_shape if rider else []),
        in_specs=[blk(0), blk(1), blk(2), pl.BlockSpec((CHUNK, LANE), lambda n: (N - 1 - n, 0)),
                  pl.BlockSpec((1, H, HD, HD), lambda n: (N - 1 - n, 0, 0, 0)), blk(0)] + [ANY] * extra,
        out_specs=[pl.BlockSpec((3 * H, CHUNK, HD), lambda n: (0, N - 1 - n, 0)),
                   pl.BlockSpec((CHUNK, LANE), lambda n: (N - 1 - n, 0))] + [ANY] * extra,
        scratch_shapes=[pltpu.VMEM((H, HD, HD), F32)] + (rider.scratch if rider else []), compiler_params=_params(),
    )(qkv_hm, qkv_hm, qkv_hm, bg, states, do_hm, *(rider.arrays if rider else []))
    return res[0], res[1], res[2:]


def _gdn_out(o_hm, proj, z_col, gdn_w, name):
    H, S, _ = o_hm.shape
    vw = H * HD
    ts = _tile(S, 256, 8)

    def body(o_ref, z_ref, w_ref, y_ref):
        z = z_ref[...].astype(F32)
        w = w_ref[...]
        parts = []
        for h in range(H):
            o = o_ref[h]
            parts.append(o * _rstd(o) * w)
        y_ref[...] = (jnp.concatenate(parts, axis=1) * _silu(z)).astype(BF16)

    return pl.pallas_call(
        body, name=name, out_shape=jax.ShapeDtypeStruct((S, vw), BF16), grid=(S // ts,),
        in_specs=[pl.BlockSpec((H, ts, HD), lambda i: (0, i, 0)), pl.BlockSpec((ts, vw), lambda i: (i, z_col)),
                  pl.BlockSpec((1, HD), lambda i: (0, 0))],
        out_specs=pl.BlockSpec((ts, vw), lambda i: (i, 0)), compiler_params=_params(),
    )(o_hm, proj, gdn_w)


def _gdn_out_bwd(o_hm, proj, z_col, gdn_w, dy, name):
    H, S, _ = o_hm.shape
    vw = H * HD
    ts = _tile(S, 256, 8)
    n = S // ts

    def body(o_ref, z_ref, w_ref, dy_ref, do_ref, dz_ref, dw_ref, acc):
        i = pl.program_id(0)
        z = z_ref[...].astype(F32)
        dy = dy_ref[...].astype(F32)
        w = w_ref[...]
        gz = dy * _silu(z)
        normed, dwr = [], jnp.zeros((ts, HD), F32)
        for h in range(H):
            o = o_ref[h]
            dxo, dwh = _rms_bwd(o, w, gz[:, HD * h:HD * (h + 1)])
            do_ref[h] = dxo
            dwr = dwr + dwh
            normed.append(o * _rstd(o) * w)
        dz_ref[...] = (dy * jnp.concatenate(normed, axis=1) * _dsilu(z)).astype(BF16)

        @pl.when(i == 0)
        def _():
            acc[...] = jnp.zeros_like(acc)

        acc[...] += _colsum8(dwr)

        @pl.when(i == n - 1)
        def _():
            dw_ref[...] = jnp.sum(acc[...], axis=0, keepdims=True)

    return pl.pallas_call(
        body, name=name, grid=(n,),
        out_shape=(jax.ShapeDtypeStruct((H, S, HD), F32), jax.ShapeDtypeStruct((S, vw), BF16),
                   jax.ShapeDtypeStruct((1, HD), F32)),
        in_specs=[pl.BlockSpec((H, ts, HD), lambda i: (0, i, 0)), pl.BlockSpec((ts, vw), lambda i: (i, z_col)),
                  pl.BlockSpec((1, HD), lambda i: (0, 0)), pl.BlockSpec((ts, vw), lambda i: (i, 0))],
        out_specs=(pl.BlockSpec((H, ts, HD), lambda i: (0, i, 0)), pl.BlockSpec((ts, vw), lambda i: (i, 0)),
                   pl.BlockSpec((1, HD), lambda i: (0, 0))),
        scratch_shapes=[pltpu.VMEM((8, HD), F32)], compiler_params=_params(),
    )(o_hm, proj, gdn_w, dy)


def _pool_trailing(ext, ts, pg, row0):
    outs, inv_cnts = [], []
    t_abs = row0 + lax.broadcasted_iota(jnp.int32, (ts, 1), 0)
    for gi, win in enumerate(POOL_WINDOWS):
        cols = pl.ds(gi * pg, pg)
        cur = ext[pl.ds(HALO, ts), cols]
        acc = cur
        for j in range(1, win):
            acc = acc + ext[pl.ds(HALO - j, ts), cols]
        inv = 1.0 / jnp.minimum(t_abs + 1, win).astype(F32)
        outs.append(acc * inv - cur)
    return outs


def _pool_fwd(proj, p_col, pool_w, pool_scale, pw, name):
    S = proj.shape[0]
    pg = pw // len(POOL_WINDOWS)
    ts = _tile(S, 512, HALO)

    def body(x_ref, h_ref, w_ref, sc_ref, o_ref, ext):
        i = pl.program_id(0)
        _fill_causal(ext, x_ref[...].astype(F32), h_ref[...].astype(F32), i)
        ys = _pool_trailing(ext, ts, pg, i * ts)
        outs = [jnp.dot(ys[gi].astype(BF16), w_ref[gi], preferred_element_type=F32) for gi in range(len(ys))]
        o_ref[...] = (jnp.concatenate(outs, axis=1) * sc_ref[...]).astype(BF16)

    return pl.pallas_call(
        body, name=name, out_shape=jax.ShapeDtypeStruct((S, pw), BF16), grid=(S // ts,),
        in_specs=[pl.BlockSpec((ts, pw), lambda i: (i, p_col)),
                  pl.BlockSpec((HALO, pw), lambda i: (jnp.maximum(i * (ts // HALO) - 1, 0), p_col)),
                  pl.BlockSpec((len(POOL_WINDOWS), pg, pg), lambda i: (0, 0, 0)), pl.BlockSpec((1, pw), lambda i: (0, 0))],
        out_specs=pl.BlockSpec((ts, pw), lambda i: (i, 0)),
        scratch_shapes=[pltpu.VMEM((ts + HALO, pw), F32)], compiler_params=_params(),
    )(proj, proj, pool_w, pool_scale)


def _pool_bwd(proj, p_col, pool_w, pool_scale, dpb, pw, name):
    S = proj.shape[0]
    G = len(POOL_WINDOWS)
    pg = pw // G
    ts = _tile(S, 512, HALO)
    n = S // ts

    def body(x_ref, h_ref, w_ref, sc_ref, d_ref, dn_ref, dp_ref, dw_ref, dsc_ref, ext, zext, wacc, sacc):
        i = pl.program_id(0)
        _fill_causal(ext, x_ref[...].astype(F32), h_ref[...].astype(F32), i)
        ys = _pool_trailing(ext, ts, pg, i * ts)
        d_ext = jnp.concatenate([d_ref[...].astype(F32), jnp.where(i < n - 1, dn_ref[...].astype(F32), 0.0)], axis=0)
        dt = d_ext * sc_ref[...]
        t_abs = i * ts + lax.broadcasted_iota(jnp.int32, (ts + HALO, 1), 0)

        @pl.when(i == 0)
        def _():
            wacc[...] = jnp.zeros_like(wacc)
            sacc[...] = jnp.zeros_like(sacc)

        dps, tfs = [], []
        for gi, win in enumerate(POOL_WINDOWS):
            cols = slice(gi * pg, (gi + 1) * pg)
            w = w_ref[gi]
            dt_g = dt[:, cols].astype(BF16)
            y_g = ys[gi].astype(BF16)
            tfs.append(jnp.dot(y_g, w, preferred_element_type=F32))
            wacc[gi] += lax.dot_general(y_g, dt_g[:ts], (((0,), (0,)), ((), ())), preferred_element_type=F32)
            dyp = lax.dot_general(dt_g, w, (((1,), (1,)), ((), ())), preferred_element_type=F32)
            zext[:, pl.ds(gi * pg, pg)] = dyp * (1.0 / jnp.minimum(t_abs + 1, win).astype(F32))
            acc = -dyp[:ts]
            for j in range(win):
                acc = acc + zext[pl.ds(j, ts), pl.ds(gi * pg, pg)]
            dps.append(acc)
        dp_ref[...] = jnp.concatenate(dps, axis=1).astype(BF16)
        sacc[...] += _colsum8(d_ext[:ts] * jnp.concatenate(tfs, axis=1))

        @pl.when(i == n - 1)
        def _():
            dw_ref[...] = wacc[...]
            dsc_ref[...] = jnp.sum(sacc[...], axis=0, keepdims=True)

    return pl.pallas_call(
        body, name=name, grid=(n,),
        out_shape=(jax.ShapeDtypeStruct((S, pw), BF16), jax.ShapeDtypeStruct((G, pg, pg), F32),
                   jax.ShapeDtypeStruct((1, pw), F32)),
        in_specs=[pl.BlockSpec((ts, pw), lambda i: (i, p_col)),
                  pl.BlockSpec((HALO, pw), lambda i: (jnp.maximum(i * (ts // HALO) - 1, 0), p_col)),
                  pl.BlockSpec((G, pg, pg), lambda i: (0, 0, 0)), pl.BlockSpec((1, pw), lambda i: (0, 0)),
                  pl.BlockSpec((ts, pw), lambda i: (i, 0)),
                  pl.BlockSpec((HALO, pw), lambda i: (jnp.minimum((i + 1) * (ts // HALO), S // HALO - 1), 0))],
        out_specs=(pl.BlockSpec((ts, pw), lambda i: (i, 0)), pl.BlockSpec((G, pg, pg), lambda i: (0, 0, 0)),
                   pl.BlockSpec((1, pw), lambda i: (0, 0))),
        scratch_shapes=[pltpu.VMEM((ts + HALO, pw), F32), pltpu.VMEM((ts + HALO, pw), F32),
                        pltpu.VMEM((G, pg, pg), F32), pltpu.VMEM((8, pw), F32)],
        compiler_params=_params(),
    )(proj, proj, pool_w, pool_scale, dpb, dpb)


def _merge(proj, ga_col, gb_col, ya, yb, name):
    S, d = ya.shape
    ts = _tile(S, 512, 16)

    def body(ga_ref, gb_ref, ya_ref, yb_ref, o_ref):
        o_ref[...] = (_sigmoid(ga_ref[...].astype(F32)) * ya_ref[...].astype(F32)
                      + _sigmoid(gb_ref[...].astype(F32)) * yb_ref[...].astype(F32)).astype(BF16)

    row = pl.BlockSpec((ts, d), lambda i: (i, 0))
    return pl.pallas_call(
        body, name=name, out_shape=jax.ShapeDtypeStruct((S, d), BF16), grid=(S // ts,),
        in_specs=[pl.BlockSpec((ts, d), lambda i: (i, ga_col)), pl.BlockSpec((ts, d), lambda i: (i, gb_col)), row, row],
        out_specs=row, compiler_params=_params(),
    )(proj, proj, ya, yb)


def _merge_bwd(proj, ga_col, gb_col, ya, yb, dm, name):
    S, d = ya.shape
    ts = _tile(S, 512, 16)

    def body(ga_ref, gb_ref, ya_ref, yb_ref, dm_ref, dya_ref, dyb_ref, dga_ref, dgb_ref):
        dmv = dm_ref[...].astype(F32)
        sa, sb = _sigmoid(ga_ref[...].astype(F32)), _sigmoid(gb_ref[...].astype(F32))
        dya_ref[...] = (dmv * sa).astype(BF16)
        dyb_ref[...] = (dmv * sb).astype(BF16)
        dga_ref[...] = (dmv * ya_ref[...].astype(F32) * sa * (1.0 - sa)).astype(BF16)
        dgb_ref[...] = (dmv * yb_ref[...].astype(F32) * sb * (1.0 - sb)).astype(BF16)

    row = pl.BlockSpec((ts, d), lambda i: (i, 0))
    o = jax.ShapeDtypeStruct((S, d), BF16)
    return pl.pallas_call(
        body, name=name, out_shape=(o, o, o, o), grid=(S // ts,),
        in_specs=[pl.BlockSpec((ts, d), lambda i: (i, ga_col)), pl.BlockSpec((ts, d), lambda i: (i, gb_col)), row, row, row],
        out_specs=(row, row, row, row), compiler_params=_params(),
    )(proj, proj, ya, yb, dm)


def _xattn_fwd(q, kv, name):
    S, d = q.shape
    M = kv.shape[0]
    hd = d // XA_HEADS
    ts = _tile(S, 512, 16)
    scale = hd ** -0.5

    def body(q_ref, k_ref, v_ref, o_ref):
        s = lax.dot_general(q_ref[...], k_ref[...], (((1,), (1,)), ((), ())), preferred_element_type=F32) * scale
        p = jnp.exp(s - jnp.max(s, axis=-1, keepdims=True))
        p = p / jnp.sum(p, axis=-1, keepdims=True)
        o_ref[...] = jnp.dot(p.astype(BF16), v_ref[...], preferred_element_type=F32).astype(BF16)

    return pl.pallas_call(
        body, name=name, out_shape=jax.ShapeDtypeStruct((S, d), BF16), grid=(S // ts, XA_HEADS),
        in_specs=[pl.BlockSpec((ts, hd), lambda i, h: (i, h)), pl.BlockSpec((M, hd), lambda i, h: (0, h)),
                  pl.BlockSpec((M, hd), lambda i, h: (0, XA_HEADS + h))],
        out_specs=pl.BlockSpec((ts, hd), lambda i, h: (i, h)), compiler_params=_params(),
    )(q, kv, kv)


def _xattn_bwd(q, kv, do, name):
    S, d = q.shape
    M = kv.shape[0]
    hd = d // XA_HEADS
    ts = _tile(S, 512, 16)
    n = S // ts
    scale = hd ** -0.5

    def body(q_ref, k_ref, v_ref, do_ref, dq_ref, dk_ref, dv_ref, kacc, vacc):
        i = pl.program_id(1)
        qv, kv_, vv, dov = q_ref[...], k_ref[...], v_ref[...], do_ref[...]
        s = lax.dot_general(qv, kv_, (((1,), (1,)), ((), ())), preferred_element_type=F32) * scale
        p = jnp.exp(s - jnp.max(s, axis=-1, keepdims=True))
        p = p / jnp.sum(p, axis=-1, keepdims=True)
        dp = lax.dot_general(dov, vv, (((1,), (1,)), ((), ())), preferred_element_type=F32)
        ds = (p * (dp - jnp.sum(p * dp, axis=-1, keepdims=True)) * scale).astype(BF16)
        dq_ref[...] = jnp.dot(ds, kv_, preferred_element_type=F32).astype(BF16)

        @pl.when(i == 0)
        def _():
            kacc[...] = jnp.zeros_like(kacc)
            vacc[...] = jnp.zeros_like(vacc)

        kacc[...] += lax.dot_general(ds, qv, (((0,), (0,)), ((), ())), preferred_element_type=F32)
        vacc[...] += lax.dot_general(p.astype(BF16), dov, (((0,), (0,)), ((), ())), preferred_element_type=F32)

        @pl.when(i == n - 1)
        def _():
            dk_ref[...] = kacc[...]
            dv_ref[...] = vacc[...]

    dq, dk, dv = pl.pallas_call(
        body, name=name, grid=(XA_HEADS, n),
        out_shape=(jax.ShapeDtypeStruct((S, d), BF16), jax.ShapeDtypeStruct((M, d), F32), jax.ShapeDtypeStruct((M, d), F32)),
        in_specs=[pl.BlockSpec((ts, hd), lambda h, i: (i, h)), pl.BlockSpec((M, hd), lambda h, i: (0, h)),
                  pl.BlockSpec((M, hd), lambda h, i: (0, XA_HEADS + h)), pl.BlockSpec((ts, hd), lambda h, i: (i, h))],
        out_specs=(pl.BlockSpec((ts, hd), lambda h, i: (i, h)), pl.BlockSpec((M, hd), lambda h, i: (0, h)),
                   pl.BlockSpec((M, hd), lambda h, i: (0, h))),
        scratch_shapes=[pltpu.VMEM((M, hd), F32), pltpu.VMEM((M, hd), F32)], compiler_params=_params(),
    )(q, kv, kv, do)
    return dq, jnp.concatenate([dk, dv], axis=1)


def _ffn_act(up, conv_w, bias, name):
    S, F2 = up.shape
    F = F2 // 2
    ts, cw = _tile(S, 512, HALO), _tile(F, 512)
    nb = F // cw

    def body(a_ref, ah_ref, b_ref, bh_ref, wa_ref, wb_ref, ba_ref, bb_ref, o_ref, ea, eb):
        i = pl.program_id(0)
        _fill_causal(ea, a_ref[...].astype(F32), ah_ref[...].astype(F32), i)
        _fill_causal(eb, b_ref[...].astype(F32), bh_ref[...].astype(F32), i)
        wa, wb, bia, bib = wa_ref[...], wb_ref[...], ba_ref[...], bb_ref[...]

        def chunk(r0):
            ua = _filter(wa, _causal_taps(ea, r0, FFN_CONV)) + bia
            ub = _filter(wb, _causal_taps(eb, r0, FFN_CONV)) + bib
            o_ref[pl.ds(r0, RC), :] = (_silu(ua) * ub).astype(BF16)

        _row_chunks(ts, chunk)

    tile = lambda c0: pl.BlockSpec((ts, cw), lambda i, j: (i, j + c0))
    vec = lambda rows, c0: pl.BlockSpec((rows, cw), lambda i, j: (0, j + c0))
    return pl.pallas_call(
        body, name=name, out_shape=jax.ShapeDtypeStruct((S, F), BF16), grid=(S // ts, nb),
        in_specs=[tile(0), _prev_halo_spec(ts, cw), tile(nb), _prev_halo_spec(ts, cw, nb),
                  vec(FFN_CONV, 0), vec(FFN_CONV, nb), vec(1, 0), vec(1, nb)],
        out_specs=pl.BlockSpec((ts, cw), lambda i, j: (i, j)),
        scratch_shapes=[pltpu.VMEM((ts + HALO, cw), F32), pltpu.VMEM((ts + HALO, cw), F32)],
        compiler_params=_params(),
    )(up, up, up, up, conv_w, conv_w, bias, bias)


def _ffn_act_bwd(up, conv_w, bias, dact, name):
    S, F2 = up.shape
    F = F2 // 2
    ts, cw = _tile(S, 512, HALO), _tile(F, 512)
    nb = F // cw
    n = S // ts

    def body(a_ref, ah_ref, b_ref, bh_ref, wa_ref, wb_ref, ba_ref, bb_ref, d_ref,
             du_ref, dwa_ref, dwb_ref, dba_ref, dbb_ref, ea, eb, wacc, bacc):
        i = pl.program_id(1)
        _fill_causal(ea, a_ref[...].astype(F32), ah_ref[...].astype(F32), i)
        _fill_causal(eb, b_ref[...].astype(F32), bh_ref[...].astype(F32), i)
        wa, wb, bia, bib = wa_ref[...], wb_ref[...], ba_ref[...], bb_ref[...]

        @pl.when(i == 0)
        def _():
            wacc[...] = jnp.zeros_like(wacc)
            bacc[...] = jnp.zeros_like(bacc)

        def chunk(r0):
            ta, tb = _causal_taps(ea, r0, FFN_CONV), _causal_taps(eb, r0, FFN_CONV)
            ua, ub = _filter(wa, ta) + bia, _filter(wb, tb) + bib
            d = d_ref[pl.ds(r0, RC), :].astype(F32)
            dua = d * ub * _dsilu(ua)
            dub = d * _silu(ua)
            du_ref[0, pl.ds(r0, RC), :] = dua.astype(BF16)
            du_ref[1, pl.ds(r0, RC), :] = dub.astype(BF16)
            for t in range(FFN_CONV):
                wacc[0, t] += _colsum8(dua * ta[FFN_CONV - 1 - t])
                wacc[1, t] += _colsum8(dub * tb[FFN_CONV - 1 - t])
            bacc[0] += _colsum8(dua)
            bacc[1] += _colsum8(dub)

        _row_chunks(ts, chunk)

        @pl.when(i == n - 1)
        def _():
            dwa_ref[...] = jnp.sum(wacc[0], axis=1)
            dwb_ref[...] = jnp.sum(wacc[1], axis=1)
            dba_ref[...] = jnp.sum(bacc[0], axis=0, keepdims=True)
            dbb_ref[...] = jnp.sum(bacc[1], axis=0, keepdims=True)

    tile = lambda c0: pl.BlockSpec((ts, cw), lambda j, i: (i, j + c0))
    halo = lambda c0: pl.BlockSpec((HALO, cw), lambda j, i: (jnp.maximum(i * (ts // HALO) - 1, 0), j + c0))
    vec = lambda rows, c0: pl.BlockSpec((rows, cw), lambda j, i: (0, j + c0))
    du, dwa, dwb, dba, dbb = pl.pallas_call(
        body, name=name, grid=(nb, n),
        out_shape=(jax.ShapeDtypeStruct((2, S, F), BF16),
                   jax.ShapeDtypeStruct((FFN_CONV, F), F32), jax.ShapeDtypeStruct((FFN_CONV, F), F32),
                   jax.ShapeDtypeStruct((1, F), F32), jax.ShapeDtypeStruct((1, F), F32)),
        in_specs=[tile(0), halo(0), tile(nb), halo(nb), vec(FFN_CONV, 0), vec(FFN_CONV, nb), vec(1, 0), vec(1, nb), tile(0)],
        out_specs=(pl.BlockSpec((2, ts, cw), lambda j, i: (0, i, j)), vec(FFN_CONV, 0), vec(FFN_CONV, 0), vec(1, 0),
                   vec(1, 0)),
        scratch_shapes=[pltpu.VMEM((ts + HALO, cw), F32), pltpu.VMEM((ts + HALO, cw), F32),
                        pltpu.VMEM((2, FFN_CONV, 8, cw), F32), pltpu.VMEM((2, 8, cw), F32)],
        compiler_params=_params(),
    )(up, up, up, up, conv_w, conv_w, bias, bias, dact)
    return du, jnp.concatenate([dwa, dwb], axis=1), jnp.concatenate([dba, dbb], axis=1)


def _adamw(gparts, w, m, v, name):
    R, C = w.shape
    tr = _tile(R, max(16, (256 * 1024) // C), 16)

    def body(g_ref, w_ref, m_ref, v_ref, go_ref, d_ref, mo_ref, vo_ref):
        g = g_ref[0].astype(F32)
        for s in range(1, N_DEV):
            g = g + g_ref[s].astype(F32)
        mn = ADAM_B1 * m_ref[...] + (1.0 - ADAM_B1) * g
        vn = ADAM_B2 * v_ref[...] + (1.0 - ADAM_B2) * (g * g)
        m_hat = mn / (1.0 - ADAM_B1 ** ADAM_STEP)
        v_hat = vn / (1.0 - ADAM_B2 ** ADAM_STEP)
        go_ref[...] = g
        d_ref[...] = -ADAM_LR * (m_hat / (jnp.sqrt(v_hat) + ADAM_EPS) + ADAM_WD * w_ref[...])
        mo_ref[...] = mn
        vo_ref[...] = vn

    row = pl.BlockSpec((tr, C), lambda i: (i, 0))
    o = jax.ShapeDtypeStruct((R, C), F32)
    return pl.pallas_call(
        body, name=name, out_shape=(o, o, o, o), grid=(R // tr,),
        in_specs=[pl.BlockSpec((N_DEV, tr, C), lambda i: (0, i, 0)), row, row, row],
        out_specs=(row, row, row, row), compiler_params=_params(),
    )(gparts, w, m, v)


def _position():
    return lax.axis_index("x"), lax.axis_index("y"), lax.axis_index("c")


class _Copies:
    def __init__(self, arrays):
        self.arrays, self.n = list(arrays), len(arrays)
        self.scratch = [pltpu.SemaphoreType.DMA((7 * self.n,)), pltpu.SemaphoreType.DMA((7 * self.n,)),
                        pltpu.SemaphoreType.DMA((self.n,))]


class _Gather(_Copies):
    def __init__(self, arrays):
        super().__init__(arrays)
        self.out_shape = [jax.ShapeDtypeStruct((N_DEV,) + b.shape, b.dtype) for b in self.arrays]

    def phases(self, x_refs, out_refs, send_sems, recv_sems, local_sems):
        n = self.n
        x, y, c = _position()
        me, sibling = (x, y, c), (x, y, 1 - c)
        chips = [(1 - x, y), (x, 1 - y), (1 - x, 1 - y)]

        def copy(a, k, blk, to, own=False):
            slot = out_refs[a].at[4 * blk[0] + 2 * blk[1] + blk[2]]
            return pltpu.make_async_remote_copy(
                src_ref=x_refs[a] if own else slot, dst_ref=slot,
                send_sem=send_sems.at[7 * a + k], recv_sem=recv_sems.at[7 * a + k], device_id=to, device_id_type=MESH)

        mine = [pltpu.make_async_copy(x_refs[a], out_refs[a].at[4 * x + 2 * y + c], local_sems.at[a]) for a in range(n)]
        first = [copy(a, 0, me, sibling, own=True) for a in range(n)]
        first += [copy(a, 1 + j, me, (*chip, c), own=True) for a in range(n) for j, chip in enumerate(chips)]
        passed = [copy(a, 4 + j, (*chip, c), sibling) for j, chip in enumerate(chips) for a in range(n)]

        def start():
            for cp in mine + first:
                cp.start()

        def hand_on():
            for j, chip in enumerate(chips):
                for a in range(n):
                    copy(a, 1 + j, (*chip, c), me).wait_recv()
                    passed[j * n + a].start()

        def finish():
            for a in range(n):
                copy(a, 0, sibling, me).wait_recv()
            for j, chip in enumerate(chips):
                for a in range(n):
                    copy(a, 4 + j, (*chip, 1 - c), me).wait_recv()
            for cp in first + passed:
                cp.wait_send()
            for cp in mine:
                cp.wait()

        return start, hand_on, finish


class _Exchange(_Copies):
    def __init__(self, arrays):
        super().__init__(arrays)
        self.out_shape = [jax.ShapeDtypeStruct(p.shape, p.dtype) for p in self.arrays]

    def phases(self, p_refs, out_refs, send_sems, recv_sems, local_sems):
        n = self.n
        x, y, c = _position()
        my_slot = 4 * x + 2 * y + c
        mine = [pltpu.make_async_copy(p_refs[a].at[my_slot], out_refs[a].at[my_slot], local_sems.at[a]) for a in range(n)]
        copies = []
        for k in range(1, N_DEV):
            px, py, pc = x ^ (k >> 2), y ^ ((k >> 1) & 1), c ^ (k & 1)
            for a in range(n):
                copies.append(pltpu.make_async_remote_copy(
                    src_ref=p_refs[a].at[4 * px + 2 * py + pc], dst_ref=out_refs[a].at[my_slot],
                    send_sem=send_sems.at[7 * a + k - 1], recv_sem=recv_sems.at[7 * a + k - 1],
                    device_id=(px, py, pc), device_id_type=MESH))

        def start():
            for cp in mine + copies:
                cp.start()

        def finish():
            for cp in copies + mine:
                cp.wait()

        return start, None, finish


def _communicate(copies, name):
    n = copies.n

    def body(*refs):
        start, hand_on, finish = copies.phases(refs[:n], refs[n:2 * n], *refs[2 * n:])
        start()
        if hand_on is not None:
            hand_on()
        finish()

    return pl.pallas_call(body, name=name, out_shape=copies.out_shape, in_specs=[ANY] * n, out_specs=[ANY] * n,
                          scratch_shapes=copies.scratch)(*copies.arrays)


def _col_pieces(col_map, shard_w):
    pieces = []
    for lo, hi, dst in col_map:
        c = lo
        while c < hi:
            j = c // shard_w
            end = min(hi, (j + 1) * shard_w)
            pieces.append((j, c - j * shard_w, end - c, dst + (c - lo)))
            c = end
    return pieces


def _assemble_cols(shards, pieces, width, name):
    _, R, Cs = shards.shape
    tr = _tile(R, 128, 16)

    def body(s_ref, o_ref):
        o_ref[...] = jnp.zeros(o_ref.shape, o_ref.dtype)
        for j, lo, n, dst in pieces:
            o_ref[:, dst:dst + n] = s_ref[j, :, lo:lo + n]

    return pl.pallas_call(
        body, name=name, out_shape=jax.ShapeDtypeStruct((R, width), shards.dtype), grid=(R // tr,),
        in_specs=[pl.BlockSpec((N_DEV, tr, Cs), lambda i: (0, i, 0))],
        out_specs=pl.BlockSpec((tr, width), lambda i: (i, 0)), compiler_params=_params(),
    )(shards)


def _split_cols(full, pieces, shard_w, name):
    R, width = full.shape
    tr = _tile(R, 128, 16)

    def body(f_ref, o_ref):
        for j, lo, n, dst in pieces:
            o_ref[j, :, lo:lo + n] = f_ref[:, dst:dst + n]

    return pl.pallas_call(
        body, name=name, out_shape=jax.ShapeDtypeStruct((N_DEV, R, shard_w), full.dtype), grid=(R // tr,),
        in_specs=[pl.BlockSpec((tr, width), lambda i: (i, 0))],
        out_specs=pl.BlockSpec((N_DEV, tr, shard_w), lambda i: (0, i, 0)), compiler_params=_params(),
    )(full)


def _pack(arrays, row_multiple=8):
    flat, layout, off = [], [], 0
    for a in arrays:
        n = a.size
        padded = -(-n // LANE) * LANE
        f = a.reshape(-1).astype(F32)
        if padded != n:
            f = jnp.pad(f, (0, padded - n))
        flat.append(f)
        layout.append((off, n, a.shape))
        off += padded
    total = -(-off // (LANE * row_multiple)) * (LANE * row_multiple)
    if total != off:
        flat.append(jnp.zeros((total - off,), F32))
    return jnp.concatenate(flat).reshape(total // LANE, LANE), layout


def _unpack(buf, layout):
    flat = buf.reshape(-1)
    return [flat[off:off + n].reshape(shape) for off, n, shape in layout]


def _cols_to_full(g):
    return jnp.transpose(g, (1, 0, 2)).reshape(g.shape[1], N_DEV * g.shape[2])


def _full_to_cols(a):
    return jnp.transpose(a.reshape(a.shape[0], N_DEV, a.shape[1] // N_DEV), (1, 0, 2))


def _rows_to_full(g):
    return g.reshape(N_DEV * g.shape[1], g.shape[2])


def _full_to_rows(a):
    return a.reshape(N_DEV, a.shape[0] // N_DEV, a.shape[1])


def _pad_cols(a, width):
    return a if a.shape[-1] == width else jnp.pad(a, [(0, 0)] * (a.ndim - 1) + [(0, width - a.shape[-1])])


SHARDED = ("w_in", "conv_qkv", "pool_w", "w_branch_a", "w_branch_b", "w_mix_out", "w_xq", "w_xkv", "w_xo", "w_up",
           "ffn_conv_w", "w_down")
REPLICATED = ("mix_pre_norm", "a_log", "dt_bias", "gdn_norm", "pool_scale", "mix_post_norm", "xa_pre_norm", "mem_norm",
              "xa_post_norm", "ffn_pre_norm", "ffn_conv_b", "ffn_post_norm")
WEIGHTS = ("mix_pre_norm", "w_in", "conv_qkv", "a_log", "dt_bias", "gdn_norm", "pool_w", "pool_scale", "w_branch_a",
           "w_branch_b", "w_mix_out", "mix_post_norm", "xa_pre_norm", "mem_norm", "w_xq", "w_xkv", "w_xo", "xa_post_norm",
           "ffn_pre_norm", "w_up", "ffn_conv_w", "ffn_conv_b", "w_down", "ffn_post_norm")
MATMUL_WEIGHTS = ("w_in", "w_branch_a", "w_branch_b", "w_mix_out", "w_xq", "w_xkv", "w_xo", "w_up", "w_down")
COL_SHARDED = ("w_in", "w_branch_b", "w_xkv", "w_up", "conv_qkv", "ffn_conv_w")
ROW_SHARDED = ("w_branch_a", "w_mix_out", "w_xq", "w_xo", "w_down")


class _Layout:
    def __init__(self, D, H, pw, F):
        self.D, self.H, self.pw, self.F = D, H, pw, F
        self.qkv_w, self.vw = 3 * H * HD, H * HD
        self.ba_w = 512 if D >= 2048 else LANE
        self.Fp = -(-F // 512) * 512 if F >= 512 else F
        q, vw = self.qkv_w, self.vw
        self.seg = dict(qkv=(0, q), z=(q, vw), ga=(q + vw, D), gb=(q + vw + D, D), p=(q + vw + 2 * D, pw),
                        ba=(q + vw + 2 * D + pw, self.ba_w))
        self.in_w = q + vw + 2 * D + pw + self.ba_w
        o_z, o_b = q, q + vw
        o_p = o_b + 2 * H
        o_ga = o_p + pw
        o_gb = o_ga + D
        self.d_in = o_gb + D
        self.in_map = [(0, o_z, self.seg["qkv"][0]), (o_z, o_b, self.seg["z"][0]), (o_b, o_p, self.seg["ba"][0]),
                       (o_p, o_ga, self.seg["p"][0]), (o_ga, o_gb, self.seg["ga"][0]), (o_gb, self.d_in, self.seg["gb"][0])]
        self.up_map = [(0, F, 0), (F, 2 * F, self.Fp)]

    def col(self, name, width):
        return self.seg[name][0] // width


def _local_step(x, mem, target, P, L, comm=None):
    D, H, pw, F, Fp = L.D, L.H, L.pw, L.F, L.Fp
    qkv_w, vw, ba_w = L.qkv_w, L.vw, L.ba_w
    col = L.col
    P = dict(P)
    win_p, cw3_p, fb_p = P["win_p"], P["cw3_p"], P["fb_p"]
    conv_qkv, pool_w = P["conv_qkv"], P["pool_w"]
    lanes = lambda vec: jnp.pad(vec.reshape(1, H).astype(F32), ((0, 0), (H, LANE - 2 * H)))
    a_log_l, dt_bias_l = lanes(P["a_log"]), lanes(P["dt_bias"])
    bf = lambda name: P[name]
    vecf = lambda name: P[name].reshape(1, -1).astype(F32)

    h1 = _prenorm(x, vecf("mix_pre_norm"), "mix_prenorm")
    proj = _matmul(h1, win_p, "nn", BF16, "in_proj", tn=768)
    qkv_hm = _qkv_conv(proj, conv_qkv, qkv_w, "qkv_conv")
    bg = _gates(proj, col("ba", LANE), a_log_l, dt_bias_l, H, "gates")
    o_hm, states, late = _gdn_fwd(qkv_hm, bg, H, "gdn_fwd", rider=comm.late_gather() if comm else None)
    if comm:
        P.update(comm.late_weights(late))
    wup_p, wdown_p = P["wup_p"], P["wdown_p"]
    oa = _gdn_out(o_hm, proj, col("z", vw), vecf("gdn_norm"), "gdn_out")
    ya = _matmul(oa, bf("w_branch_a"), "nn", BF16, "branch_a")
    pb = _pool_fwd(proj, col("p", pw), pool_w, vecf("pool_scale"), pw, "pool_fwd")
    yb = _matmul(pb, bf("w_branch_b"), "nn", BF16, "branch_b")
    merged = _merge(proj, col("ga", D), col("gb", D), ya, yb, "merge")
    y1 = _matmul(merged, bf("w_mix_out"), "nn", F32, "mix_out")
    x1, h2 = _post_pre(x, y1, vecf("mix_post_norm"), vecf("xa_pre_norm"), "mix_post")
    mn = _prenorm(mem, vecf("mem_norm"), "mem_norm")
    qx = _matmul(h2, bf("w_xq"), "nn", BF16, "xq")
    kv = _matmul(mn, bf("w_xkv"), "nn", BF16, "xkv")
    ox = _xattn_fwd(qx, kv, "xattn_fwd")
    y2 = _matmul(ox, bf("w_xo"), "nn", F32, "xo")
    x2, h3 = _post_pre(x1, y2, vecf("xa_post_norm"), vecf("ffn_pre_norm"), "xa_post")
    up = _matmul(h3, wup_p, "nn", BF16, "ffn_up")
    act = _ffn_act(up, cw3_p, fb_p, "ffn_act")
    y3 = _matmul(act, wdown_p, "nn", F32, "ffn_down")
    dx3, loss = _post_loss(x2, y3, vecf("ffn_post_norm"), target, "ffn_post_loss")

    g = {}
    dy3, g["ffn_post_norm"] = _post_bwd(y3, vecf("ffn_post_norm"), dx3, "ffn_post_bwd")
    dact = _matmul(dy3, wdown_p, "nt", BF16, "ffn_down_dx")
    g["w_down_p"] = _matmul(act, dy3, "tn", BF16, "ffn_down_dw")
    du, g["ffn_conv_w_p"], g["ffn_conv_b_p"] = _ffn_act_bwd(up, cw3_p, fb_p, dact, "ffn_act_bwd")
    dup = _conv_t(du, jnp.stack([cw3_p[:, :Fp], cw3_p[:, Fp:]]), "ffn_conv_t")
    dh3 = _matmul(dup, wup_p, "nt", F32, "ffn_up_dx")
    g["w_up_p"] = _matmul(h3, dup, "tn", BF16, "ffn_up_dw")
    dx2, g["ffn_pre_norm"] = _pre_bwd(x2, vecf("ffn_pre_norm"), dh3, dx3, "ffn_pre_bwd")
    dy2, g["xa_post_norm"] = _post_bwd(y2, vecf("xa_post_norm"), dx2, "xa_post_bwd")
    dox = _matmul(dy2, bf("w_xo"), "nt", BF16, "xo_dx")
    g["w_xo"] = _matmul(ox, dy2, "tn", BF16, "xo_dw")
    dqx, dkv = _xattn_bwd(qx, kv, dox, "xattn_bwd")
    dkv_b = dkv.astype(BF16)
    dh2 = _matmul(dqx, bf("w_xq"), "nt", F32, "xq_dx")
    g["w_xq"] = _matmul(h2, dqx, "tn", BF16, "xq_dw")
    dmn = _matmul(dkv_b, bf("w_xkv"), "nt", F32, "xkv_dx")
    g["w_xkv"] = _matmul(mn, dkv_b, "tn", BF16, "xkv_dw")
    _, g["mem_norm"] = _pre_bwd(mem, vecf("mem_norm"), dmn, jnp.zeros_like(mem), "mem_norm_bwd")
    dx1, g["xa_pre_norm"] = _pre_bwd(x1, vecf("xa_pre_norm"), dh2, dx2, "xa_pre_bwd")
    dy1, g["mix_post_norm"] = _post_bwd(y1, vecf("mix_post_norm"), dx1, "mix_post_bwd")
    dmerged = _matmul(dy1, bf("w_mix_out"), "nt", BF16, "mix_out_dx")
    g["w_mix_out"] = _matmul(merged, dy1, "tn", BF16, "mix_out_dw")
    dya, dyb, dga, dgb = _merge_bwd(proj, col("ga", D), col("gb", D), ya, yb, dmerged, "merge_bwd")
    doa = _matmul(dya, bf("w_branch_a"), "nt", BF16, "branch_a_dx")
    g["w_branch_a"] = _matmul(oa, dya, "tn", BF16, "branch_a_dw")
    dpb = _matmul(dyb, bf("w_branch_b"), "nt", BF16, "branch_b_dx")
    g["w_branch_b"] = _matmul(pb, dyb, "tn", BF16, "branch_b_dw")
    dp, g["pool_w"], g["pool_scale"] = _pool_bwd(proj, col("p", pw), pool_w, vecf("pool_scale"), dpb, pw, "pool_bwd")
    do_hm, dz, g["gdn_norm"] = _gdn_out_bwd(o_hm, proj, col("z", vw), vecf("gdn_norm"), doa, "gdn_out_bwd")
    dqkv_hm, dbg, got = _gdn_bwd(qkv_hm, bg, states, do_hm, H, "gdn_bwd", rider=comm.early_exchange(g) if comm else None)
    if comm:
        comm.receive(comm.EARLY_GRADS, got)
    dba, dal, ddt = _gates_bwd(proj, col("ba", LANE), ba_w, a_log_l, dt_bias_l, dbg, H, "gates_bwd")
    g["a_log"], g["dt_bias"] = dal[:, H:2 * H], ddt[:, H:2 * H]
    dc, g["conv_qkv"] = _qkv_conv_bwd(proj, conv_qkv, dqkv_hm, qkv_w, "qkv_conv_bwd")
    dqkv = _conv_t(dc[None], conv_qkv[None], "qkv_conv_t")[0]
    dproj = jnp.concatenate([dqkv, dz, dga, dgb, dp, dba], axis=1)
    g["w_in_p"] = _matmul(h1, dproj, "tn", BF16, "in_proj_dw", tn=768)
    if comm:
        dh1, got = _matmul(dproj, win_p, "nt", F32, "in_proj_dx", rider=comm.last_exchange(g))
        comm.receive(comm.LAST_GRADS, got)
    else:
        dh1 = _matmul(dproj, win_p, "nt", F32, "in_proj_dx")
    grad_x, g["mix_pre_norm"] = _pre_bwd(x, vecf("mix_pre_norm"), dh1, dx1, "mix_pre_bwd")
    return loss, grad_x, g


def _two_halves(a, F, Fp):
    return jnp.concatenate([_pad_cols(a[..., :F], Fp), _pad_cols(a[..., F:], Fp)], axis=-1)


def _from_halves(a, F, Fp):
    return jnp.concatenate([a[..., :F], a[..., Fp:Fp + F]], axis=-1)


class _StepComm:
    FIRST = ("w_in", "conv_qkv", "pool_w", "ffn_conv_w")
    LATE = ("w_branch_a", "w_branch_b", "w_mix_out", "w_xq", "w_xkv", "w_xo", "w_up", "w_down")
    EARLY_GRADS = ("pool_w", "w_branch_a", "w_branch_b", "w_mix_out", "w_xq", "w_xkv", "w_xo", "w_up", "ffn_conv_w",
                   "w_down")
    LAST_GRADS = ("w_in", "conv_qkv")

    def __init__(self, w, L):
        self.w, self.L = w, L
        self.in_pieces = _col_pieces(L.in_map, w["w_in"].shape[1])
        self.up_pieces = _col_pieces(L.up_map, w["w_up"].shape[1])
        self.received = {}

    def _shard(self, n):
        return self.w[n].astype(BF16) if n in MATMUL_WEIGHTS else self.w[n]

    def first_weights(self):
        L, (g, r, c) = self.L, self.w["pool_w"].shape
        G = dict(zip(self.FIRST, _communicate(_Gather([self._shard(n) for n in self.FIRST]), "gather_first")))
        return {"win_p": _assemble_cols(G["w_in"], self.in_pieces, L.in_w, "assemble_w_in"),
                "conv_qkv": _cols_to_full(G["conv_qkv"]),
                "cw3_p": _two_halves(_cols_to_full(G["ffn_conv_w"]), L.F, L.Fp),
                "pool_w": jnp.transpose(G["pool_w"], (1, 0, 2, 3)).reshape(g, N_DEV * r, c).astype(BF16)}

    def late_gather(self):
        return _Gather([self._shard(n) for n in self.LATE])

    def late_weights(self, results):
        L, G = self.L, dict(zip(self.LATE, results))
        P = {n: _rows_to_full(G[n]) for n in ("w_branch_a", "w_mix_out", "w_xq", "w_xo")}
        P.update({n: _cols_to_full(G[n]) for n in ("w_branch_b", "w_xkv")})
        P["wup_p"] = _assemble_cols(G["w_up"], self.up_pieces, 2 * L.Fp, "assemble_w_up")
        P["wdown_p"] = jnp.pad(_rows_to_full(G["w_down"]), ((0, L.Fp - L.F), (0, 0)))
        return P

    def _slices(self, g, n):
        L, w = self.L, self.w
        if n == "w_in":
            return _split_cols(g["w_in_p"], self.in_pieces, w["w_in"].shape[1], "split_w_in")
        if n == "w_up":
            return _split_cols(g["w_up_p"], self.up_pieces, w["w_up"].shape[1], "split_w_up")
        if n == "w_down":
            return _full_to_rows(g["w_down_p"][:L.F])
        if n == "ffn_conv_w":
            return _full_to_cols(_from_halves(g["ffn_conv_w_p"], L.F, L.Fp))
        if n == "pool_w":
            grp, r, c = w["pool_w"].shape
            return jnp.transpose(g[n].reshape(grp, N_DEV, r, c), (1, 0, 2, 3)).reshape(N_DEV, grp * r, c)
        return _full_to_rows(g[n]) if n in ROW_SHARDED else _full_to_cols(g[n])

    def early_exchange(self, g):
        return _Exchange([self._slices(g, n) for n in self.EARLY_GRADS])

    def last_exchange(self, g):
        return _Exchange([self._slices(g, n) for n in self.LAST_GRADS])

    def receive(self, names, results):
        self.received.update(zip(names, results))


def kernel(x, mem, mix_pre_norm, w_in, conv_qkv, a_log, dt_bias, gdn_norm, pool_w, pool_scale, w_branch_a, w_branch_b, w_mix_out, mix_post_norm, xa_pre_norm, mem_norm, w_xq, w_xkv, w_xo, xa_post_norm, ffn_pre_norm, w_up, ffn_conv_w, ffn_conv_b, w_down, ffn_post_norm, loss_target, m_mix_pre_norm, m_w_in, m_conv_qkv, m_a_log, m_dt_bias, m_gdn_norm, m_pool_w, m_pool_scale, m_w_branch_a, m_w_branch_b, m_w_mix_out, m_mix_post_norm, m_xa_pre_norm, m_mem_norm, m_w_xq, m_w_xkv, m_w_xo, m_xa_post_norm, m_ffn_pre_norm, m_w_up, m_ffn_conv_w, m_ffn_conv_b, m_w_down, m_ffn_post_norm, v_mix_pre_norm, v_w_in, v_conv_qkv, v_a_log, v_dt_bias, v_gdn_norm, v_pool_w, v_pool_scale, v_w_branch_a, v_w_branch_b, v_w_mix_out, v_mix_post_norm, v_xa_pre_norm, v_mem_norm, v_w_xq, v_w_xkv, v_w_xo, v_xa_post_norm, v_ffn_pre_norm, v_w_up, v_ffn_conv_w, v_ffn_conv_b, v_w_down, v_ffn_post_norm):
    given = dict(locals())
    w = {n: given[n][0] for n in WEIGHTS}
    m = {n: given["m_" + n][0] for n in WEIGHTS}
    v = {n: given["v_" + n][0] for n in WEIGHTS}
    D = x.shape[-1]
    F = w["w_down"].shape[0] * N_DEV
    L = _Layout(D, w["a_log"].shape[-1], w["pool_scale"].shape[-1], F)
    Fp = L.Fp

    comm = _StepComm(w, L)
    P = {n: w[n] for n in REPLICATED}
    P.update(comm.first_weights())
    P["fb_p"] = _two_halves(w["ffn_conv_b"].reshape(1, 2 * F), F, Fp)
    loss, grad_x, g = _local_step(x[0], mem[0], loss_target[0], P, L, comm)

    received = comm.received
    outs = {}
    for n in SHARDED:
        as2d = lambda a: a.reshape(-1, a.shape[-1])
        res = _adamw(received[n], as2d(w[n]), as2d(m[n]), as2d(v[n]), "adamw_" + n)
        outs[n] = [r.reshape(w[n].shape) for r in res]

    g["ffn_conv_b"] = _from_halves(g["ffn_conv_b_p"], F, Fp)
    rep_parts, rep_layout = _pack([g[n].reshape(w[n].shape) for n in REPLICATED] + [loss])
    rep_all, = _communicate(_Gather([rep_parts]), "gather_small_grads")
    zero_loss = jnp.zeros_like(loss)
    wr, _ = _pack([w[n] for n in REPLICATED] + [zero_loss])
    mr, _ = _pack([m[n] for n in REPLICATED] + [zero_loss])
    vr, _ = _pack([v[n] for n in REPLICATED] + [zero_loss])
    outs_rep = [_unpack(o, rep_layout) for o in _adamw(rep_all, wr, mr, vr, "adamw_replicated")]
    loss_total = outs_rep[0][-1][0, 0]
    for i, n in enumerate(REPLICATED):
        outs[n] = [outs_rep[k][i] for k in range(4)]

    result = [loss_total, grad_x[None]]
    for k in range(4):
        for n in WEIGHTS:
            result.append(outs[n][k][None])
    return tuple(result)
```

```python
import functools

import jax
import jax.numpy as jnp
from jax import lax
from jax.experimental import pallas as pl
from jax.experimental.pallas import tpu as pltpu

F32, BF16 = jnp.float32, jnp.bfloat16
MESH = pl.DeviceIdType.MESH
ANY = pl.BlockSpec(memory_space=pl.ANY)

N_DEV = 8
EPS = 1e-6
CHUNK = 64
HD = 128
GDN_CONV = 4
FFN_CONV = 3
POOL_WINDOWS = (2, 4, 8, 16)
XA_HEADS = 4
HALO = 16
RC = 128
LANE = 128
VMEM_LIMIT = 48 * 1024 * 1024

ADAM_LR, ADAM_B1, ADAM_B2, ADAM_EPS, ADAM_WD, ADAM_STEP = 0.001, 0.9, 0.999, 1e-08, 0.01, 10


def _tile(n, pref, align=LANE):
    best = None
    t = align
    while t <= min(n, pref):
        if n % t == 0:
            best = t
        t += align
    return best if best is not None else n


def _params(**kw):
    return pltpu.CompilerParams(vmem_limit_bytes=VMEM_LIMIT, **kw)


def _sigmoid(x):
    return 1.0 / (1.0 + jnp.exp(-x))


def _silu(x):
    return x * _sigmoid(x)


def _dsilu(x):
    s = _sigmoid(x)
    return s * (1.0 + x * (1.0 - s))


def _colsum8(t):
    return t.reshape(t.shape[0] // 8, 8, t.shape[1]).sum(axis=0)


def _ride(body, n_in, n_out, rider, first, middle, last):
    if rider is None:
        return body
    n = rider.n

    def wrapped(*refs):
        ins, r_in = refs[:n_in], refs[n_in:n_in + n]
        outs, r_out = refs[n_in + n:n_in + n + n_out], refs[n_in + n + n_out:n_in + 2 * n + n_out]
        rest = refs[n_in + 2 * n + n_out:]
        start, hand_on, finish = rider.phases(r_in, r_out, *rest[-3:])
        pl.when(first())(start)
        body(*ins, *outs, *rest[:-3])
        if hand_on is not None:
            pl.when(middle())(hand_on)
        pl.when(last())(finish)

    return wrapped


def _matmul(a, b, mode, out_dtype, name, tm=1024, tn=1024, tk=2816, rider=None):
    ga = a.shape[0] if (mode == "nt" and a.ndim == 3) else 1
    gb = b.shape[0] if (mode == "tn" and b.ndim == 3) else 1
    if mode == "nn":
        (M, K), (K2, N) = a.shape, b.shape
    elif mode == "nt":
        M, K = a.shape[-2], ga * a.shape[-1]
        N, K2 = b.shape
    else:
        K, M = a.shape
        K2, N = b.shape[-2], gb * b.shape[-1]
    assert K == K2, (name, a.shape, b.shape)
    tm, tn = _tile(M, tm), _tile(N // gb, tn)
    tk = K // ga if K // ga <= tk else _tile(K // ga, tk)
    nk = K // tk
    kpg, npg = K // ga // tk, N // gb // tn
    if mode == "nn":
        a_spec = pl.BlockSpec((tm, tk), lambda i, j, k: (i, k))
        b_spec = pl.BlockSpec((tk, tn), lambda i, j, k: (k, j))
        dims = (((1,), (0,)), ((), ()))
    elif mode == "nt":
        a_spec = (pl.BlockSpec((tm, tk), lambda i, j, k: (i, k)) if a.ndim == 2 else
                  pl.BlockSpec((None, tm, tk), lambda i, j, k: (k // kpg, i, k % kpg)))
        b_spec = pl.BlockSpec((tn, tk), lambda i, j, k: (j, k))
        dims = (((1,), (1,)), ((), ()))
    else:
        a_spec = pl.BlockSpec((tk, tm), lambda i, j, k: (k, i))
        b_spec = (pl.BlockSpec((tk, tn), lambda i, j, k: (k, j)) if b.ndim == 2 else
                  pl.BlockSpec((None, tk, tn), lambda i, j, k: (j // npg, k, j % npg)))
        dims = (((0,), (0,)), ((), ()))

    def body(a_ref, b_ref, o_ref, acc):
        part = lax.dot_general(a_ref[...], b_ref[...], dims, preferred_element_type=F32)
        if nk == 1:
            o_ref[...] = part.astype(o_ref.dtype)
        else:
            k = pl.program_id(2)

            @pl.when(k == 0)
            def _():
                acc[...] = part

            @pl.when(k > 0)
            def _():
                acc[...] += part

            @pl.when(k == nk - 1)
            def _():
                o_ref[...] = acc[...].astype(o_ref.dtype)

    grid = (M // tm, N // tn, nk)
    at = lambda step: lambda: ((pl.program_id(0) == step[0]) & (pl.program_id(1) == step[1])
                               & (pl.program_id(2) == step[2]))
    extra = rider.n if rider is not None else 0
    res = pl.pallas_call(
        _ride(body, 2, 1, rider, at((0, 0, 0)), at((grid[0] // 2, 0, 0)), at((grid[0] - 1, grid[1] - 1, nk - 1))),
        name=name, out_shape=[jax.ShapeDtypeStruct((M, N), out_dtype)] + (rider.out_shape if rider else []),
        grid=grid, in_specs=[a_spec, b_spec] + [ANY] * extra,
        out_specs=[pl.BlockSpec((tm, tn), lambda i, j, k: (i, j))] + [ANY] * extra,
        scratch_shapes=[pltpu.VMEM((tm, tn) if nk > 1 else (8, LANE), F32)] + (rider.scratch if rider else []),
        compiler_params=_params(dimension_semantics=("arbitrary",) * 3 if rider else ("parallel", "parallel", "arbitrary")),
    )(a, b, *(rider.arrays if rider else []))
    return (res[0], res[1:]) if rider else res[0]


def _rstd(xf):
    return lax.rsqrt(jnp.mean(xf * xf, axis=-1, keepdims=True) + EPS)


def _rms_bwd(xf, w, dy):
    r = _rstd(xf)
    g = dy * w
    dx = r * g - xf * (r * r * r) * jnp.mean(g * xf, axis=-1, keepdims=True)
    return dx, dy * xf * r


def _row_tile(rows):
    return _tile(rows, 256, 8)


def _prenorm(x, w, name):
    rows, d = x.shape
    ts = _row_tile(rows)

    def body(x_ref, w_ref, h_ref):
        xf = x_ref[...]
        h_ref[...] = (xf * _rstd(xf) * w_ref[...]).astype(BF16)

    return pl.pallas_call(
        body, name=name, out_shape=jax.ShapeDtypeStruct((rows, d), BF16), grid=(rows // ts,),
        in_specs=[pl.BlockSpec((ts, d), lambda i: (i, 0)), pl.BlockSpec((1, d), lambda i: (0, 0))],
        out_specs=pl.BlockSpec((ts, d), lambda i: (i, 0)), compiler_params=_params(),
    )(x, w)


def _post_pre(xres, y, w_post, w_pre, name):
    rows, d = xres.shape
    ts = _row_tile(rows)

    def body(x_ref, y_ref, wp_ref, wn_ref, xo_ref, h_ref):
        yf = y_ref[...]
        xn = x_ref[...] + yf * _rstd(yf) * wp_ref[...]
        xo_ref[...] = xn
        h_ref[...] = (xn * _rstd(xn) * wn_ref[...]).astype(BF16)

    row = pl.BlockSpec((ts, d), lambda i: (i, 0))
    vec = pl.BlockSpec((1, d), lambda i: (0, 0))
    return pl.pallas_call(
        body, name=name, grid=(rows // ts,),
        out_shape=(jax.ShapeDtypeStruct((rows, d), F32), jax.ShapeDtypeStruct((rows, d), BF16)),
        in_specs=[row, row, vec, vec], out_specs=(row, row), compiler_params=_params(),
    )(xres, y, w_post, w_pre)


def _post_loss(xres, y, w_post, target, name):
    rows, d = xres.shape
    ts = _row_tile(rows)
    n = rows // ts

    def body(x_ref, y_ref, wp_ref, t_ref, dx_ref, loss_ref, acc):
        i = pl.program_id(0)
        yf = y_ref[...]
        diff = x_ref[...] + yf * _rstd(yf) * wp_ref[...] - t_ref[...]
        dx_ref[...] = diff * (1.0 / d)

        @pl.when(i == 0)
        def _():
            acc[...] = jnp.zeros_like(acc)

        acc[...] += _colsum8(diff * diff)

        @pl.when(i == n - 1)
        def _():
            loss_ref[...] = jnp.broadcast_to((0.5 / d) * jnp.sum(acc[...]), loss_ref.shape)

    row = pl.BlockSpec((ts, d), lambda i: (i, 0))
    vec = pl.BlockSpec((1, d), lambda i: (0, 0))
    return pl.pallas_call(
        body, name=name, grid=(n,),
        out_shape=(jax.ShapeDtypeStruct((rows, d), F32), jax.ShapeDtypeStruct((1, LANE), F32)),
        in_specs=[row, row, vec, row], out_specs=(row, pl.BlockSpec((1, LANE), lambda i: (0, 0))),
        scratch_shapes=[pltpu.VMEM((8, d), F32)], compiler_params=_params(),
    )(xres, y, w_post, target)


def _post_bwd(y, w_post, dxn, name):
    rows, d = y.shape
    ts = _row_tile(rows)
    n = rows // ts

    def body(y_ref, w_ref, d_ref, dy_ref, dw_ref, acc):
        i = pl.program_id(0)
        dy, dwr = _rms_bwd(y_ref[...], w_ref[...], d_ref[...])
        dy_ref[...] = dy.astype(BF16)

        @pl.when(i == 0)
        def _():
            acc[...] = jnp.zeros_like(acc)

        acc[...] += _colsum8(dwr)

        @pl.when(i == n - 1)
        def _():
            dw_ref[...] = jnp.sum(acc[...], axis=0, keepdims=True)

    row = pl.BlockSpec((ts, d), lambda i: (i, 0))
    vec = pl.BlockSpec((1, d), lambda i: (0, 0))
    return pl.pallas_call(
        body, name=name, grid=(n,),
        out_shape=(jax.ShapeDtypeStruct((rows, d), BF16), jax.ShapeDtypeStruct((1, d), F32)),
        in_specs=[row, vec, row], out_specs=(row, vec),
        scratch_shapes=[pltpu.VMEM((8, d), F32)], compiler_params=_params(),
    )(y, w_post, dxn)


def _pre_bwd(x, w_pre, dh, dres, name):
    rows, d = x.shape
    ts = _row_tile(rows)
    n = rows // ts

    def body(x_ref, w_ref, dh_ref, dr_ref, dx_ref, dw_ref, acc):
        i = pl.program_id(0)
        dx, dwr = _rms_bwd(x_ref[...], w_ref[...], dh_ref[...].astype(F32))
        dx_ref[...] = dr_ref[...] + dx

        @pl.when(i == 0)
        def _():
            acc[...] = jnp.zeros_like(acc)

        acc[...] += _colsum8(dwr)

        @pl.when(i == n - 1)
        def _():
            dw_ref[...] = jnp.sum(acc[...], axis=0, keepdims=True)

    row = pl.BlockSpec((ts, d), lambda i: (i, 0))
    vec = pl.BlockSpec((1, d), lambda i: (0, 0))
    return pl.pallas_call(
        body, name=name, grid=(n,),
        out_shape=(jax.ShapeDtypeStruct((rows, d), F32), jax.ShapeDtypeStruct((1, d), F32)),
        in_specs=[row, vec, row, row], out_specs=(row, vec),
        scratch_shapes=[pltpu.VMEM((8, d), F32)], compiler_params=_params(),
    )(x, w_pre, dh, dres)


def _prev_halo_spec(ts, cw, col0=0):
    return pl.BlockSpec((HALO, cw), lambda i, j: (jnp.maximum(i * (ts // HALO) - 1, 0), j + col0))


def _fill_causal(ext, tile_f32, halo_f32, i):
    ext[pl.ds(0, HALO), :] = jnp.where(i > 0, halo_f32, 0.0)
    ext[pl.ds(HALO, tile_f32.shape[0]), :] = tile_f32


def _row_chunks(ts, chunk):
    assert ts % RC == 0, (ts, RC)

    def step(c, carry):
        chunk(pl.multiple_of(c * RC, RC))
        return carry

    lax.fori_loop(0, ts // RC, step, 0)


def _causal_taps(ext, r0, K):
    blk = ext[pl.ds(r0 + HALO - 8, RC + 8), :]
    return [blk[8 - j:8 - j + RC] for j in range(K)]


def _advanced_taps(ext, r0, K):
    blk = ext[pl.ds(r0, RC + 8), :]
    return [blk[j:j + RC] for j in range(K)]


def _filter(wv, taps):
    K = len(taps)
    acc = wv[K - 1:K, :] * taps[0]
    for t in range(K - 1):
        acc = acc + wv[t:t + 1, :] * taps[K - 1 - t]
    return acc


def _conv_t(dc, w, name, col_tile=512):
    G, S, C = dc.shape
    K = w.shape[1]
    ts, cw = _tile(S, 512, HALO), _tile(C, col_tile)
    n = S // ts

    def body(d_ref, nx_ref, w_ref, o_ref, ext):
        i = pl.program_id(1)
        ext[pl.ds(0, ts), :] = d_ref[...].astype(F32)
        ext[pl.ds(ts, HALO), :] = jnp.where(i < n - 1, nx_ref[...].astype(F32), 0.0)
        wv = w_ref[...]

        def chunk(r0):
            taps = _advanced_taps(ext, r0, K)
            acc = wv[K - 1:K, :] * taps[0]
            for j in range(K - 1):
                acc = acc + wv[j:j + 1, :] * taps[K - 1 - j]
            o_ref[pl.ds(r0, RC), :] = acc.astype(o_ref.dtype)

        _row_chunks(ts, chunk)

    return pl.pallas_call(
        body, name=name, out_shape=jax.ShapeDtypeStruct((G, S, C), BF16), grid=(G, n, C // cw),
        in_specs=[pl.BlockSpec((None, ts, cw), lambda g, i, j: (g, i, j)),
                  pl.BlockSpec((None, HALO, cw),
                               lambda g, i, j: (g, jnp.minimum((i + 1) * (ts // HALO), S // HALO - 1), j)),
                  pl.BlockSpec((None, K, cw), lambda g, i, j: (g, 0, j))],
        out_specs=pl.BlockSpec((None, ts, cw), lambda g, i, j: (g, i, j)),
        scratch_shapes=[pltpu.VMEM((ts + HALO, cw), F32)], compiler_params=_params(),
    )(dc, dc, w)


def _qkv_conv(proj, conv_w, qkv_w, name):
    S = proj.shape[0]
    H3 = qkv_w // HD
    H = H3 // 3
    hb = 4 if H % 4 == 0 else 1
    cw = hb * HD
    ts = _tile(S, 512, HALO)
    per_kind = H // hb

    def body(x_ref, h_ref, w_ref, o_ref, ext):
        i, j = pl.program_id(0), pl.program_id(1)
        _fill_causal(ext, x_ref[...].astype(F32), h_ref[...].astype(F32), i)
        wv = w_ref[...]
        kind = j // per_kind
        scale = jnp.where(kind == 0, HD ** -0.5, 1.0)

        def chunk(r0):
            s = _silu(_filter(wv, _causal_taps(ext, r0, GDN_CONV)))
            for a in range(hb):
                sa = s[:, HD * a:HD * (a + 1)]
                r = lax.rsqrt(jnp.sum(sa * sa, axis=-1, keepdims=True) + EPS)
                o_ref[a, pl.ds(r0, RC), :] = jnp.where(kind == 2, sa, sa * r * scale)

        _row_chunks(ts, chunk)

    return pl.pallas_call(
        body, name=name, out_shape=jax.ShapeDtypeStruct((H3, S, HD), F32), grid=(S // ts, qkv_w // cw),
        in_specs=[pl.BlockSpec((ts, cw), lambda i, j: (i, j)), _prev_halo_spec(ts, cw),
                  pl.BlockSpec((GDN_CONV, cw), lambda i, j: (0, j))],
        out_specs=pl.BlockSpec((hb, ts, HD), lambda i, j: (j, i, 0)),
        scratch_shapes=[pltpu.VMEM((ts + HALO, cw), F32)], compiler_params=_params(),
    )(proj, proj, conv_w)


def _qkv_conv_bwd(proj, conv_w, dqkv_hm, qkv_w, name):
    S = proj.shape[0]
    H = qkv_w // HD // 3
    hb = 4 if H % 4 == 0 else 1
    cw = hb * HD
    ts = _tile(S, 512, HALO)
    n = S // ts
    per_kind = H // hb

    def body(x_ref, h_ref, w_ref, d_ref, dc_ref, dw_ref, ext, acc):
        j, i = pl.program_id(0), pl.program_id(1)
        _fill_causal(ext, x_ref[...].astype(F32), h_ref[...].astype(F32), i)
        wv = w_ref[...]
        kind = j // per_kind
        scale = jnp.where(kind == 0, HD ** -0.5, 1.0)

        @pl.when(i == 0)
        def _():
            acc[...] = jnp.zeros_like(acc)

        def chunk(r0):
            taps = _causal_taps(ext, r0, GDN_CONV)
            c = _filter(wv, taps)
            s = _silu(c)
            parts = []
            for a in range(hb):
                sa = s[:, HD * a:HD * (a + 1)]
                dy = d_ref[a, pl.ds(r0, RC), :]
                r = lax.rsqrt(jnp.sum(sa * sa, axis=-1, keepdims=True) + EPS)
                dn = scale * (r * dy - sa * (r * r * r) * jnp.sum(dy * sa, axis=-1, keepdims=True))
                parts.append(jnp.where(kind == 2, dy, dn))
            dc = jnp.concatenate(parts, axis=1) * _dsilu(c)
            dc_ref[pl.ds(r0, RC), :] = dc.astype(BF16)
            for t in range(GDN_CONV):
                acc[t] += _colsum8(dc * taps[GDN_CONV - 1 - t])

        _row_chunks(ts, chunk)

        @pl.when(i == n - 1)
        def _():
            dw_ref[...] = jnp.sum(acc[...], axis=1)

    return pl.pallas_call(
        body, name=name, grid=(qkv_w // cw, n),
        out_shape=(jax.ShapeDtypeStruct((S, qkv_w), BF16), jax.ShapeDtypeStruct((GDN_CONV, qkv_w), F32)),
        in_specs=[pl.BlockSpec((ts, cw), lambda j, i: (i, j)),
                  pl.BlockSpec((HALO, cw), lambda j, i: (jnp.maximum(i * (ts // HALO) - 1, 0), j)),
                  pl.BlockSpec((GDN_CONV, cw), lambda j, i: (0, j)),
                  pl.BlockSpec((hb, ts, HD), lambda j, i: (j, i, 0))],
        out_specs=(pl.BlockSpec((ts, cw), lambda j, i: (i, j)), pl.BlockSpec((GDN_CONV, cw), lambda j, i: (0, j))),
        scratch_shapes=[pltpu.VMEM((ts + HALO, cw), F32), pltpu.VMEM((GDN_CONV, 8, cw), F32)],
        compiler_params=_params(),
    )(proj, proj, conv_w, dqkv_hm)


def _chunk_cumsum(x):
    row = lax.broadcasted_iota(jnp.int32, x.shape, 0) & (CHUNK - 1)
    s = 1
    while s < CHUNK:
        x = x + jnp.where(row >= s, pltpu.roll(x, s, axis=0), 0.0)
        s *= 2
    return x


def _chunk_rev_cumsum(x):
    rows = x.shape[0]
    row = lax.broadcasted_iota(jnp.int32, x.shape, 0) & (CHUNK - 1)
    s = 1
    while s < CHUNK:
        x = x + jnp.where(row < CHUNK - s, pltpu.roll(x, rows - s, axis=0), 0.0)
        s *= 2
    return x


def _softplus(x):
    return jnp.maximum(x, 0.0) + jnp.log1p(jnp.exp(-jnp.abs(x)))


def _gates(proj, ba_col, a_log_l, dt_bias_l, H, name):
    S = proj.shape[0]
    ts = _tile(S, 512, CHUNK)

    def body(x_ref, al_ref, dt_ref, o_ref):
        x = x_ref[...].astype(F32)
        lane = lax.broadcasted_iota(jnp.int32, x.shape, 1)
        g = -jnp.exp(al_ref[...]) * _softplus(x + dt_ref[...])
        G = _chunk_cumsum(jnp.where((lane >= H) & (lane < 2 * H), g, 0.0))
        o_ref[...] = jnp.where(lane < H, _sigmoid(x), G)

    return pl.pallas_call(
        body, name=name, out_shape=jax.ShapeDtypeStruct((S, LANE), F32), grid=(S // ts,),
        in_specs=[pl.BlockSpec((ts, LANE), lambda i: (i, ba_col)), pl.BlockSpec((1, LANE), lambda i: (0, 0)),
                  pl.BlockSpec((1, LANE), lambda i: (0, 0))],
        out_specs=pl.BlockSpec((ts, LANE), lambda i: (i, 0)), compiler_params=_params(),
    )(proj, a_log_l, dt_bias_l)


def _gates_bwd(proj, ba_col, ba_w, a_log_l, dt_bias_l, dbg, H, name):
    S = proj.shape[0]
    ts = _tile(S, 512, CHUNK)
    n = S // ts

    def body(x_ref, al_ref, dt_ref, d_ref, o_ref, dal_ref, ddt_ref, acc):
        i = pl.program_id(0)
        x = x_ref[...].astype(F32)
        d = d_ref[...]
        lane = lax.broadcasted_iota(jnp.int32, x.shape, 1)
        is_a = (lane >= H) & (lane < 2 * H)
        beta = _sigmoid(x)
        nea = -jnp.exp(al_ref[...])
        z = x + dt_ref[...]
        dg = _chunk_rev_cumsum(jnp.where(is_a, d, 0.0))
        da_raw = dg * nea * _sigmoid(z)
        o = jnp.where(lane < H, d * beta * (1.0 - beta), jnp.where(is_a, da_raw, 0.0))
        if ba_w > LANE:
            o = jnp.concatenate([o, jnp.zeros((ts, ba_w - LANE), F32)], axis=1)
        o_ref[...] = o.astype(BF16)

        @pl.when(i == 0)
        def _():
            acc[...] = jnp.zeros_like(acc)

        acc[0] += _colsum8(jnp.where(is_a, dg * nea * _softplus(z), 0.0))
        acc[1] += _colsum8(jnp.where(is_a, da_raw, 0.0))

        @pl.when(i == n - 1)
        def _():
            dal_ref[...] = jnp.sum(acc[0], axis=0, keepdims=True)
            ddt_ref[...] = jnp.sum(acc[1], axis=0, keepdims=True)

    vec = pl.BlockSpec((1, LANE), lambda i: (0, 0))
    return pl.pallas_call(
        body, name=name, grid=(n,),
        out_shape=(jax.ShapeDtypeStruct((S, ba_w), BF16), jax.ShapeDtypeStruct((1, LANE), F32),
                   jax.ShapeDtypeStruct((1, LANE), F32)),
        in_specs=[pl.BlockSpec((ts, LANE), lambda i: (i, ba_col)), vec, vec, pl.BlockSpec((ts, LANE), lambda i: (i, 0))],
        out_specs=(pl.BlockSpec((ts, ba_w), lambda i: (i, 0)), vec, vec),
        scratch_shapes=[pltpu.VMEM((2, 8, LANE), F32)], compiler_params=_params(),
    )(proj, a_log_l, dt_bias_l, dbg)


_BMM_FORMS = {"nn": "hik,hkj->hij", "nt": "hik,hjk->hij", "tn": "hki,hkj->hij"}


def _split_bf16(a):
    hi = a.astype(BF16)
    return hi, (a - hi.astype(F32)).astype(BF16)


def _bmm(a, b, form="nn", exact=False):
    e = lambda x, y: jnp.einsum(_BMM_FORMS[form], x, y, preferred_element_type=F32)
    if not exact:
        return e(a.astype(BF16), b.astype(BF16))
    (ah, al), (bh, bl) = _split_bf16(a), _split_bf16(b)
    return (e(ah, bl) + e(al, bh)) + e(ah, bh)


def _unit_lower_inverse(L, r, c):
    eye = (r == c).astype(F32)
    m = jnp.where((r >> 3) == (c >> 3), -L, 0.0)
    m2 = _bmm(m, m, exact=True)
    m4 = _bmm(m2, m2, exact=True)
    x = eye + m
    x = x + _bmm(x, m2, exact=True)
    x = x + _bmm(x, m4, exact=True)
    for sh in (3, 4, 5):
        off = ((r >> (sh + 1)) == (c >> (sh + 1))) & ((r >> sh) != (c >> sh))
        x = x - _bmm(x, _bmm(jnp.where(off, L, 0.0), x))
    return x


def _to_row(col, eye):
    return jnp.sum(jnp.where(eye, jnp.broadcast_to(col, eye.shape), 0.0), axis=1, keepdims=True)


def _to_col(rowv, eye):
    return jnp.sum(jnp.where(eye, jnp.broadcast_to(rowv, eye.shape), 0.0), axis=2, keepdims=True)


def _gdn_chunk(q, k, v, bg, H):
    shape = (H, CHUNK, CHUNK)
    r = lax.broadcasted_iota(jnp.int32, shape, 1)
    c = lax.broadcasted_iota(jnp.int32, shape, 2)
    eye, incl, strict = r == c, r >= c, r > c
    beta = jnp.stack([bg[:, h:h + 1] for h in range(H)], axis=0)
    G = jnp.stack([bg[:, H + h:H + h + 1] for h in range(H)], axis=0)
    gap = jnp.broadcast_to(G, shape) - _to_row(G, eye)
    decay = jnp.where(incl, jnp.exp(jnp.where(incl, gap, 0.0)), 0.0)
    kk = _bmm(k, k, "nt")
    L = jnp.where(strict, beta * decay * kk, 0.0)
    ainv = _unit_lower_inverse(L, r, c)
    eG = jnp.exp(G)
    u_v = _bmm(ainv, beta * v)
    w_k = _bmm(ainv, (beta * eG) * k)
    qk = _bmm(q, k, "nt", exact=True)
    GL = G[:, CHUNK - 1:CHUNK, :]
    ek = jnp.exp(GL - G)
    return dict(eye=eye, strict=strict, r=r, c=c, beta=beta, G=G, decay=decay, kk=kk, ainv=ainv, eG=eG,
                u_v=u_v, w_k=w_k, qk=qk, attn=decay * qk, GL=GL, ek=ek, cd=jnp.exp(GL))


def _chunk_steps(N):
    at = lambda step: lambda: pl.program_id(0) == step
    return at(0), at(N - max(N // 8, 1)), at(N - 1)


def _gdn_fwd(qkv_hm, bg, H, name, rider=None):
    S = qkv_hm.shape[1]
    N = S // CHUNK
    extra = rider.n if rider is not None else 0

    def body(q_ref, k_ref, v_ref, bg_ref, o_ref, st_ref, state):
        n = pl.program_id(0)

        @pl.when(n == 0)
        def _():
            state[...] = jnp.zeros_like(state)

        q, k, v = q_ref[...], k_ref[...], v_ref[...]
        t = _gdn_chunk(q, k, v, bg_ref[...], H)
        s0 = state[...]
        st_ref[0] = s0
        u = t["u_v"] - _bmm(t["w_k"], s0)
        o_ref[...] = _bmm(q * t["eG"], s0) + _bmm(t["attn"], u)
        state[...] = t["cd"] * s0 + _bmm(k * t["ek"], u, "tn")

    blk = lambda kind: pl.BlockSpec((H, CHUNK, HD), lambda n: (kind, n, 0))
    res = pl.pallas_call(
        _ride(body, 4, 2, rider, *_chunk_steps(N)), name=name, grid=(N,),
        out_shape=[jax.ShapeDtypeStruct((H, S, HD), F32), jax.ShapeDtypeStruct((N, H, HD, HD), F32)]
        + (rider.out_shape if rider else []),
        in_specs=[blk(0), blk(1), blk(2), pl.BlockSpec((CHUNK, LANE), lambda n: (n, 0))] + [ANY] * extra,
        out_specs=[pl.BlockSpec((H, CHUNK, HD), lambda n: (0, n, 0)),
                   pl.BlockSpec((1, H, HD, HD), lambda n: (n, 0, 0, 0))] + [ANY] * extra,
        scratch_shapes=[pltpu.VMEM((H, HD, HD), F32)] + (rider.scratch if rider else []), compiler_params=_params(),
    )(qkv_hm, qkv_hm, qkv_hm, bg, *(rider.arrays if rider else []))
    return res[0], res[1], res[2:]


def _gdn_bwd(qkv_hm, bg, states, do_hm, H, name, rider=None):
    S = qkv_hm.shape[1]
    N = S // CHUNK
    extra = rider.n if rider is not None else 0

    def body(q_ref, k_ref, v_ref, bg_ref, st_ref, do_ref, dqkv_ref, dbg_ref, dstate):
        n = pl.program_id(0)

        @pl.when(n == 0)
        def _():
            dstate[...] = jnp.zeros_like(dstate)

        q, k, v, do = q_ref[...], k_ref[...], v_ref[...], do_ref[...]
        t = _gdn_chunk(q, k, v, bg_ref[...], H)
        eye, beta, eG, decay, kk, ainv = t["eye"], t["beta"], t["eG"], t["decay"], t["kk"], t["ainv"]
        s0 = st_ref[0]
        ds1 = dstate[...]
        u = t["u_v"] - _bmm(t["w_k"], s0)
        qdec, kdec = q * eG, k * t["ek"]
        d_qdec = _bmm(do, s0, "nt")
        d_attn = _bmm(do, u, "nt")
        du = _bmm(t["attn"], do, "tn") + _bmm(kdec, ds1)
        d_cd = jnp.sum(jnp.sum(ds1 * s0, axis=2, keepdims=True), axis=1, keepdims=True)
        d_kdec = _bmm(u, ds1, "nt")
        d_wk = -_bmm(du, s0, "nt")
        dstate[...] = t["cd"] * ds1 + _bmm(qdec, do, "tn") - _bmm(t["w_k"], du, "tn")
        d_rv = _bmm(ainv, du, "tn")
        d_rk = _bmm(ainv, d_wk, "tn")
        dL = jnp.where(t["strict"], -(_bmm(d_rv, t["u_v"], "nt") + _bmm(d_rk, t["w_k"], "nt")), 0.0)
        rk_k = jnp.sum(d_rk * k, axis=2, keepdims=True)
        d_beta = (jnp.sum(dL * decay * kk, axis=2, keepdims=True) + jnp.sum(d_rv * v, axis=2, keepdims=True)
                  + rk_k * eG)
        d_decay = dL * beta * kk + d_attn * t["qk"]
        d_kk = dL * beta * decay
        d_qk = d_attn * decay
        dqkv_ref[pl.ds(2 * H, H)] = beta * d_rv
        dqkv_ref[pl.ds(0, H)] = _bmm(d_qk, k) + d_qdec * eG
        dqkv_ref[pl.ds(H, H)] = ((beta * eG) * d_rk + _bmm(d_kk, k) + _bmm(d_kk, k, "tn") + _bmm(d_qk, q, "tn")
                       + d_kdec * t["ek"])
        d_eG = rk_k * beta + jnp.sum(d_qdec * q, axis=2, keepdims=True)
        e = jnp.sum(d_kdec * kdec, axis=2, keepdims=True)
        T = d_decay * decay
        dG = d_eG * eG - e + jnp.sum(T, axis=2, keepdims=True) - _to_col(jnp.sum(T, axis=1, keepdims=True), eye)
        dGL = jnp.sum(e, axis=1, keepdims=True) + d_cd * t["cd"]
        row1 = lax.broadcasted_iota(jnp.int32, (H, CHUNK, 1), 1)
        dG = dG + jnp.where(row1 == CHUNK - 1, dGL, 0.0)
        lane = lax.broadcasted_iota(jnp.int32, (CHUNK, LANE), 1)
        out = jnp.zeros((CHUNK, LANE), F32)
        for h in range(H):
            out = out + jnp.where(lane == h, d_beta[h], 0.0) + jnp.where(lane == H + h, dG[h], 0.0)
        dbg_ref[...] = out

    blk = lambda kind: pl.BlockSpec((H, CHUNK, HD), lambda n: (kind, N - 1 - n, 0))
    res = pl.pallas_call(
        _ride(body, 6, 2, rider, *_chunk_steps(N)), name=name, grid=(N,),
        out_shape=[jax.ShapeDtypeStruct((3 * H, S, HD), F32), jax.ShapeDtypeStruct((S, LANE), F32)]
        + (rider.out_shape if rider else []),
        in_specs=[blk(0), blk(1), blk(2), pl.BlockSpec((CHUNK, LANE), lambda n: (N - 1 - n, 0)),
                  pl.BlockSpec((1, H, HD, HD), lambda n: (N - 1 - n, 0, 0, 0)), blk(0)] + [ANY] * extra,
        out_specs=[pl.BlockSpec((3 * H, CHUNK, HD), lambda n: (0, N - 1 - n, 0)),
                   pl.BlockSpec((CHUNK, LANE), lambda n: (N - 1 - n, 0))] + [ANY] * extra,
        scratch_shapes=[pltpu.VMEM((H, HD, HD), F32)] + (rider.scratch if rider else []), compiler_params=_params(),
    )(qkv_hm, qkv_hm, qkv_hm, bg, states, do_hm, *(rider.arrays if rider else []))
    return res[0], res[1], res[2:]


def _gdn_out(o_hm, proj, z_col, gdn_w, name):
    H, S, _ = o_hm.shape
    vw = H * HD
    ts = _tile(S, 256, 8)

    def body(o_ref, z_ref, w_ref, y_ref):
        z = z_ref[...].astype(F32)
        w = w_ref[...]
        parts = []
        for h in range(H):
            o = o_ref[h]
            parts.append(o * _rstd(o) * w)
        y_ref[...] = (jnp.concatenate(parts, axis=1) * _silu(z)).astype(BF16)

    return pl.pallas_call(
        body, name=name, out_shape=jax.ShapeDtypeStruct((S, vw), BF16), grid=(S // ts,),
        in_specs=[pl.BlockSpec((H, ts, HD), lambda i: (0, i, 0)), pl.BlockSpec((ts, vw), lambda i: (i, z_col)),
                  pl.BlockSpec((1, HD), lambda i: (0, 0))],
        out_specs=pl.BlockSpec((ts, vw), lambda i: (i, 0)), compiler_params=_params(),
    )(o_hm, proj, gdn_w)


def _gdn_out_bwd(o_hm, proj, z_col, gdn_w, dy, name):
    H, S, _ = o_hm.shape
    vw = H * HD
    ts = _tile(S, 256, 8)
    n = S // ts

    def body(o_ref, z_ref, w_ref, dy_ref, do_ref, dz_ref, dw_ref, acc):
        i = pl.program_id(0)
        z = z_ref[...].astype(F32)
        dy = dy_ref[...].astype(F32)
        w = w_ref[...]
        gz = dy * _silu(z)
        normed, dwr = [], jnp.zeros((ts, HD), F32)
        for h in range(H):
            o = o_ref[h]
            dxo, dwh = _rms_bwd(o, w, gz[:, HD * h:HD * (h + 1)])
            do_ref[h] = dxo
            dwr = dwr + dwh
            normed.append(o * _rstd(o) * w)
        dz_ref[...] = (dy * jnp.concatenate(normed, axis=1) * _dsilu(z)).astype(BF16)

        @pl.when(i == 0)
        def _():
            acc[...] = jnp.zeros_like(acc)

        acc[...] += _colsum8(dwr)

        @pl.when(i == n - 1)
        def _():
            dw_ref[...] = jnp.sum(acc[...], axis=0, keepdims=True)

    return pl.pallas_call(
        body, name=name, grid=(n,),
        out_shape=(jax.ShapeDtypeStruct((H, S, HD), F32), jax.ShapeDtypeStruct((S, vw), BF16),
                   jax.ShapeDtypeStruct((1, HD), F32)),
        in_specs=[pl.BlockSpec((H, ts, HD), lambda i: (0, i, 0)), pl.BlockSpec((ts, vw), lambda i: (i, z_col)),
                  pl.BlockSpec((1, HD), lambda i: (0, 0)), pl.BlockSpec((ts, vw), lambda i: (i, 0))],
        out_specs=(pl.BlockSpec((H, ts, HD), lambda i: (0, i, 0)), pl.BlockSpec((ts, vw), lambda i: (i, 0)),
                   pl.BlockSpec((1, HD), lambda i: (0, 0))),
        scratch_shapes=[pltpu.VMEM((8, HD), F32)], compiler_params=_params(),
    )(o_hm, proj, gdn_w, dy)


def _pool_trailing(ext, ts, pg, row0):
    outs, inv_cnts = [], []
    t_abs = row0 + lax.broadcasted_iota(jnp.int32, (ts, 1), 0)
    for gi, win in enumerate(POOL_WINDOWS):
        cols = pl.ds(gi * pg, pg)
        cur = ext[pl.ds(HALO, ts), cols]
        acc = cur
        for j in range(1, win):
            acc = acc + ext[pl.ds(HALO - j, ts), cols]
        inv = 1.0 / jnp.minimum(t_abs + 1, win).astype(F32)
        outs.append(acc * inv - cur)
    return outs


def _pool_fwd(proj, p_col, pool_w, pool_scale, pw, name):
    S = proj.shape[0]
    pg = pw // len(POOL_WINDOWS)
    ts = _tile(S, 512, HALO)

    def body(x_ref, h_ref, w_ref, sc_ref, o_ref, ext):
        i = pl.program_id(0)
        _fill_causal(ext, x_ref[...].astype(F32), h_ref[...].astype(F32), i)
        ys = _pool_trailing(ext, ts, pg, i * ts)
        outs = [jnp.dot(ys[gi].astype(BF16), w_ref[gi], preferred_element_type=F32) for gi in range(len(ys))]
        o_ref[...] = (jnp.concatenate(outs, axis=1) * sc_ref[...]).astype(BF16)

    return pl.pallas_call(
        body, name=name, out_shape=jax.ShapeDtypeStruct((S, pw), BF16), grid=(S // ts,),
        in_specs=[pl.BlockSpec((ts, pw), lambda i: (i, p_col)),
                  pl.BlockSpec((HALO, pw), lambda i: (jnp.maximum(i * (ts // HALO) - 1, 0), p_col)),
                  pl.BlockSpec((len(POOL_WINDOWS), pg, pg), lambda i: (0, 0, 0)), pl.BlockSpec((1, pw), lambda i: (0, 0))],
        out_specs=pl.BlockSpec((ts, pw), lambda i: (i, 0)),
        scratch_shapes=[pltpu.VMEM((ts + HALO, pw), F32)], compiler_params=_params(),
    )(proj, proj, pool_w, pool_scale)


def _pool_bwd(proj, p_col, pool_w, pool_scale, dpb, pw, name):
    S = proj.shape[0]
    G = len(POOL_WINDOWS)
    pg = pw // G
    ts = _tile(S, 512, HALO)
    n = S // ts

    def body(x_ref, h_ref, w_ref, sc_ref, d_ref, dn_ref, dp_ref, dw_ref, dsc_ref, ext, zext, wacc, sacc):
        i = pl.program_id(0)
        _fill_causal(ext, x_ref[...].astype(F32), h_ref[...].astype(F32), i)
        ys = _pool_trailing(ext, ts, pg, i * ts)
        d_ext = jnp.concatenate([d_ref[...].astype(F32), jnp.where(i < n - 1, dn_ref[...].astype(F32), 0.0)], axis=0)
        dt = d_ext * sc_ref[...]
        t_abs = i * ts + lax.broadcasted_iota(jnp.int32, (ts + HALO, 1), 0)

        @pl.when(i == 0)
        def _():
            wacc[...] = jnp.zeros_like(wacc)
            sacc[...] = jnp.zeros_like(sacc)

        dps, tfs = [], []
        for gi, win in enumerate(POOL_WINDOWS):
            cols = slice(gi * pg, (gi + 1) * pg)
            w = w_ref[gi]
            dt_g = dt[:, cols].astype(BF16)
            y_g = ys[gi].astype(BF16)
            tfs.append(jnp.dot(y_g, w, preferred_element_type=F32))
            wacc[gi] += lax.dot_general(y_g, dt_g[:ts], (((0,), (0,)), ((), ())), preferred_element_type=F32)
            dyp = lax.dot_general(dt_g, w, (((1,), (1,)), ((), ())), preferred_element_type=F32)
            zext[:, pl.ds(gi * pg, pg)] = dyp * (1.0 / jnp.minimum(t_abs + 1, win).astype(F32))
            acc = -dyp[:ts]
            for j in range(win):
                acc = acc + zext[pl.ds(j, ts), pl.ds(gi * pg, pg)]
            dps.append(acc)
        dp_ref[...] = jnp.concatenate(dps, axis=1).astype(BF16)
        sacc[...] += _colsum8(d_ext[:ts] * jnp.concatenate(tfs, axis=1))

        @pl.when(i == n - 1)
        def _():
            dw_ref[...] = wacc[...]
            dsc_ref[...] = jnp.sum(sacc[...], axis=0, keepdims=True)

    return pl.pallas_call(
        body, name=name, grid=(n,),
        out_shape=(jax.ShapeDtypeStruct((S, pw), BF16), jax.ShapeDtypeStruct((G, pg, pg), F32),
                   jax.ShapeDtypeStruct((1, pw), F32)),
        in_specs=[pl.BlockSpec((ts, pw), lambda i: (i, p_col)),
                  pl.BlockSpec((HALO, pw), lambda i: (jnp.maximum(i * (ts // HALO) - 1, 0), p_col)),
                  pl.BlockSpec((G, pg, pg), lambda i: (0, 0, 0)), pl.BlockSpec((1, pw), lambda i: (0, 0)),
                  pl.BlockSpec((ts, pw), lambda i: (i, 0)),
                  pl.BlockSpec((HALO, pw), lambda i: (jnp.minimum((i + 1) * (ts // HALO), S // HALO - 1), 0))],
        out_specs=(pl.BlockSpec((ts, pw), lambda i: (i, 0)), pl.BlockSpec((G, pg, pg), lambda i: (0, 0, 0)),
                   pl.BlockSpec((1, pw), lambda i: (0, 0))),
        scratch_shapes=[pltpu.VMEM((ts + HALO, pw), F32), pltpu.VMEM((ts + HALO, pw), F32),
                        pltpu.VMEM((G, pg, pg), F32), pltpu.VMEM((8, pw), F32)],
        compiler_params=_params(),
    )(proj, proj, pool_w, pool_scale, dpb, dpb)


def _merge(proj, ga_col, gb_col, ya, yb, name):
    S, d = ya.shape
    ts = _tile(S, 512, 16)

    def body(ga_ref, gb_ref, ya_ref, yb_ref, o_ref):
        o_ref[...] = (_sigmoid(ga_ref[...].astype(F32)) * ya_ref[...].astype(F32)
                      + _sigmoid(gb_ref[...].astype(F32)) * yb_ref[...].astype(F32)).astype(BF16)

    row = pl.BlockSpec((ts, d), lambda i: (i, 0))
    return pl.pallas_call(
        body, name=name, out_shape=jax.ShapeDtypeStruct((S, d), BF16), grid=(S // ts,),
        in_specs=[pl.BlockSpec((ts, d), lambda i: (i, ga_col)), pl.BlockSpec((ts, d), lambda i: (i, gb_col)), row, row],
        out_specs=row, compiler_params=_params(),
    )(proj, proj, ya, yb)


def _merge_bwd(proj, ga_col, gb_col, ya, yb, dm, name):
    S, d = ya.shape
    ts = _tile(S, 512, 16)

    def body(ga_ref, gb_ref, ya_ref, yb_ref, dm_ref, dya_ref, dyb_ref, dga_ref, dgb_ref):
        dmv = dm_ref[...].astype(F32)
        sa, sb = _sigmoid(ga_ref[...].astype(F32)), _sigmoid(gb_ref[...].astype(F32))
        dya_ref[...] = (dmv * sa).astype(BF16)
        dyb_ref[...] = (dmv * sb).astype(BF16)
        dga_ref[...] = (dmv * ya_ref[...].astype(F32) * sa * (1.0 - sa)).astype(BF16)
        dgb_ref[...] = (dmv * yb_ref[...].astype(F32) * sb * (1.0 - sb)).astype(BF16)

    row = pl.BlockSpec((ts, d), lambda i: (i, 0))
    o = jax.ShapeDtypeStruct((S, d), BF16)
    return pl.pallas_call(
        body, name=name, out_shape=(o, o, o, o), grid=(S // ts,),
        in_specs=[pl.BlockSpec((ts, d), lambda i: (i, ga_col)), pl.BlockSpec((ts, d), lambda i: (i, gb_col)), row, row, row],
        out_specs=(row, row, row, row), compiler_params=_params(),
    )(proj, proj, ya, yb, dm)


def _xattn_fwd(q, kv, name):
    S, d = q.shape
    M = kv.shape[0]
    hd = d // XA_HEADS
    ts = _tile(S, 512, 16)
    scale = hd ** -0.5

    def body(q_ref, k_ref, v_ref, o_ref):
        s = lax.dot_general(q_ref[...], k_ref[...], (((1,), (1,)), ((), ())), preferred_element_type=F32) * scale
        p = jnp.exp(s - jnp.max(s, axis=-1, keepdims=True))
        p = p / jnp.sum(p, axis=-1, keepdims=True)
        o_ref[...] = jnp.dot(p.astype(BF16), v_ref[...], preferred_element_type=F32).astype(BF16)

    return pl.pallas_call(
        body, name=name, out_shape=jax.ShapeDtypeStruct((S, d), BF16), grid=(S // ts, XA_HEADS),
        in_specs=[pl.BlockSpec((ts, hd), lambda i, h: (i, h)), pl.BlockSpec((M, hd), lambda i, h: (0, h)),
                  pl.BlockSpec((M, hd), lambda i, h: (0, XA_HEADS + h))],
        out_specs=pl.BlockSpec((ts, hd), lambda i, h: (i, h)), compiler_params=_params(),
    )(q, kv, kv)


def _xattn_bwd(q, kv, do, name):
    S, d = q.shape
    M = kv.shape[0]
    hd = d // XA_HEADS
    ts = _tile(S, 512, 16)
    n = S // ts
    scale = hd ** -0.5

    def body(q_ref, k_ref, v_ref, do_ref, dq_ref, dk_ref, dv_ref, kacc, vacc):
        i = pl.program_id(1)
        qv, kv_, vv, dov = q_ref[...], k_ref[...], v_ref[...], do_ref[...]
        s = lax.dot_general(qv, kv_, (((1,), (1,)), ((), ())), preferred_element_type=F32) * scale
        p = jnp.exp(s - jnp.max(s, axis=-1, keepdims=True))
        p = p / jnp.sum(p, axis=-1, keepdims=True)
        dp = lax.dot_general(dov, vv, (((1,), (1,)), ((), ())), preferred_element_type=F32)
        ds = (p * (dp - jnp.sum(p * dp, axis=-1, keepdims=True)) * scale).astype(BF16)
        dq_ref[...] = jnp.dot(ds, kv_, preferred_element_type=F32).astype(BF16)

        @pl.when(i == 0)
        def _():
            kacc[...] = jnp.zeros_like(kacc)
            vacc[...] = jnp.zeros_like(vacc)

        kacc[...] += lax.dot_general(ds, qv, (((0,), (0,)), ((), ())), preferred_element_type=F32)
        vacc[...] += lax.dot_general(p.astype(BF16), dov, (((0,), (0,)), ((), ())), preferred_element_type=F32)

        @pl.when(i == n - 1)
        def _():
            dk_ref[...] = kacc[...]
            dv_ref[...] = vacc[...]

    dq, dk, dv = pl.pallas_call(
        body, name=name, grid=(XA_HEADS, n),
        out_shape=(jax.ShapeDtypeStruct((S, d), BF16), jax.ShapeDtypeStruct((M, d), F32), jax.ShapeDtypeStruct((M, d), F32)),
        in_specs=[pl.BlockSpec((ts, hd), lambda h, i: (i, h)), pl.BlockSpec((M, hd), lambda h, i: (0, h)),
                  pl.BlockSpec((M, hd), lambda h, i: (0, XA_HEADS + h)), pl.BlockSpec((ts, hd), lambda h, i: (i, h))],
        out_specs=(pl.BlockSpec((ts, hd), lambda h, i: (i, h)), pl.BlockSpec((M, hd), lambda h, i: (0, h)),
                   pl.BlockSpec((M, hd), lambda h, i: (0, h))),
        scratch_shapes=[pltpu.VMEM((M, hd), F32), pltpu.VMEM((M, hd), F32)], compiler_params=_params(),
    )(q, kv, kv, do)
    return dq, jnp.concatenate([dk, dv], axis=1)


def _ffn_act(up, conv_w, bias, name):
    S, F2 = up.shape
    F = F2 // 2
    ts, cw = _tile(S, 512, HALO), _tile(F, 512)
    nb = F // cw

    def body(a_ref, ah_ref, b_ref, bh_ref, wa_ref, wb_ref, ba_ref, bb_ref, o_ref, ea, eb):
        i = pl.program_id(0)
        _fill_causal(ea, a_ref[...].astype(F32), ah_ref[...].astype(F32), i)
        _fill_causal(eb, b_ref[...].astype(F32), bh_ref[...].astype(F32), i)
        wa, wb, bia, bib = wa_ref[...], wb_ref[...], ba_ref[...], bb_ref[...]

        def chunk(r0):
            ua = _filter(wa, _causal_taps(ea, r0, FFN_CONV)) + bia
            ub = _filter(wb, _causal_taps(eb, r0, FFN_CONV)) + bib
            o_ref[pl.ds(r0, RC), :] = (_silu(ua) * ub).astype(BF16)

        _row_chunks(ts, chunk)

    tile = lambda c0: pl.BlockSpec((ts, cw), lambda i, j: (i, j + c0))
    vec = lambda rows, c0: pl.BlockSpec((rows, cw), lambda i, j: (0, j + c0))
    return pl.pallas_call(
        body, name=name, out_shape=jax.ShapeDtypeStruct((S, F), BF16), grid=(S // ts, nb),
        in_specs=[tile(0), _prev_halo_spec(ts, cw), tile(nb), _prev_halo_spec(ts, cw, nb),
                  vec(FFN_CONV, 0), vec(FFN_CONV, nb), vec(1, 0), vec(1, nb)],
        out_specs=pl.BlockSpec((ts, cw), lambda i, j: (i, j)),
        scratch_shapes=[pltpu.VMEM((ts + HALO, cw), F32), pltpu.VMEM((ts + HALO, cw), F32)],
        compiler_params=_params(),
    )(up, up, up, up, conv_w, conv_w, bias, bias)


def _ffn_act_bwd(up, conv_w, bias, dact, name):
    S, F2 = up.shape
    F = F2 // 2
    ts, cw = _tile(S, 512, HALO), _tile(F, 512)
    nb = F // cw
    n = S // ts

    def body(a_ref, ah_ref, b_ref, bh_ref, wa_ref, wb_ref, ba_ref, bb_ref, d_ref,
             du_ref, dwa_ref, dwb_ref, dba_ref, dbb_ref, ea, eb, wacc, bacc):
        i = pl.program_id(1)
        _fill_causal(ea, a_ref[...].astype(F32), ah_ref[...].astype(F32), i)
        _fill_causal(eb, b_ref[...].astype(F32), bh_ref[...].astype(F32), i)
        wa, wb, bia, bib = wa_ref[...], wb_ref[...], ba_ref[...], bb_ref[...]

        @pl.when(i == 0)
        def _():
            wacc[...] = jnp.zeros_like(wacc)
            bacc[...] = jnp.zeros_like(bacc)

        def chunk(r0):
            ta, tb = _causal_taps(ea, r0, FFN_CONV), _causal_taps(eb, r0, FFN_CONV)
            ua, ub = _filter(wa, ta) + bia, _filter(wb, tb) + bib
            d = d_ref[pl.ds(r0, RC), :].astype(F32)
            dua = d * ub * _dsilu(ua)
            dub = d * _silu(ua)
            du_ref[0, pl.ds(r0, RC), :] = dua.astype(BF16)
            du_ref[1, pl.ds(r0, RC), :] = dub.astype(BF16)
            for t in range(FFN_CONV):
                wacc[0, t] += _colsum8(dua * ta[FFN_CONV - 1 - t])
                wacc[1, t] += _colsum8(dub * tb[FFN_CONV - 1 - t])
            bacc[0] += _colsum8(dua)
            bacc[1] += _colsum8(dub)

        _row_chunks(ts, chunk)

        @pl.when(i == n - 1)
        def _():
            dwa_ref[...] = jnp.sum(wacc[0], axis=1)
            dwb_ref[...] = jnp.sum(wacc[1], axis=1)
            dba_ref[...] = jnp.sum(bacc[0], axis=0, keepdims=True)
            dbb_ref[...] = jnp.sum(bacc[1], axis=0, keepdims=True)

    tile = lambda c0: pl.BlockSpec((ts, cw), lambda j, i: (i, j + c0))
    halo = lambda c0: pl.BlockSpec((HALO, cw), lambda j, i: (jnp.maximum(i * (ts // HALO) - 1, 0), j + c0))
    vec = lambda rows, c0: pl.BlockSpec((rows, cw), lambda j, i: (0, j + c0))
    du, dwa, dwb, dba, dbb = pl.pallas_call(
        body, name=name, grid=(nb, n),
        out_shape=(jax.ShapeDtypeStruct((2, S, F), BF16),
                   jax.ShapeDtypeStruct((FFN_CONV, F), F32), jax.ShapeDtypeStruct((FFN_CONV, F), F32),
                   jax.ShapeDtypeStruct((1, F), F32), jax.ShapeDtypeStruct((1, F), F32)),
        in_specs=[tile(0), halo(0), tile(nb), halo(nb), vec(FFN_CONV, 0), vec(FFN_CONV, nb), vec(1, 0), vec(1, nb), tile(0)],
        out_specs=(pl.BlockSpec((2, ts, cw), lambda j, i: (0, i, j)), vec(FFN_CONV, 0), vec(FFN_CONV, 0), vec(1, 0),
                   vec(1, 0)),
        scratch_shapes=[pltpu.VMEM((ts + HALO, cw), F32), pltpu.VMEM((ts + HALO, cw), F32),
                        pltpu.VMEM((2, FFN_CONV, 8, cw), F32), pltpu.VMEM((2, 8, cw), F32)],
        compiler_params=_params(),
    )(up, up, up, up, conv_w, conv_w, bias, bias, dact)
    return du, jnp.concatenate([dwa, dwb], axis=1), jnp.concatenate([dba, dbb], axis=1)


def _adamw(gparts, w, m, v, name):
    R, C = w.shape
    tr = _tile(R, max(16, (256 * 1024) // C), 16)

    def body(g_ref, w_ref, m_ref, v_ref, go_ref, d_ref, mo_ref, vo_ref):
        g = g_ref[0].astype(F32)
        for s in range(1, N_DEV):
            g = g + g_ref[s].astype(F32)
        mn = ADAM_B1 * m_ref[...] + (1.0 - ADAM_B1) * g
        vn = ADAM_B2 * v_ref[...] + (1.0 - ADAM_B2) * (g * g)
        m_hat = mn / (1.0 - ADAM_B1 ** ADAM_STEP)
        v_hat = vn / (1.0 - ADAM_B2 ** ADAM_STEP)
        go_ref[...] = g
        d_ref[...] = -ADAM_LR * (m_hat / (jnp.sqrt(v_hat) + ADAM_EPS) + ADAM_WD * w_ref[...])
        mo_ref[...] = mn
        vo_ref[...] = vn

    row = pl.BlockSpec((tr, C), lambda i: (i, 0))
    o = jax.ShapeDtypeStruct((R, C), F32)
    return pl.pallas_call(
        body, name=name, out_shape=(o, o, o, o), grid=(R // tr,),
        in_specs=[pl.BlockSpec((N_DEV, tr, C), lambda i: (0, i, 0)), row, row, row],
        out_specs=(row, row, row, row), compiler_params=_params(),
    )(gparts, w, m, v)


def _position():
    return lax.axis_index("x"), lax.axis_index("y"), lax.axis_index("c")


class _Copies:
    def __init__(self, arrays):
        self.arrays, self.n = list(arrays), len(arrays)
        self.scratch = [pltpu.SemaphoreType.DMA((7 * self.n,)), pltpu.SemaphoreType.DMA((7 * self.n,)),
                        pltpu.SemaphoreType.DMA((self.n,))]


class _Gather(_Copies):
    def __init__(self, arrays):
        super().__init__(arrays)
        self.out_shape = [jax.ShapeDtypeStruct((N_DEV,) + b.shape, b.dtype) for b in self.arrays]

    def phases(self, x_refs, out_refs, send_sems, recv_sems, local_sems):
        n = self.n
        x, y, c = _position()
        me, sibling = (x, y, c), (x, y, 1 - c)
        chips = [(1 - x, y), (x, 1 - y), (1 - x, 1 - y)]

        def copy(a, k, blk, to, own=False):
            slot = out_refs[a].at[4 * blk[0] + 2 * blk[1] + blk[2]]
            return pltpu.make_async_remote_copy(
                src_ref=x_refs[a] if own else slot, dst_ref=slot,
                send_sem=send_sems.at[7 * a + k], recv_sem=recv_sems.at[7 * a + k], device_id=to, device_id_type=MESH)

        mine = [pltpu.make_async_copy(x_refs[a], out_refs[a].at[4 * x + 2 * y + c], local_sems.at[a]) for a in range(n)]
        first = [copy(a, 0, me, sibling, own=True) for a in range(n)]
        first += [copy(a, 1 + j, me, (*chip, c), own=True) for a in range(n) for j, chip in enumerate(chips)]
        passed = [copy(a, 4 + j, (*chip, c), sibling) for j, chip in enumerate(chips) for a in range(n)]

        def start():
            for cp in mine + first:
                cp.start()

        def hand_on():
            for j, chip in enumerate(chips):
                for a in range(n):
                    copy(a, 1 + j, (*chip, c), me).wait_recv()
                    passed[j * n + a].start()

        def finish():
            for a in range(n):
                copy(a, 0, sibling, me).wait_recv()
            for j, chip in enumerate(chips):
                for a in range(n):
                    copy(a, 4 + j, (*chip, 1 - c), me).wait_recv()
            for cp in first + passed:
                cp.wait_send()
            for cp in mine:
                cp.wait()

        return start, hand_on, finish


class _Exchange(_Copies):
    def __init__(self, arrays):
        super().__init__(arrays)
        self.out_shape = [jax.ShapeDtypeStruct(p.shape, p.dtype) for p in self.arrays]

    def phases(self, p_refs, out_refs, send_sems, recv_sems, local_sems):
        n = self.n
        x, y, c = _position()
        my_slot = 4 * x + 2 * y + c
        mine = [pltpu.make_async_copy(p_refs[a].at[my_slot], out_refs[a].at[my_slot], local_sems.at[a]) for a in range(n)]
        copies = []
        for k in range(1, N_DEV):
            px, py, pc = x ^ (k >> 2), y ^ ((k >> 1) & 1), c ^ (k & 1)
            for a in range(n):
                copies.append(pltpu.make_async_remote_copy(
                    src_ref=p_refs[a].at[4 * px + 2 * py + pc], dst_ref=out_refs[a].at[my_slot],
                    send_sem=send_sems.at[7 * a + k - 1], recv_sem=recv_sems.at[7 * a + k - 1],
                    device_id=(px, py, pc), device_id_type=MESH))

        def start():
            for cp in mine + copies:
                cp.start()

        def finish():
            for cp in copies + mine:
                cp.wait()

        return start, None, finish


def _communicate(copies, name):
    n = copies.n

    def body(*refs):
        start, hand_on, finish = copies.phases(refs[:n], refs[n:2 * n], *refs[2 * n:])
        start()
        if hand_on is not None:
            hand_on()
        finish()

    return pl.pallas_call(body, name=name, out_shape=copies.out_shape, in_specs=[ANY] * n, out_specs=[ANY] * n,
                          scratch_shapes=copies.scratch)(*copies.arrays)


def _col_pieces(col_map, shard_w):
    pieces = []
    for lo, hi, dst in col_map:
        c = lo
        while c < hi:
            j = c // shard_w
            end = min(hi, (j + 1) * shard_w)
            pieces.append((j, c - j * shard_w, end - c, dst + (c - lo)))
            c = end
    return pieces


def _assemble_cols(shards, pieces, width, name):
    _, R, Cs = shards.shape
    tr = _tile(R, 128, 16)

    def body(s_ref, o_ref):
        o_ref[...] = jnp.zeros(o_ref.shape, o_ref.dtype)
        for j, lo, n, dst in pieces:
            o_ref[:, dst:dst + n] = s_ref[j, :, lo:lo + n]

    return pl.pallas_call(
        body, name=name, out_shape=jax.ShapeDtypeStruct((R, width), shards.dtype), grid=(R // tr,),
        in_specs=[pl.BlockSpec((N_DEV, tr, Cs), lambda i: (0, i, 0))],
        out_specs=pl.BlockSpec((tr, width), lambda i: (i, 0)), compiler_params=_params(),
    )(shards)


def _split_cols(full, pieces, shard_w, name):
    R, width = full.shape
    tr = _tile(R, 128, 16)

    def body(f_ref, o_ref):
        for j, lo, n, dst in pieces:
            o_ref[j, :, lo:lo + n] = f_ref[:, dst:dst + n]

    return pl.pallas_call(
        body, name=name, out_shape=jax.ShapeDtypeStruct((N_DEV, R, shard_w), full.dtype), grid=(R // tr,),
        in_specs=[pl.BlockSpec((tr, width), lambda i: (i, 0))],
        out_specs=pl.BlockSpec((N_DEV, tr, shard_w), lambda i: (0, i, 0)), compiler_params=_params(),
    )(full)


def _pack(arrays, row_multiple=8):
    flat, layout, off = [], [], 0
    for a in arrays:
        n = a.size
        padded = -(-n // LANE) * LANE
        f = a.reshape(-1).astype(F32)
        if padded != n:
            f = jnp.pad(f, (0, padded - n))
        flat.append(f)
        layout.append((off, n, a.shape))
        off += padded
    total = -(-off // (LANE * row_multiple)) * (LANE * row_multiple)
    if total != off:
        flat.append(jnp.zeros((total - off,), F32))
    return jnp.concatenate(flat).reshape(total // LANE, LANE), layout


def _unpack(buf, layout):
    flat = buf.reshape(-1)
    return [flat[off:off + n].reshape(shape) for off, n, shape in layout]


def _cols_to_full(g):
    return jnp.transpose(g, (1, 0, 2)).reshape(g.shape[1], N_DEV * g.shape[2])


def _full_to_cols(a):
    return jnp.transpose(a.reshape(a.shape[0], N_DEV, a.shape[1] // N_DEV), (1, 0, 2))


def _rows_to_full(g):
    return g.reshape(N_DEV * g.shape[1], g.shape[2])


def _full_to_rows(a):
    return a.reshape(N_DEV, a.shape[0] // N_DEV, a.shape[1])


def _pad_cols(a, width):
    return a if a.shape[-1] == width else jnp.pad(a, [(0, 0)] * (a.ndim - 1) + [(0, width - a.shape[-1])])


SHARDED = ("w_in", "conv_qkv", "pool_w", "w_branch_a", "w_branch_b", "w_mix_out", "w_xq", "w_xkv", "w_xo", "w_up",
           "ffn_conv_w", "w_down")
REPLICATED = ("mix_pre_norm", "a_log", "dt_bias", "gdn_norm", "pool_scale", "mix_post_norm", "xa_pre_norm", "mem_norm",
              "xa_post_norm", "ffn_pre_norm", "ffn_conv_b", "ffn_post_norm")
WEIGHTS = ("mix_pre_norm", "w_in", "conv_qkv", "a_log", "dt_bias", "gdn_norm", "pool_w", "pool_scale", "w_branch_a",
           "w_branch_b", "w_mix_out", "mix_post_norm", "xa_pre_norm", "mem_norm", "w_xq", "w_xkv", "w_xo", "xa_post_norm",
           "ffn_pre_norm", "w_up", "ffn_conv_w", "ffn_conv_b", "w_down", "ffn_post_norm")
MATMUL_WEIGHTS = ("w_in", "w_branch_a", "w_branch_b", "w_mix_out", "w_xq", "w_xkv", "w_xo", "w_up", "w_down")
COL_SHARDED = ("w_in", "w_branch_b", "w_xkv", "w_up", "conv_qkv", "ffn_conv_w")
ROW_SHARDED = ("w_branch_a", "w_mix_out", "w_xq", "w_xo", "w_down")


class _Layout:
    def __init__(self, D, H, pw, F):
        self.D, self.H, self.pw, self.F = D, H, pw, F
        self.qkv_w, self.vw = 3 * H * HD, H * HD
        self.ba_w = 512 if D >= 2048 else LANE
        self.Fp = -(-F // 512) * 512 if F >= 512 else F
        q, vw = self.qkv_w, self.vw
        self.seg = dict(qkv=(0, q), z=(q, vw), ga=(q + vw, D), gb=(q + vw + D, D), p=(q + vw + 2 * D, pw),
                        ba=(q + vw + 2 * D + pw, self.ba_w))
        self.in_w = q + vw + 2 * D + pw + self.ba_w
        o_z, o_b = q, q + vw
        o_p = o_b + 2 * H
        o_ga = o_p + pw
        o_gb = o_ga + D
        self.d_in = o_gb + D
        self.in_map = [(0, o_z, self.seg["qkv"][0]), (o_z, o_b, self.seg["z"][0]), (o_b, o_p, self.seg["ba"][0]),
                       (o_p, o_ga, self.seg["p"][0]), (o_ga, o_gb, self.seg["ga"][0]), (o_gb, self.d_in, self.seg["gb"][0])]
        self.up_map = [(0, F, 0), (F, 2 * F, self.Fp)]

    def col(self, name, width):
        return self.seg[name][0] // width


def _local_step(x, mem, target, P, L, comm=None):
    D, H, pw, F, Fp = L.D, L.H, L.pw, L.F, L.Fp
    qkv_w, vw, ba_w = L.qkv_w, L.vw, L.ba_w
    col = L.col
    P = dict(P)
    win_p, cw3_p, fb_p = P["win_p"], P["cw3_p"], P["fb_p"]
    conv_qkv, pool_w = P["conv_qkv"], P["pool_w"]
    lanes = lambda vec: jnp.pad(vec.reshape(1, H).astype(F32), ((0, 0), (H, LANE - 2 * H)))
    a_log_l, dt_bias_l = lanes(P["a_log"]), lanes(P["dt_bias"])
    bf = lambda name: P[name]
    vecf = lambda name: P[name].reshape(1, -1).astype(F32)
    g = {}

    def carried(call, name, *args, **kw):
        if comm is not None and name in comm.GATHERS:
            *out, got = call(*args, name, rider=comm.gather(name), **kw)
            P.update(comm.weights_from(name, got))
        elif comm is not None and name in comm.EXCHANGES:
            *out, got = call(*args, name, rider=comm.exchange(name, g), **kw)
            comm.receive(name, got)
        else:
            out = call(*args, name, **kw)
            out = [out] if call is _matmul else list(out[:-1])
        return out[0] if len(out) == 1 else out

    h1 = _prenorm(x, vecf("mix_pre_norm"), "mix_prenorm")
    proj = carried(_matmul, "in_proj", h1, win_p, "nn", BF16, tn=768)
    qkv_hm = _qkv_conv(proj, conv_qkv, qkv_w, "qkv_conv")
    bg = _gates(proj, col("ba", LANE), a_log_l, dt_bias_l, H, "gates")
    o_hm, states = carried(_gdn_fwd, "gdn_fwd", qkv_hm, bg, H)
    wup_p, wdown_p = P["wup_p"], P["wdown_p"]
    oa = _gdn_out(o_hm, proj, col("z", vw), vecf("gdn_norm"), "gdn_out")
    ya = _matmul(oa, bf("w_branch_a"), "nn", BF16, "branch_a")
    pb = _pool_fwd(proj, col("p", pw), pool_w, vecf("pool_scale"), pw, "pool_fwd")
    yb = _matmul(pb, bf("w_branch_b"), "nn", BF16, "branch_b")
    merged = _merge(proj, col("ga", D), col("gb", D), ya, yb, "merge")
    y1 = _matmul(merged, bf("w_mix_out"), "nn", F32, "mix_out")
    x1, h2 = _post_pre(x, y1, vecf("mix_post_norm"), vecf("xa_pre_norm"), "mix_post")
    mn = _prenorm(mem, vecf("mem_norm"), "mem_norm")
    qx = _matmul(h2, bf("w_xq"), "nn", BF16, "xq")
    kv = _matmul(mn, bf("w_xkv"), "nn", BF16, "xkv")
    ox = _xattn_fwd(qx, kv, "xattn_fwd")
    y2 = _matmul(ox, bf("w_xo"), "nn", F32, "xo")
    x2, h3 = _post_pre(x1, y2, vecf("xa_post_norm"), vecf("ffn_pre_norm"), "xa_post")
    up = _matmul(h3, wup_p, "nn", BF16, "ffn_up")
    act = _ffn_act(up, cw3_p, fb_p, "ffn_act")
    y3 = _matmul(act, wdown_p, "nn", F32, "ffn_down")
    dx3, loss = _post_loss(x2, y3, vecf("ffn_post_norm"), target, "ffn_post_loss")

    dy3, g["ffn_post_norm"] = _post_bwd(y3, vecf("ffn_post_norm"), dx3, "ffn_post_bwd")
    dact = _matmul(dy3, wdown_p, "nt", BF16, "ffn_down_dx")
    g["w_down_p"] = _matmul(act, dy3, "tn", BF16, "ffn_down_dw")
    du, g["ffn_conv_w_p"], g["ffn_conv_b_p"] = _ffn_act_bwd(up, cw3_p, fb_p, dact, "ffn_act_bwd")
    dup = _conv_t(du, jnp.stack([cw3_p[:, :Fp], cw3_p[:, Fp:]]), "ffn_conv_t")
    dh3 = carried(_matmul, "ffn_up_dx", dup, wup_p, "nt", F32)
    g["w_up_p"] = _matmul(h3, dup, "tn", BF16, "ffn_up_dw", tn=1408)
    dx2, g["ffn_pre_norm"] = _pre_bwd(x2, vecf("ffn_pre_norm"), dh3, dx3, "ffn_pre_bwd")
    dy2, g["xa_post_norm"] = _post_bwd(y2, vecf("xa_post_norm"), dx2, "xa_post_bwd")
    dox = _matmul(dy2, bf("w_xo"), "nt", BF16, "xo_dx")
    g["w_xo"] = _matmul(ox, dy2, "tn", BF16, "xo_dw")
    dqx, dkv = _xattn_bwd(qx, kv, dox, "xattn_bwd")
    dkv_b = dkv.astype(BF16)
    dh2 = _matmul(dqx, bf("w_xq"), "nt", F32, "xq_dx")
    g["w_xq"] = _matmul(h2, dqx, "tn", BF16, "xq_dw")
    dmn = _matmul(dkv_b, bf("w_xkv"), "nt", F32, "xkv_dx")
    g["w_xkv"] = _matmul(mn, dkv_b, "tn", BF16, "xkv_dw")
    _, g["mem_norm"] = _pre_bwd(mem, vecf("mem_norm"), dmn, jnp.zeros_like(mem), "mem_norm_bwd")
    dx1, g["xa_pre_norm"] = _pre_bwd(x1, vecf("xa_pre_norm"), dh2, dx2, "xa_pre_bwd")
    dy1, g["mix_post_norm"] = _post_bwd(y1, vecf("mix_post_norm"), dx1, "mix_post_bwd")
    dmerged = _matmul(dy1, bf("w_mix_out"), "nt", BF16, "mix_out_dx")
    g["w_mix_out"] = _matmul(merged, dy1, "tn", BF16, "mix_out_dw")
    dya, dyb, dga, dgb = _merge_bwd(proj, col("ga", D), col("gb", D), ya, yb, dmerged, "merge_bwd")
    doa = _matmul(dya, bf("w_branch_a"), "nt", BF16, "branch_a_dx")
    g["w_branch_a"] = _matmul(oa, dya, "tn", BF16, "branch_a_dw")
    dpb = _matmul(dyb, bf("w_branch_b"), "nt", BF16, "branch_b_dx")
    g["w_branch_b"] = _matmul(pb, dyb, "tn", BF16, "branch_b_dw")
    dp, g["pool_w"], g["pool_scale"] = _pool_bwd(proj, col("p", pw), pool_w, vecf("pool_scale"), dpb, pw, "pool_bwd")
    do_hm, dz, g["gdn_norm"] = _gdn_out_bwd(o_hm, proj, col("z", vw), vecf("gdn_norm"), doa, "gdn_out_bwd")
    dqkv_hm, dbg = carried(_gdn_bwd, "gdn_bwd", qkv_hm, bg, states, do_hm, H)
    dba, dal, ddt = _gates_bwd(proj, col("ba", LANE), ba_w, a_log_l, dt_bias_l, dbg, H, "gates_bwd")
    g["a_log"], g["dt_bias"] = dal[:, H:2 * H], ddt[:, H:2 * H]
    dc, g["conv_qkv"] = _qkv_conv_bwd(proj, conv_qkv, dqkv_hm, qkv_w, "qkv_conv_bwd")
    dqkv = _conv_t(dc[None], conv_qkv[None], "qkv_conv_t")[0]
    dproj = jnp.concatenate([dqkv, dz, dga, dgb, dp, dba], axis=1)
    g["w_in_p"] = carried(_matmul, "in_proj_dw", h1, dproj, "tn", BF16, tn=768)
    dh1 = carried(_matmul, "in_proj_dx", dproj, win_p, "nt", F32)
    grad_x, g["mix_pre_norm"] = _pre_bwd(x, vecf("mix_pre_norm"), dh1, dx1, "mix_pre_bwd")
    return loss, grad_x, g


def _two_halves(a, F, Fp):
    return jnp.concatenate([_pad_cols(a[..., :F], Fp), _pad_cols(a[..., F:], Fp)], axis=-1)


def _from_halves(a, F, Fp):
    return jnp.concatenate([a[..., :F], a[..., Fp:Fp + F]], axis=-1)


class _StepComm:
    FIRST = ("w_in", "conv_qkv", "pool_w", "ffn_conv_w")
    GATHERS = {"in_proj": ("w_branch_a", "w_branch_b", "w_mix_out", "w_xq", "w_xo"),
               "gdn_fwd": ("w_xkv", "w_up", "w_down")}
    EXCHANGES = {"ffn_up_dx": ("w_down", "ffn_conv_w"),
                 "gdn_bwd": ("pool_w", "w_branch_a", "w_branch_b", "w_mix_out", "w_xq", "w_xkv", "w_xo"),
                 "in_proj_dw": ("w_up",),
                 "in_proj_dx": ("w_in", "conv_qkv")}

    def __init__(self, w, L):
        self.w, self.L = w, L
        self.in_pieces = _col_pieces(L.in_map, w["w_in"].shape[1])
        self.up_pieces = _col_pieces(L.up_map, w["w_up"].shape[1])
        self.received = {}

    def _shard(self, n):
        return self.w[n].astype(BF16) if n in MATMUL_WEIGHTS else self.w[n]

    def first_weights(self):
        L, (g, r, c) = self.L, self.w["pool_w"].shape
        G = dict(zip(self.FIRST, _communicate(_Gather([self._shard(n) for n in self.FIRST]), "gather_first")))
        return {"win_p": _assemble_cols(G["w_in"], self.in_pieces, L.in_w, "assemble_w_in"),
                "conv_qkv": _cols_to_full(G["conv_qkv"]),
                "cw3_p": _two_halves(_cols_to_full(G["ffn_conv_w"]), L.F, L.Fp),
                "pool_w": jnp.transpose(G["pool_w"], (1, 0, 2, 3)).reshape(g, N_DEV * r, c).astype(BF16)}

    def gather(self, call):
        return _Gather([self._shard(n) for n in self.GATHERS[call]])

    def weights_from(self, call, results):
        L, P = self.L, {}
        for n, shards in zip(self.GATHERS[call], results):
            if n == "w_up":
                P["wup_p"] = _assemble_cols(shards, self.up_pieces, 2 * L.Fp, "assemble_w_up")
            elif n == "w_down":
                P["wdown_p"] = jnp.pad(_rows_to_full(shards), ((0, L.Fp - L.F), (0, 0)))
            else:
                P[n] = _rows_to_full(shards) if n in ROW_SHARDED else _cols_to_full(shards)
        return P

    def _slices(self, g, n):
        L, w = self.L, self.w
        if n == "w_in":
            return _split_cols(g["w_in_p"], self.in_pieces, w["w_in"].shape[1], "split_w_in")
        if n == "w_up":
            return _split_cols(g["w_up_p"], self.up_pieces, w["w_up"].shape[1], "split_w_up")
        if n == "w_down":
            return _full_to_rows(g["w_down_p"][:L.F])
        if n == "ffn_conv_w":
            return _full_to_cols(_from_halves(g["ffn_conv_w_p"], L.F, L.Fp))
        if n == "pool_w":
            grp, r, c = w["pool_w"].shape
            return jnp.transpose(g[n].reshape(grp, N_DEV, r, c), (1, 0, 2, 3)).reshape(N_DEV, grp * r, c)
        return _full_to_rows(g[n]) if n in ROW_SHARDED else _full_to_cols(g[n])

    def exchange(self, call, g):
        return _Exchange([self._slices(g, n) for n in self.EXCHANGES[call]])

    def receive(self, call, results):
        self.received.update(zip(self.EXCHANGES[call], results))


def kernel(x, mem, mix_pre_norm, w_in, conv_qkv, a_log, dt_bias, gdn_norm, pool_w, pool_scale, w_branch_a, w_branch_b, w_mix_out, mix_post_norm, xa_pre_norm, mem_norm, w_xq, w_xkv, w_xo, xa_post_norm, ffn_pre_norm, w_up, ffn_conv_w, ffn_conv_b, w_down, ffn_post_norm, loss_target, m_mix_pre_norm, m_w_in, m_conv_qkv, m_a_log, m_dt_bias, m_gdn_norm, m_pool_w, m_pool_scale, m_w_branch_a, m_w_branch_b, m_w_mix_out, m_mix_post_norm, m_xa_pre_norm, m_mem_norm, m_w_xq, m_w_xkv, m_w_xo, m_xa_post_norm, m_ffn_pre_norm, m_w_up, m_ffn_conv_w, m_ffn_conv_b, m_w_down, m_ffn_post_norm, v_mix_pre_norm, v_w_in, v_conv_qkv, v_a_log, v_dt_bias, v_gdn_norm, v_pool_w, v_pool_scale, v_w_branch_a, v_w_branch_b, v_w_mix_out, v_mix_post_norm, v_xa_pre_norm, v_mem_norm, v_w_xq, v_w_xkv, v_w_xo, v_xa_post_norm, v_ffn_pre_norm, v_w_up, v_ffn_conv_w, v_ffn_conv_b, v_w_down, v_ffn_post_norm):
    given = dict(locals())
    w = {n: given[n][0] for n in WEIGHTS}
    m = {n: given["m_" + n][0] for n in WEIGHTS}
    v = {n: given["v_" + n][0] for n in WEIGHTS}
    D = x.shape[-1]
    F = w["w_down"].shape[0] * N_DEV
    L = _Layout(D, w["a_log"].shape[-1], w["pool_scale"].shape[-1], F)
    Fp = L.Fp

    comm = _StepComm(w, L)
    P = {n: w[n] for n in REPLICATED}
    P.update(comm.first_weights())
    P["fb_p"] = _two_halves(w["ffn_conv_b"].reshape(1, 2 * F), F, Fp)
    loss, grad_x, g = _local_step(x[0], mem[0], loss_target[0], P, L, comm)

    received = comm.received
    outs = {}
    for n in SHARDED:
        as2d = lambda a: a.reshape(-1, a.shape[-1])
        res = _adamw(received[n], as2d(w[n]), as2d(m[n]), as2d(v[n]), "adamw_" + n)
        outs[n] = [r.reshape(w[n].shape) for r in res]

    g["ffn_conv_b"] = _from_halves(g["ffn_conv_b_p"], F, Fp)
    rep_parts, rep_layout = _pack([g[n].reshape(w[n].shape) for n in REPLICATED] + [loss])
    rep_all, = _communicate(_Gather([rep_parts]), "gather_small_grads")
    zero_loss = jnp.zeros_like(loss)
    wr, _ = _pack([w[n] for n in REPLICATED] + [zero_loss])
    mr, _ = _pack([m[n] for n in REPLICATED] + [zero_loss])
    vr, _ = _pack([v[n] for n in REPLICATED] + [zero_loss])
    outs_rep = [_unpack(o, rep_layout) for o in _adamw(rep_all, wr, mr, vr, "adamw_replicated")]
    loss_total = outs_rep[0][-1][0, 0]
    for i, n in enumerate(REPLICATED):
        outs[n] = [outs_rep[k][i] for k in range(4)]

    result = [loss_total, grad_x[None]]
    for k in range(4):
        for n in WEIGHTS:
            result.append(outs[n][k][None])
    return tuple(result)
```

```python
import functools

import jax
import jax.numpy as jnp
from jax import lax
from jax.experimental import pallas as pl
from jax.experimental.pallas import tpu as pltpu

F32, BF16 = jnp.float32, jnp.bfloat16
MESH = pl.DeviceIdType.MESH
ANY = pl.BlockSpec(memory_space=pl.ANY)

N_DEV = 8
EPS = 1e-6
CHUNK = 64
HD = 128
GDN_CONV = 4
FFN_CONV = 3
POOL_WINDOWS = (2, 4, 8, 16)
XA_HEADS = 4
HALO = 16
RC = 128
SEQ_TILE = 1024
LANE = 128
VMEM_LIMIT = 48 * 1024 * 1024

ADAM_LR, ADAM_B1, ADAM_B2, ADAM_EPS, ADAM_WD, ADAM_STEP = 0.001, 0.9, 0.999, 1e-08, 0.01, 10


def _tile(n, pref, align=LANE):
    best = None
    t = align
    while t <= min(n, pref):
        if n % t == 0:
            best = t
        t += align
    return best if best is not None else n


def _params(**kw):
    return pltpu.CompilerParams(vmem_limit_bytes=VMEM_LIMIT, **kw)


def _sigmoid(x):
    return 1.0 / (1.0 + jnp.exp(-x))


def _silu(x):
    return x * _sigmoid(x)


def _dsilu(x):
    s = _sigmoid(x)
    return s * (1.0 + x * (1.0 - s))


def _colsum8(t):
    return t.reshape(t.shape[0] // 8, 8, t.shape[1]).sum(axis=0)


def _ride(body, n_in, n_out, rider, first, middle, last):
    if rider is None:
        return body
    n = rider.n

    def wrapped(*refs):
        ins, r_in = refs[:n_in], refs[n_in:n_in + n]
        outs, r_out = refs[n_in + n:n_in + n + n_out], refs[n_in + n + n_out:n_in + 2 * n + n_out]
        rest = refs[n_in + 2 * n + n_out:]
        start, hand_on, finish = rider.phases(r_in, r_out, *rest[-3:])
        pl.when(first())(start)
        body(*ins, *outs, *rest[:-3])
        if hand_on is not None:
            pl.when(middle())(hand_on)
        pl.when(last())(finish)

    return wrapped


def _matmul(a, b, mode, out_dtype, name, tm=1024, tn=1024, tk=2816, rider=None):
    ga = a.shape[0] if (mode == "nt" and a.ndim == 3) else 1
    gb = b.shape[0] if (mode == "tn" and b.ndim == 3) else 1
    if mode == "nn":
        (M, K), (K2, N) = a.shape, b.shape
    elif mode == "nt":
        M, K = a.shape[-2], ga * a.shape[-1]
        N, K2 = b.shape
    else:
        K, M = a.shape
        K2, N = b.shape[-2], gb * b.shape[-1]
    assert K == K2, (name, a.shape, b.shape)
    tm, tn = _tile(M, tm), _tile(N // gb, tn)
    tk = K // ga if K // ga <= tk else _tile(K // ga, tk)
    nk = K // tk
    kpg, npg = K // ga // tk, N // gb // tn
    if mode == "nn":
        a_spec = pl.BlockSpec((tm, tk), lambda i, j, k: (i, k))
        b_spec = pl.BlockSpec((tk, tn), lambda i, j, k: (k, j))
        dims = (((1,), (0,)), ((), ()))
    elif mode == "nt":
        a_spec = (pl.BlockSpec((tm, tk), lambda i, j, k: (i, k)) if a.ndim == 2 else
                  pl.BlockSpec((None, tm, tk), lambda i, j, k: (k // kpg, i, k % kpg)))
        b_spec = pl.BlockSpec((tn, tk), lambda i, j, k: (j, k))
        dims = (((1,), (1,)), ((), ()))
    else:
        a_spec = pl.BlockSpec((tk, tm), lambda i, j, k: (k, i))
        b_spec = (pl.BlockSpec((tk, tn), lambda i, j, k: (k, j)) if b.ndim == 2 else
                  pl.BlockSpec((None, tk, tn), lambda i, j, k: (j // npg, k, j % npg)))
        dims = (((0,), (0,)), ((), ()))

    def body(a_ref, b_ref, o_ref, acc):
        part = lax.dot_general(a_ref[...], b_ref[...], dims, preferred_element_type=F32)
        if nk == 1:
            o_ref[...] = part.astype(o_ref.dtype)
        else:
            k = pl.program_id(2)

            @pl.when(k == 0)
            def _():
                acc[...] = part

            @pl.when(k > 0)
            def _():
                acc[...] += part

            @pl.when(k == nk - 1)
            def _():
                o_ref[...] = acc[...].astype(o_ref.dtype)

    grid = (M // tm, N // tn, nk)
    at = lambda step: lambda: ((pl.program_id(0) == step[0]) & (pl.program_id(1) == step[1])
                               & (pl.program_id(2) == step[2]))
    extra = rider.n if rider is not None else 0
    res = pl.pallas_call(
        _ride(body, 2, 1, rider, at((0, 0, 0)), at((grid[0] // 2, 0, 0)), at((grid[0] - 1, grid[1] - 1, nk - 1))),
        name=name, out_shape=[jax.ShapeDtypeStruct((M, N), out_dtype)] + (rider.out_shape if rider else []),
        grid=grid, in_specs=[a_spec, b_spec] + [ANY] * extra,
        out_specs=[pl.BlockSpec((tm, tn), lambda i, j, k: (i, j))] + [ANY] * extra,
        scratch_shapes=[pltpu.VMEM((tm, tn) if nk > 1 else (8, LANE), F32)] + (rider.scratch if rider else []),
        compiler_params=_params(dimension_semantics=("arbitrary",) * 3 if rider else ("parallel", "parallel", "arbitrary")),
    )(a, b, *(rider.arrays if rider else []))
    return (res[0], res[1:]) if rider else res[0]


def _rstd(xf):
    return lax.rsqrt(jnp.mean(xf * xf, axis=-1, keepdims=True) + EPS)


def _rms_bwd(xf, w, dy):
    r = _rstd(xf)
    g = dy * w
    dx = r * g - xf * (r * r * r) * jnp.mean(g * xf, axis=-1, keepdims=True)
    return dx, dy * xf * r


def _row_tile(rows):
    return _tile(rows, 256, 8)


def _prenorm(x, w, name):
    rows, d = x.shape
    ts = _row_tile(rows)

    def body(x_ref, w_ref, h_ref):
        xf = x_ref[...]
        h_ref[...] = (xf * _rstd(xf) * w_ref[...]).astype(BF16)

    return pl.pallas_call(
        body, name=name, out_shape=jax.ShapeDtypeStruct((rows, d), BF16), grid=(rows // ts,),
        in_specs=[pl.BlockSpec((ts, d), lambda i: (i, 0)), pl.BlockSpec((1, d), lambda i: (0, 0))],
        out_specs=pl.BlockSpec((ts, d), lambda i: (i, 0)), compiler_params=_params(),
    )(x, w)


def _post_pre(xres, y, w_post, w_pre, name):
    rows, d = xres.shape
    ts = _row_tile(rows)

    def body(x_ref, y_ref, wp_ref, wn_ref, xo_ref, h_ref):
        yf = y_ref[...]
        xn = x_ref[...] + yf * _rstd(yf) * wp_ref[...]
        xo_ref[...] = xn
        h_ref[...] = (xn * _rstd(xn) * wn_ref[...]).astype(BF16)

    row = pl.BlockSpec((ts, d), lambda i: (i, 0))
    vec = pl.BlockSpec((1, d), lambda i: (0, 0))
    return pl.pallas_call(
        body, name=name, grid=(rows // ts,),
        out_shape=(jax.ShapeDtypeStruct((rows, d), F32), jax.ShapeDtypeStruct((rows, d), BF16)),
        in_specs=[row, row, vec, vec], out_specs=(row, row), compiler_params=_params(),
    )(xres, y, w_post, w_pre)


def _post_loss(xres, y, w_post, target, name):
    rows, d = xres.shape
    ts = _row_tile(rows)
    n = rows // ts

    def body(x_ref, y_ref, wp_ref, t_ref, dx_ref, loss_ref, acc):
        i = pl.program_id(0)
        yf = y_ref[...]
        diff = x_ref[...] + yf * _rstd(yf) * wp_ref[...] - t_ref[...]
        dx_ref[...] = diff * (1.0 / d)

        @pl.when(i == 0)
        def _():
            acc[...] = jnp.zeros_like(acc)

        acc[...] += _colsum8(diff * diff)

        @pl.when(i == n - 1)
        def _():
            loss_ref[...] = jnp.broadcast_to((0.5 / d) * jnp.sum(acc[...]), loss_ref.shape)

    row = pl.BlockSpec((ts, d), lambda i: (i, 0))
    vec = pl.BlockSpec((1, d), lambda i: (0, 0))
    return pl.pallas_call(
        body, name=name, grid=(n,),
        out_shape=(jax.ShapeDtypeStruct((rows, d), F32), jax.ShapeDtypeStruct((1, LANE), F32)),
        in_specs=[row, row, vec, row], out_specs=(row, pl.BlockSpec((1, LANE), lambda i: (0, 0))),
        scratch_shapes=[pltpu.VMEM((8, d), F32)], compiler_params=_params(),
    )(xres, y, w_post, target)


def _post_bwd(y, w_post, dxn, name):
    rows, d = y.shape
    ts = _row_tile(rows)
    n = rows // ts

    def body(y_ref, w_ref, d_ref, dy_ref, dw_ref, acc):
        i = pl.program_id(0)
        dy, dwr = _rms_bwd(y_ref[...], w_ref[...], d_ref[...])
        dy_ref[...] = dy.astype(BF16)

        @pl.when(i == 0)
        def _():
            acc[...] = jnp.zeros_like(acc)

        acc[...] += _colsum8(dwr)

        @pl.when(i == n - 1)
        def _():
            dw_ref[...] = jnp.sum(acc[...], axis=0, keepdims=True)

    row = pl.BlockSpec((ts, d), lambda i: (i, 0))
    vec = pl.BlockSpec((1, d), lambda i: (0, 0))
    return pl.pallas_call(
        body, name=name, grid=(n,),
        out_shape=(jax.ShapeDtypeStruct((rows, d), BF16), jax.ShapeDtypeStruct((1, d), F32)),
        in_specs=[row, vec, row], out_specs=(row, vec),
        scratch_shapes=[pltpu.VMEM((8, d), F32)], compiler_params=_params(),
    )(y, w_post, dxn)


def _pre_bwd(x, w_pre, dh, dres, name):
    rows, d = x.shape
    ts = _row_tile(rows)
    n = rows // ts

    def body(x_ref, w_ref, dh_ref, dr_ref, dx_ref, dw_ref, acc):
        i = pl.program_id(0)
        dx, dwr = _rms_bwd(x_ref[...], w_ref[...], dh_ref[...].astype(F32))
        dx_ref[...] = dr_ref[...] + dx

        @pl.when(i == 0)
        def _():
            acc[...] = jnp.zeros_like(acc)

        acc[...] += _colsum8(dwr)

        @pl.when(i == n - 1)
        def _():
            dw_ref[...] = jnp.sum(acc[...], axis=0, keepdims=True)

    row = pl.BlockSpec((ts, d), lambda i: (i, 0))
    vec = pl.BlockSpec((1, d), lambda i: (0, 0))
    return pl.pallas_call(
        body, name=name, grid=(n,),
        out_shape=(jax.ShapeDtypeStruct((rows, d), F32), jax.ShapeDtypeStruct((1, d), F32)),
        in_specs=[row, vec, row, row], out_specs=(row, vec),
        scratch_shapes=[pltpu.VMEM((8, d), F32)], compiler_params=_params(),
    )(x, w_pre, dh, dres)


def _prev_halo_spec(ts, cw, col0=0):
    return pl.BlockSpec((HALO, cw), lambda i, j: (jnp.maximum(i * (ts // HALO) - 1, 0), j + col0))


def _fill_causal(ext, tile_f32, halo_f32, i):
    ext[pl.ds(0, HALO), :] = jnp.where(i > 0, halo_f32, 0.0)
    ext[pl.ds(HALO, tile_f32.shape[0]), :] = tile_f32


def _row_chunks(ts, chunk):
    assert ts % RC == 0, (ts, RC)

    def step(c, carry):
        chunk(pl.multiple_of(c * RC, RC))
        return carry

    lax.fori_loop(0, ts // RC, step, 0)


def _causal_taps(ext, r0, K):
    blk = ext[pl.ds(r0 + HALO - 8, RC + 8), :]
    return [blk[8 - j:8 - j + RC] for j in range(K)]


def _advanced_taps(ext, r0, K):
    blk = ext[pl.ds(r0, RC + 8), :]
    return [blk[j:j + RC] for j in range(K)]


def _filter(wv, taps):
    K = len(taps)
    acc = wv[K - 1:K, :] * taps[0]
    for t in range(K - 1):
        acc = acc + wv[t:t + 1, :] * taps[K - 1 - t]
    return acc


def _conv_t(dc, w, name, col_tile=512):
    G, S, C = dc.shape
    K = w.shape[1]
    ts, cw = _tile(S, SEQ_TILE, HALO), _tile(C, col_tile)
    n = S // ts

    def body(d_ref, nx_ref, w_ref, o_ref, ext):
        i = pl.program_id(1)
        ext[pl.ds(0, ts), :] = d_ref[...].astype(F32)
        ext[pl.ds(ts, HALO), :] = jnp.where(i < n - 1, nx_ref[...].astype(F32), 0.0)
        wv = w_ref[...]

        def chunk(r0):
            taps = _advanced_taps(ext, r0, K)
            acc = wv[K - 1:K, :] * taps[0]
            for j in range(K - 1):
                acc = acc + wv[j:j + 1, :] * taps[K - 1 - j]
            o_ref[pl.ds(r0, RC), :] = acc.astype(o_ref.dtype)

        _row_chunks(ts, chunk)

    return pl.pallas_call(
        body, name=name, out_shape=jax.ShapeDtypeStruct((G, S, C), BF16), grid=(G, n, C // cw),
        in_specs=[pl.BlockSpec((None, ts, cw), lambda g, i, j: (g, i, j)),
                  pl.BlockSpec((None, HALO, cw),
                               lambda g, i, j: (g, jnp.minimum((i + 1) * (ts // HALO), S // HALO - 1), j)),
                  pl.BlockSpec((None, K, cw), lambda g, i, j: (g, 0, j))],
        out_specs=pl.BlockSpec((None, ts, cw), lambda g, i, j: (g, i, j)),
        scratch_shapes=[pltpu.VMEM((ts + HALO, cw), F32)], compiler_params=_params(),
    )(dc, dc, w)


def _qkv_conv(proj, conv_w, qkv_w, name):
    S = proj.shape[0]
    H3 = qkv_w // HD
    H = H3 // 3
    hb = 4 if H % 4 == 0 else 1
    cw = hb * HD
    ts = _tile(S, SEQ_TILE, HALO)
    per_kind = H // hb

    def body(x_ref, h_ref, w_ref, o_ref, ext):
        i, j = pl.program_id(0), pl.program_id(1)
        _fill_causal(ext, x_ref[...].astype(F32), h_ref[...].astype(F32), i)
        wv = w_ref[...]
        kind = j // per_kind
        scale = jnp.where(kind == 0, HD ** -0.5, 1.0)

        def chunk(r0):
            s = _silu(_filter(wv, _causal_taps(ext, r0, GDN_CONV)))
            for a in range(hb):
                sa = s[:, HD * a:HD * (a + 1)]
                r = lax.rsqrt(jnp.sum(sa * sa, axis=-1, keepdims=True) + EPS)
                o_ref[a, pl.ds(r0, RC), :] = jnp.where(kind == 2, sa, sa * r * scale)

        _row_chunks(ts, chunk)

    return pl.pallas_call(
        body, name=name, out_shape=jax.ShapeDtypeStruct((H3, S, HD), F32), grid=(S // ts, qkv_w // cw),
        in_specs=[pl.BlockSpec((ts, cw), lambda i, j: (i, j)), _prev_halo_spec(ts, cw),
                  pl.BlockSpec((GDN_CONV, cw), lambda i, j: (0, j))],
        out_specs=pl.BlockSpec((hb, ts, HD), lambda i, j: (j, i, 0)),
        scratch_shapes=[pltpu.VMEM((ts + HALO, cw), F32)], compiler_params=_params(),
    )(proj, proj, conv_w)


def _qkv_conv_bwd(proj, conv_w, dqkv_hm, qkv_w, name):
    S = proj.shape[0]
    H = qkv_w // HD // 3
    hb = 4 if H % 4 == 0 else 1
    cw = hb * HD
    ts = _tile(S, SEQ_TILE, HALO)
    n = S // ts
    per_kind = H // hb

    def body(x_ref, h_ref, w_ref, d_ref, dc_ref, dw_ref, ext, acc):
        j, i = pl.program_id(0), pl.program_id(1)
        _fill_causal(ext, x_ref[...].astype(F32), h_ref[...].astype(F32), i)
        wv = w_ref[...]
        kind = j // per_kind
        scale = jnp.where(kind == 0, HD ** -0.5, 1.0)

        @pl.when(i == 0)
        def _():
            acc[...] = jnp.zeros_like(acc)

        def chunk(r0):
            taps = _causal_taps(ext, r0, GDN_CONV)
            c = _filter(wv, taps)
            s = _silu(c)
            parts = []
            for a in range(hb):
                sa = s[:, HD * a:HD * (a + 1)]
                dy = d_ref[a, pl.ds(r0, RC), :]
                r = lax.rsqrt(jnp.sum(sa * sa, axis=-1, keepdims=True) + EPS)
                dn = scale * (r * dy - sa * (r * r * r) * jnp.sum(dy * sa, axis=-1, keepdims=True))
                parts.append(jnp.where(kind == 2, dy, dn))
            dc = jnp.concatenate(parts, axis=1) * _dsilu(c)
            dc_ref[pl.ds(r0, RC), :] = dc.astype(BF16)
            for t in range(GDN_CONV):
                acc[t] += _colsum8(dc * taps[GDN_CONV - 1 - t])

        _row_chunks(ts, chunk)

        @pl.when(i == n - 1)
        def _():
            dw_ref[...] = jnp.sum(acc[...], axis=1)

    return pl.pallas_call(
        body, name=name, grid=(qkv_w // cw, n),
        out_shape=(jax.ShapeDtypeStruct((S, qkv_w), BF16), jax.ShapeDtypeStruct((GDN_CONV, qkv_w), F32)),
        in_specs=[pl.BlockSpec((ts, cw), lambda j, i: (i, j)),
                  pl.BlockSpec((HALO, cw), lambda j, i: (jnp.maximum(i * (ts // HALO) - 1, 0), j)),
                  pl.BlockSpec((GDN_CONV, cw), lambda j, i: (0, j)),
                  pl.BlockSpec((hb, ts, HD), lambda j, i: (j, i, 0))],
        out_specs=(pl.BlockSpec((ts, cw), lambda j, i: (i, j)), pl.BlockSpec((GDN_CONV, cw), lambda j, i: (0, j))),
        scratch_shapes=[pltpu.VMEM((ts + HALO, cw), F32), pltpu.VMEM((GDN_CONV, 8, cw), F32)],
        compiler_params=_params(),
    )(proj, proj, conv_w, dqkv_hm)


def _chunk_cumsum(x):
    row = lax.broadcasted_iota(jnp.int32, x.shape, 0) & (CHUNK - 1)
    s = 1
    while s < CHUNK:
        x = x + jnp.where(row >= s, pltpu.roll(x, s, axis=0), 0.0)
        s *= 2
    return x


def _chunk_rev_cumsum(x):
    rows = x.shape[0]
    row = lax.broadcasted_iota(jnp.int32, x.shape, 0) & (CHUNK - 1)
    s = 1
    while s < CHUNK:
        x = x + jnp.where(row < CHUNK - s, pltpu.roll(x, rows - s, axis=0), 0.0)
        s *= 2
    return x


def _softplus(x):
    return jnp.maximum(x, 0.0) + jnp.log1p(jnp.exp(-jnp.abs(x)))


def _gates(proj, ba_col, a_log_l, dt_bias_l, H, name):
    S = proj.shape[0]
    ts = _tile(S, 512, CHUNK)

    def body(x_ref, al_ref, dt_ref, o_ref):
        x = x_ref[...].astype(F32)
        lane = lax.broadcasted_iota(jnp.int32, x.shape, 1)
        g = -jnp.exp(al_ref[...]) * _softplus(x + dt_ref[...])
        G = _chunk_cumsum(jnp.where((lane >= H) & (lane < 2 * H), g, 0.0))
        o_ref[...] = jnp.where(lane < H, _sigmoid(x), G)

    return pl.pallas_call(
        body, name=name, out_shape=jax.ShapeDtypeStruct((S, LANE), F32), grid=(S // ts,),
        in_specs=[pl.BlockSpec((ts, LANE), lambda i: (i, ba_col)), pl.BlockSpec((1, LANE), lambda i: (0, 0)),
                  pl.BlockSpec((1, LANE), lambda i: (0, 0))],
        out_specs=pl.BlockSpec((ts, LANE), lambda i: (i, 0)), compiler_params=_params(),
    )(proj, a_log_l, dt_bias_l)


def _gates_bwd(proj, ba_col, ba_w, a_log_l, dt_bias_l, dbg, H, name):
    S = proj.shape[0]
    ts = _tile(S, 512, CHUNK)
    n = S // ts

    def body(x_ref, al_ref, dt_ref, d_ref, o_ref, dal_ref, ddt_ref, acc):
        i = pl.program_id(0)
        x = x_ref[...].astype(F32)
        d = d_ref[...]
        lane = lax.broadcasted_iota(jnp.int32, x.shape, 1)
        is_a = (lane >= H) & (lane < 2 * H)
        beta = _sigmoid(x)
        nea = -jnp.exp(al_ref[...])
        z = x + dt_ref[...]
        dg = _chunk_rev_cumsum(jnp.where(is_a, d, 0.0))
        da_raw = dg * nea * _sigmoid(z)
        o = jnp.where(lane < H, d * beta * (1.0 - beta), jnp.where(is_a, da_raw, 0.0))
        if ba_w > LANE:
            o = jnp.concatenate([o, jnp.zeros((ts, ba_w - LANE), F32)], axis=1)
        o_ref[...] = o.astype(BF16)

        @pl.when(i == 0)
        def _():
            acc[...] = jnp.zeros_like(acc)

        acc[0] += _colsum8(jnp.where(is_a, dg * nea * _softplus(z), 0.0))
        acc[1] += _colsum8(jnp.where(is_a, da_raw, 0.0))

        @pl.when(i == n - 1)
        def _():
            dal_ref[...] = jnp.sum(acc[0], axis=0, keepdims=True)
            ddt_ref[...] = jnp.sum(acc[1], axis=0, keepdims=True)

    vec = pl.BlockSpec((1, LANE), lambda i: (0, 0))
    return pl.pallas_call(
        body, name=name, grid=(n,),
        out_shape=(jax.ShapeDtypeStruct((S, ba_w), BF16), jax.ShapeDtypeStruct((1, LANE), F32),
                   jax.ShapeDtypeStruct((1, LANE), F32)),
        in_specs=[pl.BlockSpec((ts, LANE), lambda i: (i, ba_col)), vec, vec, pl.BlockSpec((ts, LANE), lambda i: (i, 0))],
        out_specs=(pl.BlockSpec((ts, ba_w), lambda i: (i, 0)), vec, vec),
        scratch_shapes=[pltpu.VMEM((2, 8, LANE), F32)], compiler_params=_params(),
    )(proj, a_log_l, dt_bias_l, dbg)


_BMM_FORMS = {"nn": "hik,hkj->hij", "nt": "hik,hjk->hij", "tn": "hki,hkj->hij"}


def _split_bf16(a):
    hi = a.astype(BF16)
    return hi, (a - hi.astype(F32)).astype(BF16)


def _bmm(a, b, form="nn", exact=False):
    e = lambda x, y: jnp.einsum(_BMM_FORMS[form], x, y, preferred_element_type=F32)
    if not exact:
        return e(a.astype(BF16), b.astype(BF16))
    (ah, al), (bh, bl) = _split_bf16(a), _split_bf16(b)
    return (e(ah, bl) + e(al, bh)) + e(ah, bh)


def _unit_lower_inverse(L, r, c):
    eye = (r == c).astype(F32)
    m = jnp.where((r >> 3) == (c >> 3), -L, 0.0)
    m2 = _bmm(m, m, exact=True)
    m4 = _bmm(m2, m2, exact=True)
    x = eye + m
    x = x + _bmm(x, m2, exact=True)
    x = x + _bmm(x, m4, exact=True)
    for sh in (3, 4, 5):
        off = ((r >> (sh + 1)) == (c >> (sh + 1))) & ((r >> sh) != (c >> sh))
        x = x - _bmm(x, _bmm(jnp.where(off, L, 0.0), x))
    return x


def _to_row(col, eye):
    return jnp.sum(jnp.where(eye, jnp.broadcast_to(col, eye.shape), 0.0), axis=1, keepdims=True)


def _to_col(rowv, eye):
    return jnp.sum(jnp.where(eye, jnp.broadcast_to(rowv, eye.shape), 0.0), axis=2, keepdims=True)


def _gdn_chunk(q, k, v, bg, H):
    shape = (H, CHUNK, CHUNK)
    r = lax.broadcasted_iota(jnp.int32, shape, 1)
    c = lax.broadcasted_iota(jnp.int32, shape, 2)
    eye, incl, strict = r == c, r >= c, r > c
    beta = jnp.stack([bg[:, h:h + 1] for h in range(H)], axis=0)
    G = jnp.stack([bg[:, H + h:H + h + 1] for h in range(H)], axis=0)
    gap = jnp.broadcast_to(G, shape) - _to_row(G, eye)
    decay = jnp.where(incl, jnp.exp(jnp.where(incl, gap, 0.0)), 0.0)
    kk = _bmm(k, k, "nt")
    L = jnp.where(strict, beta * decay * kk, 0.0)
    ainv = _unit_lower_inverse(L, r, c)
    eG = jnp.exp(G)
    u_v = _bmm(ainv, beta * v)
    w_k = _bmm(ainv, (beta * eG) * k)
    qk = _bmm(q, k, "nt", exact=True)
    GL = G[:, CHUNK - 1:CHUNK, :]
    ek = jnp.exp(GL - G)
    return dict(eye=eye, strict=strict, r=r, c=c, beta=beta, G=G, decay=decay, kk=kk, ainv=ainv, eG=eG,
                u_v=u_v, w_k=w_k, qk=qk, attn=decay * qk, GL=GL, ek=ek, cd=jnp.exp(GL))


def _chunk_steps(N):
    at = lambda step: lambda: pl.program_id(0) == step
    return at(0), at(N - max(N // 8, 1)), at(N - 1)


def _gdn_fwd(qkv_hm, bg, H, name, rider=None):
    S = qkv_hm.shape[1]
    N = S // CHUNK
    extra = rider.n if rider is not None else 0

    def body(q_ref, k_ref, v_ref, bg_ref, o_ref, st_ref, state):
        n = pl.program_id(0)

        @pl.when(n == 0)
        def _():
            state[...] = jnp.zeros_like(state)

        q, k, v = q_ref[...], k_ref[...], v_ref[...]
        t = _gdn_chunk(q, k, v, bg_ref[...], H)
        s0 = state[...]
        st_ref[0] = s0
        u = t["u_v"] - _bmm(t["w_k"], s0)
        o_ref[...] = _bmm(q * t["eG"], s0) + _bmm(t["attn"], u)
        state[...] = t["cd"] * s0 + _bmm(k * t["ek"], u, "tn")

    blk = lambda kind: pl.BlockSpec((H, CHUNK, HD), lambda n: (kind, n, 0))
    res = pl.pallas_call(
        _ride(body, 4, 2, rider, *_chunk_steps(N)), name=name, grid=(N,),
        out_shape=[jax.ShapeDtypeStruct((H, S, HD), F32), jax.ShapeDtypeStruct((N, H, HD, HD), F32)]
        + (rider.out_shape if rider else []),
        in_specs=[blk(0), blk(1), blk(2), pl.BlockSpec((CHUNK, LANE), lambda n: (n, 0))] + [ANY] * extra,
        out_specs=[pl.BlockSpec((H, CHUNK, HD), lambda n: (0, n, 0)),
                   pl.BlockSpec((1, H, HD, HD), lambda n: (n, 0, 0, 0))] + [ANY] * extra,
        scratch_shapes=[pltpu.VMEM((H, HD, HD), F32)] + (rider.scratch if rider else []), compiler_params=_params(),
    )(qkv_hm, qkv_hm, qkv_hm, bg, *(rider.arrays if rider else []))
    return res[0], res[1], res[2:]


def _gdn_bwd(qkv_hm, bg, states, do_hm, H, name, rider=None):
    S = qkv_hm.shape[1]
    N = S // CHUNK
    extra = rider.n if rider is not None else 0

    def body(q_ref, k_ref, v_ref, bg_ref, st_ref, do_ref, dqkv_ref, dbg_ref, dstate):
        n = pl.program_id(0)

        @pl.when(n == 0)
        def _():
            dstate[...] = jnp.zeros_like(dstate)

        q, k, v, do = q_ref[...], k_ref[...], v_ref[...], do_ref[...]
        t = _gdn_chunk(q, k, v, bg_ref[...], H)
        eye, beta, eG, decay, kk, ainv = t["eye"], t["beta"], t["eG"], t["decay"], t["kk"], t["ainv"]
        s0 = st_ref[0]
        ds1 = dstate[...]
        u = t["u_v"] - _bmm(t["w_k"], s0)
        qdec, kdec = q * eG, k * t["ek"]
        d_qdec = _bmm(do, s0, "nt")
        d_attn = _bmm(do, u, "nt")
        du = _bmm(t["attn"], do, "tn") + _bmm(kdec, ds1)
        d_cd = jnp.sum(jnp.sum(ds1 * s0, axis=2, keepdims=True), axis=1, keepdims=True)
        d_kdec = _bmm(u, ds1, "nt")
        d_wk = -_bmm(du, s0, "nt")
        dstate[...] = t["cd"] * ds1 + _bmm(qdec, do, "tn") - _bmm(t["w_k"], du, "tn")
        d_rv = _bmm(ainv, du, "tn")
        d_rk = _bmm(ainv, d_wk, "tn")
        dL = jnp.where(t["strict"], -(_bmm(d_rv, t["u_v"], "nt") + _bmm(d_rk, t["w_k"], "nt")), 0.0)
        rk_k = jnp.sum(d_rk * k, axis=2, keepdims=True)
        d_beta = (jnp.sum(dL * decay * kk, axis=2, keepdims=True) + jnp.sum(d_rv * v, axis=2, keepdims=True)
                  + rk_k * eG)
        d_decay = dL * beta * kk + d_attn * t["qk"]
        d_kk = dL * beta * decay
        d_qk = d_attn * decay
        dqkv_ref[pl.ds(2 * H, H)] = beta * d_rv
        dqkv_ref[pl.ds(0, H)] = _bmm(d_qk, k) + d_qdec * eG
        dqkv_ref[pl.ds(H, H)] = ((beta * eG) * d_rk + _bmm(d_kk, k) + _bmm(d_kk, k, "tn") + _bmm(d_qk, q, "tn")
                       + d_kdec * t["ek"])
        d_eG = rk_k * beta + jnp.sum(d_qdec * q, axis=2, keepdims=True)
        e = jnp.sum(d_kdec * kdec, axis=2, keepdims=True)
        T = d_decay * decay
        dG = d_eG * eG - e + jnp.sum(T, axis=2, keepdims=True) - _to_col(jnp.sum(T, axis=1, keepdims=True), eye)
        dGL = jnp.sum(e, axis=1, keepdims=True) + d_cd * t["cd"]
        row1 = lax.broadcasted_iota(jnp.int32, (H, CHUNK, 1), 1)
        dG = dG + jnp.where(row1 == CHUNK - 1, dGL, 0.0)
        lane = lax.broadcasted_iota(jnp.int32, (CHUNK, LANE), 1)
        out = jnp.zeros((CHUNK, LANE), F32)
        for h in range(H):
            out = out + jnp.where(lane == h, d_beta[h], 0.0) + jnp.where(lane == H + h, dG[h], 0.0)
        dbg_ref[...] = out

    blk = lambda kind: pl.BlockSpec((H, CHUNK, HD), lambda n: (kind, N - 1 - n, 0))
    res = pl.pallas_call(
        _ride(body, 6, 2, rider, *_chunk_steps(N)), name=name, grid=(N,),
        out_shape=[jax.ShapeDtypeStruct((3 * H, S, HD), F32), jax.ShapeDtypeStruct((S, LANE), F32)]
        + (rider.out_shape if rider else []),
        in_specs=[blk(0), blk(1), blk(2), pl.BlockSpec((CHUNK, LANE), lambda n: (N - 1 - n, 0)),
                  pl.BlockSpec((1, H, HD, HD), lambda n: (N - 1 - n, 0, 0, 0)), blk(0)] + [ANY] * extra,
        out_specs=[pl.BlockSpec((3 * H, CHUNK, HD), lambda n: (0, N - 1 - n, 0)),
                   pl.BlockSpec((CHUNK, LANE), lambda n: (N - 1 - n, 0))] + [ANY] * extra,
        scratch_shapes=[pltpu.VMEM((H, HD, HD), F32)] + (rider.scratch if rider else []), compiler_params=_params(),
    )(qkv_hm, qkv_hm, qkv_hm, bg, states, do_hm, *(rider.arrays if rider else []))
    return res[0], res[1], res[2:]


def _gdn_out(o_hm, proj, z_col, gdn_w, name):
    H, S, _ = o_hm.shape
    vw = H * HD
    ts = _tile(S, 256, 8)

    def body(o_ref, z_ref, w_ref, y_ref):
        z = z_ref[...].astype(F32)
        w = w_ref[...]
        parts = []
        for h in range(H):
            o = o_ref[h]
            parts.append(o * _rstd(o) * w)
        y_ref[...] = (jnp.concatenate(parts, axis=1) * _silu(z)).astype(BF16)

    return pl.pallas_call(
        body, name=name, out_shape=jax.ShapeDtypeStruct((S, vw), BF16), grid=(S // ts,),
        in_specs=[pl.BlockSpec((H, ts, HD), lambda i: (0, i, 0)), pl.BlockSpec((ts, vw), lambda i: (i, z_col)),
                  pl.BlockSpec((1, HD), lambda i: (0, 0))],
        out_specs=pl.BlockSpec((ts, vw), lambda i: (i, 0)), compiler_params=_params(),
    )(o_hm, proj, gdn_w)


def _gdn_out_bwd(o_hm, proj, z_col, gdn_w, dy, name):
    H, S, _ = o_hm.shape
    vw = H * HD
    ts = _tile(S, 256, 8)
    n = S // ts

    def body(o_ref, z_ref, w_ref, dy_ref, do_ref, dz_ref, dw_ref, acc):
        i = pl.program_id(0)
        z = z_ref[...].astype(F32)
        dy = dy_ref[...].astype(F32)
        w = w_ref[...]
        gz = dy * _silu(z)
        normed, dwr = [], jnp.zeros((ts, HD), F32)
        for h in range(H):
            o = o_ref[h]
            dxo, dwh = _rms_bwd(o, w, gz[:, HD * h:HD * (h + 1)])
            do_ref[h] = dxo
            dwr = dwr + dwh
            normed.append(o * _rstd(o) * w)
        dz_ref[...] = (dy * jnp.concatenate(normed, axis=1) * _dsilu(z)).astype(BF16)

        @pl.when(i == 0)
        def _():
            acc[...] = jnp.zeros_like(acc)

        acc[...] += _colsum8(dwr)

        @pl.when(i == n - 1)
        def _():
            dw_ref[...] = jnp.sum(acc[...], axis=0, keepdims=True)

    return pl.pallas_call(
        body, name=name, grid=(n,),
        out_shape=(jax.ShapeDtypeStruct((H, S, HD), F32), jax.ShapeDtypeStruct((S, vw), BF16),
                   jax.ShapeDtypeStruct((1, HD), F32)),
        in_specs=[pl.BlockSpec((H, ts, HD), lambda i: (0, i, 0)), pl.BlockSpec((ts, vw), lambda i: (i, z_col)),
                  pl.BlockSpec((1, HD), lambda i: (0, 0)), pl.BlockSpec((ts, vw), lambda i: (i, 0))],
        out_specs=(pl.BlockSpec((H, ts, HD), lambda i: (0, i, 0)), pl.BlockSpec((ts, vw), lambda i: (i, 0)),
                   pl.BlockSpec((1, HD), lambda i: (0, 0))),
        scratch_shapes=[pltpu.VMEM((8, HD), F32)], compiler_params=_params(),
    )(o_hm, proj, gdn_w, dy)


def _pool_trailing(ext, ts, pg, row0):
    outs, inv_cnts = [], []
    t_abs = row0 + lax.broadcasted_iota(jnp.int32, (ts, 1), 0)
    for gi, win in enumerate(POOL_WINDOWS):
        cols = pl.ds(gi * pg, pg)
        cur = ext[pl.ds(HALO, ts), cols]
        acc = cur
        for j in range(1, win):
            acc = acc + ext[pl.ds(HALO - j, ts), cols]
        inv = 1.0 / jnp.minimum(t_abs + 1, win).astype(F32)
        outs.append(acc * inv - cur)
    return outs


def _pool_fwd(proj, p_col, pool_w, pool_scale, pw, name):
    S = proj.shape[0]
    pg = pw // len(POOL_WINDOWS)
    ts = _tile(S, 512, HALO)

    def body(x_ref, h_ref, w_ref, sc_ref, o_ref, ext):
        i = pl.program_id(0)
        _fill_causal(ext, x_ref[...].astype(F32), h_ref[...].astype(F32), i)
        ys = _pool_trailing(ext, ts, pg, i * ts)
        outs = [jnp.dot(ys[gi].astype(BF16), w_ref[gi], preferred_element_type=F32) for gi in range(len(ys))]
        o_ref[...] = (jnp.concatenate(outs, axis=1) * sc_ref[...]).astype(BF16)

    return pl.pallas_call(
        body, name=name, out_shape=jax.ShapeDtypeStruct((S, pw), BF16), grid=(S // ts,),
        in_specs=[pl.BlockSpec((ts, pw), lambda i: (i, p_col)),
                  pl.BlockSpec((HALO, pw), lambda i: (jnp.maximum(i * (ts // HALO) - 1, 0), p_col)),
                  pl.BlockSpec((len(POOL_WINDOWS), pg, pg), lambda i: (0, 0, 0)), pl.BlockSpec((1, pw), lambda i: (0, 0))],
        out_specs=pl.BlockSpec((ts, pw), lambda i: (i, 0)),
        scratch_shapes=[pltpu.VMEM((ts + HALO, pw), F32)], compiler_params=_params(),
    )(proj, proj, pool_w, pool_scale)


def _pool_bwd(proj, p_col, pool_w, pool_scale, dpb, pw, name):
    S = proj.shape[0]
    G = len(POOL_WINDOWS)
    pg = pw // G
    ts = _tile(S, 512, HALO)
    n = S // ts

    def body(x_ref, h_ref, w_ref, sc_ref, d_ref, dn_ref, dp_ref, dw_ref, dsc_ref, ext, zext, wacc, sacc):
        i = pl.program_id(0)
        _fill_causal(ext, x_ref[...].astype(F32), h_ref[...].astype(F32), i)
        ys = _pool_trailing(ext, ts, pg, i * ts)
        d_ext = jnp.concatenate([d_ref[...].astype(F32), jnp.where(i < n - 1, dn_ref[...].astype(F32), 0.0)], axis=0)
        dt = d_ext * sc_ref[...]
        t_abs = i * ts + lax.broadcasted_iota(jnp.int32, (ts + HALO, 1), 0)

        @pl.when(i == 0)
        def _():
            wacc[...] = jnp.zeros_like(wacc)
            sacc[...] = jnp.zeros_like(sacc)

        dps, tfs = [], []
        for gi, win in enumerate(POOL_WINDOWS):
            cols = slice(gi * pg, (gi + 1) * pg)
            w = w_ref[gi]
            dt_g = dt[:, cols].astype(BF16)
            y_g = ys[gi].astype(BF16)
            tfs.append(jnp.dot(y_g, w, preferred_element_type=F32))
            wacc[gi] += lax.dot_general(y_g, dt_g[:ts], (((0,), (0,)), ((), ())), preferred_element_type=F32)
            dyp = lax.dot_general(dt_g, w, (((1,), (1,)), ((), ())), preferred_element_type=F32)
            zext[:, pl.ds(gi * pg, pg)] = dyp * (1.0 / jnp.minimum(t_abs + 1, win).astype(F32))
            acc = -dyp[:ts]
            for j in range(win):
                acc = acc + zext[pl.ds(j, ts), pl.ds(gi * pg, pg)]
            dps.append(acc)
        dp_ref[...] = jnp.concatenate(dps, axis=1).astype(BF16)
        sacc[...] += _colsum8(d_ext[:ts] * jnp.concatenate(tfs, axis=1))

        @pl.when(i == n - 1)
        def _():
            dw_ref[...] = wacc[...]
            dsc_ref[...] = jnp.sum(sacc[...], axis=0, keepdims=True)

    return pl.pallas_call(
        body, name=name, grid=(n,),
        out_shape=(jax.ShapeDtypeStruct((S, pw), BF16), jax.ShapeDtypeStruct((G, pg, pg), F32),
                   jax.ShapeDtypeStruct((1, pw), F32)),
        in_specs=[pl.BlockSpec((ts, pw), lambda i: (i, p_col)),
                  pl.BlockSpec((HALO, pw), lambda i: (jnp.maximum(i * (ts // HALO) - 1, 0), p_col)),
                  pl.BlockSpec((G, pg, pg), lambda i: (0, 0, 0)), pl.BlockSpec((1, pw), lambda i: (0, 0)),
                  pl.BlockSpec((ts, pw), lambda i: (i, 0)),
                  pl.BlockSpec((HALO, pw), lambda i: (jnp.minimum((i + 1) * (ts // HALO), S // HALO - 1), 0))],
        out_specs=(pl.BlockSpec((ts, pw), lambda i: (i, 0)), pl.BlockSpec((G, pg, pg), lambda i: (0, 0, 0)),
                   pl.BlockSpec((1, pw), lambda i: (0, 0))),
        scratch_shapes=[pltpu.VMEM((ts + HALO, pw), F32), pltpu.VMEM((ts + HALO, pw), F32),
                        pltpu.VMEM((G, pg, pg), F32), pltpu.VMEM((8, pw), F32)],
        compiler_params=_params(),
    )(proj, proj, pool_w, pool_scale, dpb, dpb)


def _merge(proj, ga_col, gb_col, ya, yb, name):
    S, d = ya.shape
    ts = _tile(S, 512, 16)

    def body(ga_ref, gb_ref, ya_ref, yb_ref, o_ref):
        o_ref[...] = (_sigmoid(ga_ref[...].astype(F32)) * ya_ref[...].astype(F32)
                      + _sigmoid(gb_ref[...].astype(F32)) * yb_ref[...].astype(F32)).astype(BF16)

    row = pl.BlockSpec((ts, d), lambda i: (i, 0))
    return pl.pallas_call(
        body, name=name, out_shape=jax.ShapeDtypeStruct((S, d), BF16), grid=(S // ts,),
        in_specs=[pl.BlockSpec((ts, d), lambda i: (i, ga_col)), pl.BlockSpec((ts, d), lambda i: (i, gb_col)), row, row],
        out_specs=row, compiler_params=_params(),
    )(proj, proj, ya, yb)


def _merge_bwd(proj, ga_col, gb_col, ya, yb, dm, name):
    S, d = ya.shape
    ts = _tile(S, 512, 16)

    def body(ga_ref, gb_ref, ya_ref, yb_ref, dm_ref, dya_ref, dyb_ref, dga_ref, dgb_ref):
        dmv = dm_ref[...].astype(F32)
        sa, sb = _sigmoid(ga_ref[...].astype(F32)), _sigmoid(gb_ref[...].astype(F32))
        dya_ref[...] = (dmv * sa).astype(BF16)
        dyb_ref[...] = (dmv * sb).astype(BF16)
        dga_ref[...] = (dmv * ya_ref[...].astype(F32) * sa * (1.0 - sa)).astype(BF16)
        dgb_ref[...] = (dmv * yb_ref[...].astype(F32) * sb * (1.0 - sb)).astype(BF16)

    row = pl.BlockSpec((ts, d), lambda i: (i, 0))
    o = jax.ShapeDtypeStruct((S, d), BF16)
    return pl.pallas_call(
        body, name=name, out_shape=(o, o, o, o), grid=(S // ts,),
        in_specs=[pl.BlockSpec((ts, d), lambda i: (i, ga_col)), pl.BlockSpec((ts, d), lambda i: (i, gb_col)), row, row, row],
        out_specs=(row, row, row, row), compiler_params=_params(),
    )(proj, proj, ya, yb, dm)


def _xattn_fwd(q, kv, name):
    S, d = q.shape
    M = kv.shape[0]
    hd = d // XA_HEADS
    ts = _tile(S, 512, 16)
    scale = hd ** -0.5

    def body(q_ref, k_ref, v_ref, o_ref):
        s = lax.dot_general(q_ref[...], k_ref[...], (((1,), (1,)), ((), ())), preferred_element_type=F32) * scale
        p = jnp.exp(s - jnp.max(s, axis=-1, keepdims=True))
        p = p / jnp.sum(p, axis=-1, keepdims=True)
        o_ref[...] = jnp.dot(p.astype(BF16), v_ref[...], preferred_element_type=F32).astype(BF16)

    return pl.pallas_call(
        body, name=name, out_shape=jax.ShapeDtypeStruct((S, d), BF16), grid=(S // ts, XA_HEADS),
        in_specs=[pl.BlockSpec((ts, hd), lambda i, h: (i, h)), pl.BlockSpec((M, hd), lambda i, h: (0, h)),
                  pl.BlockSpec((M, hd), lambda i, h: (0, XA_HEADS + h))],
        out_specs=pl.BlockSpec((ts, hd), lambda i, h: (i, h)), compiler_params=_params(),
    )(q, kv, kv)


def _xattn_bwd(q, kv, do, name):
    S, d = q.shape
    M = kv.shape[0]
    hd = d // XA_HEADS
    ts = _tile(S, 512, 16)
    n = S // ts
    scale = hd ** -0.5

    def body(q_ref, k_ref, v_ref, do_ref, dq_ref, dk_ref, dv_ref, kacc, vacc):
        i = pl.program_id(1)
        qv, kv_, vv, dov = q_ref[...], k_ref[...], v_ref[...], do_ref[...]
        s = lax.dot_general(qv, kv_, (((1,), (1,)), ((), ())), preferred_element_type=F32) * scale
        p = jnp.exp(s - jnp.max(s, axis=-1, keepdims=True))
        p = p / jnp.sum(p, axis=-1, keepdims=True)
        dp = lax.dot_general(dov, vv, (((1,), (1,)), ((), ())), preferred_element_type=F32)
        ds = (p * (dp - jnp.sum(p * dp, axis=-1, keepdims=True)) * scale).astype(BF16)
        dq_ref[...] = jnp.dot(ds, kv_, preferred_element_type=F32).astype(BF16)

        @pl.when(i == 0)
        def _():
            kacc[...] = jnp.zeros_like(kacc)
            vacc[...] = jnp.zeros_like(vacc)

        kacc[...] += lax.dot_general(ds, qv, (((0,), (0,)), ((), ())), preferred_element_type=F32)
        vacc[...] += lax.dot_general(p.astype(BF16), dov, (((0,), (0,)), ((), ())), preferred_element_type=F32)

        @pl.when(i == n - 1)
        def _():
            dk_ref[...] = kacc[...]
            dv_ref[...] = vacc[...]

    dq, dk, dv = pl.pallas_call(
        body, name=name, grid=(XA_HEADS, n),
        out_shape=(jax.ShapeDtypeStruct((S, d), BF16), jax.ShapeDtypeStruct((M, d), F32), jax.ShapeDtypeStruct((M, d), F32)),
        in_specs=[pl.BlockSpec((ts, hd), lambda h, i: (i, h)), pl.BlockSpec((M, hd), lambda h, i: (0, h)),
                  pl.BlockSpec((M, hd), lambda h, i: (0, XA_HEADS + h)), pl.BlockSpec((ts, hd), lambda h, i: (i, h))],
        out_specs=(pl.BlockSpec((ts, hd), lambda h, i: (i, h)), pl.BlockSpec((M, hd), lambda h, i: (0, h)),
                   pl.BlockSpec((M, hd), lambda h, i: (0, h))),
        scratch_shapes=[pltpu.VMEM((M, hd), F32), pltpu.VMEM((M, hd), F32)], compiler_params=_params(),
    )(q, kv, kv, do)
    return dq, jnp.concatenate([dk, dv], axis=1)


def _ffn_act(up, conv_w, bias, name):
    S, F2 = up.shape
    F = F2 // 2
    ts, cw = _tile(S, SEQ_TILE, HALO), _tile(F, 512)
    nb = F // cw

    def body(a_ref, ah_ref, b_ref, bh_ref, wa_ref, wb_ref, ba_ref, bb_ref, o_ref, ea, eb):
        i = pl.program_id(0)
        _fill_causal(ea, a_ref[...].astype(F32), ah_ref[...].astype(F32), i)
        _fill_causal(eb, b_ref[...].astype(F32), bh_ref[...].astype(F32), i)
        wa, wb, bia, bib = wa_ref[...], wb_ref[...], ba_ref[...], bb_ref[...]

        def chunk(r0):
            ua = _filter(wa, _causal_taps(ea, r0, FFN_CONV)) + bia
            ub = _filter(wb, _causal_taps(eb, r0, FFN_CONV)) + bib
            o_ref[pl.ds(r0, RC), :] = (_silu(ua) * ub).astype(BF16)

        _row_chunks(ts, chunk)

    tile = lambda c0: pl.BlockSpec((ts, cw), lambda i, j: (i, j + c0))
    vec = lambda rows, c0: pl.BlockSpec((rows, cw), lambda i, j: (0, j + c0))
    return pl.pallas_call(
        body, name=name, out_shape=jax.ShapeDtypeStruct((S, F), BF16), grid=(S // ts, nb),
        in_specs=[tile(0), _prev_halo_spec(ts, cw), tile(nb), _prev_halo_spec(ts, cw, nb),
                  vec(FFN_CONV, 0), vec(FFN_CONV, nb), vec(1, 0), vec(1, nb)],
        out_specs=pl.BlockSpec((ts, cw), lambda i, j: (i, j)),
        scratch_shapes=[pltpu.VMEM((ts + HALO, cw), F32), pltpu.VMEM((ts + HALO, cw), F32)],
        compiler_params=_params(),
    )(up, up, up, up, conv_w, conv_w, bias, bias)


def _ffn_act_bwd(up, conv_w, bias, dact, name):
    S, F2 = up.shape
    F = F2 // 2
    ts, cw = _tile(S, SEQ_TILE, HALO), _tile(F, 512)
    nb = F // cw
    n = S // ts

    def body(a_ref, ah_ref, b_ref, bh_ref, wa_ref, wb_ref, ba_ref, bb_ref, d_ref,
             du_ref, dwa_ref, dwb_ref, dba_ref, dbb_ref, ea, eb, wacc, bacc):
        i = pl.program_id(1)
        _fill_causal(ea, a_ref[...].astype(F32), ah_ref[...].astype(F32), i)
        _fill_causal(eb, b_ref[...].astype(F32), bh_ref[...].astype(F32), i)
        wa, wb, bia, bib = wa_ref[...], wb_ref[...], ba_ref[...], bb_ref[...]

        @pl.when(i == 0)
        def _():
            wacc[...] = jnp.zeros_like(wacc)
            bacc[...] = jnp.zeros_like(bacc)

        def chunk(r0):
            ta, tb = _causal_taps(ea, r0, FFN_CONV), _causal_taps(eb, r0, FFN_CONV)
            ua, ub = _filter(wa, ta) + bia, _filter(wb, tb) + bib
            d = d_ref[pl.ds(r0, RC), :].astype(F32)
            dua = d * ub * _dsilu(ua)
            dub = d * _silu(ua)
            du_ref[0, pl.ds(r0, RC), :] = dua.astype(BF16)
            du_ref[1, pl.ds(r0, RC), :] = dub.astype(BF16)
            for t in range(FFN_CONV):
                wacc[0, t] += _colsum8(dua * ta[FFN_CONV - 1 - t])
                wacc[1, t] += _colsum8(dub * tb[FFN_CONV - 1 - t])
            bacc[0] += _colsum8(dua)
            bacc[1] += _colsum8(dub)

        _row_chunks(ts, chunk)

        @pl.when(i == n - 1)
        def _():
            dwa_ref[...] = jnp.sum(wacc[0], axis=1)
            dwb_ref[...] = jnp.sum(wacc[1], axis=1)
            dba_ref[...] = jnp.sum(bacc[0], axis=0, keepdims=True)
            dbb_ref[...] = jnp.sum(bacc[1], axis=0, keepdims=True)

    tile = lambda c0: pl.BlockSpec((ts, cw), lambda j, i: (i, j + c0))
    halo = lambda c0: pl.BlockSpec((HALO, cw), lambda j, i: (jnp.maximum(i * (ts // HALO) - 1, 0), j + c0))
    vec = lambda rows, c0: pl.BlockSpec((rows, cw), lambda j, i: (0, j + c0))
    du, dwa, dwb, dba, dbb = pl.pallas_call(
        body, name=name, grid=(nb, n),
        out_shape=(jax.ShapeDtypeStruct((2, S, F), BF16),
                   jax.ShapeDtypeStruct((FFN_CONV, F), F32), jax.ShapeDtypeStruct((FFN_CONV, F), F32),
                   jax.ShapeDtypeStruct((1, F), F32), jax.ShapeDtypeStruct((1, F), F32)),
        in_specs=[tile(0), halo(0), tile(nb), halo(nb), vec(FFN_CONV, 0), vec(FFN_CONV, nb), vec(1, 0), vec(1, nb), tile(0)],
        out_specs=(pl.BlockSpec((2, ts, cw), lambda j, i: (0, i, j)), vec(FFN_CONV, 0), vec(FFN_CONV, 0), vec(1, 0),
                   vec(1, 0)),
        scratch_shapes=[pltpu.VMEM((ts + HALO, cw), F32), pltpu.VMEM((ts + HALO, cw), F32),
                        pltpu.VMEM((2, FFN_CONV, 8, cw), F32), pltpu.VMEM((2, 8, cw), F32)],
        compiler_params=_params(),
    )(up, up, up, up, conv_w, conv_w, bias, bias, dact)
    return du, jnp.concatenate([dwa, dwb], axis=1), jnp.concatenate([dba, dbb], axis=1)


def _adamw(gparts, w, m, v, name):
    R, C = w.shape
    tr = _tile(R, max(16, (256 * 1024) // C), 16)

    def body(g_ref, w_ref, m_ref, v_ref, go_ref, d_ref, mo_ref, vo_ref):
        g = g_ref[0].astype(F32)
        for s in range(1, N_DEV):
            g = g + g_ref[s].astype(F32)
        mn = ADAM_B1 * m_ref[...] + (1.0 - ADAM_B1) * g
        vn = ADAM_B2 * v_ref[...] + (1.0 - ADAM_B2) * (g * g)
        m_hat = mn / (1.0 - ADAM_B1 ** ADAM_STEP)
        v_hat = vn / (1.0 - ADAM_B2 ** ADAM_STEP)
        go_ref[...] = g
        d_ref[...] = -ADAM_LR * (m_hat / (jnp.sqrt(v_hat) + ADAM_EPS) + ADAM_WD * w_ref[...])
        mo_ref[...] = mn
        vo_ref[...] = vn

    row = pl.BlockSpec((tr, C), lambda i: (i, 0))
    o = jax.ShapeDtypeStruct((R, C), F32)
    return pl.pallas_call(
        body, name=name, out_shape=(o, o, o, o), grid=(R // tr,),
        in_specs=[pl.BlockSpec((N_DEV, tr, C), lambda i: (0, i, 0)), row, row, row],
        out_specs=(row, row, row, row), compiler_params=_params(),
    )(gparts, w, m, v)


def _position():
    return lax.axis_index("x"), lax.axis_index("y"), lax.axis_index("c")


class _Copies:
    def __init__(self, arrays):
        self.arrays, self.n = list(arrays), len(arrays)
        self.scratch = [pltpu.SemaphoreType.DMA((7 * self.n,)), pltpu.SemaphoreType.DMA((7 * self.n,)),
                        pltpu.SemaphoreType.DMA((self.n,))]


class _Gather(_Copies):
    def __init__(self, arrays):
        super().__init__(arrays)
        self.out_shape = [jax.ShapeDtypeStruct((N_DEV,) + b.shape, b.dtype) for b in self.arrays]

    def phases(self, x_refs, out_refs, send_sems, recv_sems, local_sems):
        n = self.n
        x, y, c = _position()
        me, sibling = (x, y, c), (x, y, 1 - c)
        chips = [(1 - x, y), (x, 1 - y), (1 - x, 1 - y)]

        def copy(a, k, blk, to, own=False):
            slot = out_refs[a].at[4 * blk[0] + 2 * blk[1] + blk[2]]
            return pltpu.make_async_remote_copy(
                src_ref=x_refs[a] if own else slot, dst_ref=slot,
                send_sem=send_sems.at[7 * a + k], recv_sem=recv_sems.at[7 * a + k], device_id=to, device_id_type=MESH)

        mine = [pltpu.make_async_copy(x_refs[a], out_refs[a].at[4 * x + 2 * y + c], local_sems.at[a]) for a in range(n)]
        first = [copy(a, 0, me, sibling, own=True) for a in range(n)]
        first += [copy(a, 1 + j, me, (*chip, c), own=True) for a in range(n) for j, chip in enumerate(chips)]
        passed = [copy(a, 4 + j, (*chip, c), sibling) for j, chip in enumerate(chips) for a in range(n)]

        def start():
            for cp in mine + first:
                cp.start()

        def hand_on():
            for j, chip in enumerate(chips):
                for a in range(n):
                    copy(a, 1 + j, (*chip, c), me).wait_recv()
                    passed[j * n + a].start()

        def finish():
            for a in range(n):
                copy(a, 0, sibling, me).wait_recv()
            for j, chip in enumerate(chips):
                for a in range(n):
                    copy(a, 4 + j, (*chip, 1 - c), me).wait_recv()
            for cp in first + passed:
                cp.wait_send()
            for cp in mine:
                cp.wait()

        return start, hand_on, finish


class _Exchange(_Copies):
    def __init__(self, arrays):
        super().__init__(arrays)
        self.out_shape = [jax.ShapeDtypeStruct(p.shape, p.dtype) for p in self.arrays]

    def phases(self, p_refs, out_refs, send_sems, recv_sems, local_sems):
        n = self.n
        x, y, c = _position()
        my_slot = 4 * x + 2 * y + c
        mine = [pltpu.make_async_copy(p_refs[a].at[my_slot], out_refs[a].at[my_slot], local_sems.at[a]) for a in range(n)]
        copies = []
        for k in range(1, N_DEV):
            px, py, pc = x ^ (k >> 2), y ^ ((k >> 1) & 1), c ^ (k & 1)
            for a in range(n):
                copies.append(pltpu.make_async_remote_copy(
                    src_ref=p_refs[a].at[4 * px + 2 * py + pc], dst_ref=out_refs[a].at[my_slot],
                    send_sem=send_sems.at[7 * a + k - 1], recv_sem=recv_sems.at[7 * a + k - 1],
                    device_id=(px, py, pc), device_id_type=MESH))

        def start():
            for cp in mine + copies:
                cp.start()

        def finish():
            for cp in copies + mine:
                cp.wait()

        return start, None, finish


def _communicate(copies, name):
    n = copies.n

    def body(*refs):
        start, hand_on, finish = copies.phases(refs[:n], refs[n:2 * n], *refs[2 * n:])
        start()
        if hand_on is not None:
            hand_on()
        finish()

    return pl.pallas_call(body, name=name, out_shape=copies.out_shape, in_specs=[ANY] * n, out_specs=[ANY] * n,
                          scratch_shapes=copies.scratch)(*copies.arrays)


def _col_pieces(col_map, shard_w):
    pieces = []
    for lo, hi, dst in col_map:
        c = lo
        while c < hi:
            j = c // shard_w
            end = min(hi, (j + 1) * shard_w)
            pieces.append((j, c - j * shard_w, end - c, dst + (c - lo)))
            c = end
    return pieces


def _assemble_cols(shards, pieces, width, name):
    _, R, Cs = shards.shape
    tr = _tile(R, 128, 16)

    def body(s_ref, o_ref):
        o_ref[...] = jnp.zeros(o_ref.shape, o_ref.dtype)
        for j, lo, n, dst in pieces:
            o_ref[:, dst:dst + n] = s_ref[j, :, lo:lo + n]

    return pl.pallas_call(
        body, name=name, out_shape=jax.ShapeDtypeStruct((R, width), shards.dtype), grid=(R // tr,),
        in_specs=[pl.BlockSpec((N_DEV, tr, Cs), lambda i: (0, i, 0))],
        out_specs=pl.BlockSpec((tr, width), lambda i: (i, 0)), compiler_params=_params(),
    )(shards)


def _split_cols(full, pieces, shard_w, name):
    R, width = full.shape
    tr = _tile(R, 128, 16)

    def body(f_ref, o_ref):
        for j, lo, n, dst in pieces:
            o_ref[j, :, lo:lo + n] = f_ref[:, dst:dst + n]

    return pl.pallas_call(
        body, name=name, out_shape=jax.ShapeDtypeStruct((N_DEV, R, shard_w), full.dtype), grid=(R // tr,),
        in_specs=[pl.BlockSpec((tr, width), lambda i: (i, 0))],
        out_specs=pl.BlockSpec((N_DEV, tr, shard_w), lambda i: (0, i, 0)), compiler_params=_params(),
    )(full)


def _pack(arrays, row_multiple=8):
    flat, layout, off = [], [], 0
    for a in arrays:
        n = a.size
        padded = -(-n // LANE) * LANE
        f = a.reshape(-1).astype(F32)
        if padded != n:
            f = jnp.pad(f, (0, padded - n))
        flat.append(f)
        layout.append((off, n, a.shape))
        off += padded
    total = -(-off // (LANE * row_multiple)) * (LANE * row_multiple)
    if total != off:
        flat.append(jnp.zeros((total - off,), F32))
    return jnp.concatenate(flat).reshape(total // LANE, LANE), layout


def _unpack(buf, layout):
    flat = buf.reshape(-1)
    return [flat[off:off + n].reshape(shape) for off, n, shape in layout]


def _cols_to_full(g):
    return jnp.transpose(g, (1, 0, 2)).reshape(g.shape[1], N_DEV * g.shape[2])


def _full_to_cols(a):
    return jnp.transpose(a.reshape(a.shape[0], N_DEV, a.shape[1] // N_DEV), (1, 0, 2))


def _rows_to_full(g):
    return g.reshape(N_DEV * g.shape[1], g.shape[2])


def _full_to_rows(a):
    return a.reshape(N_DEV, a.shape[0] // N_DEV, a.shape[1])


def _pad_cols(a, width):
    return a if a.shape[-1] == width else jnp.pad(a, [(0, 0)] * (a.ndim - 1) + [(0, width - a.shape[-1])])


SHARDED = ("w_in", "conv_qkv", "pool_w", "w_branch_a", "w_branch_b", "w_mix_out", "w_xq", "w_xkv", "w_xo", "w_up",
           "ffn_conv_w", "w_down")
REPLICATED = ("mix_pre_norm", "a_log", "dt_bias", "gdn_norm", "pool_scale", "mix_post_norm", "xa_pre_norm", "mem_norm",
              "xa_post_norm", "ffn_pre_norm", "ffn_conv_b", "ffn_post_norm")
WEIGHTS = ("mix_pre_norm", "w_in", "conv_qkv", "a_log", "dt_bias", "gdn_norm", "pool_w", "pool_scale", "w_branch_a",
           "w_branch_b", "w_mix_out", "mix_post_norm", "xa_pre_norm", "mem_norm", "w_xq", "w_xkv", "w_xo", "xa_post_norm",
           "ffn_pre_norm", "w_up", "ffn_conv_w", "ffn_conv_b", "w_down", "ffn_post_norm")
MATMUL_WEIGHTS = ("w_in", "w_branch_a", "w_branch_b", "w_mix_out", "w_xq", "w_xkv", "w_xo", "w_up", "w_down")
COL_SHARDED = ("w_in", "w_branch_b", "w_xkv", "w_up", "conv_qkv", "ffn_conv_w")
ROW_SHARDED = ("w_branch_a", "w_mix_out", "w_xq", "w_xo", "w_down")


class _Layout:
    def __init__(self, D, H, pw, F):
        self.D, self.H, self.pw, self.F = D, H, pw, F
        self.qkv_w, self.vw = 3 * H * HD, H * HD
        self.ba_w = 512 if D >= 2048 else LANE
        self.Fp = -(-F // 512) * 512 if F >= 512 else F
        q, vw = self.qkv_w, self.vw
        self.seg = dict(qkv=(0, q), z=(q, vw), ga=(q + vw, D), gb=(q + vw + D, D), p=(q + vw + 2 * D, pw),
                        ba=(q + vw + 2 * D + pw, self.ba_w))
        self.in_w = q + vw + 2 * D + pw + self.ba_w
        o_z, o_b = q, q + vw
        o_p = o_b + 2 * H
        o_ga = o_p + pw
        o_gb = o_ga + D
        self.d_in = o_gb + D
        self.in_map = [(0, o_z, self.seg["qkv"][0]), (o_z, o_b, self.seg["z"][0]), (o_b, o_p, self.seg["ba"][0]),
                       (o_p, o_ga, self.seg["p"][0]), (o_ga, o_gb, self.seg["ga"][0]), (o_gb, self.d_in, self.seg["gb"][0])]
        self.up_map = [(0, F, 0), (F, 2 * F, self.Fp)]

    def col(self, name, width):
        return self.seg[name][0] // width


def _local_step(x, mem, target, P, L, comm=None):
    D, H, pw, F, Fp = L.D, L.H, L.pw, L.F, L.Fp
    qkv_w, vw, ba_w = L.qkv_w, L.vw, L.ba_w
    col = L.col
    P = dict(P)
    win_p, cw3_p, fb_p = P["win_p"], P["cw3_p"], P["fb_p"]
    conv_qkv, pool_w = P["conv_qkv"], P["pool_w"]
    lanes = lambda vec: jnp.pad(vec.reshape(1, H).astype(F32), ((0, 0), (H, LANE - 2 * H)))
    a_log_l, dt_bias_l = lanes(P["a_log"]), lanes(P["dt_bias"])
    bf = lambda name: P[name]
    vecf = lambda name: P[name].reshape(1, -1).astype(F32)
    g = {}

    def carried(call, name, *args, **kw):
        if comm is not None and name in comm.GATHERS:
            *out, got = call(*args, name, rider=comm.gather(name), **kw)
            P.update(comm.weights_from(name, got))
        elif comm is not None and name in comm.EXCHANGES:
            *out, got = call(*args, name, rider=comm.exchange(name, g), **kw)
            comm.receive(name, got)
        else:
            out = call(*args, name, **kw)
            out = [out] if call is _matmul else list(out[:-1])
        return out[0] if len(out) == 1 else out

    h1 = _prenorm(x, vecf("mix_pre_norm"), "mix_prenorm")
    proj = carried(_matmul, "in_proj", h1, win_p, "nn", BF16, tn=1536)
    qkv_hm = _qkv_conv(proj, conv_qkv, qkv_w, "qkv_conv")
    bg = _gates(proj, col("ba", LANE), a_log_l, dt_bias_l, H, "gates")
    o_hm, states = carried(_gdn_fwd, "gdn_fwd", qkv_hm, bg, H)
    wup_p, wdown_p = P["wup_p"], P["wdown_p"]
    oa = _gdn_out(o_hm, proj, col("z", vw), vecf("gdn_norm"), "gdn_out")
    ya = _matmul(oa, bf("w_branch_a"), "nn", BF16, "branch_a")
    pb = _pool_fwd(proj, col("p", pw), pool_w, vecf("pool_scale"), pw, "pool_fwd")
    yb = _matmul(pb, bf("w_branch_b"), "nn", BF16, "branch_b")
    merged = _merge(proj, col("ga", D), col("gb", D), ya, yb, "merge")
    y1 = _matmul(merged, bf("w_mix_out"), "nn", F32, "mix_out")
    x1, h2 = _post_pre(x, y1, vecf("mix_post_norm"), vecf("xa_pre_norm"), "mix_post")
    mn = _prenorm(mem, vecf("mem_norm"), "mem_norm")
    qx = _matmul(h2, bf("w_xq"), "nn", BF16, "xq")
    kv = _matmul(mn, bf("w_xkv"), "nn", BF16, "xkv")
    ox = _xattn_fwd(qx, kv, "xattn_fwd")
    y2 = _matmul(ox, bf("w_xo"), "nn", F32, "xo")
    x2, h3 = _post_pre(x1, y2, vecf("xa_post_norm"), vecf("ffn_pre_norm"), "xa_post")
    up = _matmul(h3, wup_p, "nn", BF16, "ffn_up")
    act = _ffn_act(up, cw3_p, fb_p, "ffn_act")
    y3 = _matmul(act, wdown_p, "nn", F32, "ffn_down")
    dx3, loss = _post_loss(x2, y3, vecf("ffn_post_norm"), target, "ffn_post_loss")

    dy3, g["ffn_post_norm"] = _post_bwd(y3, vecf("ffn_post_norm"), dx3, "ffn_post_bwd")
    dact = _matmul(dy3, wdown_p, "nt", BF16, "ffn_down_dx")
    g["w_down_p"] = _matmul(act, dy3, "tn", BF16, "ffn_down_dw")
    du, g["ffn_conv_w_p"], g["ffn_conv_b_p"] = _ffn_act_bwd(up, cw3_p, fb_p, dact, "ffn_act_bwd")
    dup = _conv_t(du, jnp.stack([cw3_p[:, :Fp], cw3_p[:, Fp:]]), "ffn_conv_t")
    dh3 = carried(_matmul, "ffn_up_dx", dup, wup_p, "nt", BF16)
    g["w_up_p"] = _matmul(h3, dup, "tn", BF16, "ffn_up_dw", tn=1408)
    dx2, g["ffn_pre_norm"] = _pre_bwd(x2, vecf("ffn_pre_norm"), dh3, dx3, "ffn_pre_bwd")
    dy2, g["xa_post_norm"] = _post_bwd(y2, vecf("xa_post_norm"), dx2, "xa_post_bwd")
    dox = _matmul(dy2, bf("w_xo"), "nt", BF16, "xo_dx")
    g["w_xo"] = _matmul(ox, dy2, "tn", BF16, "xo_dw")
    dqx, dkv = _xattn_bwd(qx, kv, dox, "xattn_bwd")
    dkv_b = dkv.astype(BF16)
    dh2 = _matmul(dqx, bf("w_xq"), "nt", BF16, "xq_dx")
    g["w_xq"] = _matmul(h2, dqx, "tn", BF16, "xq_dw")
    dmn = _matmul(dkv_b, bf("w_xkv"), "nt", F32, "xkv_dx")
    g["w_xkv"] = _matmul(mn, dkv_b, "tn", BF16, "xkv_dw")
    _, g["mem_norm"] = _pre_bwd(mem, vecf("mem_norm"), dmn, jnp.zeros_like(mem), "mem_norm_bwd")
    dx1, g["xa_pre_norm"] = _pre_bwd(x1, vecf("xa_pre_norm"), dh2, dx2, "xa_pre_bwd")
    dy1, g["mix_post_norm"] = _post_bwd(y1, vecf("mix_post_norm"), dx1, "mix_post_bwd")
    dmerged = _matmul(dy1, bf("w_mix_out"), "nt", BF16, "mix_out_dx")
    g["w_mix_out"] = _matmul(merged, dy1, "tn", BF16, "mix_out_dw")
    dya, dyb, dga, dgb = _merge_bwd(proj, col("ga", D), col("gb", D), ya, yb, dmerged, "merge_bwd")
    doa = _matmul(dya, bf("w_branch_a"), "nt", BF16, "branch_a_dx")
    g["w_branch_a"] = _matmul(oa, dya, "tn", BF16, "branch_a_dw")
    dpb = _matmul(dyb, bf("w_branch_b"), "nt", BF16, "branch_b_dx")
    g["w_branch_b"] = _matmul(pb, dyb, "tn", BF16, "branch_b_dw")
    dp, g["pool_w"], g["pool_scale"] = _pool_bwd(proj, col("p", pw), pool_w, vecf("pool_scale"), dpb, pw, "pool_bwd")
    do_hm, dz, g["gdn_norm"] = _gdn_out_bwd(o_hm, proj, col("z", vw), vecf("gdn_norm"), doa, "gdn_out_bwd")
    dqkv_hm, dbg = carried(_gdn_bwd, "gdn_bwd", qkv_hm, bg, states, do_hm, H)
    dba, dal, ddt = _gates_bwd(proj, col("ba", LANE), ba_w, a_log_l, dt_bias_l, dbg, H, "gates_bwd")
    g["a_log"], g["dt_bias"] = dal[:, H:2 * H], ddt[:, H:2 * H]
    dc, g["conv_qkv"] = _qkv_conv_bwd(proj, conv_qkv, dqkv_hm, qkv_w, "qkv_conv_bwd")
    dqkv = _conv_t(dc[None], conv_qkv[None], "qkv_conv_t")[0]
    dproj = jnp.concatenate([dqkv, dz, dga, dgb, dp, dba], axis=1)
    g["w_in_p"] = carried(_matmul, "in_proj_dw", h1, dproj, "tn", BF16, tn=1536)
    dh1 = carried(_matmul, "in_proj_dx", dproj, win_p, "nt", BF16)
    grad_x, g["mix_pre_norm"] = _pre_bwd(x, vecf("mix_pre_norm"), dh1, dx1, "mix_pre_bwd")
    return loss, grad_x, g


def _two_halves(a, F, Fp):
    return jnp.concatenate([_pad_cols(a[..., :F], Fp), _pad_cols(a[..., F:], Fp)], axis=-1)


def _from_halves(a, F, Fp):
    return jnp.concatenate([a[..., :F], a[..., Fp:Fp + F]], axis=-1)


class _StepComm:
    FIRST = ("w_in", "conv_qkv", "pool_w", "ffn_conv_w")
    GATHERS = {"in_proj": ("w_branch_a", "w_branch_b", "w_mix_out", "w_xq", "w_xo"),
               "gdn_fwd": ("w_xkv", "w_up", "w_down")}
    EXCHANGES = {"ffn_up_dx": ("w_down", "ffn_conv_w"),
                 "gdn_bwd": ("pool_w", "w_branch_a", "w_branch_b", "w_mix_out", "w_xq", "w_xkv", "w_xo"),
                 "in_proj_dw": ("w_up",),
                 "in_proj_dx": ("w_in", "conv_qkv")}

    def __init__(self, w, L):
        self.w, self.L = w, L
        self.in_pieces = _col_pieces(L.in_map, w["w_in"].shape[1])
        self.up_pieces = _col_pieces(L.up_map, w["w_up"].shape[1])
        self.received = {}

    def _shard(self, n):
        return self.w[n].astype(BF16) if n in MATMUL_WEIGHTS else self.w[n]

    def first_weights(self):
        L, (g, r, c) = self.L, self.w["pool_w"].shape
        G = dict(zip(self.FIRST, _communicate(_Gather([self._shard(n) for n in self.FIRST]), "gather_first")))
        return {"win_p": _assemble_cols(G["w_in"], self.in_pieces, L.in_w, "assemble_w_in"),
                "conv_qkv": _cols_to_full(G["conv_qkv"]),
                "cw3_p": _two_halves(_cols_to_full(G["ffn_conv_w"]), L.F, L.Fp),
                "pool_w": jnp.transpose(G["pool_w"], (1, 0, 2, 3)).reshape(g, N_DEV * r, c).astype(BF16)}

    def gather(self, call):
        return _Gather([self._shard(n) for n in self.GATHERS[call]])

    def weights_from(self, call, results):
        L, P = self.L, {}
        for n, shards in zip(self.GATHERS[call], results):
            if n == "w_up":
                P["wup_p"] = _assemble_cols(shards, self.up_pieces, 2 * L.Fp, "assemble_w_up")
            elif n == "w_down":
                P["wdown_p"] = jnp.pad(_rows_to_full(shards), ((0, L.Fp - L.F), (0, 0)))
            else:
                P[n] = _rows_to_full(shards) if n in ROW_SHARDED else _cols_to_full(shards)
        return P

    def _slices(self, g, n):
        L, w = self.L, self.w
        if n == "w_in":
            return _split_cols(g["w_in_p"], self.in_pieces, w["w_in"].shape[1], "split_w_in")
        if n == "w_up":
            return _split_cols(g["w_up_p"], self.up_pieces, w["w_up"].shape[1], "split_w_up")
        if n == "w_down":
            return _full_to_rows(g["w_down_p"][:L.F])
        if n == "ffn_conv_w":
            return _full_to_cols(_from_halves(g["ffn_conv_w_p"], L.F, L.Fp))
        if n == "pool_w":
            grp, r, c = w["pool_w"].shape
            return jnp.transpose(g[n].reshape(grp, N_DEV, r, c), (1, 0, 2, 3)).reshape(N_DEV, grp * r, c)
        return _full_to_rows(g[n]) if n in ROW_SHARDED else _full_to_cols(g[n])

    def exchange(self, call, g):
        return _Exchange([self._slices(g, n) for n in self.EXCHANGES[call]])

    def receive(self, call, results):
        self.received.update(zip(self.EXCHANGES[call], results))


def kernel(x, mem, mix_pre_norm, w_in, conv_qkv, a_log, dt_bias, gdn_norm, pool_w, pool_scale, w_branch_a, w_branch_b, w_mix_out, mix_post_norm, xa_pre_norm, mem_norm, w_xq, w_xkv, w_xo, xa_post_norm, ffn_pre_norm, w_up, ffn_conv_w, ffn_conv_b, w_down, ffn_post_norm, loss_target, m_mix_pre_norm, m_w_in, m_conv_qkv, m_a_log, m_dt_bias, m_gdn_norm, m_pool_w, m_pool_scale, m_w_branch_a, m_w_branch_b, m_w_mix_out, m_mix_post_norm, m_xa_pre_norm, m_mem_norm, m_w_xq, m_w_xkv, m_w_xo, m_xa_post_norm, m_ffn_pre_norm, m_w_up, m_ffn_conv_w, m_ffn_conv_b, m_w_down, m_ffn_post_norm, v_mix_pre_norm, v_w_in, v_conv_qkv, v_a_log, v_dt_bias, v_gdn_norm, v_pool_w, v_pool_scale, v_w_branch_a, v_w_branch_b, v_w_mix_out, v_mix_post_norm, v_xa_pre_norm, v_mem_norm, v_w_xq, v_w_xkv, v_w_xo, v_xa_post_norm, v_ffn_pre_norm, v_w_up, v_ffn_conv_w, v_ffn_conv_b, v_w_down, v_ffn_post_norm):
    given = dict(locals())
    w = {n: given[n][0] for n in WEIGHTS}
    m = {n: given["m_" + n][0] for n in WEIGHTS}
    v = {n: given["v_" + n][0] for n in WEIGHTS}
    D = x.shape[-1]
    F = w["w_down"].shape[0] * N_DEV
    L = _Layout(D, w["a_log"].shape[-1], w["pool_scale"].shape[-1], F)
    Fp = L.Fp

    comm = _StepComm(w, L)
    P = {n: w[n] for n in REPLICATED}
    P.update(comm.first_weights())
    P["fb_p"] = _two_halves(w["ffn_conv_b"].reshape(1, 2 * F), F, Fp)
    loss, grad_x, g = _local_step(x[0], mem[0], loss_target[0], P, L, comm)

    received = comm.received
    outs = {}
    for n in SHARDED:
        as2d = lambda a: a.reshape(-1, a.shape[-1])
        res = _adamw(received[n], as2d(w[n]), as2d(m[n]), as2d(v[n]), "adamw_" + n)
        outs[n] = [r.reshape(w[n].shape) for r in res]

    g["ffn_conv_b"] = _from_halves(g["ffn_conv_b_p"], F, Fp)
    rep_parts, rep_layout = _pack([g[n].reshape(w[n].shape) for n in REPLICATED] + [loss])
    rep_all, = _communicate(_Gather([rep_parts]), "gather_small_grads")
    zero_loss = jnp.zeros_like(loss)
    wr, _ = _pack([w[n] for n in REPLICATED] + [zero_loss])
    mr, _ = _pack([m[n] for n in REPLICATED] + [zero_loss])
    vr, _ = _pack([v[n] for n in REPLICATED] + [zero_loss])
    outs_rep = [_unpack(o, rep_layout) for o in _adamw(rep_all, wr, mr, vr, "adamw_replicated")]
    loss_total = outs_rep[0][-1][0, 0]
    for i, n in enumerate(REPLICATED):
        outs[n] = [outs_rep[k][i] for k in range(4)]

    result = [loss_total, grad_x[None]]
    for k in range(4):
        for n in WEIGHTS:
            result.append(outs[n][k][None])
    return tuple(result)
```

```python
import functools

import jax
import jax.numpy as jnp
from jax import lax
from jax.experimental import pallas as pl
from jax.experimental.pallas import tpu as pltpu

F32, BF16 = jnp.float32, jnp.bfloat16
MESH = pl.DeviceIdType.MESH
ANY = pl.BlockSpec(memory_space=pl.ANY)

N_DEV = 8
EPS = 1e-6
CHUNK = 64
HD = 128
GDN_CONV = 4
FFN_CONV = 3
POOL_WINDOWS = (2, 4, 8, 16)
XA_HEADS = 4
HALO = 16
RC = 128
SEQ_TILE = 1024
LANE = 128
VMEM_LIMIT = 48 * 1024 * 1024

ADAM_LR, ADAM_B1, ADAM_B2, ADAM_EPS, ADAM_WD, ADAM_STEP = 0.001, 0.9, 0.999, 1e-08, 0.01, 10


def _tile(n, pref, align=LANE):
    best = None
    t = align
    while t <= min(n, pref):
        if n % t == 0:
            best = t
        t += align
    return best if best is not None else n


def _params(**kw):
    return pltpu.CompilerParams(vmem_limit_bytes=VMEM_LIMIT, **kw)


def _sigmoid(x):
    return 1.0 / (1.0 + jnp.exp(-x))


def _silu(x):
    return x * _sigmoid(x)


def _dsilu(x):
    s = _sigmoid(x)
    return s * (1.0 + x * (1.0 - s))


def _colsum8(t):
    return t.reshape(t.shape[0] // 8, 8, t.shape[1]).sum(axis=0)


def _ride(body, n_in, n_out, rider, first, middle, last):
    if rider is None:
        return body
    n = rider.n

    def wrapped(*refs):
        ins, r_in = refs[:n_in], refs[n_in:n_in + n]
        outs, r_out = refs[n_in + n:n_in + n + n_out], refs[n_in + n + n_out:n_in + 2 * n + n_out]
        rest = refs[n_in + 2 * n + n_out:]
        start, hand_on, finish = rider.phases(r_in, r_out, *rest[-3:])
        pl.when(first())(start)
        body(*ins, *outs, *rest[:-3])
        if hand_on is not None:
            pl.when(middle())(hand_on)
        pl.when(last())(finish)

    return wrapped


def _matmul(a, b, mode, out_dtype, name, tm=1024, tn=1024, tk=2816, rider=None):
    ga = a.shape[0] if (mode == "nt" and a.ndim == 3) else 1
    gb = b.shape[0] if (mode == "tn" and b.ndim == 3) else 1
    if mode == "nn":
        (M, K), (K2, N) = a.shape, b.shape
    elif mode == "nt":
        M, K = a.shape[-2], ga * a.shape[-1]
        N, K2 = b.shape
    else:
        K, M = a.shape
        K2, N = b.shape[-2], gb * b.shape[-1]
    assert K == K2, (name, a.shape, b.shape)
    tm, tn = _tile(M, tm), _tile(N // gb, tn)
    tk = K // ga if K // ga <= tk else _tile(K // ga, tk)
    nk = K // tk
    kpg, npg = K // ga // tk, N // gb // tn
    if mode == "nn":
        a_spec = pl.BlockSpec((tm, tk), lambda i, j, k: (i, k))
        b_spec = pl.BlockSpec((tk, tn), lambda i, j, k: (k, j))
        dims = (((1,), (0,)), ((), ()))
    elif mode == "nt":
        a_spec = (pl.BlockSpec((tm, tk), lambda i, j, k: (i, k)) if a.ndim == 2 else
                  pl.BlockSpec((None, tm, tk), lambda i, j, k: (k // kpg, i, k % kpg)))
        b_spec = pl.BlockSpec((tn, tk), lambda i, j, k: (j, k))
        dims = (((1,), (1,)), ((), ()))
    else:
        a_spec = pl.BlockSpec((tk, tm), lambda i, j, k: (k, i))
        b_spec = (pl.BlockSpec((tk, tn), lambda i, j, k: (k, j)) if b.ndim == 2 else
                  pl.BlockSpec((None, tk, tn), lambda i, j, k: (j // npg, k, j % npg)))
        dims = (((0,), (0,)), ((), ()))

    def body(a_ref, b_ref, o_ref, acc):
        part = lax.dot_general(a_ref[...], b_ref[...], dims, preferred_element_type=F32)
        if nk == 1:
            o_ref[...] = part.astype(o_ref.dtype)
        else:
            k = pl.program_id(2)

            @pl.when(k == 0)
            def _():
                acc[...] = part

            @pl.when(k > 0)
            def _():
                acc[...] += part

            @pl.when(k == nk - 1)
            def _():
                o_ref[...] = acc[...].astype(o_ref.dtype)

    grid = (M // tm, N // tn, nk)
    at = lambda step: lambda: ((pl.program_id(0) == step[0]) & (pl.program_id(1) == step[1])
                               & (pl.program_id(2) == step[2]))
    extra = rider.n if rider is not None else 0
    res = pl.pallas_call(
        _ride(body, 2, 1, rider, at((0, 0, 0)), at((grid[0] // 2, 0, 0)), at((grid[0] - 1, grid[1] - 1, nk - 1))),
        name=name, out_shape=[jax.ShapeDtypeStruct((M, N), out_dtype)] + (rider.out_shape if rider else []),
        grid=grid, in_specs=[a_spec, b_spec] + [ANY] * extra,
        out_specs=[pl.BlockSpec((tm, tn), lambda i, j, k: (i, j))] + [ANY] * extra,
        scratch_shapes=[pltpu.VMEM((tm, tn) if nk > 1 else (8, LANE), F32)] + (rider.scratch if rider else []),
        compiler_params=_params(dimension_semantics=("arbitrary",) * 3 if rider else ("parallel", "parallel", "arbitrary")),
    )(a, b, *(rider.arrays if rider else []))
    return (res[0], res[1:]) if rider else res[0]


def _rstd(xf):
    return lax.rsqrt(jnp.mean(xf * xf, axis=-1, keepdims=True) + EPS)


def _rms_bwd(xf, w, dy):
    r = _rstd(xf)
    g = dy * w
    dx = r * g - xf * (r * r * r) * jnp.mean(g * xf, axis=-1, keepdims=True)
    return dx, dy * xf * r


def _row_tile(rows):
    return _tile(rows, 512, 8)


def _prenorm(x, w, name):
    rows, d = x.shape
    ts = _row_tile(rows)

    def body(x_ref, w_ref, h_ref):
        xf = x_ref[...]
        h_ref[...] = (xf * _rstd(xf) * w_ref[...]).astype(BF16)

    return pl.pallas_call(
        body, name=name, out_shape=jax.ShapeDtypeStruct((rows, d), BF16), grid=(rows // ts,),
        in_specs=[pl.BlockSpec((ts, d), lambda i: (i, 0)), pl.BlockSpec((1, d), lambda i: (0, 0))],
        out_specs=pl.BlockSpec((ts, d), lambda i: (i, 0)), compiler_params=_params(),
    )(x, w)


def _post_pre(xres, y, w_post, w_pre, name):
    rows, d = xres.shape
    ts = _row_tile(rows)

    def body(x_ref, y_ref, wp_ref, wn_ref, xo_ref, h_ref):
        yf = y_ref[...]
        xn = x_ref[...] + yf * _rstd(yf) * wp_ref[...]
        xo_ref[...] = xn
        h_ref[...] = (xn * _rstd(xn) * wn_ref[...]).astype(BF16)

    row = pl.BlockSpec((ts, d), lambda i: (i, 0))
    vec = pl.BlockSpec((1, d), lambda i: (0, 0))
    return pl.pallas_call(
        body, name=name, grid=(rows // ts,),
        out_shape=(jax.ShapeDtypeStruct((rows, d), F32), jax.ShapeDtypeStruct((rows, d), BF16)),
        in_specs=[row, row, vec, vec], out_specs=(row, row), compiler_params=_params(),
    )(xres, y, w_post, w_pre)


def _post_loss(xres, y, w_post, target, name):
    rows, d = xres.shape
    ts = _row_tile(rows)
    n = rows // ts

    def body(x_ref, y_ref, wp_ref, t_ref, dx_ref, loss_ref, acc):
        i = pl.program_id(0)
        yf = y_ref[...]
        diff = x_ref[...] + yf * _rstd(yf) * wp_ref[...] - t_ref[...]
        dx_ref[...] = diff * (1.0 / d)

        @pl.when(i == 0)
        def _():
            acc[...] = jnp.zeros_like(acc)

        acc[...] += _colsum8(diff * diff)

        @pl.when(i == n - 1)
        def _():
            loss_ref[...] = jnp.broadcast_to((0.5 / d) * jnp.sum(acc[...]), loss_ref.shape)

    row = pl.BlockSpec((ts, d), lambda i: (i, 0))
    vec = pl.BlockSpec((1, d), lambda i: (0, 0))
    return pl.pallas_call(
        body, name=name, grid=(n,),
        out_shape=(jax.ShapeDtypeStruct((rows, d), F32), jax.ShapeDtypeStruct((1, LANE), F32)),
        in_specs=[row, row, vec, row], out_specs=(row, pl.BlockSpec((1, LANE), lambda i: (0, 0))),
        scratch_shapes=[pltpu.VMEM((8, d), F32)], compiler_params=_params(),
    )(xres, y, w_post, target)


def _post_bwd(y, w_post, dxn, name):
    rows, d = y.shape
    ts = _row_tile(rows)
    n = rows // ts

    def body(y_ref, w_ref, d_ref, dy_ref, dw_ref, acc):
        i = pl.program_id(0)
        dy, dwr = _rms_bwd(y_ref[...], w_ref[...], d_ref[...])
        dy_ref[...] = dy.astype(BF16)

        @pl.when(i == 0)
        def _():
            acc[...] = jnp.zeros_like(acc)

        acc[...] += _colsum8(dwr)

        @pl.when(i == n - 1)
        def _():
            dw_ref[...] = jnp.sum(acc[...], axis=0, keepdims=True)

    row = pl.BlockSpec((ts, d), lambda i: (i, 0))
    vec = pl.BlockSpec((1, d), lambda i: (0, 0))
    return pl.pallas_call(
        body, name=name, grid=(n,),
        out_shape=(jax.ShapeDtypeStruct((rows, d), BF16), jax.ShapeDtypeStruct((1, d), F32)),
        in_specs=[row, vec, row], out_specs=(row, vec),
        scratch_shapes=[pltpu.VMEM((8, d), F32)], compiler_params=_params(),
    )(y, w_post, dxn)


def _pre_bwd(x, w_pre, dh, dres, name):
    rows, d = x.shape
    ts = _row_tile(rows)
    n = rows // ts

    def body(x_ref, w_ref, dh_ref, dr_ref, dx_ref, dw_ref, acc):
        i = pl.program_id(0)
        dx, dwr = _rms_bwd(x_ref[...], w_ref[...], dh_ref[...].astype(F32))
        dx_ref[...] = dr_ref[...] + dx

        @pl.when(i == 0)
        def _():
            acc[...] = jnp.zeros_like(acc)

        acc[...] += _colsum8(dwr)

        @pl.when(i == n - 1)
        def _():
            dw_ref[...] = jnp.sum(acc[...], axis=0, keepdims=True)

    row = pl.BlockSpec((ts, d), lambda i: (i, 0))
    vec = pl.BlockSpec((1, d), lambda i: (0, 0))
    return pl.pallas_call(
        body, name=name, grid=(n,),
        out_shape=(jax.ShapeDtypeStruct((rows, d), F32), jax.ShapeDtypeStruct((1, d), F32)),
        in_specs=[row, vec, row, row], out_specs=(row, vec),
        scratch_shapes=[pltpu.VMEM((8, d), F32)], compiler_params=_params(),
    )(x, w_pre, dh, dres)


def _prev_halo_spec(ts, cw, col0=0):
    return pl.BlockSpec((HALO, cw), lambda i, j: (jnp.maximum(i * (ts // HALO) - 1, 0), j + col0))


def _fill_causal(ext, tile_f32, halo_f32, i):
    ext[pl.ds(0, HALO), :] = jnp.where(i > 0, halo_f32, 0.0)
    ext[pl.ds(HALO, tile_f32.shape[0]), :] = tile_f32


def _row_chunks(ts, chunk):
    assert ts % RC == 0, (ts, RC)

    def step(c, carry):
        chunk(pl.multiple_of(c * RC, RC))
        return carry

    lax.fori_loop(0, ts // RC, step, 0)


def _causal_taps(ext, r0, K):
    blk = ext[pl.ds(r0 + HALO - 8, RC + 8), :]
    return [blk[8 - j:8 - j + RC] for j in range(K)]


def _advanced_taps(ext, r0, K):
    blk = ext[pl.ds(r0, RC + 8), :]
    return [blk[j:j + RC] for j in range(K)]


def _filter(wv, taps):
    K = len(taps)
    acc = wv[K - 1:K, :] * taps[0]
    for t in range(K - 1):
        acc = acc + wv[t:t + 1, :] * taps[K - 1 - t]
    return acc


def _conv_t(dc, w, name, col_tile=512):
    G, S, C = dc.shape
    K = w.shape[1]
    ts, cw = _tile(S, SEQ_TILE, HALO), _tile(C, col_tile)
    n = S // ts

    def body(d_ref, nx_ref, w_ref, o_ref, ext):
        i = pl.program_id(1)
        ext[pl.ds(0, ts), :] = d_ref[...].astype(F32)
        ext[pl.ds(ts, HALO), :] = jnp.where(i < n - 1, nx_ref[...].astype(F32), 0.0)
        wv = w_ref[...]

        def chunk(r0):
            taps = _advanced_taps(ext, r0, K)
            acc = wv[K - 1:K, :] * taps[0]
            for j in range(K - 1):
                acc = acc + wv[j:j + 1, :] * taps[K - 1 - j]
            o_ref[pl.ds(r0, RC), :] = acc.astype(o_ref.dtype)

        _row_chunks(ts, chunk)

    return pl.pallas_call(
        body, name=name, out_shape=jax.ShapeDtypeStruct((G, S, C), BF16), grid=(G, n, C // cw),
        in_specs=[pl.BlockSpec((None, ts, cw), lambda g, i, j: (g, i, j)),
                  pl.BlockSpec((None, HALO, cw),
                               lambda g, i, j: (g, jnp.minimum((i + 1) * (ts // HALO), S // HALO - 1), j)),
                  pl.BlockSpec((None, K, cw), lambda g, i, j: (g, 0, j))],
        out_specs=pl.BlockSpec((None, ts, cw), lambda g, i, j: (g, i, j)),
        scratch_shapes=[pltpu.VMEM((ts + HALO, cw), F32)], compiler_params=_params(),
    )(dc, dc, w)


def _qkv_conv(proj, conv_w, qkv_w, name):
    S = proj.shape[0]
    H3 = qkv_w // HD
    H = H3 // 3
    hb = 4 if H % 4 == 0 else 1
    cw = hb * HD
    ts = _tile(S, SEQ_TILE, HALO)
    per_kind = H // hb

    def body(x_ref, h_ref, w_ref, o_ref, ext):
        i, j = pl.program_id(0), pl.program_id(1)
        _fill_causal(ext, x_ref[...].astype(F32), h_ref[...].astype(F32), i)
        wv = w_ref[...]
        kind = j // per_kind
        scale = jnp.where(kind == 0, HD ** -0.5, 1.0)

        def chunk(r0):
            s = _silu(_filter(wv, _causal_taps(ext, r0, GDN_CONV)))
            for a in range(hb):
                sa = s[:, HD * a:HD * (a + 1)]
                r = lax.rsqrt(jnp.sum(sa * sa, axis=-1, keepdims=True) + EPS)
                o_ref[a, pl.ds(r0, RC), :] = jnp.where(kind == 2, sa, sa * r * scale)

        _row_chunks(ts, chunk)

    return pl.pallas_call(
        body, name=name, out_shape=jax.ShapeDtypeStruct((H3, S, HD), F32), grid=(S // ts, qkv_w // cw),
        in_specs=[pl.BlockSpec((ts, cw), lambda i, j: (i, j)), _prev_halo_spec(ts, cw),
                  pl.BlockSpec((GDN_CONV, cw), lambda i, j: (0, j))],
        out_specs=pl.BlockSpec((hb, ts, HD), lambda i, j: (j, i, 0)),
        scratch_shapes=[pltpu.VMEM((ts + HALO, cw), F32)], compiler_params=_params(),
    )(proj, proj, conv_w)


def _qkv_conv_bwd(proj, conv_w, dqkv_hm, qkv_w, name):
    S = proj.shape[0]
    H = qkv_w // HD // 3
    hb = 4 if H % 4 == 0 else 1
    cw = hb * HD
    ts = _tile(S, SEQ_TILE, HALO)
    n = S // ts
    per_kind = H // hb

    def body(x_ref, h_ref, w_ref, d_ref, dc_ref, dw_ref, ext, acc):
        j, i = pl.program_id(0), pl.program_id(1)
        _fill_causal(ext, x_ref[...].astype(F32), h_ref[...].astype(F32), i)
        wv = w_ref[...]
        kind = j // per_kind
        scale = jnp.where(kind == 0, HD ** -0.5, 1.0)

        @pl.when(i == 0)
        def _():
            acc[...] = jnp.zeros_like(acc)

        def chunk(r0):
            taps = _causal_taps(ext, r0, GDN_CONV)
            c = _filter(wv, taps)
            s = _silu(c)
            parts = []
            for a in range(hb):
                sa = s[:, HD * a:HD * (a + 1)]
                dy = d_ref[a, pl.ds(r0, RC), :]
                r = lax.rsqrt(jnp.sum(sa * sa, axis=-1, keepdims=True) + EPS)
                dn = scale * (r * dy - sa * (r * r * r) * jnp.sum(dy * sa, axis=-1, keepdims=True))
                parts.append(jnp.where(kind == 2, dy, dn))
            dc = jnp.concatenate(parts, axis=1) * _dsilu(c)
            dc_ref[pl.ds(r0, RC), :] = dc.astype(BF16)
            for t in range(GDN_CONV):
                acc[t] += _colsum8(dc * taps[GDN_CONV - 1 - t])

        _row_chunks(ts, chunk)

        @pl.when(i == n - 1)
        def _():
            dw_ref[...] = jnp.sum(acc[...], axis=1)

    return pl.pallas_call(
        body, name=name, grid=(qkv_w // cw, n),
        out_shape=(jax.ShapeDtypeStruct((S, qkv_w), BF16), jax.ShapeDtypeStruct((GDN_CONV, qkv_w), F32)),
        in_specs=[pl.BlockSpec((ts, cw), lambda j, i: (i, j)),
                  pl.BlockSpec((HALO, cw), lambda j, i: (jnp.maximum(i * (ts // HALO) - 1, 0), j)),
                  pl.BlockSpec((GDN_CONV, cw), lambda j, i: (0, j)),
                  pl.BlockSpec((hb, ts, HD), lambda j, i: (j, i, 0))],
        out_specs=(pl.BlockSpec((ts, cw), lambda j, i: (i, j)), pl.BlockSpec((GDN_CONV, cw), lambda j, i: (0, j))),
        scratch_shapes=[pltpu.VMEM((ts + HALO, cw), F32), pltpu.VMEM((GDN_CONV, 8, cw), F32)],
        compiler_params=_params(),
    )(proj, proj, conv_w, dqkv_hm)


def _chunk_cumsum(x):
    row = lax.broadcasted_iota(jnp.int32, x.shape, 0) & (CHUNK - 1)
    s = 1
    while s < CHUNK:
        x = x + jnp.where(row >= s, pltpu.roll(x, s, axis=0), 0.0)
        s *= 2
    return x


def _chunk_rev_cumsum(x):
    rows = x.shape[0]
    row = lax.broadcasted_iota(jnp.int32, x.shape, 0) & (CHUNK - 1)
    s = 1
    while s < CHUNK:
        x = x + jnp.where(row < CHUNK - s, pltpu.roll(x, rows - s, axis=0), 0.0)
        s *= 2
    return x


def _softplus(x):
    return jnp.maximum(x, 0.0) + jnp.log1p(jnp.exp(-jnp.abs(x)))


def _gates(proj, ba_col, a_log_l, dt_bias_l, H, name):
    S = proj.shape[0]
    ts = _tile(S, 512, CHUNK)

    def body(x_ref, al_ref, dt_ref, o_ref):
        x = x_ref[...].astype(F32)
        lane = lax.broadcasted_iota(jnp.int32, x.shape, 1)
        g = -jnp.exp(al_ref[...]) * _softplus(x + dt_ref[...])
        G = _chunk_cumsum(jnp.where((lane >= H) & (lane < 2 * H), g, 0.0))
        o_ref[...] = jnp.where(lane < H, _sigmoid(x), G)

    return pl.pallas_call(
        body, name=name, out_shape=jax.ShapeDtypeStruct((S, LANE), F32), grid=(S // ts,),
        in_specs=[pl.BlockSpec((ts, LANE), lambda i: (i, ba_col)), pl.BlockSpec((1, LANE), lambda i: (0, 0)),
                  pl.BlockSpec((1, LANE), lambda i: (0, 0))],
        out_specs=pl.BlockSpec((ts, LANE), lambda i: (i, 0)), compiler_params=_params(),
    )(proj, a_log_l, dt_bias_l)


def _gates_bwd(proj, ba_col, ba_w, a_log_l, dt_bias_l, dbg, H, name):
    S = proj.shape[0]
    ts = _tile(S, 512, CHUNK)
    n = S // ts

    def body(x_ref, al_ref, dt_ref, d_ref, o_ref, dal_ref, ddt_ref, acc):
        i = pl.program_id(0)
        x = x_ref[...].astype(F32)
        d = d_ref[...]
        lane = lax.broadcasted_iota(jnp.int32, x.shape, 1)
        is_a = (lane >= H) & (lane < 2 * H)
        beta = _sigmoid(x)
        nea = -jnp.exp(al_ref[...])
        z = x + dt_ref[...]
        dg = _chunk_rev_cumsum(jnp.where(is_a, d, 0.0))
        da_raw = dg * nea * _sigmoid(z)
        o = jnp.where(lane < H, d * beta * (1.0 - beta), jnp.where(is_a, da_raw, 0.0))
        if ba_w > LANE:
            o = jnp.concatenate([o, jnp.zeros((ts, ba_w - LANE), F32)], axis=1)
        o_ref[...] = o.astype(BF16)

        @pl.when(i == 0)
        def _():
            acc[...] = jnp.zeros_like(acc)

        acc[0] += _colsum8(jnp.where(is_a, dg * nea * _softplus(z), 0.0))
        acc[1] += _colsum8(jnp.where(is_a, da_raw, 0.0))

        @pl.when(i == n - 1)
        def _():
            dal_ref[...] = jnp.sum(acc[0], axis=0, keepdims=True)
            ddt_ref[...] = jnp.sum(acc[1], axis=0, keepdims=True)

    vec = pl.BlockSpec((1, LANE), lambda i: (0, 0))
    return pl.pallas_call(
        body, name=name, grid=(n,),
        out_shape=(jax.ShapeDtypeStruct((S, ba_w), BF16), jax.ShapeDtypeStruct((1, LANE), F32),
                   jax.ShapeDtypeStruct((1, LANE), F32)),
        in_specs=[pl.BlockSpec((ts, LANE), lambda i: (i, ba_col)), vec, vec, pl.BlockSpec((ts, LANE), lambda i: (i, 0))],
        out_specs=(pl.BlockSpec((ts, ba_w), lambda i: (i, 0)), vec, vec),
        scratch_shapes=[pltpu.VMEM((2, 8, LANE), F32)], compiler_params=_params(),
    )(proj, a_log_l, dt_bias_l, dbg)


_BMM_FORMS = {"nn": "hik,hkj->hij", "nt": "hik,hjk->hij", "tn": "hki,hkj->hij"}


def _split_bf16(a):
    hi = a.astype(BF16)
    return hi, (a - hi.astype(F32)).astype(BF16)


def _bmm(a, b, form="nn", exact=False):
    e = lambda x, y: jnp.einsum(_BMM_FORMS[form], x, y, preferred_element_type=F32)
    if not exact:
        return e(a.astype(BF16), b.astype(BF16))
    (ah, al), (bh, bl) = _split_bf16(a), _split_bf16(b)
    return (e(ah, bl) + e(al, bh)) + e(ah, bh)


def _unit_lower_inverse(L, r, c):
    eye = (r == c).astype(F32)
    m = jnp.where((r >> 3) == (c >> 3), -L, 0.0)
    m2 = _bmm(m, m, exact=True)
    m4 = _bmm(m2, m2, exact=True)
    x = eye + m
    x = x + _bmm(x, m2, exact=True)
    x = x + _bmm(x, m4, exact=True)
    for sh in (3, 4, 5):
        off = ((r >> (sh + 1)) == (c >> (sh + 1))) & ((r >> sh) != (c >> sh))
        x = x - _bmm(x, _bmm(jnp.where(off, L, 0.0), x))
    return x


def _to_row(col, eye):
    return jnp.sum(jnp.where(eye, jnp.broadcast_to(col, eye.shape), 0.0), axis=1, keepdims=True)


def _to_col(rowv, eye):
    return jnp.sum(jnp.where(eye, jnp.broadcast_to(rowv, eye.shape), 0.0), axis=2, keepdims=True)


def _gdn_chunk(q, k, v, bg, H):
    shape = (H, CHUNK, CHUNK)
    r = lax.broadcasted_iota(jnp.int32, shape, 1)
    c = lax.broadcasted_iota(jnp.int32, shape, 2)
    eye, incl, strict = r == c, r >= c, r > c
    beta = jnp.stack([bg[:, h:h + 1] for h in range(H)], axis=0)
    G = jnp.stack([bg[:, H + h:H + h + 1] for h in range(H)], axis=0)
    gap = jnp.broadcast_to(G, shape) - _to_row(G, eye)
    decay = jnp.where(incl, jnp.exp(jnp.where(incl, gap, 0.0)), 0.0)
    kk = _bmm(k, k, "nt")
    L = jnp.where(strict, beta * decay * kk, 0.0)
    ainv = _unit_lower_inverse(L, r, c)
    eG = jnp.exp(G)
    u_v = _bmm(ainv, beta * v)
    w_k = _bmm(ainv, (beta * eG) * k)
    qk = _bmm(q, k, "nt", exact=True)
    GL = G[:, CHUNK - 1:CHUNK, :]
    ek = jnp.exp(GL - G)
    return dict(eye=eye, strict=strict, r=r, c=c, beta=beta, G=G, decay=decay, kk=kk, ainv=ainv, eG=eG,
                u_v=u_v, w_k=w_k, qk=qk, attn=decay * qk, GL=GL, ek=ek, cd=jnp.exp(GL))


def _chunk_steps(N):
    at = lambda step: lambda: pl.program_id(0) == step
    return at(0), at(N - max(N // 8, 1)), at(N - 1)


def _gdn_fwd(qkv_hm, bg, H, name, rider=None):
    S = qkv_hm.shape[1]
    N = S // CHUNK
    extra = rider.n if rider is not None else 0

    def body(q_ref, k_ref, v_ref, bg_ref, o_ref, st_ref, state):
        n = pl.program_id(0)

        @pl.when(n == 0)
        def _():
            state[...] = jnp.zeros_like(state)

        q, k, v = q_ref[...], k_ref[...], v_ref[...]
        t = _gdn_chunk(q, k, v, bg_ref[...], H)
        s0 = state[...]
        st_ref[0] = s0
        u = t["u_v"] - _bmm(t["w_k"], s0)
        o_ref[...] = _bmm(q * t["eG"], s0) + _bmm(t["attn"], u)
        state[...] = t["cd"] * s0 + _bmm(k * t["ek"], u, "tn")

    blk = lambda kind: pl.BlockSpec((H, CHUNK, HD), lambda n: (kind, n, 0))
    res = pl.pallas_call(
        _ride(body, 4, 2, rider, *_chunk_steps(N)), name=name, grid=(N,),
        out_shape=[jax.ShapeDtypeStruct((H, S, HD), F32), jax.ShapeDtypeStruct((N, H, HD, HD), F32)]
        + (rider.out_shape if rider else []),
        in_specs=[blk(0), blk(1), blk(2), pl.BlockSpec((CHUNK, LANE), lambda n: (n, 0))] + [ANY] * extra,
        out_specs=[pl.BlockSpec((H, CHUNK, HD), lambda n: (0, n, 0)),
                   pl.BlockSpec((1, H, HD, HD), lambda n: (n, 0, 0, 0))] + [ANY] * extra,
        scratch_shapes=[pltpu.VMEM((H, HD, HD), F32)] + (rider.scratch if rider else []), compiler_params=_params(),
    )(qkv_hm, qkv_hm, qkv_hm, bg, *(rider.arrays if rider else []))
    return res[0], res[1], res[2:]


def _gdn_bwd(qkv_hm, bg, states, do_hm, H, name, rider=None):
    S = qkv_hm.shape[1]
    N = S // CHUNK
    extra = rider.n if rider is not None else 0

    def body(q_ref, k_ref, v_ref, bg_ref, st_ref, do_ref, dqkv_ref, dbg_ref, dstate):
        n = pl.program_id(0)

        @pl.when(n == 0)
        def _():
            dstate[...] = jnp.zeros_like(dstate)

        q, k, v, do = q_ref[...], k_ref[...], v_ref[...], do_ref[...]
        t = _gdn_chunk(q, k, v, bg_ref[...], H)
        eye, beta, eG, decay, kk, ainv = t["eye"], t["beta"], t["eG"], t["decay"], t["kk"], t["ainv"]
        s0 = st_ref[0]
        ds1 = dstate[...]
        u = t["u_v"] - _bmm(t["w_k"], s0)
        qdec, kdec = q * eG, k * t["ek"]
        d_qdec = _bmm(do, s0, "nt")
        d_attn = _bmm(do, u, "nt")
        du = _bmm(t["attn"], do, "tn") + _bmm(kdec, ds1)
        d_cd = jnp.sum(jnp.sum(ds1 * s0, axis=2, keepdims=True), axis=1, keepdims=True)
        d_kdec = _bmm(u, ds1, "nt")
        d_wk = -_bmm(du, s0, "nt")
        dstate[...] = t["cd"] * ds1 + _bmm(qdec, do, "tn") - _bmm(t["w_k"], du, "tn")
        d_rv = _bmm(ainv, du, "tn")
        d_rk = _bmm(ainv, d_wk, "tn")
        dL = jnp.where(t["strict"], -(_bmm(d_rv, t["u_v"], "nt") + _bmm(d_rk, t["w_k"], "nt")), 0.0)
        rk_k = jnp.sum(d_rk * k, axis=2, keepdims=True)
        d_beta = (jnp.sum(dL * decay * kk, axis=2, keepdims=True) + jnp.sum(d_rv * v, axis=2, keepdims=True)
                  + rk_k * eG)
        d_decay = dL * beta * kk + d_attn * t["qk"]
        d_kk = dL * beta * decay
        d_qk = d_attn * decay
        dqkv_ref[pl.ds(2 * H, H)] = beta * d_rv
        dqkv_ref[pl.ds(0, H)] = _bmm(d_qk, k) + d_qdec * eG
        dqkv_ref[pl.ds(H, H)] = ((beta * eG) * d_rk + _bmm(d_kk, k) + _bmm(d_kk, k, "tn") + _bmm(d_qk, q, "tn")
                       + d_kdec * t["ek"])
        d_eG = rk_k * beta + jnp.sum(d_qdec * q, axis=2, keepdims=True)
        e = jnp.sum(d_kdec * kdec, axis=2, keepdims=True)
        T = d_decay * decay
        dG = d_eG * eG - e + jnp.sum(T, axis=2, keepdims=True) - _to_col(jnp.sum(T, axis=1, keepdims=True), eye)
        dGL = jnp.sum(e, axis=1, keepdims=True) + d_cd * t["cd"]
        row1 = lax.broadcasted_iota(jnp.int32, (H, CHUNK, 1), 1)
        dG = dG + jnp.where(row1 == CHUNK - 1, dGL, 0.0)
        lane = lax.broadcasted_iota(jnp.int32, (CHUNK, LANE), 1)
        out = jnp.zeros((CHUNK, LANE), F32)
        for h in range(H):
            out = out + jnp.where(lane == h, d_beta[h], 0.0) + jnp.where(lane == H + h, dG[h], 0.0)
        dbg_ref[...] = out

    blk = lambda kind: pl.BlockSpec((H, CHUNK, HD), lambda n: (kind, N - 1 - n, 0))
    res = pl.pallas_call(
        _ride(body, 6, 2, rider, *_chunk_steps(N)), name=name, grid=(N,),
        out_shape=[jax.ShapeDtypeStruct((3 * H, S, HD), F32), jax.ShapeDtypeStruct((S, LANE), F32)]
        + (rider.out_shape if rider else []),
        in_specs=[blk(0), blk(1), blk(2), pl.BlockSpec((CHUNK, LANE), lambda n: (N - 1 - n, 0)),
                  pl.BlockSpec((1, H, HD, HD), lambda n: (N - 1 - n, 0, 0, 0)), blk(0)] + [ANY] * extra,
        out_specs=[pl.BlockSpec((3 * H, CHUNK, HD), lambda n: (0, N - 1 - n, 0)),
                   pl.BlockSpec((CHUNK, LANE), lambda n: (N - 1 - n, 0))] + [ANY] * extra,
        scratch_shapes=[pltpu.VMEM((H, HD, HD), F32)] + (rider.scratch if rider else []), compiler_params=_params(),
    )(qkv_hm, qkv_hm, qkv_hm, bg, states, do_hm, *(rider.arrays if rider else []))
    return res[0], res[1], res[2:]


def _gdn_out(o_hm, proj, z_col, gdn_w, name):
    H, S, _ = o_hm.shape
    vw = H * HD
    ts = _tile(S, 256, 8)

    def body(o_ref, z_ref, w_ref, y_ref):
        z = z_ref[...].astype(F32)
        w = w_ref[...]
        parts = []
        for h in range(H):
            o = o_ref[h]
            parts.append(o * _rstd(o) * w)
        y_ref[...] = (jnp.concatenate(parts, axis=1) * _silu(z)).astype(BF16)

    return pl.pallas_call(
        body, name=name, out_shape=jax.ShapeDtypeStruct((S, vw), BF16), grid=(S // ts,),
        in_specs=[pl.BlockSpec((H, ts, HD), lambda i: (0, i, 0)), pl.BlockSpec((ts, vw), lambda i: (i, z_col)),
                  pl.BlockSpec((1, HD), lambda i: (0, 0))],
        out_specs=pl.BlockSpec((ts, vw), lambda i: (i, 0)), compiler_params=_params(),
    )(o_hm, proj, gdn_w)


def _gdn_out_bwd(o_hm, proj, z_col, gdn_w, dy, name):
    H, S, _ = o_hm.shape
    vw = H * HD
    ts = _tile(S, 256, 8)
    n = S // ts

    def body(o_ref, z_ref, w_ref, dy_ref, do_ref, dz_ref, dw_ref, acc):
        i = pl.program_id(0)
        z = z_ref[...].astype(F32)
        dy = dy_ref[...].astype(F32)
        w = w_ref[...]
        gz = dy * _silu(z)
        normed, dwr = [], jnp.zeros((ts, HD), F32)
        for h in range(H):
            o = o_ref[h]
            dxo, dwh = _rms_bwd(o, w, gz[:, HD * h:HD * (h + 1)])
            do_ref[h] = dxo
            dwr = dwr + dwh
            normed.append(o * _rstd(o) * w)
        dz_ref[...] = (dy * jnp.concatenate(normed, axis=1) * _dsilu(z)).astype(BF16)

        @pl.when(i == 0)
        def _():
            acc[...] = jnp.zeros_like(acc)

        acc[...] += _colsum8(dwr)

        @pl.when(i == n - 1)
        def _():
            dw_ref[...] = jnp.sum(acc[...], axis=0, keepdims=True)

    return pl.pallas_call(
        body, name=name, grid=(n,),
        out_shape=(jax.ShapeDtypeStruct((H, S, HD), F32), jax.ShapeDtypeStruct((S, vw), BF16),
                   jax.ShapeDtypeStruct((1, HD), F32)),
        in_specs=[pl.BlockSpec((H, ts, HD), lambda i: (0, i, 0)), pl.BlockSpec((ts, vw), lambda i: (i, z_col)),
                  pl.BlockSpec((1, HD), lambda i: (0, 0)), pl.BlockSpec((ts, vw), lambda i: (i, 0))],
        out_specs=(pl.BlockSpec((H, ts, HD), lambda i: (0, i, 0)), pl.BlockSpec((ts, vw), lambda i: (i, 0)),
                   pl.BlockSpec((1, HD), lambda i: (0, 0))),
        scratch_shapes=[pltpu.VMEM((8, HD), F32)], compiler_params=_params(),
    )(o_hm, proj, gdn_w, dy)


def _pool_trailing(ext, ts, pg, row0):
    outs, inv_cnts = [], []
    t_abs = row0 + lax.broadcasted_iota(jnp.int32, (ts, 1), 0)
    for gi, win in enumerate(POOL_WINDOWS):
        cols = pl.ds(gi * pg, pg)
        cur = ext[pl.ds(HALO, ts), cols]
        acc = cur
        for j in range(1, win):
            acc = acc + ext[pl.ds(HALO - j, ts), cols]
        inv = 1.0 / jnp.minimum(t_abs + 1, win).astype(F32)
        outs.append(acc * inv - cur)
    return outs


def _pool_fwd(proj, p_col, pool_w, pool_scale, pw, name):
    S = proj.shape[0]
    pg = pw // len(POOL_WINDOWS)
    ts = _tile(S, 512, HALO)

    def body(x_ref, h_ref, w_ref, sc_ref, o_ref, ext):
        i = pl.program_id(0)
        _fill_causal(ext, x_ref[...].astype(F32), h_ref[...].astype(F32), i)
        ys = _pool_trailing(ext, ts, pg, i * ts)
        outs = [jnp.dot(ys[gi].astype(BF16), w_ref[gi], preferred_element_type=F32) for gi in range(len(ys))]
        o_ref[...] = (jnp.concatenate(outs, axis=1) * sc_ref[...]).astype(BF16)

    return pl.pallas_call(
        body, name=name, out_shape=jax.ShapeDtypeStruct((S, pw), BF16), grid=(S // ts,),
        in_specs=[pl.BlockSpec((ts, pw), lambda i: (i, p_col)),
                  pl.BlockSpec((HALO, pw), lambda i: (jnp.maximum(i * (ts // HALO) - 1, 0), p_col)),
                  pl.BlockSpec((len(POOL_WINDOWS), pg, pg), lambda i: (0, 0, 0)), pl.BlockSpec((1, pw), lambda i: (0, 0))],
        out_specs=pl.BlockSpec((ts, pw), lambda i: (i, 0)),
        scratch_shapes=[pltpu.VMEM((ts + HALO, pw), F32)], compiler_params=_params(),
    )(proj, proj, pool_w, pool_scale)


def _pool_bwd(proj, p_col, pool_w, pool_scale, dpb, pw, name):
    S = proj.shape[0]
    G = len(POOL_WINDOWS)
    pg = pw // G
    ts = _tile(S, 512, HALO)
    n = S // ts

    def body(x_ref, h_ref, w_ref, sc_ref, d_ref, dn_ref, dp_ref, dw_ref, dsc_ref, ext, zext, wacc, sacc):
        i = pl.program_id(0)
        _fill_causal(ext, x_ref[...].astype(F32), h_ref[...].astype(F32), i)
        ys = _pool_trailing(ext, ts, pg, i * ts)
        d_ext = jnp.concatenate([d_ref[...].astype(F32), jnp.where(i < n - 1, dn_ref[...].astype(F32), 0.0)], axis=0)
        dt = d_ext * sc_ref[...]
        t_abs = i * ts + lax.broadcasted_iota(jnp.int32, (ts + HALO, 1), 0)

        @pl.when(i == 0)
        def _():
            wacc[...] = jnp.zeros_like(wacc)
            sacc[...] = jnp.zeros_like(sacc)

        dps, tfs = [], []
        for gi, win in enumerate(POOL_WINDOWS):
            cols = slice(gi * pg, (gi + 1) * pg)
            w = w_ref[gi]
            dt_g = dt[:, cols].astype(BF16)
            y_g = ys[gi].astype(BF16)
            tfs.append(jnp.dot(y_g, w, preferred_element_type=F32))
            wacc[gi] += lax.dot_general(y_g, dt_g[:ts], (((0,), (0,)), ((), ())), preferred_element_type=F32)
            dyp = lax.dot_general(dt_g, w, (((1,), (1,)), ((), ())), preferred_element_type=F32)
            zext[:, pl.ds(gi * pg, pg)] = dyp * (1.0 / jnp.minimum(t_abs + 1, win).astype(F32))
            acc = -dyp[:ts]
            for j in range(win):
                acc = acc + zext[pl.ds(j, ts), pl.ds(gi * pg, pg)]
            dps.append(acc)
        dp_ref[...] = jnp.concatenate(dps, axis=1).astype(BF16)
        sacc[...] += _colsum8(d_ext[:ts] * jnp.concatenate(tfs, axis=1))

        @pl.when(i == n - 1)
        def _():
            dw_ref[...] = wacc[...]
            dsc_ref[...] = jnp.sum(sacc[...], axis=0, keepdims=True)

    return pl.pallas_call(
        body, name=name, grid=(n,),
        out_shape=(jax.ShapeDtypeStruct((S, pw), BF16), jax.ShapeDtypeStruct((G, pg, pg), F32),
                   jax.ShapeDtypeStruct((1, pw), F32)),
        in_specs=[pl.BlockSpec((ts, pw), lambda i: (i, p_col)),
                  pl.BlockSpec((HALO, pw), lambda i: (jnp.maximum(i * (ts // HALO) - 1, 0), p_col)),
                  pl.BlockSpec((G, pg, pg), lambda i: (0, 0, 0)), pl.BlockSpec((1, pw), lambda i: (0, 0)),
                  pl.BlockSpec((ts, pw), lambda i: (i, 0)),
                  pl.BlockSpec((HALO, pw), lambda i: (jnp.minimum((i + 1) * (ts // HALO), S // HALO - 1), 0))],
        out_specs=(pl.BlockSpec((ts, pw), lambda i: (i, 0)), pl.BlockSpec((G, pg, pg), lambda i: (0, 0, 0)),
                   pl.BlockSpec((1, pw), lambda i: (0, 0))),
        scratch_shapes=[pltpu.VMEM((ts + HALO, pw), F32), pltpu.VMEM((ts + HALO, pw), F32),
                        pltpu.VMEM((G, pg, pg), F32), pltpu.VMEM((8, pw), F32)],
        compiler_params=_params(),
    )(proj, proj, pool_w, pool_scale, dpb, dpb)


def _merge(proj, ga_col, gb_col, ya, yb, name):
    S, d = ya.shape
    ts = _tile(S, 512, 16)

    def body(ga_ref, gb_ref, ya_ref, yb_ref, o_ref):
        o_ref[...] = (_sigmoid(ga_ref[...].astype(F32)) * ya_ref[...].astype(F32)
                      + _sigmoid(gb_ref[...].astype(F32)) * yb_ref[...].astype(F32)).astype(BF16)

    row = pl.BlockSpec((ts, d), lambda i: (i, 0))
    return pl.pallas_call(
        body, name=name, out_shape=jax.ShapeDtypeStruct((S, d), BF16), grid=(S // ts,),
        in_specs=[pl.BlockSpec((ts, d), lambda i: (i, ga_col)), pl.BlockSpec((ts, d), lambda i: (i, gb_col)), row, row],
        out_specs=row, compiler_params=_params(),
    )(proj, proj, ya, yb)


def _merge_bwd(proj, ga_col, gb_col, ya, yb, dm, name):
    S, d = ya.shape
    ts = _tile(S, 512, 16)

    def body(ga_ref, gb_ref, ya_ref, yb_ref, dm_ref, dya_ref, dyb_ref, dga_ref, dgb_ref):
        dmv = dm_ref[...].astype(F32)
        sa, sb = _sigmoid(ga_ref[...].astype(F32)), _sigmoid(gb_ref[...].astype(F32))
        dya_ref[...] = (dmv * sa).astype(BF16)
        dyb_ref[...] = (dmv * sb).astype(BF16)
        dga_ref[...] = (dmv * ya_ref[...].astype(F32) * sa * (1.0 - sa)).astype(BF16)
        dgb_ref[...] = (dmv * yb_ref[...].astype(F32) * sb * (1.0 - sb)).astype(BF16)

    row = pl.BlockSpec((ts, d), lambda i: (i, 0))
    o = jax.ShapeDtypeStruct((S, d), BF16)
    return pl.pallas_call(
        body, name=name, out_shape=(o, o, o, o), grid=(S // ts,),
        in_specs=[pl.BlockSpec((ts, d), lambda i: (i, ga_col)), pl.BlockSpec((ts, d), lambda i: (i, gb_col)), row, row, row],
        out_specs=(row, row, row, row), compiler_params=_params(),
    )(proj, proj, ya, yb, dm)


def _xattn_fwd(q, kv, name):
    S, d = q.shape
    M = kv.shape[0]
    hd = d // XA_HEADS
    ts = _tile(S, 512, 16)
    scale = hd ** -0.5

    def body(q_ref, k_ref, v_ref, o_ref):
        s = lax.dot_general(q_ref[...], k_ref[...], (((1,), (1,)), ((), ())), preferred_element_type=F32) * scale
        p = jnp.exp(s - jnp.max(s, axis=-1, keepdims=True))
        p = p / jnp.sum(p, axis=-1, keepdims=True)
        o_ref[...] = jnp.dot(p.astype(BF16), v_ref[...], preferred_element_type=F32).astype(BF16)

    return pl.pallas_call(
        body, name=name, out_shape=jax.ShapeDtypeStruct((S, d), BF16), grid=(S // ts, XA_HEADS),
        in_specs=[pl.BlockSpec((ts, hd), lambda i, h: (i, h)), pl.BlockSpec((M, hd), lambda i, h: (0, h)),
                  pl.BlockSpec((M, hd), lambda i, h: (0, XA_HEADS + h))],
        out_specs=pl.BlockSpec((ts, hd), lambda i, h: (i, h)), compiler_params=_params(),
    )(q, kv, kv)


def _xattn_bwd(q, kv, do, name):
    S, d = q.shape
    M = kv.shape[0]
    hd = d // XA_HEADS
    ts = _tile(S, 512, 16)
    n = S // ts
    scale = hd ** -0.5

    def body(q_ref, k_ref, v_ref, do_ref, dq_ref, dk_ref, dv_ref, kacc, vacc):
        i = pl.program_id(1)
        qv, kv_, vv, dov = q_ref[...], k_ref[...], v_ref[...], do_ref[...]
        s = lax.dot_general(qv, kv_, (((1,), (1,)), ((), ())), preferred_element_type=F32) * scale
        p = jnp.exp(s - jnp.max(s, axis=-1, keepdims=True))
        p = p / jnp.sum(p, axis=-1, keepdims=True)
        dp = lax.dot_general(dov, vv, (((1,), (1,)), ((), ())), preferred_element_type=F32)
        ds = (p * (dp - jnp.sum(p * dp, axis=-1, keepdims=True)) * scale).astype(BF16)
        dq_ref[...] = jnp.dot(ds, kv_, preferred_element_type=F32).astype(BF16)

        @pl.when(i == 0)
        def _():
            kacc[...] = jnp.zeros_like(kacc)
            vacc[...] = jnp.zeros_like(vacc)

        kacc[...] += lax.dot_general(ds, qv, (((0,), (0,)), ((), ())), preferred_element_type=F32)
        vacc[...] += lax.dot_general(p.astype(BF16), dov, (((0,), (0,)), ((), ())), preferred_element_type=F32)

        @pl.when(i == n - 1)
        def _():
            dk_ref[...] = kacc[...]
            dv_ref[...] = vacc[...]

    dq, dk, dv = pl.pallas_call(
        body, name=name, grid=(XA_HEADS, n),
        out_shape=(jax.ShapeDtypeStruct((S, d), BF16), jax.ShapeDtypeStruct((M, d), F32), jax.ShapeDtypeStruct((M, d), F32)),
        in_specs=[pl.BlockSpec((ts, hd), lambda h, i: (i, h)), pl.BlockSpec((M, hd), lambda h, i: (0, h)),
                  pl.BlockSpec((M, hd), lambda h, i: (0, XA_HEADS + h)), pl.BlockSpec((ts, hd), lambda h, i: (i, h))],
        out_specs=(pl.BlockSpec((ts, hd), lambda h, i: (i, h)), pl.BlockSpec((M, hd), lambda h, i: (0, h)),
                   pl.BlockSpec((M, hd), lambda h, i: (0, h))),
        scratch_shapes=[pltpu.VMEM((M, hd), F32), pltpu.VMEM((M, hd), F32)], compiler_params=_params(),
    )(q, kv, kv, do)
    return dq, jnp.concatenate([dk, dv], axis=1)


def _ffn_act(up, conv_w, bias, name):
    S, F2 = up.shape
    F = F2 // 2
    ts, cw = _tile(S, SEQ_TILE, HALO), _tile(F, 512)
    nb = F // cw

    def body(a_ref, ah_ref, b_ref, bh_ref, wa_ref, wb_ref, ba_ref, bb_ref, o_ref, ea, eb):
        i = pl.program_id(0)
        _fill_causal(ea, a_ref[...].astype(F32), ah_ref[...].astype(F32), i)
        _fill_causal(eb, b_ref[...].astype(F32), bh_ref[...].astype(F32), i)
        wa, wb, bia, bib = wa_ref[...], wb_ref[...], ba_ref[...], bb_ref[...]

        def chunk(r0):
            ua = _filter(wa, _causal_taps(ea, r0, FFN_CONV)) + bia
            ub = _filter(wb, _causal_taps(eb, r0, FFN_CONV)) + bib
            o_ref[pl.ds(r0, RC), :] = (_silu(ua) * ub).astype(BF16)

        _row_chunks(ts, chunk)

    tile = lambda c0: pl.BlockSpec((ts, cw), lambda i, j: (i, j + c0))
    vec = lambda rows, c0: pl.BlockSpec((rows, cw), lambda i, j: (0, j + c0))
    return pl.pallas_call(
        body, name=name, out_shape=jax.ShapeDtypeStruct((S, F), BF16), grid=(S // ts, nb),
        in_specs=[tile(0), _prev_halo_spec(ts, cw), tile(nb), _prev_halo_spec(ts, cw, nb),
                  vec(FFN_CONV, 0), vec(FFN_CONV, nb), vec(1, 0), vec(1, nb)],
        out_specs=pl.BlockSpec((ts, cw), lambda i, j: (i, j)),
        scratch_shapes=[pltpu.VMEM((ts + HALO, cw), F32), pltpu.VMEM((ts + HALO, cw), F32)],
        compiler_params=_params(),
    )(up, up, up, up, conv_w, conv_w, bias, bias)


def _ffn_act_bwd(up, conv_w, bias, dact, name):
    S, F2 = up.shape
    F = F2 // 2
    ts, cw = _tile(S, SEQ_TILE, HALO), _tile(F, 512)
    nb = F // cw
    n = S // ts

    def body(a_ref, ah_ref, b_ref, bh_ref, wa_ref, wb_ref, ba_ref, bb_ref, d_ref,
             du_ref, dwa_ref, dwb_ref, dba_ref, dbb_ref, ea, eb, wacc, bacc):
        i = pl.program_id(1)
        _fill_causal(ea, a_ref[...].astype(F32), ah_ref[...].astype(F32), i)
        _fill_causal(eb, b_ref[...].astype(F32), bh_ref[...].astype(F32), i)
        wa, wb, bia, bib = wa_ref[...], wb_ref[...], ba_ref[...], bb_ref[...]

        @pl.when(i == 0)
        def _():
            wacc[...] = jnp.zeros_like(wacc)
            bacc[...] = jnp.zeros_like(bacc)

        def chunk(r0):
            ta, tb = _causal_taps(ea, r0, FFN_CONV), _causal_taps(eb, r0, FFN_CONV)
            ua, ub = _filter(wa, ta) + bia, _filter(wb, tb) + bib
            d = d_ref[pl.ds(r0, RC), :].astype(F32)
            dua = d * ub * _dsilu(ua)
            dub = d * _silu(ua)
            du_ref[0, pl.ds(r0, RC), :] = dua.astype(BF16)
            du_ref[1, pl.ds(r0, RC), :] = dub.astype(BF16)
            for t in range(FFN_CONV):
                wacc[0, t] += _colsum8(dua * ta[FFN_CONV - 1 - t])
                wacc[1, t] += _colsum8(dub * tb[FFN_CONV - 1 - t])
            bacc[0] += _colsum8(dua)
            bacc[1] += _colsum8(dub)

        _row_chunks(ts, chunk)

        @pl.when(i == n - 1)
        def _():
            dwa_ref[...] = jnp.sum(wacc[0], axis=1)
            dwb_ref[...] = jnp.sum(wacc[1], axis=1)
            dba_ref[...] = jnp.sum(bacc[0], axis=0, keepdims=True)
            dbb_ref[...] = jnp.sum(bacc[1], axis=0, keepdims=True)

    tile = lambda c0: pl.BlockSpec((ts, cw), lambda j, i: (i, j + c0))
    halo = lambda c0: pl.BlockSpec((HALO, cw), lambda j, i: (jnp.maximum(i * (ts // HALO) - 1, 0), j + c0))
    vec = lambda rows, c0: pl.BlockSpec((rows, cw), lambda j, i: (0, j + c0))
    du, dwa, dwb, dba, dbb = pl.pallas_call(
        body, name=name, grid=(nb, n),
        out_shape=(jax.ShapeDtypeStruct((2, S, F), BF16),
                   jax.ShapeDtypeStruct((FFN_CONV, F), F32), jax.ShapeDtypeStruct((FFN_CONV, F), F32),
                   jax.ShapeDtypeStruct((1, F), F32), jax.ShapeDtypeStruct((1, F), F32)),
        in_specs=[tile(0), halo(0), tile(nb), halo(nb), vec(FFN_CONV, 0), vec(FFN_CONV, nb), vec(1, 0), vec(1, nb), tile(0)],
        out_specs=(pl.BlockSpec((2, ts, cw), lambda j, i: (0, i, j)), vec(FFN_CONV, 0), vec(FFN_CONV, 0), vec(1, 0),
                   vec(1, 0)),
        scratch_shapes=[pltpu.VMEM((ts + HALO, cw), F32), pltpu.VMEM((ts + HALO, cw), F32),
                        pltpu.VMEM((2, FFN_CONV, 8, cw), F32), pltpu.VMEM((2, 8, cw), F32)],
        compiler_params=_params(),
    )(up, up, up, up, conv_w, conv_w, bias, bias, dact)
    return du, jnp.concatenate([dwa, dwb], axis=1), jnp.concatenate([dba, dbb], axis=1)


def _adamw(gparts, w, m, v, name):
    R, C = w.shape
    tr = _tile(R, max(16, (256 * 1024) // C), 16)

    def body(g_ref, w_ref, m_ref, v_ref, go_ref, d_ref, mo_ref, vo_ref):
        g = g_ref[0].astype(F32)
        for s in range(1, N_DEV):
            g = g + g_ref[s].astype(F32)
        mn = ADAM_B1 * m_ref[...] + (1.0 - ADAM_B1) * g
        vn = ADAM_B2 * v_ref[...] + (1.0 - ADAM_B2) * (g * g)
        m_hat = mn / (1.0 - ADAM_B1 ** ADAM_STEP)
        v_hat = vn / (1.0 - ADAM_B2 ** ADAM_STEP)
        go_ref[...] = g
        d_ref[...] = -ADAM_LR * (m_hat / (jnp.sqrt(v_hat) + ADAM_EPS) + ADAM_WD * w_ref[...])
        mo_ref[...] = mn
        vo_ref[...] = vn

    row = pl.BlockSpec((tr, C), lambda i: (i, 0))
    o = jax.ShapeDtypeStruct((R, C), F32)
    return pl.pallas_call(
        body, name=name, out_shape=(o, o, o, o), grid=(R // tr,),
        in_specs=[pl.BlockSpec((N_DEV, tr, C), lambda i: (0, i, 0)), row, row, row],
        out_specs=(row, row, row, row), compiler_params=_params(),
    )(gparts, w, m, v)


def _position():
    return lax.axis_index("x"), lax.axis_index("y"), lax.axis_index("c")


class _Copies:
    def __init__(self, arrays):
        self.arrays, self.n = list(arrays), len(arrays)
        self.scratch = [pltpu.SemaphoreType.DMA((7 * self.n,)), pltpu.SemaphoreType.DMA((7 * self.n,)),
                        pltpu.SemaphoreType.DMA((self.n,))]


class _Gather(_Copies):
    def __init__(self, arrays):
        super().__init__(arrays)
        self.out_shape = [jax.ShapeDtypeStruct((N_DEV,) + b.shape, b.dtype) for b in self.arrays]

    def phases(self, x_refs, out_refs, send_sems, recv_sems, local_sems):
        n = self.n
        x, y, c = _position()
        me, sibling = (x, y, c), (x, y, 1 - c)
        chips = [(1 - x, y), (x, 1 - y), (1 - x, 1 - y)]

        def copy(a, k, blk, to, own=False):
            slot = out_refs[a].at[4 * blk[0] + 2 * blk[1] + blk[2]]
            return pltpu.make_async_remote_copy(
                src_ref=x_refs[a] if own else slot, dst_ref=slot,
                send_sem=send_sems.at[7 * a + k], recv_sem=recv_sems.at[7 * a + k], device_id=to, device_id_type=MESH)

        mine = [pltpu.make_async_copy(x_refs[a], out_refs[a].at[4 * x + 2 * y + c], local_sems.at[a]) for a in range(n)]
        first = [copy(a, 0, me, sibling, own=True) for a in range(n)]
        first += [copy(a, 1 + j, me, (*chip, c), own=True) for a in range(n) for j, chip in enumerate(chips)]
        passed = [copy(a, 4 + j, (*chip, c), sibling) for j, chip in enumerate(chips) for a in range(n)]

        def start():
            for cp in mine + first:
                cp.start()

        def hand_on():
            for j, chip in enumerate(chips):
                for a in range(n):
                    copy(a, 1 + j, (*chip, c), me).wait_recv()
                    passed[j * n + a].start()

        def finish():
            for a in range(n):
                copy(a, 0, sibling, me).wait_recv()
            for j, chip in enumerate(chips):
                for a in range(n):
                    copy(a, 4 + j, (*chip, 1 - c), me).wait_recv()
            for cp in first + passed:
                cp.wait_send()
            for cp in mine:
                cp.wait()

        return start, hand_on, finish


class _Exchange(_Copies):
    def __init__(self, arrays):
        super().__init__(arrays)
        self.out_shape = [jax.ShapeDtypeStruct(p.shape, p.dtype) for p in self.arrays]

    def phases(self, p_refs, out_refs, send_sems, recv_sems, local_sems):
        n = self.n
        x, y, c = _position()
        my_slot = 4 * x + 2 * y + c
        mine = [pltpu.make_async_copy(p_refs[a].at[my_slot], out_refs[a].at[my_slot], local_sems.at[a]) for a in range(n)]
        copies = []
        for k in range(1, N_DEV):
            px, py, pc = x ^ (k >> 2), y ^ ((k >> 1) & 1), c ^ (k & 1)
            for a in range(n):
                copies.append(pltpu.make_async_remote_copy(
                    src_ref=p_refs[a].at[4 * px + 2 * py + pc], dst_ref=out_refs[a].at[my_slot],
                    send_sem=send_sems.at[7 * a + k - 1], recv_sem=recv_sems.at[7 * a + k - 1],
                    device_id=(px, py, pc), device_id_type=MESH))

        def start():
            for cp in mine + copies:
                cp.start()

        def finish():
            for cp in copies + mine:
                cp.wait()

        return start, None, finish


def _communicate(copies, name):
    n = copies.n

    def body(*refs):
        start, hand_on, finish = copies.phases(refs[:n], refs[n:2 * n], *refs[2 * n:])
        start()
        if hand_on is not None:
            hand_on()
        finish()

    return pl.pallas_call(body, name=name, out_shape=copies.out_shape, in_specs=[ANY] * n, out_specs=[ANY] * n,
                          scratch_shapes=copies.scratch)(*copies.arrays)


def _col_pieces(col_map, shard_w):
    pieces = []
    for lo, hi, dst in col_map:
        c = lo
        while c < hi:
            j = c // shard_w
            end = min(hi, (j + 1) * shard_w)
            pieces.append((j, c - j * shard_w, end - c, dst + (c - lo)))
            c = end
    return pieces


def _assemble_cols(shards, pieces, width, name):
    _, R, Cs = shards.shape
    tr = _tile(R, 128, 16)

    def body(s_ref, o_ref):
        o_ref[...] = jnp.zeros(o_ref.shape, o_ref.dtype)
        for j, lo, n, dst in pieces:
            o_ref[:, dst:dst + n] = s_ref[j, :, lo:lo + n]

    return pl.pallas_call(
        body, name=name, out_shape=jax.ShapeDtypeStruct((R, width), shards.dtype), grid=(R // tr,),
        in_specs=[pl.BlockSpec((N_DEV, tr, Cs), lambda i: (0, i, 0))],
        out_specs=pl.BlockSpec((tr, width), lambda i: (i, 0)), compiler_params=_params(),
    )(shards)


def _split_cols(full, pieces, shard_w, name):
    R, width = full.shape
    tr = _tile(R, 128, 16)

    def body(f_ref, o_ref):
        for j, lo, n, dst in pieces:
            o_ref[j, :, lo:lo + n] = f_ref[:, dst:dst + n]

    return pl.pallas_call(
        body, name=name, out_shape=jax.ShapeDtypeStruct((N_DEV, R, shard_w), full.dtype), grid=(R // tr,),
        in_specs=[pl.BlockSpec((tr, width), lambda i: (i, 0))],
        out_specs=pl.BlockSpec((N_DEV, tr, shard_w), lambda i: (0, i, 0)), compiler_params=_params(),
    )(full)


def _pack(arrays, row_multiple=8):
    flat, layout, off = [], [], 0
    for a in arrays:
        n = a.size
        padded = -(-n // LANE) * LANE
        f = a.reshape(-1).astype(F32)
        if padded != n:
            f = jnp.pad(f, (0, padded - n))
        flat.append(f)
        layout.append((off, n, a.shape))
        off += padded
    total = -(-off // (LANE * row_multiple)) * (LANE * row_multiple)
    if total != off:
        flat.append(jnp.zeros((total - off,), F32))
    return jnp.concatenate(flat).reshape(total // LANE, LANE), layout


def _unpack(buf, layout):
    flat = buf.reshape(-1)
    return [flat[off:off + n].reshape(shape) for off, n, shape in layout]


def _cols_to_full(g):
    return jnp.transpose(g, (1, 0, 2)).reshape(g.shape[1], N_DEV * g.shape[2])


def _full_to_cols(a):
    return jnp.transpose(a.reshape(a.shape[0], N_DEV, a.shape[1] // N_DEV), (1, 0, 2))


def _rows_to_full(g):
    return g.reshape(N_DEV * g.shape[1], g.shape[2])


def _full_to_rows(a):
    return a.reshape(N_DEV, a.shape[0] // N_DEV, a.shape[1])


def _pad_cols(a, width):
    return a if a.shape[-1] == width else jnp.pad(a, [(0, 0)] * (a.ndim - 1) + [(0, width - a.shape[-1])])


SHARDED = ("w_in", "conv_qkv", "pool_w", "w_branch_a", "w_branch_b", "w_mix_out", "w_xq", "w_xkv", "w_xo", "w_up",
           "ffn_conv_w", "w_down")
REPLICATED = ("mix_pre_norm", "a_log", "dt_bias", "gdn_norm", "pool_scale", "mix_post_norm", "xa_pre_norm", "mem_norm",
              "xa_post_norm", "ffn_pre_norm", "ffn_conv_b", "ffn_post_norm")
WEIGHTS = ("mix_pre_norm", "w_in", "conv_qkv", "a_log", "dt_bias", "gdn_norm", "pool_w", "pool_scale", "w_branch_a",
           "w_branch_b", "w_mix_out", "mix_post_norm", "xa_pre_norm", "mem_norm", "w_xq", "w_xkv", "w_xo", "xa_post_norm",
           "ffn_pre_norm", "w_up", "ffn_conv_w", "ffn_conv_b", "w_down", "ffn_post_norm")
MATMUL_WEIGHTS = ("w_in", "w_branch_a", "w_branch_b", "w_mix_out", "w_xq", "w_xkv", "w_xo", "w_up", "w_down")
COL_SHARDED = ("w_in", "w_branch_b", "w_xkv", "w_up", "conv_qkv", "ffn_conv_w")
ROW_SHARDED = ("w_branch_a", "w_mix_out", "w_xq", "w_xo", "w_down")


class _Layout:
    def __init__(self, D, H, pw, F):
        self.D, self.H, self.pw, self.F = D, H, pw, F
        self.qkv_w, self.vw = 3 * H * HD, H * HD
        self.ba_w = 512 if D >= 2048 else LANE
        self.Fp = -(-F // 512) * 512 if F >= 512 else F
        q, vw = self.qkv_w, self.vw
        self.seg = dict(qkv=(0, q), z=(q, vw), ga=(q + vw, D), gb=(q + vw + D, D), p=(q + vw + 2 * D, pw),
                        ba=(q + vw + 2 * D + pw, self.ba_w))
        self.in_w = q + vw + 2 * D + pw + self.ba_w
        o_z, o_b = q, q + vw
        o_p = o_b + 2 * H
        o_ga = o_p + pw
        o_gb = o_ga + D
        self.d_in = o_gb + D
        self.in_map = [(0, o_z, self.seg["qkv"][0]), (o_z, o_b, self.seg["z"][0]), (o_b, o_p, self.seg["ba"][0]),
                       (o_p, o_ga, self.seg["p"][0]), (o_ga, o_gb, self.seg["ga"][0]), (o_gb, self.d_in, self.seg["gb"][0])]
        self.up_map = [(0, F, 0), (F, 2 * F, self.Fp)]

    def col(self, name, width):
        return self.seg[name][0] // width


def _local_step(x, mem, target, P, L, comm=None):
    D, H, pw, F, Fp = L.D, L.H, L.pw, L.F, L.Fp
    qkv_w, vw, ba_w = L.qkv_w, L.vw, L.ba_w
    col = L.col
    P = dict(P)
    win_p, cw3_p, fb_p = P["win_p"], P["cw3_p"], P["fb_p"]
    conv_qkv, pool_w = P["conv_qkv"], P["pool_w"]
    lanes = lambda vec: jnp.pad(vec.reshape(1, H).astype(F32), ((0, 0), (H, LANE - 2 * H)))
    a_log_l, dt_bias_l = lanes(P["a_log"]), lanes(P["dt_bias"])
    bf = lambda name: P[name]
    vecf = lambda name: P[name].reshape(1, -1).astype(F32)
    g = {}

    def carried(call, name, *args, **kw):
        if comm is not None and name in comm.GATHERS:
            *out, got = call(*args, name, rider=comm.gather(name), **kw)
            P.update(comm.weights_from(name, got))
        elif comm is not None and name in comm.EXCHANGES:
            *out, got = call(*args, name, rider=comm.exchange(name, g), **kw)
            comm.receive(name, got)
        else:
            out = call(*args, name, **kw)
            out = [out] if call is _matmul else list(out[:-1])
        return out[0] if len(out) == 1 else out

    h1 = _prenorm(x, vecf("mix_pre_norm"), "mix_prenorm")
    proj = carried(_matmul, "in_proj", h1, win_p, "nn", BF16, tn=1536)
    qkv_hm = _qkv_conv(proj, conv_qkv, qkv_w, "qkv_conv")
    bg = _gates(proj, col("ba", LANE), a_log_l, dt_bias_l, H, "gates")
    o_hm, states = carried(_gdn_fwd, "gdn_fwd", qkv_hm, bg, H)
    wup_p, wdown_p = P["wup_p"], P["wdown_p"]
    oa = _gdn_out(o_hm, proj, col("z", vw), vecf("gdn_norm"), "gdn_out")
    ya = _matmul(oa, bf("w_branch_a"), "nn", BF16, "branch_a")
    pb = _pool_fwd(proj, col("p", pw), pool_w, vecf("pool_scale"), pw, "pool_fwd")
    yb = _matmul(pb, bf("w_branch_b"), "nn", BF16, "branch_b")
    merged = _merge(proj, col("ga", D), col("gb", D), ya, yb, "merge")
    y1 = _matmul(merged, bf("w_mix_out"), "nn", F32, "mix_out")
    x1, h2 = _post_pre(x, y1, vecf("mix_post_norm"), vecf("xa_pre_norm"), "mix_post")
    mn = _prenorm(mem, vecf("mem_norm"), "mem_norm")
    qx = _matmul(h2, bf("w_xq"), "nn", BF16, "xq")
    kv = _matmul(mn, bf("w_xkv"), "nn", BF16, "xkv")
    ox = _xattn_fwd(qx, kv, "xattn_fwd")
    y2 = _matmul(ox, bf("w_xo"), "nn", F32, "xo")
    x2, h3 = _post_pre(x1, y2, vecf("xa_post_norm"), vecf("ffn_pre_norm"), "xa_post")
    up = _matmul(h3, wup_p, "nn", BF16, "ffn_up")
    act = _ffn_act(up, cw3_p, fb_p, "ffn_act")
    y3 = _matmul(act, wdown_p, "nn", F32, "ffn_down")
    dx3, loss = _post_loss(x2, y3, vecf("ffn_post_norm"), target, "ffn_post_loss")

    dy3, g["ffn_post_norm"] = _post_bwd(y3, vecf("ffn_post_norm"), dx3, "ffn_post_bwd")
    dact = _matmul(dy3, wdown_p, "nt", BF16, "ffn_down_dx")
    g["w_down_p"] = _matmul(act, dy3, "tn", BF16, "ffn_down_dw", tk=4096)
    du, g["ffn_conv_w_p"], g["ffn_conv_b_p"] = _ffn_act_bwd(up, cw3_p, fb_p, dact, "ffn_act_bwd")
    dup = _conv_t(du, jnp.stack([cw3_p[:, :Fp], cw3_p[:, Fp:]]), "ffn_conv_t")
    dh3 = carried(_matmul, "ffn_up_dx", dup, wup_p, "nt", BF16)
    g["w_up_p"] = _matmul(h3, dup, "tn", BF16, "ffn_up_dw", tn=512, tk=4096)
    dx2, g["ffn_pre_norm"] = _pre_bwd(x2, vecf("ffn_pre_norm"), dh3, dx3, "ffn_pre_bwd")
    dy2, g["xa_post_norm"] = _post_bwd(y2, vecf("xa_post_norm"), dx2, "xa_post_bwd")
    dox = _matmul(dy2, bf("w_xo"), "nt", BF16, "xo_dx")
    g["w_xo"] = _matmul(ox, dy2, "tn", BF16, "xo_dw", tn=512, tk=4096)
    dqx, dkv = _xattn_bwd(qx, kv, dox, "xattn_bwd")
    dkv_b = dkv.astype(BF16)
    dh2 = _matmul(dqx, bf("w_xq"), "nt", BF16, "xq_dx")
    g["w_xq"] = _matmul(h2, dqx, "tn", BF16, "xq_dw", tn=512, tk=4096)
    dmn = _matmul(dkv_b, bf("w_xkv"), "nt", F32, "xkv_dx")
    g["w_xkv"] = _matmul(mn, dkv_b, "tn", BF16, "xkv_dw")
    _, g["mem_norm"] = _pre_bwd(mem, vecf("mem_norm"), dmn, jnp.zeros_like(mem), "mem_norm_bwd")
    dx1, g["xa_pre_norm"] = _pre_bwd(x1, vecf("xa_pre_norm"), dh2, dx2, "xa_pre_bwd")
    dy1, g["mix_post_norm"] = _post_bwd(y1, vecf("mix_post_norm"), dx1, "mix_post_bwd")
    dmerged = _matmul(dy1, bf("w_mix_out"), "nt", BF16, "mix_out_dx")
    g["w_mix_out"] = _matmul(merged, dy1, "tn", BF16, "mix_out_dw", tn=512, tk=4096)
    dya, dyb, dga, dgb = _merge_bwd(proj, col("ga", D), col("gb", D), ya, yb, dmerged, "merge_bwd")
    doa = _matmul(dya, bf("w_branch_a"), "nt", BF16, "branch_a_dx")
    g["w_branch_a"] = _matmul(oa, dya, "tn", BF16, "branch_a_dw", tn=512, tk=4096)
    dpb = _matmul(dyb, bf("w_branch_b"), "nt", BF16, "branch_b_dx")
    g["w_branch_b"] = _matmul(pb, dyb, "tn", BF16, "branch_b_dw")
    dp, g["pool_w"], g["pool_scale"] = _pool_bwd(proj, col("p", pw), pool_w, vecf("pool_scale"), dpb, pw, "pool_bwd")
    do_hm, dz, g["gdn_norm"] = _gdn_out_bwd(o_hm, proj, col("z", vw), vecf("gdn_norm"), doa, "gdn_out_bwd")
    dqkv_hm, dbg = carried(_gdn_bwd, "gdn_bwd", qkv_hm, bg, states, do_hm, H)
    dba, dal, ddt = _gates_bwd(proj, col("ba", LANE), ba_w, a_log_l, dt_bias_l, dbg, H, "gates_bwd")
    g["a_log"], g["dt_bias"] = dal[:, H:2 * H], ddt[:, H:2 * H]
    dc, g["conv_qkv"] = _qkv_conv_bwd(proj, conv_qkv, dqkv_hm, qkv_w, "qkv_conv_bwd")
    dqkv = _conv_t(dc[None], conv_qkv[None], "qkv_conv_t")[0]
    dproj = jnp.concatenate([dqkv, dz, dga, dgb, dp, dba], axis=1)
    g["w_in_p"] = carried(_matmul, "in_proj_dw", h1, dproj, "tn", BF16, tn=768, tk=4096)
    dh1 = carried(_matmul, "in_proj_dx", dproj, win_p, "nt", BF16, tk=4608)
    grad_x, g["mix_pre_norm"] = _pre_bwd(x, vecf("mix_pre_norm"), dh1, dx1, "mix_pre_bwd")
    return loss, grad_x, g


def _two_halves(a, F, Fp):
    return jnp.concatenate([_pad_cols(a[..., :F], Fp), _pad_cols(a[..., F:], Fp)], axis=-1)


def _from_halves(a, F, Fp):
    return jnp.concatenate([a[..., :F], a[..., Fp:Fp + F]], axis=-1)


class _StepComm:
    FIRST = ("w_in", "conv_qkv", "pool_w", "ffn_conv_w")
    GATHERS = {"in_proj": ("w_branch_a", "w_branch_b", "w_mix_out", "w_xq", "w_xo"),
               "gdn_fwd": ("w_xkv", "w_up", "w_down")}
    EXCHANGES = {"ffn_up_dx": ("w_down", "ffn_conv_w"),
                 "gdn_bwd": ("pool_w", "w_branch_a", "w_branch_b", "w_mix_out", "w_xq", "w_xkv", "w_xo"),
                 "in_proj_dw": ("w_up",),
                 "in_proj_dx": ("w_in", "conv_qkv")}

    def __init__(self, w, L):
        self.w, self.L = w, L
        self.in_pieces = _col_pieces(L.in_map, w["w_in"].shape[1])
        self.up_pieces = _col_pieces(L.up_map, w["w_up"].shape[1])
        self.received = {}

    def _shard(self, n):
        return self.w[n].astype(BF16) if n in MATMUL_WEIGHTS else self.w[n]

    def first_weights(self):
        L, (g, r, c) = self.L, self.w["pool_w"].shape
        G = dict(zip(self.FIRST, _communicate(_Gather([self._shard(n) for n in self.FIRST]), "gather_first")))
        return {"win_p": _assemble_cols(G["w_in"], self.in_pieces, L.in_w, "assemble_w_in"),
                "conv_qkv": _cols_to_full(G["conv_qkv"]),
                "cw3_p": _two_halves(_cols_to_full(G["ffn_conv_w"]), L.F, L.Fp),
                "pool_w": jnp.transpose(G["pool_w"], (1, 0, 2, 3)).reshape(g, N_DEV * r, c).astype(BF16)}

    def gather(self, call):
        return _Gather([self._shard(n) for n in self.GATHERS[call]])

    def weights_from(self, call, results):
        L, P = self.L, {}
        for n, shards in zip(self.GATHERS[call], results):
            if n == "w_up":
                P["wup_p"] = _assemble_cols(shards, self.up_pieces, 2 * L.Fp, "assemble_w_up")
            elif n == "w_down":
                P["wdown_p"] = jnp.pad(_rows_to_full(shards), ((0, L.Fp - L.F), (0, 0)))
            else:
                P[n] = _rows_to_full(shards) if n in ROW_SHARDED else _cols_to_full(shards)
        return P

    def _slices(self, g, n):
        L, w = self.L, self.w
        if n == "w_in":
            return _split_cols(g["w_in_p"], self.in_pieces, w["w_in"].shape[1], "split_w_in")
        if n == "w_up":
            return _split_cols(g["w_up_p"], self.up_pieces, w["w_up"].shape[1], "split_w_up")
        if n == "w_down":
            return _full_to_rows(g["w_down_p"][:L.F])
        if n == "ffn_conv_w":
            return _full_to_cols(_from_halves(g["ffn_conv_w_p"], L.F, L.Fp))
        if n == "pool_w":
            grp, r, c = w["pool_w"].shape
            return jnp.transpose(g[n].reshape(grp, N_DEV, r, c), (1, 0, 2, 3)).reshape(N_DEV, grp * r, c)
        return _full_to_rows(g[n]) if n in ROW_SHARDED else _full_to_cols(g[n])

    def exchange(self, call, g):
        return _Exchange([self._slices(g, n) for n in self.EXCHANGES[call]])

    def receive(self, call, results):
        self.received.update(zip(self.EXCHANGES[call], results))


def kernel(x, mem, mix_pre_norm, w_in, conv_qkv, a_log, dt_bias, gdn_norm, pool_w, pool_scale, w_branch_a, w_branch_b, w_mix_out, mix_post_norm, xa_pre_norm, mem_norm, w_xq, w_xkv, w_xo, xa_post_norm, ffn_pre_norm, w_up, ffn_conv_w, ffn_conv_b, w_down, ffn_post_norm, loss_target, m_mix_pre_norm, m_w_in, m_conv_qkv, m_a_log, m_dt_bias, m_gdn_norm, m_pool_w, m_pool_scale, m_w_branch_a, m_w_branch_b, m_w_mix_out, m_mix_post_norm, m_xa_pre_norm, m_mem_norm, m_w_xq, m_w_xkv, m_w_xo, m_xa_post_norm, m_ffn_pre_norm, m_w_up, m_ffn_conv_w, m_ffn_conv_b, m_w_down, m_ffn_post_norm, v_mix_pre_norm, v_w_in, v_conv_qkv, v_a_log, v_dt_bias, v_gdn_norm, v_pool_w, v_pool_scale, v_w_branch_a, v_w_branch_b, v_w_mix_out, v_mix_post_norm, v_xa_pre_norm, v_mem_norm, v_w_xq, v_w_xkv, v_w_xo, v_xa_post_norm, v_ffn_pre_norm, v_w_up, v_ffn_conv_w, v_ffn_conv_b, v_w_down, v_ffn_post_norm):
    given = dict(locals())
    w = {n: given[n][0] for n in WEIGHTS}
    m = {n: given["m_" + n][0] for n in WEIGHTS}
    v = {n: given["v_" + n][0] for n in WEIGHTS}
    D = x.shape[-1]
    F = w["w_down"].shape[0] * N_DEV
    L = _Layout(D, w["a_log"].shape[-1], w["pool_scale"].shape[-1], F)
    Fp = L.Fp

    comm = _StepComm(w, L)
    P = {n: w[n] for n in REPLICATED}
    P.update(comm.first_weights())
    P["fb_p"] = _two_halves(w["ffn_conv_b"].reshape(1, 2 * F), F, Fp)
    loss, grad_x, g = _local_step(x[0], mem[0], loss_target[0], P, L, comm)

    received = comm.received
    outs = {}
    for n in SHARDED:
        as2d = lambda a: a.reshape(-1, a.shape[-1])
        res = _adamw(received[n], as2d(w[n]), as2d(m[n]), as2d(v[n]), "adamw_" + n)
        outs[n] = [r.reshape(w[n].shape) for r in res]

    g["ffn_conv_b"] = _from_halves(g["ffn_conv_b_p"], F, Fp)
    rep_parts, rep_layout = _pack([g[n].reshape(w[n].shape) for n in REPLICATED] + [loss])
    rep_all, = _communicate(_Gather([rep_parts]), "gather_small_grads")
    zero_loss = jnp.zeros_like(loss)
    wr, _ = _pack([w[n] for n in REPLICATED] + [zero_loss])
    mr, _ = _pack([m[n] for n in REPLICATED] + [zero_loss])
    vr, _ = _pack([v[n] for n in REPLICATED] + [zero_loss])
    outs_rep = [_unpack(o, rep_layout) for o in _adamw(rep_all, wr, mr, vr, "adamw_replicated")]
    loss_total = outs_rep[0][-1][0, 0]
    for i, n in enumerate(REPLICATED):
        outs[n] = [outs_rep[k][i] for k in range(4)]

    result = [loss_total, grad_x[None]]
    for k in range(4):
        for n in WEIGHTS:
            result.append(outs[n][k][None])
    return tuple(result)
```

```python
import functools

import jax
import jax.numpy as jnp
from jax import lax
from jax.experimental import pallas as pl
from jax.experimental.pallas import tpu as pltpu

F32, BF16 = jnp.float32, jnp.bfloat16
MESH = pl.DeviceIdType.MESH
ANY = pl.BlockSpec(memory_space=pl.ANY)

N_DEV = 8
EPS = 1e-6
CHUNK = 64
HD = 128
GDN_CONV = 4
FFN_CONV = 3
POOL_WINDOWS = (2, 4, 8, 16)
XA_HEADS = 4
HALO = 16
RC = 128
SEQ_TILE = 1024
LANE = 128
VMEM_LIMIT = 48 * 1024 * 1024

ADAM_LR, ADAM_B1, ADAM_B2, ADAM_EPS, ADAM_WD, ADAM_STEP = 0.001, 0.9, 0.999, 1e-08, 0.01, 10


def _tile(n, pref, align=LANE):
    best = None
    t = align
    while t <= min(n, pref):
        if n % t == 0:
            best = t
        t += align
    return best if best is not None else n


def _params(**kw):
    return pltpu.CompilerParams(vmem_limit_bytes=VMEM_LIMIT, **kw)


def _sigmoid(x):
    return 1.0 / (1.0 + jnp.exp(-x))


def _silu(x):
    return x * _sigmoid(x)


def _dsilu(x):
    s = _sigmoid(x)
    return s * (1.0 + x * (1.0 - s))


def _colsum8(t):
    return t.reshape(t.shape[0] // 8, 8, t.shape[1]).sum(axis=0)


def _ride(body, n_in, n_out, rider, first, middle, last):
    if rider is None:
        return body
    n = rider.n

    def wrapped(*refs):
        ins, r_in = refs[:n_in], refs[n_in:n_in + n]
        outs, r_out = refs[n_in + n:n_in + n + n_out], refs[n_in + n + n_out:n_in + 2 * n + n_out]
        rest = refs[n_in + 2 * n + n_out:]
        start, hand_on, finish = rider.phases(r_in, r_out, *rest[-3:])
        pl.when(first())(start)
        body(*ins, *outs, *rest[:-3])
        if hand_on is not None:
            pl.when(middle())(hand_on)
        pl.when(last())(finish)

    return wrapped


def _matmul(a, b, mode, out_dtype, name, tm=1024, tn=1024, tk=2816, rider=None):
    ga = a.shape[0] if (mode == "nt" and a.ndim == 3) else 1
    gb = b.shape[0] if (mode == "tn" and b.ndim == 3) else 1
    if mode == "nn":
        (M, K), (K2, N) = a.shape, b.shape
    elif mode == "nt":
        M, K = a.shape[-2], ga * a.shape[-1]
        N, K2 = b.shape
    else:
        K, M = a.shape
        K2, N = b.shape[-2], gb * b.shape[-1]
    assert K == K2, (name, a.shape, b.shape)
    tm, tn = _tile(M, tm), _tile(N // gb, tn)
    tk = K // ga if K // ga <= tk else _tile(K // ga, tk)
    nk = K // tk
    kpg, npg = K // ga // tk, N // gb // tn
    if mode == "nn":
        a_spec = pl.BlockSpec((tm, tk), lambda i, j, k: (i, k))
        b_spec = pl.BlockSpec((tk, tn), lambda i, j, k: (k, j))
        dims = (((1,), (0,)), ((), ()))
    elif mode == "nt":
        a_spec = (pl.BlockSpec((tm, tk), lambda i, j, k: (i, k)) if a.ndim == 2 else
                  pl.BlockSpec((None, tm, tk), lambda i, j, k: (k // kpg, i, k % kpg)))
        b_spec = pl.BlockSpec((tn, tk), lambda i, j, k: (j, k))
        dims = (((1,), (1,)), ((), ()))
    else:
        a_spec = pl.BlockSpec((tk, tm), lambda i, j, k: (k, i))
        b_spec = (pl.BlockSpec((tk, tn), lambda i, j, k: (k, j)) if b.ndim == 2 else
                  pl.BlockSpec((None, tk, tn), lambda i, j, k: (j // npg, k, j % npg)))
        dims = (((0,), (0,)), ((), ()))

    def body(a_ref, b_ref, o_ref, acc):
        part = lax.dot_general(a_ref[...], b_ref[...], dims, preferred_element_type=F32)
        if nk == 1:
            o_ref[...] = part.astype(o_ref.dtype)
        else:
            k = pl.program_id(2)

            @pl.when(k == 0)
            def _():
                acc[...] = part

            @pl.when(k > 0)
            def _():
                acc[...] += part

            @pl.when(k == nk - 1)
            def _():
                o_ref[...] = acc[...].astype(o_ref.dtype)

    grid = (M // tm, N // tn, nk)
    at = lambda step: lambda: ((pl.program_id(0) == step[0]) & (pl.program_id(1) == step[1])
                               & (pl.program_id(2) == step[2]))
    extra = rider.n if rider is not None else 0
    res = pl.pallas_call(
        _ride(body, 2, 1, rider, at((0, 0, 0)), at((grid[0] // 2, 0, 0)), at((grid[0] - 1, grid[1] - 1, nk - 1))),
        name=name, out_shape=[jax.ShapeDtypeStruct((M, N), out_dtype)] + (rider.out_shape if rider else []),
        grid=grid, in_specs=[a_spec, b_spec] + [ANY] * extra,
        out_specs=[pl.BlockSpec((tm, tn), lambda i, j, k: (i, j))] + [ANY] * extra,
        scratch_shapes=[pltpu.VMEM((tm, tn) if nk > 1 else (8, LANE), F32)] + (rider.scratch if rider else []),
        compiler_params=_params(dimension_semantics=("arbitrary",) * 3 if rider else ("parallel", "parallel", "arbitrary")),
    )(a, b, *(rider.arrays if rider else []))
    return (res[0], res[1:]) if rider else res[0]


def _rstd(xf):
    return lax.rsqrt(jnp.mean(xf * xf, axis=-1, keepdims=True) + EPS)


def _rms_bwd(xf, w, dy):
    r = _rstd(xf)
    g = dy * w
    dx = r * g - xf * (r * r * r) * jnp.mean(g * xf, axis=-1, keepdims=True)
    return dx, dy * xf * r


def _row_tile(rows):
    return _tile(rows, 512, 8)


def _prenorm(x, w, name):
    rows, d = x.shape
    ts = _row_tile(rows)

    def body(x_ref, w_ref, h_ref):
        xf = x_ref[...]
        h_ref[...] = (xf * _rstd(xf) * w_ref[...]).astype(BF16)

    return pl.pallas_call(
        body, name=name, out_shape=jax.ShapeDtypeStruct((rows, d), BF16), grid=(rows // ts,),
        in_specs=[pl.BlockSpec((ts, d), lambda i: (i, 0)), pl.BlockSpec((1, d), lambda i: (0, 0))],
        out_specs=pl.BlockSpec((ts, d), lambda i: (i, 0)), compiler_params=_params(),
    )(x, w)


def _post_pre(xres, y, w_post, w_pre, name):
    rows, d = xres.shape
    ts = _row_tile(rows)

    def body(x_ref, y_ref, wp_ref, wn_ref, xo_ref, h_ref):
        yf = y_ref[...]
        xn = x_ref[...] + yf * _rstd(yf) * wp_ref[...]
        xo_ref[...] = xn
        h_ref[...] = (xn * _rstd(xn) * wn_ref[...]).astype(BF16)

    row = pl.BlockSpec((ts, d), lambda i: (i, 0))
    vec = pl.BlockSpec((1, d), lambda i: (0, 0))
    return pl.pallas_call(
        body, name=name, grid=(rows // ts,),
        out_shape=(jax.ShapeDtypeStruct((rows, d), F32), jax.ShapeDtypeStruct((rows, d), BF16)),
        in_specs=[row, row, vec, vec], out_specs=(row, row), compiler_params=_params(),
    )(xres, y, w_post, w_pre)


def _post_loss(xres, y, w_post, target, name):
    rows, d = xres.shape
    ts = _row_tile(rows)
    n = rows // ts

    def body(x_ref, y_ref, wp_ref, t_ref, dx_ref, loss_ref, acc):
        i = pl.program_id(0)
        yf = y_ref[...]
        diff = x_ref[...] + yf * _rstd(yf) * wp_ref[...] - t_ref[...]
        dx_ref[...] = diff * (1.0 / d)

        @pl.when(i == 0)
        def _():
            acc[...] = jnp.zeros_like(acc)

        acc[...] += _colsum8(diff * diff)

        @pl.when(i == n - 1)
        def _():
            loss_ref[...] = jnp.broadcast_to((0.5 / d) * jnp.sum(acc[...]), loss_ref.shape)

    row = pl.BlockSpec((ts, d), lambda i: (i, 0))
    vec = pl.BlockSpec((1, d), lambda i: (0, 0))
    return pl.pallas_call(
        body, name=name, grid=(n,),
        out_shape=(jax.ShapeDtypeStruct((rows, d), F32), jax.ShapeDtypeStruct((1, LANE), F32)),
        in_specs=[row, row, vec, row], out_specs=(row, pl.BlockSpec((1, LANE), lambda i: (0, 0))),
        scratch_shapes=[pltpu.VMEM((8, d), F32)], compiler_params=_params(),
    )(xres, y, w_post, target)


def _post_bwd(y, w_post, dxn, name):
    rows, d = y.shape
    ts = _row_tile(rows)
    n = rows // ts

    def body(y_ref, w_ref, d_ref, dy_ref, dw_ref, acc):
        i = pl.program_id(0)
        dy, dwr = _rms_bwd(y_ref[...], w_ref[...], d_ref[...])
        dy_ref[...] = dy.astype(BF16)

        @pl.when(i == 0)
        def _():
            acc[...] = jnp.zeros_like(acc)

        acc[...] += _colsum8(dwr)

        @pl.when(i == n - 1)
        def _():
            dw_ref[...] = jnp.sum(acc[...], axis=0, keepdims=True)

    row = pl.BlockSpec((ts, d), lambda i: (i, 0))
    vec = pl.BlockSpec((1, d), lambda i: (0, 0))
    return pl.pallas_call(
        body, name=name, grid=(n,),
        out_shape=(jax.ShapeDtypeStruct((rows, d), BF16), jax.ShapeDtypeStruct((1, d), F32)),
        in_specs=[row, vec, row], out_specs=(row, vec),
        scratch_shapes=[pltpu.VMEM((8, d), F32)], compiler_params=_params(),
    )(y, w_post, dxn)


def _pre_bwd(x, w_pre, dh, dres, name):
    rows, d = x.shape
    ts = _row_tile(rows)
    n = rows // ts

    def body(x_ref, w_ref, dh_ref, dr_ref, dx_ref, dw_ref, acc):
        i = pl.program_id(0)
        dx, dwr = _rms_bwd(x_ref[...], w_ref[...], dh_ref[...].astype(F32))
        dx_ref[...] = dr_ref[...] + dx

        @pl.when(i == 0)
        def _():
            acc[...] = jnp.zeros_like(acc)

        acc[...] += _colsum8(dwr)

        @pl.when(i == n - 1)
        def _():
            dw_ref[...] = jnp.sum(acc[...], axis=0, keepdims=True)

    row = pl.BlockSpec((ts, d), lambda i: (i, 0))
    vec = pl.BlockSpec((1, d), lambda i: (0, 0))
    return pl.pallas_call(
        body, name=name, grid=(n,),
        out_shape=(jax.ShapeDtypeStruct((rows, d), F32), jax.ShapeDtypeStruct((1, d), F32)),
        in_specs=[row, vec, row, row], out_specs=(row, vec),
        scratch_shapes=[pltpu.VMEM((8, d), F32)], compiler_params=_params(),
    )(x, w_pre, dh, dres)


def _prev_halo_spec(ts, cw, col0=0):
    return pl.BlockSpec((HALO, cw), lambda i, j: (jnp.maximum(i * (ts // HALO) - 1, 0), j + col0))


def _fill_causal(ext, tile_f32, halo_f32, i):
    ext[pl.ds(0, HALO), :] = jnp.where(i > 0, halo_f32, 0.0)
    ext[pl.ds(HALO, tile_f32.shape[0]), :] = tile_f32


def _row_chunks(ts, chunk):
    assert ts % RC == 0, (ts, RC)

    def step(c, carry):
        chunk(pl.multiple_of(c * RC, RC))
        return carry

    lax.fori_loop(0, ts // RC, step, 0)


def _causal_taps(ext, r0, K):
    blk = ext[pl.ds(r0 + HALO - 8, RC + 8), :]
    return [blk[8 - j:8 - j + RC] for j in range(K)]


def _advanced_taps(ext, r0, K):
    blk = ext[pl.ds(r0, RC + 8), :]
    return [blk[j:j + RC] for j in range(K)]


def _filter(wv, taps):
    K = len(taps)
    acc = wv[K - 1:K, :] * taps[0]
    for t in range(K - 1):
        acc = acc + wv[t:t + 1, :] * taps[K - 1 - t]
    return acc


def _conv_t(dc, w, name, col_tile=1024):
    G, S, C = dc.shape
    K = w.shape[1]
    ts, cw = _tile(S, SEQ_TILE, HALO), _tile(C, col_tile)
    n = S // ts

    def body(d_ref, nx_ref, w_ref, o_ref, ext):
        i = pl.program_id(1)
        ext[pl.ds(0, ts), :] = d_ref[...].astype(F32)
        ext[pl.ds(ts, HALO), :] = jnp.where(i < n - 1, nx_ref[...].astype(F32), 0.0)
        wv = w_ref[...]

        def chunk(r0):
            taps = _advanced_taps(ext, r0, K)
            acc = wv[K - 1:K, :] * taps[0]
            for j in range(K - 1):
                acc = acc + wv[j:j + 1, :] * taps[K - 1 - j]
            o_ref[pl.ds(r0, RC), :] = acc.astype(o_ref.dtype)

        _row_chunks(ts, chunk)

    return pl.pallas_call(
        body, name=name, out_shape=jax.ShapeDtypeStruct((G, S, C), BF16), grid=(G, n, C // cw),
        in_specs=[pl.BlockSpec((None, ts, cw), lambda g, i, j: (g, i, j)),
                  pl.BlockSpec((None, HALO, cw),
                               lambda g, i, j: (g, jnp.minimum((i + 1) * (ts // HALO), S // HALO - 1), j)),
                  pl.BlockSpec((None, K, cw), lambda g, i, j: (g, 0, j))],
        out_specs=pl.BlockSpec((None, ts, cw), lambda g, i, j: (g, i, j)),
        scratch_shapes=[pltpu.VMEM((ts + HALO, cw), F32)], compiler_params=_params(),
    )(dc, dc, w)


def _qkv_conv(proj, conv_w, qkv_w, name):
    S = proj.shape[0]
    H3 = qkv_w // HD
    H = H3 // 3
    hb = 4 if H % 4 == 0 else 1
    cw = hb * HD
    ts = _tile(S, SEQ_TILE, HALO)
    per_kind = H // hb

    def body(x_ref, h_ref, w_ref, o_ref, ext):
        i, j = pl.program_id(0), pl.program_id(1)
        _fill_causal(ext, x_ref[...].astype(F32), h_ref[...].astype(F32), i)
        wv = w_ref[...]
        kind = j // per_kind
        scale = jnp.where(kind == 0, HD ** -0.5, 1.0)

        def chunk(r0):
            s = _silu(_filter(wv, _causal_taps(ext, r0, GDN_CONV)))
            for a in range(hb):
                sa = s[:, HD * a:HD * (a + 1)]
                r = lax.rsqrt(jnp.sum(sa * sa, axis=-1, keepdims=True) + EPS)
                o_ref[a, pl.ds(r0, RC), :] = jnp.where(kind == 2, sa, sa * r * scale)

        _row_chunks(ts, chunk)

    return pl.pallas_call(
        body, name=name, out_shape=jax.ShapeDtypeStruct((H3, S, HD), F32), grid=(S // ts, qkv_w // cw),
        in_specs=[pl.BlockSpec((ts, cw), lambda i, j: (i, j)), _prev_halo_spec(ts, cw),
                  pl.BlockSpec((GDN_CONV, cw), lambda i, j: (0, j))],
        out_specs=pl.BlockSpec((hb, ts, HD), lambda i, j: (j, i, 0)),
        scratch_shapes=[pltpu.VMEM((ts + HALO, cw), F32)], compiler_params=_params(),
    )(proj, proj, conv_w)


def _qkv_conv_bwd(proj, conv_w, dqkv_hm, qkv_w, name):
    S = proj.shape[0]
    H = qkv_w // HD // 3
    hb = 4 if H % 4 == 0 else 1
    cw = hb * HD
    ts = _tile(S, SEQ_TILE, HALO)
    n = S // ts
    per_kind = H // hb

    def body(x_ref, h_ref, w_ref, d_ref, dc_ref, dw_ref, ext, acc):
        j, i = pl.program_id(0), pl.program_id(1)
        _fill_causal(ext, x_ref[...].astype(F32), h_ref[...].astype(F32), i)
        wv = w_ref[...]
        kind = j // per_kind
        scale = jnp.where(kind == 0, HD ** -0.5, 1.0)

        @pl.when(i == 0)
        def _():
            acc[...] = jnp.zeros_like(acc)

        def chunk(r0):
            taps = _causal_taps(ext, r0, GDN_CONV)
            c = _filter(wv, taps)
            s = _silu(c)
            parts = []
            for a in range(hb):
                sa = s[:, HD * a:HD * (a + 1)]
                dy = d_ref[a, pl.ds(r0, RC), :]
                r = lax.rsqrt(jnp.sum(sa * sa, axis=-1, keepdims=True) + EPS)
                dn = scale * (r * dy - sa * (r * r * r) * jnp.sum(dy * sa, axis=-1, keepdims=True))
                parts.append(jnp.where(kind == 2, dy, dn))
            dc = jnp.concatenate(parts, axis=1) * _dsilu(c)
            dc_ref[pl.ds(r0, RC), :] = dc.astype(BF16)
            for t in range(GDN_CONV):
                acc[t] += _colsum8(dc * taps[GDN_CONV - 1 - t])

        _row_chunks(ts, chunk)

        @pl.when(i == n - 1)
        def _():
            dw_ref[...] = jnp.sum(acc[...], axis=1)

    return pl.pallas_call(
        body, name=name, grid=(qkv_w // cw, n),
        out_shape=(jax.ShapeDtypeStruct((S, qkv_w), BF16), jax.ShapeDtypeStruct((GDN_CONV, qkv_w), F32)),
        in_specs=[pl.BlockSpec((ts, cw), lambda j, i: (i, j)),
                  pl.BlockSpec((HALO, cw), lambda j, i: (jnp.maximum(i * (ts // HALO) - 1, 0), j)),
                  pl.BlockSpec((GDN_CONV, cw), lambda j, i: (0, j)),
                  pl.BlockSpec((hb, ts, HD), lambda j, i: (j, i, 0))],
        out_specs=(pl.BlockSpec((ts, cw), lambda j, i: (i, j)), pl.BlockSpec((GDN_CONV, cw), lambda j, i: (0, j))),
        scratch_shapes=[pltpu.VMEM((ts + HALO, cw), F32), pltpu.VMEM((GDN_CONV, 8, cw), F32)],
        compiler_params=_params(),
    )(proj, proj, conv_w, dqkv_hm)


def _chunk_cumsum(x):
    row = lax.broadcasted_iota(jnp.int32, x.shape, 0) & (CHUNK - 1)
    s = 1
    while s < CHUNK:
        x = x + jnp.where(row >= s, pltpu.roll(x, s, axis=0), 0.0)
        s *= 2
    return x


def _chunk_rev_cumsum(x):
    rows = x.shape[0]
    row = lax.broadcasted_iota(jnp.int32, x.shape, 0) & (CHUNK - 1)
    s = 1
    while s < CHUNK:
        x = x + jnp.where(row < CHUNK - s, pltpu.roll(x, rows - s, axis=0), 0.0)
        s *= 2
    return x


def _softplus(x):
    return jnp.maximum(x, 0.0) + jnp.log1p(jnp.exp(-jnp.abs(x)))


def _gates(proj, ba_col, a_log_l, dt_bias_l, H, name):
    S = proj.shape[0]
    ts = _tile(S, 512, CHUNK)

    def body(x_ref, al_ref, dt_ref, o_ref):
        x = x_ref[...].astype(F32)
        lane = lax.broadcasted_iota(jnp.int32, x.shape, 1)
        g = -jnp.exp(al_ref[...]) * _softplus(x + dt_ref[...])
        G = _chunk_cumsum(jnp.where((lane >= H) & (lane < 2 * H), g, 0.0))
        o_ref[...] = jnp.where(lane < H, _sigmoid(x), G)

    return pl.pallas_call(
        body, name=name, out_shape=jax.ShapeDtypeStruct((S, LANE), F32), grid=(S // ts,),
        in_specs=[pl.BlockSpec((ts, LANE), lambda i: (i, ba_col)), pl.BlockSpec((1, LANE), lambda i: (0, 0)),
                  pl.BlockSpec((1, LANE), lambda i: (0, 0))],
        out_specs=pl.BlockSpec((ts, LANE), lambda i: (i, 0)), compiler_params=_params(),
    )(proj, a_log_l, dt_bias_l)


def _gates_bwd(proj, ba_col, ba_w, a_log_l, dt_bias_l, dbg, H, name):
    S = proj.shape[0]
    ts = _tile(S, 512, CHUNK)
    n = S // ts

    def body(x_ref, al_ref, dt_ref, d_ref, o_ref, dal_ref, ddt_ref, acc):
        i = pl.program_id(0)
        x = x_ref[...].astype(F32)
        d = d_ref[...]
        lane = lax.broadcasted_iota(jnp.int32, x.shape, 1)
        is_a = (lane >= H) & (lane < 2 * H)
        beta = _sigmoid(x)
        nea = -jnp.exp(al_ref[...])
        z = x + dt_ref[...]
        dg = _chunk_rev_cumsum(jnp.where(is_a, d, 0.0))
        da_raw = dg * nea * _sigmoid(z)
        o = jnp.where(lane < H, d * beta * (1.0 - beta), jnp.where(is_a, da_raw, 0.0))
        if ba_w > LANE:
            o = jnp.concatenate([o, jnp.zeros((ts, ba_w - LANE), F32)], axis=1)
        o_ref[...] = o.astype(BF16)

        @pl.when(i == 0)
        def _():
            acc[...] = jnp.zeros_like(acc)

        acc[0] += _colsum8(jnp.where(is_a, dg * nea * _softplus(z), 0.0))
        acc[1] += _colsum8(jnp.where(is_a, da_raw, 0.0))

        @pl.when(i == n - 1)
        def _():
            dal_ref[...] = jnp.sum(acc[0], axis=0, keepdims=True)
            ddt_ref[...] = jnp.sum(acc[1], axis=0, keepdims=True)

    vec = pl.BlockSpec((1, LANE), lambda i: (0, 0))
    return pl.pallas_call(
        body, name=name, grid=(n,),
        out_shape=(jax.ShapeDtypeStruct((S, ba_w), BF16), jax.ShapeDtypeStruct((1, LANE), F32),
                   jax.ShapeDtypeStruct((1, LANE), F32)),
        in_specs=[pl.BlockSpec((ts, LANE), lambda i: (i, ba_col)), vec, vec, pl.BlockSpec((ts, LANE), lambda i: (i, 0))],
        out_specs=(pl.BlockSpec((ts, ba_w), lambda i: (i, 0)), vec, vec),
        scratch_shapes=[pltpu.VMEM((2, 8, LANE), F32)], compiler_params=_params(),
    )(proj, a_log_l, dt_bias_l, dbg)


_BMM_FORMS = {"nn": "hik,hkj->hij", "nt": "hik,hjk->hij", "tn": "hki,hkj->hij"}


def _split_bf16(a):
    hi = a.astype(BF16)
    return hi, (a - hi.astype(F32)).astype(BF16)


def _bmm(a, b, form="nn", exact=False):
    e = lambda x, y: jnp.einsum(_BMM_FORMS[form], x, y, preferred_element_type=F32)
    if not exact:
        return e(a.astype(BF16), b.astype(BF16))
    (ah, al), (bh, bl) = _split_bf16(a), _split_bf16(b)
    return (e(ah, bl) + e(al, bh)) + e(ah, bh)


def _unit_lower_inverse(L, r, c):
    eye = (r == c).astype(F32)
    m = jnp.where((r >> 3) == (c >> 3), -L, 0.0)
    m2 = _bmm(m, m, exact=True)
    m4 = _bmm(m2, m2, exact=True)
    x = eye + m
    x = x + _bmm(x, m2, exact=True)
    x = x + _bmm(x, m4, exact=True)
    for sh in (3, 4, 5):
        off = ((r >> (sh + 1)) == (c >> (sh + 1))) & ((r >> sh) != (c >> sh))
        x = x - _bmm(x, _bmm(jnp.where(off, L, 0.0), x))
    return x


def _to_row(col, eye):
    return jnp.sum(jnp.where(eye, jnp.broadcast_to(col, eye.shape), 0.0), axis=1, keepdims=True)


def _to_col(rowv, eye):
    return jnp.sum(jnp.where(eye, jnp.broadcast_to(rowv, eye.shape), 0.0), axis=2, keepdims=True)


def _gdn_chunk(q, k, v, bg, H):
    shape = (H, CHUNK, CHUNK)
    r = lax.broadcasted_iota(jnp.int32, shape, 1)
    c = lax.broadcasted_iota(jnp.int32, shape, 2)
    eye, incl, strict = r == c, r >= c, r > c
    beta = jnp.stack([bg[:, h:h + 1] for h in range(H)], axis=0)
    G = jnp.stack([bg[:, H + h:H + h + 1] for h in range(H)], axis=0)
    gap = jnp.broadcast_to(G, shape) - _to_row(G, eye)
    decay = jnp.where(incl, jnp.exp(jnp.where(incl, gap, 0.0)), 0.0)
    kk = _bmm(k, k, "nt")
    L = jnp.where(strict, beta * decay * kk, 0.0)
    ainv = _unit_lower_inverse(L, r, c)
    eG = jnp.exp(G)
    u_v = _bmm(ainv, beta * v)
    w_k = _bmm(ainv, (beta * eG) * k)
    qk = _bmm(q, k, "nt", exact=True)
    GL = G[:, CHUNK - 1:CHUNK, :]
    ek = jnp.exp(GL - G)
    return dict(eye=eye, strict=strict, r=r, c=c, beta=beta, G=G, decay=decay, kk=kk, ainv=ainv, eG=eG,
                u_v=u_v, w_k=w_k, qk=qk, attn=decay * qk, GL=GL, ek=ek, cd=jnp.exp(GL))


def _chunk_steps(N):
    at = lambda step: lambda: pl.program_id(0) == step
    return at(0), at(N - max(N // 8, 1)), at(N - 1)


def _gdn_fwd(qkv_hm, bg, H, name, rider=None):
    S = qkv_hm.shape[1]
    N = S // CHUNK
    extra = rider.n if rider is not None else 0

    def body(q_ref, k_ref, v_ref, bg_ref, o_ref, st_ref, state):
        n = pl.program_id(0)

        @pl.when(n == 0)
        def _():
            state[...] = jnp.zeros_like(state)

        q, k, v = q_ref[...], k_ref[...], v_ref[...]
        t = _gdn_chunk(q, k, v, bg_ref[...], H)
        s0 = state[...]
        st_ref[0] = s0
        u = t["u_v"] - _bmm(t["w_k"], s0)
        o_ref[...] = _bmm(q * t["eG"], s0) + _bmm(t["attn"], u)
        state[...] = t["cd"] * s0 + _bmm(k * t["ek"], u, "tn")

    blk = lambda kind: pl.BlockSpec((H, CHUNK, HD), lambda n: (kind, n, 0))
    res = pl.pallas_call(
        _ride(body, 4, 2, rider, *_chunk_steps(N)), name=name, grid=(N,),
        out_shape=[jax.ShapeDtypeStruct((H, S, HD), F32), jax.ShapeDtypeStruct((N, H, HD, HD), F32)]
        + (rider.out_shape if rider else []),
        in_specs=[blk(0), blk(1), blk(2), pl.BlockSpec((CHUNK, LANE), lambda n: (n, 0))] + [ANY] * extra,
        out_specs=[pl.BlockSpec((H, CHUNK, HD), lambda n: (0, n, 0)),
                   pl.BlockSpec((1, H, HD, HD), lambda n: (n, 0, 0, 0))] + [ANY] * extra,
        scratch_shapes=[pltpu.VMEM((H, HD, HD), F32)] + (rider.scratch if rider else []), compiler_params=_params(),
    )(qkv_hm, qkv_hm, qkv_hm, bg, *(rider.arrays if rider else []))
    return res[0], res[1], res[2:]


def _gdn_bwd(qkv_hm, bg, states, do_hm, H, name, rider=None):
    S = qkv_hm.shape[1]
    N = S // CHUNK
    extra = rider.n if rider is not None else 0

    def body(q_ref, k_ref, v_ref, bg_ref, st_ref, do_ref, dqkv_ref, dbg_ref, dstate):
        n = pl.program_id(0)

        @pl.when(n == 0)
        def _():
            dstate[...] = jnp.zeros_like(dstate)

        q, k, v, do = q_ref[...], k_ref[...], v_ref[...], do_ref[...]
        t = _gdn_chunk(q, k, v, bg_ref[...], H)
        eye, beta, eG, decay, kk, ainv = t["eye"], t["beta"], t["eG"], t["decay"], t["kk"], t["ainv"]
        s0 = st_ref[0]
        ds1 = dstate[...]
        u = t["u_v"] - _bmm(t["w_k"], s0)
        qdec, kdec = q * eG, k * t["ek"]
        d_qdec = _bmm(do, s0, "nt")
        d_attn = _bmm(do, u, "nt")
        du = _bmm(t["attn"], do, "tn") + _bmm(kdec, ds1)
        d_cd = jnp.sum(jnp.sum(ds1 * s0, axis=2, keepdims=True), axis=1, keepdims=True)
        d_kdec = _bmm(u, ds1, "nt")
        d_wk = -_bmm(du, s0, "nt")
        dstate[...] = t["cd"] * ds1 + _bmm(qdec, do, "tn") - _bmm(t["w_k"], du, "tn")
        d_rv = _bmm(ainv, du, "tn")
        d_rk = _bmm(ainv, d_wk, "tn")
        dL = jnp.where(t["strict"], -(_bmm(d_rv, t["u_v"], "nt") + _bmm(d_rk, t["w_k"], "nt")), 0.0)
        rk_k = jnp.sum(d_rk * k, axis=2, keepdims=True)
        d_beta = (jnp.sum(dL * decay * kk, axis=2, keepdims=True) + jnp.sum(d_rv * v, axis=2, keepdims=True)
                  + rk_k * eG)
        d_decay = dL * beta * kk + d_attn * t["qk"]
        d_kk = dL * beta * decay
        d_qk = d_attn * decay
        dqkv_ref[pl.ds(2 * H, H)] = beta * d_rv
        dqkv_ref[pl.ds(0, H)] = _bmm(d_qk, k) + d_qdec * eG
        dqkv_ref[pl.ds(H, H)] = ((beta * eG) * d_rk + _bmm(d_kk, k) + _bmm(d_kk, k, "tn") + _bmm(d_qk, q, "tn")
                       + d_kdec * t["ek"])
        d_eG = rk_k * beta + jnp.sum(d_qdec * q, axis=2, keepdims=True)
        e = jnp.sum(d_kdec * kdec, axis=2, keepdims=True)
        T = d_decay * decay
        dG = d_eG * eG - e + jnp.sum(T, axis=2, keepdims=True) - _to_col(jnp.sum(T, axis=1, keepdims=True), eye)
        dGL = jnp.sum(e, axis=1, keepdims=True) + d_cd * t["cd"]
        row1 = lax.broadcasted_iota(jnp.int32, (H, CHUNK, 1), 1)
        dG = dG + jnp.where(row1 == CHUNK - 1, dGL, 0.0)
        lane = lax.broadcasted_iota(jnp.int32, (CHUNK, LANE), 1)
        out = jnp.zeros((CHUNK, LANE), F32)
        for h in range(H):
            out = out + jnp.where(lane == h, d_beta[h], 0.0) + jnp.where(lane == H + h, dG[h], 0.0)
        dbg_ref[...] = out

    blk = lambda kind: pl.BlockSpec((H, CHUNK, HD), lambda n: (kind, N - 1 - n, 0))
    res = pl.pallas_call(
        _ride(body, 6, 2, rider, *_chunk_steps(N)), name=name, grid=(N,),
        out_shape=[jax.ShapeDtypeStruct((3 * H, S, HD), F32), jax.ShapeDtypeStruct((S, LANE), F32)]
        + (rider.out_shape if rider else []),
        in_specs=[blk(0), blk(1), blk(2), pl.BlockSpec((CHUNK, LANE), lambda n: (N - 1 - n, 0)),
                  pl.BlockSpec((1, H, HD, HD), lambda n: (N - 1 - n, 0, 0, 0)), blk(0)] + [ANY] * extra,
        out_specs=[pl.BlockSpec((3 * H, CHUNK, HD), lambda n: (0, N - 1 - n, 0)),
                   pl.BlockSpec((CHUNK, LANE), lambda n: (N - 1 - n, 0))] + [ANY] * extra,
        scratch_shapes=[pltpu.VMEM((H, HD, HD), F32)] + (rider.scratch if rider else []), compiler_params=_params(),
    )(qkv_hm, qkv_hm, qkv_hm, bg, states, do_hm, *(rider.arrays if rider else []))
    return res[0], res[1], res[2:]


def _gdn_out(o_hm, proj, z_col, gdn_w, name):
    H, S, _ = o_hm.shape
    vw = H * HD
    ts = _tile(S, 512, 8)

    def body(o_ref, z_ref, w_ref, y_ref):
        z = z_ref[...].astype(F32)
        w = w_ref[...]
        parts = []
        for h in range(H):
            o = o_ref[h]
            parts.append(o * _rstd(o) * w)
        y_ref[...] = (jnp.concatenate(parts, axis=1) * _silu(z)).astype(BF16)

    return pl.pallas_call(
        body, name=name, out_shape=jax.ShapeDtypeStruct((S, vw), BF16), grid=(S // ts,),
        in_specs=[pl.BlockSpec((H, ts, HD), lambda i: (0, i, 0)), pl.BlockSpec((ts, vw), lambda i: (i, z_col)),
                  pl.BlockSpec((1, HD), lambda i: (0, 0))],
        out_specs=pl.BlockSpec((ts, vw), lambda i: (i, 0)), compiler_params=_params(),
    )(o_hm, proj, gdn_w)


def _gdn_out_bwd(o_hm, proj, z_col, gdn_w, dy, name):
    H, S, _ = o_hm.shape
    vw = H * HD
    ts = _tile(S, 512, 8)
    n = S // ts

    def body(o_ref, z_ref, w_ref, dy_ref, do_ref, dz_ref, dw_ref, acc):
        i = pl.program_id(0)
        z = z_ref[...].astype(F32)
        dy = dy_ref[...].astype(F32)
        w = w_ref[...]
        gz = dy * _silu(z)
        normed, dwr = [], jnp.zeros((ts, HD), F32)
        for h in range(H):
            o = o_ref[h]
            dxo, dwh = _rms_bwd(o, w, gz[:, HD * h:HD * (h + 1)])
            do_ref[h] = dxo
            dwr = dwr + dwh
            normed.append(o * _rstd(o) * w)
        dz_ref[...] = (dy * jnp.concatenate(normed, axis=1) * _dsilu(z)).astype(BF16)

        @pl.when(i == 0)
        def _():
            acc[...] = jnp.zeros_like(acc)

        acc[...] += _colsum8(dwr)

        @pl.when(i == n - 1)
        def _():
            dw_ref[...] = jnp.sum(acc[...], axis=0, keepdims=True)

    return pl.pallas_call(
        body, name=name, grid=(n,),
        out_shape=(jax.ShapeDtypeStruct((H, S, HD), F32), jax.ShapeDtypeStruct((S, vw), BF16),
                   jax.ShapeDtypeStruct((1, HD), F32)),
        in_specs=[pl.BlockSpec((H, ts, HD), lambda i: (0, i, 0)), pl.BlockSpec((ts, vw), lambda i: (i, z_col)),
                  pl.BlockSpec((1, HD), lambda i: (0, 0)), pl.BlockSpec((ts, vw), lambda i: (i, 0))],
        out_specs=(pl.BlockSpec((H, ts, HD), lambda i: (0, i, 0)), pl.BlockSpec((ts, vw), lambda i: (i, 0)),
                   pl.BlockSpec((1, HD), lambda i: (0, 0))),
        scratch_shapes=[pltpu.VMEM((8, HD), F32)], compiler_params=_params(),
    )(o_hm, proj, gdn_w, dy)


def _pool_trailing(ext, ts, pg, row0):
    outs, inv_cnts = [], []
    t_abs = row0 + lax.broadcasted_iota(jnp.int32, (ts, 1), 0)
    for gi, win in enumerate(POOL_WINDOWS):
        cols = pl.ds(gi * pg, pg)
        cur = ext[pl.ds(HALO, ts), cols]
        acc = cur
        for j in range(1, win):
            acc = acc + ext[pl.ds(HALO - j, ts), cols]
        inv = 1.0 / jnp.minimum(t_abs + 1, win).astype(F32)
        outs.append(acc * inv - cur)
    return outs


def _pool_fwd(proj, p_col, pool_w, pool_scale, pw, name):
    S = proj.shape[0]
    pg = pw // len(POOL_WINDOWS)
    ts = _tile(S, 512, HALO)

    def body(x_ref, h_ref, w_ref, sc_ref, o_ref, ext):
        i = pl.program_id(0)
        _fill_causal(ext, x_ref[...].astype(F32), h_ref[...].astype(F32), i)
        ys = _pool_trailing(ext, ts, pg, i * ts)
        outs = [jnp.dot(ys[gi].astype(BF16), w_ref[gi], preferred_element_type=F32) for gi in range(len(ys))]
        o_ref[...] = (jnp.concatenate(outs, axis=1) * sc_ref[...]).astype(BF16)

    return pl.pallas_call(
        body, name=name, out_shape=jax.ShapeDtypeStruct((S, pw), BF16), grid=(S // ts,),
        in_specs=[pl.BlockSpec((ts, pw), lambda i: (i, p_col)),
                  pl.BlockSpec((HALO, pw), lambda i: (jnp.maximum(i * (ts // HALO) - 1, 0), p_col)),
                  pl.BlockSpec((len(POOL_WINDOWS), pg, pg), lambda i: (0, 0, 0)), pl.BlockSpec((1, pw), lambda i: (0, 0))],
        out_specs=pl.BlockSpec((ts, pw), lambda i: (i, 0)),
        scratch_shapes=[pltpu.VMEM((ts + HALO, pw), F32)], compiler_params=_params(),
    )(proj, proj, pool_w, pool_scale)


def _pool_bwd(proj, p_col, pool_w, pool_scale, dpb, pw, name):
    S = proj.shape[0]
    G = len(POOL_WINDOWS)
    pg = pw // G
    ts = _tile(S, 512, HALO)
    n = S // ts

    def body(x_ref, h_ref, w_ref, sc_ref, d_ref, dn_ref, dp_ref, dw_ref, dsc_ref, ext, zext, wacc, sacc):
        i = pl.program_id(0)
        _fill_causal(ext, x_ref[...].astype(F32), h_ref[...].astype(F32), i)
        ys = _pool_trailing(ext, ts, pg, i * ts)
        d_ext = jnp.concatenate([d_ref[...].astype(F32), jnp.where(i < n - 1, dn_ref[...].astype(F32), 0.0)], axis=0)
        dt = d_ext * sc_ref[...]
        t_abs = i * ts + lax.broadcasted_iota(jnp.int32, (ts + HALO, 1), 0)

        @pl.when(i == 0)
        def _():
            wacc[...] = jnp.zeros_like(wacc)
            sacc[...] = jnp.zeros_like(sacc)

        dps, tfs = [], []
        for gi, win in enumerate(POOL_WINDOWS):
            cols = slice(gi * pg, (gi + 1) * pg)
            w = w_ref[gi]
            dt_g = dt[:, cols].astype(BF16)
            y_g = ys[gi].astype(BF16)
            tfs.append(jnp.dot(y_g, w, preferred_element_type=F32))
            wacc[gi] += lax.dot_general(y_g, dt_g[:ts], (((0,), (0,)), ((), ())), preferred_element_type=F32)
            dyp = lax.dot_general(dt_g, w, (((1,), (1,)), ((), ())), preferred_element_type=F32)
            zext[:, pl.ds(gi * pg, pg)] = dyp * (1.0 / jnp.minimum(t_abs + 1, win).astype(F32))
            acc = -dyp[:ts]
            for j in range(win):
                acc = acc + zext[pl.ds(j, ts), pl.ds(gi * pg, pg)]
            dps.append(acc)
        dp_ref[...] = jnp.concatenate(dps, axis=1).astype(BF16)
        sacc[...] += _colsum8(d_ext[:ts] * jnp.concatenate(tfs, axis=1))

        @pl.when(i == n - 1)
        def _():
            dw_ref[...] = wacc[...]
            dsc_ref[...] = jnp.sum(sacc[...], axis=0, keepdims=True)

    return pl.pallas_call(
        body, name=name, grid=(n,),
        out_shape=(jax.ShapeDtypeStruct((S, pw), BF16), jax.ShapeDtypeStruct((G, pg, pg), F32),
                   jax.ShapeDtypeStruct((1, pw), F32)),
        in_specs=[pl.BlockSpec((ts, pw), lambda i: (i, p_col)),
                  pl.BlockSpec((HALO, pw), lambda i: (jnp.maximum(i * (ts // HALO) - 1, 0), p_col)),
                  pl.BlockSpec((G, pg, pg), lambda i: (0, 0, 0)), pl.BlockSpec((1, pw), lambda i: (0, 0)),
                  pl.BlockSpec((ts, pw), lambda i: (i, 0)),
                  pl.BlockSpec((HALO, pw), lambda i: (jnp.minimum((i + 1) * (ts // HALO), S // HALO - 1), 0))],
        out_specs=(pl.BlockSpec((ts, pw), lambda i: (i, 0)), pl.BlockSpec((G, pg, pg), lambda i: (0, 0, 0)),
                   pl.BlockSpec((1, pw), lambda i: (0, 0))),
        scratch_shapes=[pltpu.VMEM((ts + HALO, pw), F32), pltpu.VMEM((ts + HALO, pw), F32),
                        pltpu.VMEM((G, pg, pg), F32), pltpu.VMEM((8, pw), F32)],
        compiler_params=_params(),
    )(proj, proj, pool_w, pool_scale, dpb, dpb)


def _merge(proj, ga_col, gb_col, ya, yb, name):
    S, d = ya.shape
    ts = _tile(S, 512, 16)

    def body(ga_ref, gb_ref, ya_ref, yb_ref, o_ref):
        o_ref[...] = (_sigmoid(ga_ref[...].astype(F32)) * ya_ref[...].astype(F32)
                      + _sigmoid(gb_ref[...].astype(F32)) * yb_ref[...].astype(F32)).astype(BF16)

    row = pl.BlockSpec((ts, d), lambda i: (i, 0))
    return pl.pallas_call(
        body, name=name, out_shape=jax.ShapeDtypeStruct((S, d), BF16), grid=(S // ts,),
        in_specs=[pl.BlockSpec((ts, d), lambda i: (i, ga_col)), pl.BlockSpec((ts, d), lambda i: (i, gb_col)), row, row],
        out_specs=row, compiler_params=_params(),
    )(proj, proj, ya, yb)


def _merge_bwd(proj, ga_col, gb_col, ya, yb, dm, name):
    S, d = ya.shape
    ts = _tile(S, 512, 16)

    def body(ga_ref, gb_ref, ya_ref, yb_ref, dm_ref, dya_ref, dyb_ref, dga_ref, dgb_ref):
        dmv = dm_ref[...].astype(F32)
        sa, sb = _sigmoid(ga_ref[...].astype(F32)), _sigmoid(gb_ref[...].astype(F32))
        dya_ref[...] = (dmv * sa).astype(BF16)
        dyb_ref[...] = (dmv * sb).astype(BF16)
        dga_ref[...] = (dmv * ya_ref[...].astype(F32) * sa * (1.0 - sa)).astype(BF16)
        dgb_ref[...] = (dmv * yb_ref[...].astype(F32) * sb * (1.0 - sb)).astype(BF16)

    row = pl.BlockSpec((ts, d), lambda i: (i, 0))
    o = jax.ShapeDtypeStruct((S, d), BF16)
    return pl.pallas_call(
        body, name=name, out_shape=(o, o, o, o), grid=(S // ts,),
        in_specs=[pl.BlockSpec((ts, d), lambda i: (i, ga_col)), pl.BlockSpec((ts, d), lambda i: (i, gb_col)), row, row, row],
        out_specs=(row, row, row, row), compiler_params=_params(),
    )(proj, proj, ya, yb, dm)


def _xattn_fwd(q, kv, name):
    S, d = q.shape
    M = kv.shape[0]
    hd = d // XA_HEADS
    ts = _tile(S, 512, 16)
    scale = hd ** -0.5

    def body(q_ref, k_ref, v_ref, o_ref):
        s = lax.dot_general(q_ref[...], k_ref[...], (((1,), (1,)), ((), ())), preferred_element_type=F32) * scale
        p = jnp.exp(s - jnp.max(s, axis=-1, keepdims=True))
        p = p / jnp.sum(p, axis=-1, keepdims=True)
        o_ref[...] = jnp.dot(p.astype(BF16), v_ref[...], preferred_element_type=F32).astype(BF16)

    return pl.pallas_call(
        body, name=name, out_shape=jax.ShapeDtypeStruct((S, d), BF16), grid=(S // ts, XA_HEADS),
        in_specs=[pl.BlockSpec((ts, hd), lambda i, h: (i, h)), pl.BlockSpec((M, hd), lambda i, h: (0, h)),
                  pl.BlockSpec((M, hd), lambda i, h: (0, XA_HEADS + h))],
        out_specs=pl.BlockSpec((ts, hd), lambda i, h: (i, h)), compiler_params=_params(),
    )(q, kv, kv)


def _xattn_bwd(q, kv, do, name):
    S, d = q.shape
    M = kv.shape[0]
    hd = d // XA_HEADS
    ts = _tile(S, 512, 16)
    n = S // ts
    scale = hd ** -0.5

    def body(q_ref, k_ref, v_ref, do_ref, dq_ref, dk_ref, dv_ref, kacc, vacc):
        i = pl.program_id(1)
        qv, kv_, vv, dov = q_ref[...], k_ref[...], v_ref[...], do_ref[...]
        s = lax.dot_general(qv, kv_, (((1,), (1,)), ((), ())), preferred_element_type=F32) * scale
        p = jnp.exp(s - jnp.max(s, axis=-1, keepdims=True))
        p = p / jnp.sum(p, axis=-1, keepdims=True)
        dp = lax.dot_general(dov, vv, (((1,), (1,)), ((), ())), preferred_element_type=F32)
        ds = (p * (dp - jnp.sum(p * dp, axis=-1, keepdims=True)) * scale).astype(BF16)
        dq_ref[...] = jnp.dot(ds, kv_, preferred_element_type=F32).astype(BF16)

        @pl.when(i == 0)
        def _():
            kacc[...] = jnp.zeros_like(kacc)
            vacc[...] = jnp.zeros_like(vacc)

        kacc[...] += lax.dot_general(ds, qv, (((0,), (0,)), ((), ())), preferred_element_type=F32)
        vacc[...] += lax.dot_general(p.astype(BF16), dov, (((0,), (0,)), ((), ())), preferred_element_type=F32)

        @pl.when(i == n - 1)
        def _():
            dk_ref[...] = kacc[...]
            dv_ref[...] = vacc[...]

    dq, dk, dv = pl.pallas_call(
        body, name=name, grid=(XA_HEADS, n),
        out_shape=(jax.ShapeDtypeStruct((S, d), BF16), jax.ShapeDtypeStruct((M, d), F32), jax.ShapeDtypeStruct((M, d), F32)),
        in_specs=[pl.BlockSpec((ts, hd), lambda h, i: (i, h)), pl.BlockSpec((M, hd), lambda h, i: (0, h)),
                  pl.BlockSpec((M, hd), lambda h, i: (0, XA_HEADS + h)), pl.BlockSpec((ts, hd), lambda h, i: (i, h))],
        out_specs=(pl.BlockSpec((ts, hd), lambda h, i: (i, h)), pl.BlockSpec((M, hd), lambda h, i: (0, h)),
                   pl.BlockSpec((M, hd), lambda h, i: (0, h))),
        scratch_shapes=[pltpu.VMEM((M, hd), F32), pltpu.VMEM((M, hd), F32)], compiler_params=_params(),
    )(q, kv, kv, do)
    return dq, jnp.concatenate([dk, dv], axis=1)


def _ffn_act(up, conv_w, bias, name):
    S, F2 = up.shape
    F = F2 // 2
    ts, cw = _tile(S, SEQ_TILE, HALO), _tile(F, 512)
    nb = F // cw

    def body(a_ref, ah_ref, b_ref, bh_ref, wa_ref, wb_ref, ba_ref, bb_ref, o_ref, ea, eb):
        i = pl.program_id(0)
        _fill_causal(ea, a_ref[...].astype(F32), ah_ref[...].astype(F32), i)
        _fill_causal(eb, b_ref[...].astype(F32), bh_ref[...].astype(F32), i)
        wa, wb, bia, bib = wa_ref[...], wb_ref[...], ba_ref[...], bb_ref[...]

        def chunk(r0):
            ua = _filter(wa, _causal_taps(ea, r0, FFN_CONV)) + bia
            ub = _filter(wb, _causal_taps(eb, r0, FFN_CONV)) + bib
            o_ref[pl.ds(r0, RC), :] = (_silu(ua) * ub).astype(BF16)

        _row_chunks(ts, chunk)

    tile = lambda c0: pl.BlockSpec((ts, cw), lambda i, j: (i, j + c0))
    vec = lambda rows, c0: pl.BlockSpec((rows, cw), lambda i, j: (0, j + c0))
    return pl.pallas_call(
        body, name=name, out_shape=jax.ShapeDtypeStruct((S, F), BF16), grid=(S // ts, nb),
        in_specs=[tile(0), _prev_halo_spec(ts, cw), tile(nb), _prev_halo_spec(ts, cw, nb),
                  vec(FFN_CONV, 0), vec(FFN_CONV, nb), vec(1, 0), vec(1, nb)],
        out_specs=pl.BlockSpec((ts, cw), lambda i, j: (i, j)),
        scratch_shapes=[pltpu.VMEM((ts + HALO, cw), F32), pltpu.VMEM((ts + HALO, cw), F32)],
        compiler_params=_params(),
    )(up, up, up, up, conv_w, conv_w, bias, bias)


def _ffn_act_bwd(up, conv_w, bias, dact, name):
    S, F2 = up.shape
    F = F2 // 2
    ts, cw = _tile(S, SEQ_TILE, HALO), _tile(F, 512)
    nb = F // cw
    n = S // ts

    def body(a_ref, ah_ref, b_ref, bh_ref, wa_ref, wb_ref, ba_ref, bb_ref, d_ref,
             du_ref, dwa_ref, dwb_ref, dba_ref, dbb_ref, ea, eb, wacc, bacc):
        i = pl.program_id(1)
        _fill_causal(ea, a_ref[...].astype(F32), ah_ref[...].astype(F32), i)
        _fill_causal(eb, b_ref[...].astype(F32), bh_ref[...].astype(F32), i)
        wa, wb, bia, bib = wa_ref[...], wb_ref[...], ba_ref[...], bb_ref[...]

        @pl.when(i == 0)
        def _():
            wacc[...] = jnp.zeros_like(wacc)
            bacc[...] = jnp.zeros_like(bacc)

        def chunk(r0):
            ta, tb = _causal_taps(ea, r0, FFN_CONV), _causal_taps(eb, r0, FFN_CONV)
            ua, ub = _filter(wa, ta) + bia, _filter(wb, tb) + bib
            d = d_ref[pl.ds(r0, RC), :].astype(F32)
            dua = d * ub * _dsilu(ua)
            dub = d * _silu(ua)
            du_ref[0, pl.ds(r0, RC), :] = dua.astype(BF16)
            du_ref[1, pl.ds(r0, RC), :] = dub.astype(BF16)
            for t in range(FFN_CONV):
                wacc[0, t] += _colsum8(dua * ta[FFN_CONV - 1 - t])
                wacc[1, t] += _colsum8(dub * tb[FFN_CONV - 1 - t])
            bacc[0] += _colsum8(dua)
            bacc[1] += _colsum8(dub)

        _row_chunks(ts, chunk)

        @pl.when(i == n - 1)
        def _():
            dwa_ref[...] = jnp.sum(wacc[0], axis=1)
            dwb_ref[...] = jnp.sum(wacc[1], axis=1)
            dba_ref[...] = jnp.sum(bacc[0], axis=0, keepdims=True)
            dbb_ref[...] = jnp.sum(bacc[1], axis=0, keepdims=True)

    tile = lambda c0: pl.BlockSpec((ts, cw), lambda j, i: (i, j + c0))
    halo = lambda c0: pl.BlockSpec((HALO, cw), lambda j, i: (jnp.maximum(i * (ts // HALO) - 1, 0), j + c0))
    vec = lambda rows, c0: pl.BlockSpec((rows, cw), lambda j, i: (0, j + c0))
    du, dwa, dwb, dba, dbb = pl.pallas_call(
        body, name=name, grid=(nb, n),
        out_shape=(jax.ShapeDtypeStruct((2, S, F), BF16),
                   jax.ShapeDtypeStruct((FFN_CONV, F), F32), jax.ShapeDtypeStruct((FFN_CONV, F), F32),
                   jax.ShapeDtypeStruct((1, F), F32), jax.ShapeDtypeStruct((1, F), F32)),
        in_specs=[tile(0), halo(0), tile(nb), halo(nb), vec(FFN_CONV, 0), vec(FFN_CONV, nb), vec(1, 0), vec(1, nb), tile(0)],
        out_specs=(pl.BlockSpec((2, ts, cw), lambda j, i: (0, i, j)), vec(FFN_CONV, 0), vec(FFN_CONV, 0), vec(1, 0),
                   vec(1, 0)),
        scratch_shapes=[pltpu.VMEM((ts + HALO, cw), F32), pltpu.VMEM((ts + HALO, cw), F32),
                        pltpu.VMEM((2, FFN_CONV, 8, cw), F32), pltpu.VMEM((2, 8, cw), F32)],
        compiler_params=_params(),
    )(up, up, up, up, conv_w, conv_w, bias, bias, dact)
    return du, jnp.concatenate([dwa, dwb], axis=1), jnp.concatenate([dba, dbb], axis=1)


def _adamw(gparts, w, m, v, name):
    R, C = w.shape
    tr = _tile(R, max(16, (256 * 1024) // C), 16)

    def body(g_ref, w_ref, m_ref, v_ref, go_ref, d_ref, mo_ref, vo_ref):
        g = g_ref[0].astype(F32)
        for s in range(1, N_DEV):
            g = g + g_ref[s].astype(F32)
        mn = ADAM_B1 * m_ref[...] + (1.0 - ADAM_B1) * g
        vn = ADAM_B2 * v_ref[...] + (1.0 - ADAM_B2) * (g * g)
        m_hat = mn / (1.0 - ADAM_B1 ** ADAM_STEP)
        v_hat = vn / (1.0 - ADAM_B2 ** ADAM_STEP)
        go_ref[...] = g
        d_ref[...] = -ADAM_LR * (m_hat / (jnp.sqrt(v_hat) + ADAM_EPS) + ADAM_WD * w_ref[...])
        mo_ref[...] = mn
        vo_ref[...] = vn

    row = pl.BlockSpec((tr, C), lambda i: (i, 0))
    o = jax.ShapeDtypeStruct((R, C), F32)
    return pl.pallas_call(
        body, name=name, out_shape=(o, o, o, o), grid=(R // tr,),
        in_specs=[pl.BlockSpec((N_DEV, tr, C), lambda i: (0, i, 0)), row, row, row],
        out_specs=(row, row, row, row), compiler_params=_params(),
    )(gparts, w, m, v)


def _position():
    return lax.axis_index("x"), lax.axis_index("y"), lax.axis_index("c")


class _Copies:
    def __init__(self, arrays):
        self.arrays, self.n = list(arrays), len(arrays)
        self.scratch = [pltpu.SemaphoreType.DMA((7 * self.n,)), pltpu.SemaphoreType.DMA((7 * self.n,)),
                        pltpu.SemaphoreType.DMA((self.n,))]


class _Gather(_Copies):
    def __init__(self, arrays):
        super().__init__(arrays)
        self.out_shape = [jax.ShapeDtypeStruct((N_DEV,) + b.shape, b.dtype) for b in self.arrays]

    def phases(self, x_refs, out_refs, send_sems, recv_sems, local_sems):
        n = self.n
        x, y, c = _position()
        me, sibling = (x, y, c), (x, y, 1 - c)
        chips = [(1 - x, y), (x, 1 - y), (1 - x, 1 - y)]

        def copy(a, k, blk, to, own=False):
            slot = out_refs[a].at[4 * blk[0] + 2 * blk[1] + blk[2]]
            return pltpu.make_async_remote_copy(
                src_ref=x_refs[a] if own else slot, dst_ref=slot,
                send_sem=send_sems.at[7 * a + k], recv_sem=recv_sems.at[7 * a + k], device_id=to, device_id_type=MESH)

        mine = [pltpu.make_async_copy(x_refs[a], out_refs[a].at[4 * x + 2 * y + c], local_sems.at[a]) for a in range(n)]
        first = [copy(a, 0, me, sibling, own=True) for a in range(n)]
        first += [copy(a, 1 + j, me, (*chip, c), own=True) for a in range(n) for j, chip in enumerate(chips)]
        passed = [copy(a, 4 + j, (*chip, c), sibling) for j, chip in enumerate(chips) for a in range(n)]

        def start():
            for cp in mine + first:
                cp.start()

        def hand_on():
            for j, chip in enumerate(chips):
                for a in range(n):
                    copy(a, 1 + j, (*chip, c), me).wait_recv()
                    passed[j * n + a].start()

        def finish():
            for a in range(n):
                copy(a, 0, sibling, me).wait_recv()
            for j, chip in enumerate(chips):
                for a in range(n):
                    copy(a, 4 + j, (*chip, 1 - c), me).wait_recv()
            for cp in first + passed:
                cp.wait_send()
            for cp in mine:
                cp.wait()

        return start, hand_on, finish


class _Exchange(_Copies):
    def __init__(self, arrays):
        super().__init__(arrays)
        self.out_shape = [jax.ShapeDtypeStruct(p.shape, p.dtype) for p in self.arrays]

    def phases(self, p_refs, out_refs, send_sems, recv_sems, local_sems):
        n = self.n
        x, y, c = _position()
        my_slot = 4 * x + 2 * y + c
        mine = [pltpu.make_async_copy(p_refs[a].at[my_slot], out_refs[a].at[my_slot], local_sems.at[a]) for a in range(n)]
        copies = []
        for k in range(1, N_DEV):
            px, py, pc = x ^ (k >> 2), y ^ ((k >> 1) & 1), c ^ (k & 1)
            for a in range(n):
                copies.append(pltpu.make_async_remote_copy(
                    src_ref=p_refs[a].at[4 * px + 2 * py + pc], dst_ref=out_refs[a].at[my_slot],
                    send_sem=send_sems.at[7 * a + k - 1], recv_sem=recv_sems.at[7 * a + k - 1],
                    device_id=(px, py, pc), device_id_type=MESH))

        def start():
            for cp in mine + copies:
                cp.start()

        def finish():
            for cp in copies + mine:
                cp.wait()

        return start, None, finish


def _communicate(copies, name):
    n = copies.n

    def body(*refs):
        start, hand_on, finish = copies.phases(refs[:n], refs[n:2 * n], *refs[2 * n:])
        start()
        if hand_on is not None:
            hand_on()
        finish()

    return pl.pallas_call(body, name=name, out_shape=copies.out_shape, in_specs=[ANY] * n, out_specs=[ANY] * n,
                          scratch_shapes=copies.scratch)(*copies.arrays)


def _col_pieces(col_map, shard_w):
    pieces = []
    for lo, hi, dst in col_map:
        c = lo
        while c < hi:
            j = c // shard_w
            end = min(hi, (j + 1) * shard_w)
            pieces.append((j, c - j * shard_w, end - c, dst + (c - lo)))
            c = end
    return pieces


def _assemble_cols(shards, pieces, width, name):
    _, R, Cs = shards.shape
    tr = _tile(R, 128, 16)

    def body(s_ref, o_ref):
        o_ref[...] = jnp.zeros(o_ref.shape, o_ref.dtype)
        for j, lo, n, dst in pieces:
            o_ref[:, dst:dst + n] = s_ref[j, :, lo:lo + n]

    return pl.pallas_call(
        body, name=name, out_shape=jax.ShapeDtypeStruct((R, width), shards.dtype), grid=(R // tr,),
        in_specs=[pl.BlockSpec((N_DEV, tr, Cs), lambda i: (0, i, 0))],
        out_specs=pl.BlockSpec((tr, width), lambda i: (i, 0)), compiler_params=_params(),
    )(shards)


def _split_cols(full, pieces, shard_w, name):
    R, width = full.shape
    tr = _tile(R, 128, 16)

    def body(f_ref, o_ref):
        for j, lo, n, dst in pieces:
            o_ref[j, :, lo:lo + n] = f_ref[:, dst:dst + n]

    return pl.pallas_call(
        body, name=name, out_shape=jax.ShapeDtypeStruct((N_DEV, R, shard_w), full.dtype), grid=(R // tr,),
        in_specs=[pl.BlockSpec((tr, width), lambda i: (i, 0))],
        out_specs=pl.BlockSpec((N_DEV, tr, shard_w), lambda i: (0, i, 0)), compiler_params=_params(),
    )(full)


def _pack(arrays, row_multiple=8):
    flat, layout, off = [], [], 0
    for a in arrays:
        n = a.size
        padded = -(-n // LANE) * LANE
        f = a.reshape(-1).astype(F32)
        if padded != n:
            f = jnp.pad(f, (0, padded - n))
        flat.append(f)
        layout.append((off, n, a.shape))
        off += padded
    total = -(-off // (LANE * row_multiple)) * (LANE * row_multiple)
    if total != off:
        flat.append(jnp.zeros((total - off,), F32))
    return jnp.concatenate(flat).reshape(total // LANE, LANE), layout


def _unpack(buf, layout):
    flat = buf.reshape(-1)
    return [flat[off:off + n].reshape(shape) for off, n, shape in layout]


def _cols_to_full(g):
    return jnp.transpose(g, (1, 0, 2)).reshape(g.shape[1], N_DEV * g.shape[2])


def _full_to_cols(a):
    return jnp.transpose(a.reshape(a.shape[0], N_DEV, a.shape[1] // N_DEV), (1, 0, 2))


def _rows_to_full(g):
    return g.reshape(N_DEV * g.shape[1], g.shape[2])


def _full_to_rows(a):
    return a.reshape(N_DEV, a.shape[0] // N_DEV, a.shape[1])


def _pad_cols(a, width):
    return a if a.shape[-1] == width else jnp.pad(a, [(0, 0)] * (a.ndim - 1) + [(0, width - a.shape[-1])])


SHARDED = ("w_in", "conv_qkv", "pool_w", "w_branch_a", "w_branch_b", "w_mix_out", "w_xq", "w_xkv", "w_xo", "w_up",
           "ffn_conv_w", "w_down")
REPLICATED = ("mix_pre_norm", "a_log", "dt_bias", "gdn_norm", "pool_scale", "mix_post_norm", "xa_pre_norm", "mem_norm",
              "xa_post_norm", "ffn_pre_norm", "ffn_conv_b", "ffn_post_norm")
WEIGHTS = ("mix_pre_norm", "w_in", "conv_qkv", "a_log", "dt_bias", "gdn_norm", "pool_w", "pool_scale", "w_branch_a",
           "w_branch_b", "w_mix_out", "mix_post_norm", "xa_pre_norm", "mem_norm", "w_xq", "w_xkv", "w_xo", "xa_post_norm",
           "ffn_pre_norm", "w_up", "ffn_conv_w", "ffn_conv_b", "w_down", "ffn_post_norm")
MATMUL_WEIGHTS = ("w_in", "w_branch_a", "w_branch_b", "w_mix_out", "w_xq", "w_xkv", "w_xo", "w_up", "w_down")
COL_SHARDED = ("w_in", "w_branch_b", "w_xkv", "w_up", "conv_qkv", "ffn_conv_w")
ROW_SHARDED = ("w_branch_a", "w_mix_out", "w_xq", "w_xo", "w_down")


class _Layout:
    def __init__(self, D, H, pw, F):
        self.D, self.H, self.pw, self.F = D, H, pw, F
        self.qkv_w, self.vw = 3 * H * HD, H * HD
        self.ba_w = 512 if D >= 2048 else LANE
        self.Fp = -(-F // 512) * 512 if F >= 512 else F
        q, vw = self.qkv_w, self.vw
        self.seg = dict(qkv=(0, q), z=(q, vw), ga=(q + vw, D), gb=(q + vw + D, D), p=(q + vw + 2 * D, pw),
                        ba=(q + vw + 2 * D + pw, self.ba_w))
        self.in_w = q + vw + 2 * D + pw + self.ba_w
        o_z, o_b = q, q + vw
        o_p = o_b + 2 * H
        o_ga = o_p + pw
        o_gb = o_ga + D
        self.d_in = o_gb + D
        self.in_map = [(0, o_z, self.seg["qkv"][0]), (o_z, o_b, self.seg["z"][0]), (o_b, o_p, self.seg["ba"][0]),
                       (o_p, o_ga, self.seg["p"][0]), (o_ga, o_gb, self.seg["ga"][0]), (o_gb, self.d_in, self.seg["gb"][0])]
        self.up_map = [(0, F, 0), (F, 2 * F, self.Fp)]

    def col(self, name, width):
        return self.seg[name][0] // width


def _local_step(x, mem, target, P, L, comm=None):
    D, H, pw, F, Fp = L.D, L.H, L.pw, L.F, L.Fp
    qkv_w, vw, ba_w = L.qkv_w, L.vw, L.ba_w
    col = L.col
    P = dict(P)
    win_p, cw3_p, fb_p = P["win_p"], P["cw3_p"], P["fb_p"]
    conv_qkv, pool_w = P["conv_qkv"], P["pool_w"]
    lanes = lambda vec: jnp.pad(vec.reshape(1, H).astype(F32), ((0, 0), (H, LANE - 2 * H)))
    a_log_l, dt_bias_l = lanes(P["a_log"]), lanes(P["dt_bias"])
    bf = lambda name: P[name]
    vecf = lambda name: P[name].reshape(1, -1).astype(F32)
    g = {}

    def carried(call, name, *args, **kw):
        if comm is not None and name in comm.GATHERS:
            *out, got = call(*args, name, rider=comm.gather(name), **kw)
            P.update(comm.weights_from(name, got))
        elif comm is not None and name in comm.EXCHANGES:
            *out, got = call(*args, name, rider=comm.exchange(name, g), **kw)
            comm.receive(name, got)
        else:
            out = call(*args, name, **kw)
            out = [out] if call is _matmul else list(out[:-1])
        return out[0] if len(out) == 1 else out

    h1 = _prenorm(x, vecf("mix_pre_norm"), "mix_prenorm")
    proj = carried(_matmul, "in_proj", h1, win_p, "nn", BF16, tn=1536)
    qkv_hm = _qkv_conv(proj, conv_qkv, qkv_w, "qkv_conv")
    bg = _gates(proj, col("ba", LANE), a_log_l, dt_bias_l, H, "gates")
    o_hm, states = carried(_gdn_fwd, "gdn_fwd", qkv_hm, bg, H)
    wup_p, wdown_p = P["wup_p"], P["wdown_p"]
    oa = _gdn_out(o_hm, proj, col("z", vw), vecf("gdn_norm"), "gdn_out")
    ya = _matmul(oa, bf("w_branch_a"), "nn", BF16, "branch_a")
    pb = _pool_fwd(proj, col("p", pw), pool_w, vecf("pool_scale"), pw, "pool_fwd")
    yb = _matmul(pb, bf("w_branch_b"), "nn", BF16, "branch_b")
    merged = _merge(proj, col("ga", D), col("gb", D), ya, yb, "merge")
    y1 = _matmul(merged, bf("w_mix_out"), "nn", F32, "mix_out")
    x1, h2 = _post_pre(x, y1, vecf("mix_post_norm"), vecf("xa_pre_norm"), "mix_post")
    mn = _prenorm(mem, vecf("mem_norm"), "mem_norm")
    qx = _matmul(h2, bf("w_xq"), "nn", BF16, "xq")
    kv = _matmul(mn, bf("w_xkv"), "nn", BF16, "xkv")
    ox = _xattn_fwd(qx, kv, "xattn_fwd")
    y2 = _matmul(ox, bf("w_xo"), "nn", F32, "xo")
    x2, h3 = _post_pre(x1, y2, vecf("xa_post_norm"), vecf("ffn_pre_norm"), "xa_post")
    up = _matmul(h3, wup_p, "nn", BF16, "ffn_up")
    act = _ffn_act(up, cw3_p, fb_p, "ffn_act")
    y3 = _matmul(act, wdown_p, "nn", F32, "ffn_down")
    dx3, loss = _post_loss(x2, y3, vecf("ffn_post_norm"), target, "ffn_post_loss")

    dy3, g["ffn_post_norm"] = _post_bwd(y3, vecf("ffn_post_norm"), dx3, "ffn_post_bwd")
    dact = _matmul(dy3, wdown_p, "nt", BF16, "ffn_down_dx")
    g["w_down_p"] = _matmul(act, dy3, "tn", BF16, "ffn_down_dw", tk=4096)
    du, g["ffn_conv_w_p"], g["ffn_conv_b_p"] = _ffn_act_bwd(up, cw3_p, fb_p, dact, "ffn_act_bwd")
    dup = _conv_t(du, jnp.stack([cw3_p[:, :Fp], cw3_p[:, Fp:]]), "ffn_conv_t")
    dh3 = carried(_matmul, "ffn_up_dx", dup, wup_p, "nt", BF16)
    g["w_up_p"] = _matmul(h3, dup, "tn", BF16, "ffn_up_dw", tn=512, tk=4096)
    dx2, g["ffn_pre_norm"] = _pre_bwd(x2, vecf("ffn_pre_norm"), dh3, dx3, "ffn_pre_bwd")
    dy2, g["xa_post_norm"] = _post_bwd(y2, vecf("xa_post_norm"), dx2, "xa_post_bwd")
    dox = _matmul(dy2, bf("w_xo"), "nt", BF16, "xo_dx")
    g["w_xo"] = _matmul(ox, dy2, "tn", BF16, "xo_dw", tn=512, tk=4096)
    dqx, dkv = _xattn_bwd(qx, kv, dox, "xattn_bwd")
    dkv_b = dkv.astype(BF16)
    dh2 = _matmul(dqx, bf("w_xq"), "nt", BF16, "xq_dx")
    g["w_xq"] = _matmul(h2, dqx, "tn", BF16, "xq_dw", tn=512, tk=4096)
    dmn = _matmul(dkv_b, bf("w_xkv"), "nt", F32, "xkv_dx")
    g["w_xkv"] = _matmul(mn, dkv_b, "tn", BF16, "xkv_dw")
    _, g["mem_norm"] = _pre_bwd(mem, vecf("mem_norm"), dmn, jnp.zeros_like(mem), "mem_norm_bwd")
    dx1, g["xa_pre_norm"] = _pre_bwd(x1, vecf("xa_pre_norm"), dh2, dx2, "xa_pre_bwd")
    dy1, g["mix_post_norm"] = _post_bwd(y1, vecf("mix_post_norm"), dx1, "mix_post_bwd")
    dmerged = _matmul(dy1, bf("w_mix_out"), "nt", BF16, "mix_out_dx")
    g["w_mix_out"] = _matmul(merged, dy1, "tn", BF16, "mix_out_dw", tn=512, tk=4096)
    dya, dyb, dga, dgb = _merge_bwd(proj, col("ga", D), col("gb", D), ya, yb, dmerged, "merge_bwd")
    doa = _matmul(dya, bf("w_branch_a"), "nt", BF16, "branch_a_dx")
    g["w_branch_a"] = _matmul(oa, dya, "tn", BF16, "branch_a_dw", tn=512, tk=4096)
    dpb = _matmul(dyb, bf("w_branch_b"), "nt", BF16, "branch_b_dx")
    g["w_branch_b"] = _matmul(pb, dyb, "tn", BF16, "branch_b_dw")
    dp, g["pool_w"], g["pool_scale"] = _pool_bwd(proj, col("p", pw), pool_w, vecf("pool_scale"), dpb, pw, "pool_bwd")
    do_hm, dz, g["gdn_norm"] = _gdn_out_bwd(o_hm, proj, col("z", vw), vecf("gdn_norm"), doa, "gdn_out_bwd")
    dqkv_hm, dbg = carried(_gdn_bwd, "gdn_bwd", qkv_hm, bg, states, do_hm, H)
    dba, dal, ddt = _gates_bwd(proj, col("ba", LANE), ba_w, a_log_l, dt_bias_l, dbg, H, "gates_bwd")
    g["a_log"], g["dt_bias"] = dal[:, H:2 * H], ddt[:, H:2 * H]
    dc, g["conv_qkv"] = _qkv_conv_bwd(proj, conv_qkv, dqkv_hm, qkv_w, "qkv_conv_bwd")
    dqkv = _conv_t(dc[None], conv_qkv[None], "qkv_conv_t")[0]
    dproj = jnp.concatenate([dqkv, dz, dga, dgb, dp, dba], axis=1)
    g["w_in_p"] = carried(_matmul, "in_proj_dw", h1, dproj, "tn", BF16, tn=768, tk=4096)
    dh1 = carried(_matmul, "in_proj_dx", dproj, win_p, "nt", BF16, tk=4608)
    grad_x, g["mix_pre_norm"] = _pre_bwd(x, vecf("mix_pre_norm"), dh1, dx1, "mix_pre_bwd")
    return loss, grad_x, g


def _two_halves(a, F, Fp):
    return jnp.concatenate([_pad_cols(a[..., :F], Fp), _pad_cols(a[..., F:], Fp)], axis=-1)


def _from_halves(a, F, Fp):
    return jnp.concatenate([a[..., :F], a[..., Fp:Fp + F]], axis=-1)


class _StepComm:
    FIRST = ("w_in", "conv_qkv", "pool_w", "ffn_conv_w")
    GATHERS = {"in_proj": ("w_branch_a", "w_branch_b", "w_mix_out", "w_xq", "w_xo"),
               "gdn_fwd": ("w_xkv", "w_up", "w_down")}
    EXCHANGES = {"ffn_up_dx": ("w_down", "ffn_conv_w"),
                 "gdn_bwd": ("pool_w", "w_branch_a", "w_branch_b", "w_mix_out", "w_xq", "w_xkv", "w_xo"),
                 "in_proj_dw": ("w_up",),
                 "in_proj_dx": ("w_in", "conv_qkv")}

    def __init__(self, w, L):
        self.w, self.L = w, L
        self.in_pieces = _col_pieces(L.in_map, w["w_in"].shape[1])
        self.up_pieces = _col_pieces(L.up_map, w["w_up"].shape[1])
        self.received = {}

    def _shard(self, n):
        return self.w[n].astype(BF16) if n in MATMUL_WEIGHTS else self.w[n]

    def first_weights(self):
        L, (g, r, c) = self.L, self.w["pool_w"].shape
        G = dict(zip(self.FIRST, _communicate(_Gather([self._shard(n) for n in self.FIRST]), "gather_first")))
        return {"win_p": _assemble_cols(G["w_in"], self.in_pieces, L.in_w, "assemble_w_in"),
                "conv_qkv": _cols_to_full(G["conv_qkv"]),
                "cw3_p": _two_halves(_cols_to_full(G["ffn_conv_w"]), L.F, L.Fp),
                "pool_w": jnp.transpose(G["pool_w"], (1, 0, 2, 3)).reshape(g, N_DEV * r, c).astype(BF16)}

    def gather(self, call):
        return _Gather([self._shard(n) for n in self.GATHERS[call]])

    def weights_from(self, call, results):
        L, P = self.L, {}
        for n, shards in zip(self.GATHERS[call], results):
            if n == "w_up":
                P["wup_p"] = _assemble_cols(shards, self.up_pieces, 2 * L.Fp, "assemble_w_up")
            elif n == "w_down":
                P["wdown_p"] = jnp.pad(_rows_to_full(shards), ((0, L.Fp - L.F), (0, 0)))
            else:
                P[n] = _rows_to_full(shards) if n in ROW_SHARDED else _cols_to_full(shards)
        return P

    def _slices(self, g, n):
        L, w = self.L, self.w
        if n == "w_in":
            return _split_cols(g["w_in_p"], self.in_pieces, w["w_in"].shape[1], "split_w_in")
        if n == "w_up":
            return _split_cols(g["w_up_p"], self.up_pieces, w["w_up"].shape[1], "split_w_up")
        if n == "w_down":
            return _full_to_rows(g["w_down_p"][:L.F])
        if n == "ffn_conv_w":
            return _full_to_cols(_from_halves(g["ffn_conv_w_p"], L.F, L.Fp))
        if n == "pool_w":
            grp, r, c = w["pool_w"].shape
            return jnp.transpose(g[n].reshape(grp, N_DEV, r, c), (1, 0, 2, 3)).reshape(N_DEV, grp * r, c)
        return _full_to_rows(g[n]) if n in ROW_SHARDED else _full_to_cols(g[n])

    def exchange(self, call, g):
        return _Exchange([self._slices(g, n) for n in self.EXCHANGES[call]])

    def receive(self, call, results):
        self.received.update(zip(self.EXCHANGES[call], results))


def kernel(x, mem, mix_pre_norm, w_in, conv_qkv, a_log, dt_bias, gdn_norm, pool_w, pool_scale, w_branch_a, w_branch_b, w_mix_out, mix_post_norm, xa_pre_norm, mem_norm, w_xq, w_xkv, w_xo, xa_post_norm, ffn_pre_norm, w_up, ffn_conv_w, ffn_conv_b, w_down, ffn_post_norm, loss_target, m_mix_pre_norm, m_w_in, m_conv_qkv, m_a_log, m_dt_bias, m_gdn_norm, m_pool_w, m_pool_scale, m_w_branch_a, m_w_branch_b, m_w_mix_out, m_mix_post_norm, m_xa_pre_norm, m_mem_norm, m_w_xq, m_w_xkv, m_w_xo, m_xa_post_norm, m_ffn_pre_norm, m_w_up, m_ffn_conv_w, m_ffn_conv_b, m_w_down, m_ffn_post_norm, v_mix_pre_norm, v_w_in, v_conv_qkv, v_a_log, v_dt_bias, v_gdn_norm, v_pool_w, v_pool_scale, v_w_branch_a, v_w_branch_b, v_w_mix_out, v_mix_post_norm, v_xa_pre_norm, v_mem_norm, v_w_xq, v_w_xkv, v_w_xo, v_xa_post_norm, v_ffn_pre_norm, v_w_up, v_ffn_conv_w, v_ffn_conv_b, v_w_down, v_ffn_post_norm):
    given = dict(locals())
    w = {n: given[n][0] for n in WEIGHTS}
    m = {n: given["m_" + n][0] for n in WEIGHTS}
    v = {n: given["v_" + n][0] for n in WEIGHTS}
    D = x.shape[-1]
    F = w["w_down"].shape[0] * N_DEV
    L = _Layout(D, w["a_log"].shape[-1], w["pool_scale"].shape[-1], F)
    Fp = L.Fp

    comm = _StepComm(w, L)
    P = {n: w[n] for n in REPLICATED}
    P.update(comm.first_weights())
    P["fb_p"] = _two_halves(w["ffn_conv_b"].reshape(1, 2 * F), F, Fp)
    loss, grad_x, g = _local_step(x[0], mem[0], loss_target[0], P, L, comm)

    received = comm.received
    outs = {}
    for n in SHARDED:
        as2d = lambda a: a.reshape(-1, a.shape[-1])
        res = _adamw(received[n], as2d(w[n]), as2d(m[n]), as2d(v[n]), "adamw_" + n)
        outs[n] = [r.reshape(w[n].shape) for r in res]

    g["ffn_conv_b"] = _from_halves(g["ffn_conv_b_p"], F, Fp)
    rep_parts, rep_layout = _pack([g[n].reshape(w[n].shape) for n in REPLICATED] + [loss])
    rep_all, = _communicate(_Gather([rep_parts]), "gather_small_grads")
    zero_loss = jnp.zeros_like(loss)
    wr, _ = _pack([w[n] for n in REPLICATED] + [zero_loss])
    mr, _ = _pack([m[n] for n in REPLICATED] + [zero_loss])
    vr, _ = _pack([v[n] for n in REPLICATED] + [zero_loss])
    outs_rep = [_unpack(o, rep_layout) for o in _adamw(rep_all, wr, mr, vr, "adamw_replicated")]
    loss_total = outs_rep[0][-1][0, 0]
    for i, n in enumerate(REPLICATED):
        outs[n] = [outs_rep[k][i] for k in range(4)]

    result = [loss_total, grad_x[None]]
    for k in range(4):
        for n in WEIGHTS:
            result.append(outs[n][k][None])
    return tuple(result)
```

```python
import functools

import jax
import jax.numpy as jnp
from jax import lax
from jax.experimental import pallas as pl
from jax.experimental.pallas import tpu as pltpu

F32, BF16 = jnp.float32, jnp.bfloat16
MESH = pl.DeviceIdType.MESH
ANY = pl.BlockSpec(memory_space=pl.ANY)

N_DEV = 8
EPS = 1e-6
CHUNK = 64
HD = 128
GDN_CONV = 4
FFN_CONV = 3
POOL_WINDOWS = (2, 4, 8, 16)
XA_HEADS = 4
HALO = 16
RC = 128
SEQ_TILE = 1024
LANE = 128
VMEM_LIMIT = 48 * 1024 * 1024

ADAM_LR, ADAM_B1, ADAM_B2, ADAM_EPS, ADAM_WD, ADAM_STEP = 0.001, 0.9, 0.999, 1e-08, 0.01, 10


def _tile(n, pref, align=LANE):
    best = None
    t = align
    while t <= min(n, pref):
        if n % t == 0:
            best = t
        t += align
    return best if best is not None else n


def _params(**kw):
    return pltpu.CompilerParams(vmem_limit_bytes=VMEM_LIMIT, **kw)


def _sigmoid(x):
    return 1.0 / (1.0 + jnp.exp(-x))


def _silu(x):
    return x * _sigmoid(x)


def _dsilu(x):
    s = _sigmoid(x)
    return s * (1.0 + x * (1.0 - s))


def _colsum8(t):
    return t.reshape(t.shape[0] // 8, 8, t.shape[1]).sum(axis=0)


def _ride(body, n_in, n_out, rider, first, middle, last):
    if rider is None:
        return body
    n = rider.n

    def wrapped(*refs):
        ins, r_in = refs[:n_in], refs[n_in:n_in + n]
        outs, r_out = refs[n_in + n:n_in + n + n_out], refs[n_in + n + n_out:n_in + 2 * n + n_out]
        rest = refs[n_in + 2 * n + n_out:]
        start, hand_on, finish = rider.phases(r_in, r_out, *rest[-3:])
        pl.when(first())(start)
        body(*ins, *outs, *rest[:-3])
        if hand_on is not None:
            pl.when(middle())(hand_on)
        pl.when(last())(finish)

    return wrapped


def _matmul(a, b, mode, out_dtype, name, tm=1024, tn=1024, tk=2816, rider=None):
    ga = a.shape[0] if (mode == "nt" and a.ndim == 3) else 1
    gb = b.shape[0] if (mode == "tn" and b.ndim == 3) else 1
    if mode == "nn":
        (M, K), (K2, N) = a.shape, b.shape
    elif mode == "nt":
        M, K = a.shape[-2], ga * a.shape[-1]
        N, K2 = b.shape
    else:
        K, M = a.shape
        K2, N = b.shape[-2], gb * b.shape[-1]
    assert K == K2, (name, a.shape, b.shape)
    tm, tn = _tile(M, tm), _tile(N // gb, tn)
    tk = K // ga if K // ga <= tk else _tile(K // ga, tk)
    nk = K // tk
    kpg, npg = K // ga // tk, N // gb // tn
    if mode == "nn":
        a_spec = pl.BlockSpec((tm, tk), lambda i, j, k: (i, k))
        b_spec = pl.BlockSpec((tk, tn), lambda i, j, k: (k, j))
        dims = (((1,), (0,)), ((), ()))
    elif mode == "nt":
        a_spec = (pl.BlockSpec((tm, tk), lambda i, j, k: (i, k)) if a.ndim == 2 else
                  pl.BlockSpec((None, tm, tk), lambda i, j, k: (k // kpg, i, k % kpg)))
        b_spec = pl.BlockSpec((tn, tk), lambda i, j, k: (j, k))
        dims = (((1,), (1,)), ((), ()))
    else:
        a_spec = pl.BlockSpec((tk, tm), lambda i, j, k: (k, i))
        b_spec = (pl.BlockSpec((tk, tn), lambda i, j, k: (k, j)) if b.ndim == 2 else
                  pl.BlockSpec((None, tk, tn), lambda i, j, k: (j // npg, k, j % npg)))
        dims = (((0,), (0,)), ((), ()))

    def body(a_ref, b_ref, o_ref, acc):
        part = lax.dot_general(a_ref[...], b_ref[...], dims, preferred_element_type=F32)
        if nk == 1:
            o_ref[...] = part.astype(o_ref.dtype)
        else:
            k = pl.program_id(2)

            @pl.when(k == 0)
            def _():
                acc[...] = part

            @pl.when(k > 0)
            def _():
                acc[...] += part

            @pl.when(k == nk - 1)
            def _():
                o_ref[...] = acc[...].astype(o_ref.dtype)

    grid = (M // tm, N // tn, nk)
    at = lambda step: lambda: ((pl.program_id(0) == step[0]) & (pl.program_id(1) == step[1])
                               & (pl.program_id(2) == step[2]))
    extra = rider.n if rider is not None else 0
    res = pl.pallas_call(
        _ride(body, 2, 1, rider, at((0, 0, 0)), at((grid[0] // 2, 0, 0)), at((grid[0] - 1, grid[1] - 1, nk - 1))),
        name=name, out_shape=[jax.ShapeDtypeStruct((M, N), out_dtype)] + (rider.out_shape if rider else []),
        grid=grid, in_specs=[a_spec, b_spec] + [ANY] * extra,
        out_specs=[pl.BlockSpec((tm, tn), lambda i, j, k: (i, j))] + [ANY] * extra,
        scratch_shapes=[pltpu.VMEM((tm, tn) if nk > 1 else (8, LANE), F32)] + (rider.scratch if rider else []),
        compiler_params=_params(dimension_semantics=("arbitrary",) * 3 if rider else ("parallel", "parallel", "arbitrary")),
    )(a, b, *(rider.arrays if rider else []))
    return (res[0], res[1:]) if rider else res[0]


def _rstd(xf):
    return lax.rsqrt(jnp.mean(xf * xf, axis=-1, keepdims=True) + EPS)


def _rms_bwd(xf, w, dy):
    r = _rstd(xf)
    g = dy * w
    dx = r * g - xf * (r * r * r) * jnp.mean(g * xf, axis=-1, keepdims=True)
    return dx, dy * xf * r


def _row_tile(rows):
    return _tile(rows, 512, 8)


def _prenorm(x, w, name):
    rows, d = x.shape
    ts = _row_tile(rows)

    def body(x_ref, w_ref, h_ref):
        xf = x_ref[...]
        h_ref[...] = (xf * _rstd(xf) * w_ref[...]).astype(BF16)

    return pl.pallas_call(
        body, name=name, out_shape=jax.ShapeDtypeStruct((rows, d), BF16), grid=(rows // ts,),
        in_specs=[pl.BlockSpec((ts, d), lambda i: (i, 0)), pl.BlockSpec((1, d), lambda i: (0, 0))],
        out_specs=pl.BlockSpec((ts, d), lambda i: (i, 0)), compiler_params=_params(),
    )(x, w)


def _post_pre(xres, y, w_post, w_pre, name):
    rows, d = xres.shape
    ts = _row_tile(rows)

    def body(x_ref, y_ref, wp_ref, wn_ref, xo_ref, h_ref):
        yf = y_ref[...]
        xn = x_ref[...] + yf * _rstd(yf) * wp_ref[...]
        xo_ref[...] = xn
        h_ref[...] = (xn * _rstd(xn) * wn_ref[...]).astype(BF16)

    row = pl.BlockSpec((ts, d), lambda i: (i, 0))
    vec = pl.BlockSpec((1, d), lambda i: (0, 0))
    return pl.pallas_call(
        body, name=name, grid=(rows // ts,),
        out_shape=(jax.ShapeDtypeStruct((rows, d), F32), jax.ShapeDtypeStruct((rows, d), BF16)),
        in_specs=[row, row, vec, vec], out_specs=(row, row), compiler_params=_params(),
    )(xres, y, w_post, w_pre)


def _post_loss(xres, y, w_post, target, name):
    rows, d = xres.shape
    ts = _row_tile(rows)
    n = rows // ts

    def body(x_ref, y_ref, wp_ref, t_ref, dx_ref, loss_ref, acc):
        i = pl.program_id(0)
        yf = y_ref[...]
        diff = x_ref[...] + yf * _rstd(yf) * wp_ref[...] - t_ref[...]
        dx_ref[...] = diff * (1.0 / d)

        @pl.when(i == 0)
        def _():
            acc[...] = jnp.zeros_like(acc)

        acc[...] += _colsum8(diff * diff)

        @pl.when(i == n - 1)
        def _():
            loss_ref[...] = jnp.broadcast_to((0.5 / d) * jnp.sum(acc[...]), loss_ref.shape)

    row = pl.BlockSpec((ts, d), lambda i: (i, 0))
    vec = pl.BlockSpec((1, d), lambda i: (0, 0))
    return pl.pallas_call(
        body, name=name, grid=(n,),
        out_shape=(jax.ShapeDtypeStruct((rows, d), F32), jax.ShapeDtypeStruct((1, LANE), F32)),
        in_specs=[row, row, vec, row], out_specs=(row, pl.BlockSpec((1, LANE), lambda i: (0, 0))),
        scratch_shapes=[pltpu.VMEM((8, d), F32)], compiler_params=_params(),
    )(xres, y, w_post, target)


def _post_bwd(y, w_post, dxn, name):
    rows, d = y.shape
    ts = _row_tile(rows)
    n = rows // ts

    def body(y_ref, w_ref, d_ref, dy_ref, dw_ref, acc):
        i = pl.program_id(0)
        dy, dwr = _rms_bwd(y_ref[...], w_ref[...], d_ref[...])
        dy_ref[...] = dy.astype(BF16)

        @pl.when(i == 0)
        def _():
            acc[...] = jnp.zeros_like(acc)

        acc[...] += _colsum8(dwr)

        @pl.when(i == n - 1)
        def _():
            dw_ref[...] = jnp.sum(acc[...], axis=0, keepdims=True)

    row = pl.BlockSpec((ts, d), lambda i: (i, 0))
    vec = pl.BlockSpec((1, d), lambda i: (0, 0))
    return pl.pallas_call(
        body, name=name, grid=(n,),
        out_shape=(jax.ShapeDtypeStruct((rows, d), BF16), jax.ShapeDtypeStruct((1, d), F32)),
        in_specs=[row, vec, row], out_specs=(row, vec),
        scratch_shapes=[pltpu.VMEM((8, d), F32)], compiler_params=_params(),
    )(y, w_post, dxn)


def _pre_bwd(x, w_pre, dh, dres, name):
    rows, d = x.shape
    ts = _row_tile(rows)
    n = rows // ts

    def body(x_ref, w_ref, dh_ref, dr_ref, dx_ref, dw_ref, acc):
        i = pl.program_id(0)
        dx, dwr = _rms_bwd(x_ref[...], w_ref[...], dh_ref[...].astype(F32))
        dx_ref[...] = dr_ref[...] + dx

        @pl.when(i == 0)
        def _():
            acc[...] = jnp.zeros_like(acc)

        acc[...] += _colsum8(dwr)

        @pl.when(i == n - 1)
        def _():
            dw_ref[...] = jnp.sum(acc[...], axis=0, keepdims=True)

    row = pl.BlockSpec((ts, d), lambda i: (i, 0))
    vec = pl.BlockSpec((1, d), lambda i: (0, 0))
    return pl.pallas_call(
        body, name=name, grid=(n,),
        out_shape=(jax.ShapeDtypeStruct((rows, d), F32), jax.ShapeDtypeStruct((1, d), F32)),
        in_specs=[row, vec, row, row], out_specs=(row, vec),
        scratch_shapes=[pltpu.VMEM((8, d), F32)], compiler_params=_params(),
    )(x, w_pre, dh, dres)


def _prev_halo_spec(ts, cw, col0=0):
    return pl.BlockSpec((HALO, cw), lambda i, j: (jnp.maximum(i * (ts // HALO) - 1, 0), j + col0))


def _fill_causal(ext, tile_f32, halo_f32, i):
    ext[pl.ds(0, HALO), :] = jnp.where(i > 0, halo_f32, 0.0)
    ext[pl.ds(HALO, tile_f32.shape[0]), :] = tile_f32


def _row_chunks(ts, chunk):
    assert ts % RC == 0, (ts, RC)

    def step(c, carry):
        chunk(pl.multiple_of(c * RC, RC))
        return carry

    lax.fori_loop(0, ts // RC, step, 0)


def _causal_taps(ext, r0, K):
    blk = ext[pl.ds(r0 + HALO - 8, RC + 8), :]
    return [blk[8 - j:8 - j + RC] for j in range(K)]


def _advanced_taps(ext, r0, K):
    blk = ext[pl.ds(r0, RC + 8), :]
    return [blk[j:j + RC] for j in range(K)]


def _filter(wv, taps):
    K = len(taps)
    acc = wv[K - 1:K, :] * taps[0]
    for t in range(K - 1):
        acc = acc + wv[t:t + 1, :] * taps[K - 1 - t]
    return acc


def _conv_t(dc, w, name, col_tile=512):
    G, S, C = dc.shape
    K = w.shape[1]
    ts, cw = _tile(S, SEQ_TILE, HALO), _tile(C, col_tile)
    n = S // ts

    def body(d_ref, nx_ref, w_ref, o_ref, ext):
        i = pl.program_id(1)
        ext[pl.ds(0, ts), :] = d_ref[...].astype(F32)
        ext[pl.ds(ts, HALO), :] = jnp.where(i < n - 1, nx_ref[...].astype(F32), 0.0)
        wv = w_ref[...]

        def chunk(r0):
            taps = _advanced_taps(ext, r0, K)
            acc = wv[K - 1:K, :] * taps[0]
            for j in range(K - 1):
                acc = acc + wv[j:j + 1, :] * taps[K - 1 - j]
            o_ref[pl.ds(r0, RC), :] = acc.astype(o_ref.dtype)

        _row_chunks(ts, chunk)

    return pl.pallas_call(
        body, name=name, out_shape=jax.ShapeDtypeStruct((G, S, C), BF16), grid=(G, n, C // cw),
        in_specs=[pl.BlockSpec((None, ts, cw), lambda g, i, j: (g, i, j)),
                  pl.BlockSpec((None, HALO, cw),
                               lambda g, i, j: (g, jnp.minimum((i + 1) * (ts // HALO), S // HALO - 1), j)),
                  pl.BlockSpec((None, K, cw), lambda g, i, j: (g, 0, j))],
        out_specs=pl.BlockSpec((None, ts, cw), lambda g, i, j: (g, i, j)),
        scratch_shapes=[pltpu.VMEM((ts + HALO, cw), F32)], compiler_params=_params(),
    )(dc, dc, w)


def _qkv_conv(proj, conv_w, qkv_w, name):
    S = proj.shape[0]
    H3 = qkv_w // HD
    H = H3 // 3
    hb = 4 if H % 4 == 0 else 1
    cw = hb * HD
    ts = _tile(S, SEQ_TILE, HALO)
    per_kind = H // hb

    def body(x_ref, h_ref, w_ref, o_ref, ext):
        i, j = pl.program_id(0), pl.program_id(1)
        _fill_causal(ext, x_ref[...].astype(F32), h_ref[...].astype(F32), i)
        wv = w_ref[...]
        kind = j // per_kind
        scale = jnp.where(kind == 0, HD ** -0.5, 1.0)

        def chunk(r0):
            s = _silu(_filter(wv, _causal_taps(ext, r0, GDN_CONV)))
            for a in range(hb):
                sa = s[:, HD * a:HD * (a + 1)]
                r = lax.rsqrt(jnp.sum(sa * sa, axis=-1, keepdims=True) + EPS)
                o_ref[a, pl.ds(r0, RC), :] = jnp.where(kind == 2, sa, sa * r * scale)

        _row_chunks(ts, chunk)

    return pl.pallas_call(
        body, name=name, out_shape=jax.ShapeDtypeStruct((H3, S, HD), F32), grid=(S // ts, qkv_w // cw),
        in_specs=[pl.BlockSpec((ts, cw), lambda i, j: (i, j)), _prev_halo_spec(ts, cw),
                  pl.BlockSpec((GDN_CONV, cw), lambda i, j: (0, j))],
        out_specs=pl.BlockSpec((hb, ts, HD), lambda i, j: (j, i, 0)),
        scratch_shapes=[pltpu.VMEM((ts + HALO, cw), F32)], compiler_params=_params(),
    )(proj, proj, conv_w)


def _qkv_conv_bwd(proj, conv_w, dqkv_hm, qkv_w, name):
    S = proj.shape[0]
    H = qkv_w // HD // 3
    hb = 4 if H % 4 == 0 else 1
    cw = hb * HD
    ts = _tile(S, SEQ_TILE, HALO)
    n = S // ts
    per_kind = H // hb

    def body(x_ref, h_ref, w_ref, d_ref, dc_ref, dw_ref, ext, acc):
        j, i = pl.program_id(0), pl.program_id(1)
        _fill_causal(ext, x_ref[...].astype(F32), h_ref[...].astype(F32), i)
        wv = w_ref[...]
        kind = j // per_kind
        scale = jnp.where(kind == 0, HD ** -0.5, 1.0)

        @pl.when(i == 0)
        def _():
            acc[...] = jnp.zeros_like(acc)

        def chunk(r0):
            taps = _causal_taps(ext, r0, GDN_CONV)
            c = _filter(wv, taps)
            s = _silu(c)
            parts = []
            for a in range(hb):
                sa = s[:, HD * a:HD * (a + 1)]
                dy = d_ref[a, pl.ds(r0, RC), :]
                r = lax.rsqrt(jnp.sum(sa * sa, axis=-1, keepdims=True) + EPS)
                dn = scale * (r * dy - sa * (r * r * r) * jnp.sum(dy * sa, axis=-1, keepdims=True))
                parts.append(jnp.where(kind == 2, dy, dn))
            dc = jnp.concatenate(parts, axis=1) * _dsilu(c)
            dc_ref[pl.ds(r0, RC), :] = dc.astype(BF16)
            for t in range(GDN_CONV):
                acc[t] += _colsum8(dc * taps[GDN_CONV - 1 - t])

        _row_chunks(ts, chunk)

        @pl.when(i == n - 1)
        def _():
            dw_ref[...] = jnp.sum(acc[...], axis=1)

    return pl.pallas_call(
        body, name=name, grid=(qkv_w // cw, n),
        out_shape=(jax.ShapeDtypeStruct((S, qkv_w), BF16), jax.ShapeDtypeStruct((GDN_CONV, qkv_w), F32)),
        in_specs=[pl.BlockSpec((ts, cw), lambda j, i: (i, j)),
                  pl.BlockSpec((HALO, cw), lambda j, i: (jnp.maximum(i * (ts // HALO) - 1, 0), j)),
                  pl.BlockSpec((GDN_CONV, cw), lambda j, i: (0, j)),
                  pl.BlockSpec((hb, ts, HD), lambda j, i: (j, i, 0))],
        out_specs=(pl.BlockSpec((ts, cw), lambda j, i: (i, j)), pl.BlockSpec((GDN_CONV, cw), lambda j, i: (0, j))),
        scratch_shapes=[pltpu.VMEM((ts + HALO, cw), F32), pltpu.VMEM((GDN_CONV, 8, cw), F32)],
        compiler_params=_params(),
    )(proj, proj, conv_w, dqkv_hm)


def _chunk_cumsum(x):
    row = lax.broadcasted_iota(jnp.int32, x.shape, 0) & (CHUNK - 1)
    s = 1
    while s < CHUNK:
        x = x + jnp.where(row >= s, pltpu.roll(x, s, axis=0), 0.0)
        s *= 2
    return x


def _chunk_rev_cumsum(x):
    rows = x.shape[0]
    row = lax.broadcasted_iota(jnp.int32, x.shape, 0) & (CHUNK - 1)
    s = 1
    while s < CHUNK:
        x = x + jnp.where(row < CHUNK - s, pltpu.roll(x, rows - s, axis=0), 0.0)
        s *= 2
    return x


def _softplus(x):
    return jnp.maximum(x, 0.0) + jnp.log1p(jnp.exp(-jnp.abs(x)))


def _gates(proj, ba_col, a_log_l, dt_bias_l, H, name):
    S = proj.shape[0]
    ts = _tile(S, 512, CHUNK)

    def body(x_ref, al_ref, dt_ref, o_ref):
        x = x_ref[...].astype(F32)
        lane = lax.broadcasted_iota(jnp.int32, x.shape, 1)
        g = -jnp.exp(al_ref[...]) * _softplus(x + dt_ref[...])
        G = _chunk_cumsum(jnp.where((lane >= H) & (lane < 2 * H), g, 0.0))
        o_ref[...] = jnp.where(lane < H, _sigmoid(x), G)

    return pl.pallas_call(
        body, name=name, out_shape=jax.ShapeDtypeStruct((S, LANE), F32), grid=(S // ts,),
        in_specs=[pl.BlockSpec((ts, LANE), lambda i: (i, ba_col)), pl.BlockSpec((1, LANE), lambda i: (0, 0)),
                  pl.BlockSpec((1, LANE), lambda i: (0, 0))],
        out_specs=pl.BlockSpec((ts, LANE), lambda i: (i, 0)), compiler_params=_params(),
    )(proj, a_log_l, dt_bias_l)


def _gates_bwd(proj, ba_col, ba_w, a_log_l, dt_bias_l, dbg, H, name):
    S = proj.shape[0]
    ts = _tile(S, 512, CHUNK)
    n = S // ts

    def body(x_ref, al_ref, dt_ref, d_ref, o_ref, dal_ref, ddt_ref, acc):
        i = pl.program_id(0)
        x = x_ref[...].astype(F32)
        d = d_ref[...]
        lane = lax.broadcasted_iota(jnp.int32, x.shape, 1)
        is_a = (lane >= H) & (lane < 2 * H)
        beta = _sigmoid(x)
        nea = -jnp.exp(al_ref[...])
        z = x + dt_ref[...]
        dg = _chunk_rev_cumsum(jnp.where(is_a, d, 0.0))
        da_raw = dg * nea * _sigmoid(z)
        o = jnp.where(lane < H, d * beta * (1.0 - beta), jnp.where(is_a, da_raw, 0.0))
        if ba_w > LANE:
            o = jnp.concatenate([o, jnp.zeros((ts, ba_w - LANE), F32)], axis=1)
        o_ref[...] = o.astype(BF16)

        @pl.when(i == 0)
        def _():
            acc[...] = jnp.zeros_like(acc)

        acc[0] += _colsum8(jnp.where(is_a, dg * nea * _softplus(z), 0.0))
        acc[1] += _colsum8(jnp.where(is_a, da_raw, 0.0))

        @pl.when(i == n - 1)
        def _():
            dal_ref[...] = jnp.sum(acc[0], axis=0, keepdims=True)
            ddt_ref[...] = jnp.sum(acc[1], axis=0, keepdims=True)

    vec = pl.BlockSpec((1, LANE), lambda i: (0, 0))
    return pl.pallas_call(
        body, name=name, grid=(n,),
        out_shape=(jax.ShapeDtypeStruct((S, ba_w), BF16), jax.ShapeDtypeStruct((1, LANE), F32),
                   jax.ShapeDtypeStruct((1, LANE), F32)),
        in_specs=[pl.BlockSpec((ts, LANE), lambda i: (i, ba_col)), vec, vec, pl.BlockSpec((ts, LANE), lambda i: (i, 0))],
        out_specs=(pl.BlockSpec((ts, ba_w), lambda i: (i, 0)), vec, vec),
        scratch_shapes=[pltpu.VMEM((2, 8, LANE), F32)], compiler_params=_params(),
    )(proj, a_log_l, dt_bias_l, dbg)


_BMM_FORMS = {"nn": "hik,hkj->hij", "nt": "hik,hjk->hij", "tn": "hki,hkj->hij"}


def _split_bf16(a):
    hi = a.astype(BF16)
    return hi, (a - hi.astype(F32)).astype(BF16)


def _bmm(a, b, form="nn", exact=False):
    e = lambda x, y: jnp.einsum(_BMM_FORMS[form], x, y, preferred_element_type=F32)
    if not exact:
        return e(a.astype(BF16), b.astype(BF16))
    (ah, al), (bh, bl) = _split_bf16(a), _split_bf16(b)
    return (e(ah, bl) + e(al, bh)) + e(ah, bh)


def _unit_lower_inverse(L, r, c):
    eye = (r == c).astype(F32)
    m = jnp.where((r >> 3) == (c >> 3), -L, 0.0)
    m2 = _bmm(m, m, exact=True)
    m4 = _bmm(m2, m2, exact=True)
    x = eye + m
    x = x + _bmm(x, m2, exact=True)
    x = x + _bmm(x, m4, exact=True)
    for sh in (3, 4, 5):
        off = ((r >> (sh + 1)) == (c >> (sh + 1))) & ((r >> sh) != (c >> sh))
        x = x - _bmm(x, _bmm(jnp.where(off, L, 0.0), x))
    return x


def _to_row(col, eye):
    return jnp.sum(jnp.where(eye, jnp.broadcast_to(col, eye.shape), 0.0), axis=1, keepdims=True)


def _to_col(rowv, eye):
    return jnp.sum(jnp.where(eye, jnp.broadcast_to(rowv, eye.shape), 0.0), axis=2, keepdims=True)


def _gdn_chunk(q, k, v, bg, H, P=1):
    shape = (H * P, CHUNK, CHUNK)
    r = lax.broadcasted_iota(jnp.int32, shape, 1)
    c = lax.broadcasted_iota(jnp.int32, shape, 2)
    eye, incl, strict = r == c, r >= c, r > c
    rows = lambda p: slice(p * CHUNK, (p + 1) * CHUNK)
    beta = jnp.stack([bg[rows(p), h:h + 1] for h in range(H) for p in range(P)], axis=0)
    G = jnp.stack([bg[rows(p), H + h:H + h + 1] for h in range(H) for p in range(P)], axis=0)
    gap = jnp.broadcast_to(G, shape) - _to_row(G, eye)
    decay = jnp.where(incl, jnp.exp(jnp.where(incl, gap, 0.0)), 0.0)
    kk = _bmm(k, k, "nt")
    L = jnp.where(strict, beta * decay * kk, 0.0)
    ainv = _unit_lower_inverse(L, r, c)
    eG = jnp.exp(G)
    u_v = _bmm(ainv, beta * v)
    w_k = _bmm(ainv, (beta * eG) * k)
    qk = _bmm(q, k, "nt", exact=True)
    GL = G[:, CHUNK - 1:CHUNK, :]
    ek = jnp.exp(GL - G)
    return dict(eye=eye, strict=strict, r=r, c=c, beta=beta, G=G, decay=decay, kk=kk, ainv=ainv, eG=eG,
                u_v=u_v, w_k=w_k, qk=qk, attn=decay * qk, GL=GL, ek=ek, cd=jnp.exp(GL))


def _chunk_steps(N):
    at = lambda step: lambda: pl.program_id(0) == step
    return at(0), at(N - max(N // 8, 1)), at(N - 1)


def _gdn_fwd(qkv_hm, bg, H, name, rider=None):
    S = qkv_hm.shape[1]
    N = S // CHUNK
    P = 2 if N % 2 == 0 else 1
    extra = rider.n if rider is not None else 0

    def body(q_ref, k_ref, v_ref, bg_ref, o_ref, st_ref, state):
        n = pl.program_id(0)

        @pl.when(n == 0)
        def _():
            state[...] = jnp.zeros_like(state)

        split = lambda x: x.reshape(H * P, CHUNK, HD)
        pick = lambda x, p: x.reshape((H, P) + x.shape[1:])[:, p]
        q, k, v = split(q_ref[...]), split(k_ref[...]), split(v_ref[...])
        t = _gdn_chunk(q, k, v, bg_ref[...], H, P)
        qdec, kdec = q * t["eG"], k * t["ek"]
        s0 = state[...]
        outs = []
        for p in range(P):
            st_ref[p] = s0
            u = pick(t["u_v"], p) - _bmm(pick(t["w_k"], p), s0)
            outs.append(_bmm(pick(qdec, p), s0) + _bmm(pick(t["attn"], p), u))
            s0 = pick(t["cd"], p) * s0 + _bmm(pick(kdec, p), u, "tn")
        state[...] = s0
        o_ref[...] = jnp.concatenate(outs, axis=1)

    blk = lambda kind: pl.BlockSpec((H, P * CHUNK, HD), lambda n: (kind, n, 0))
    res = pl.pallas_call(
        _ride(body, 4, 2, rider, *_chunk_steps(N // P)), name=name, grid=(N // P,),
        out_shape=[jax.ShapeDtypeStruct((H, S, HD), F32), jax.ShapeDtypeStruct((N, H, HD, HD), F32)]
        + (rider.out_shape if rider else []),
        in_specs=[blk(0), blk(1), blk(2), pl.BlockSpec((P * CHUNK, LANE), lambda n: (n, 0))] + [ANY] * extra,
        out_specs=[pl.BlockSpec((H, P * CHUNK, HD), lambda n: (0, n, 0)),
                   pl.BlockSpec((P, H, HD, HD), lambda n: (n, 0, 0, 0))] + [ANY] * extra,
        scratch_shapes=[pltpu.VMEM((H, HD, HD), F32)] + (rider.scratch if rider else []), compiler_params=_params(),
    )(qkv_hm, qkv_hm, qkv_hm, bg, *(rider.arrays if rider else []))
    return res[0], res[1], res[2:]


def _gdn_bwd(qkv_hm, bg, states, do_hm, H, name, rider=None):
    S = qkv_hm.shape[1]
    N = S // CHUNK
    extra = rider.n if rider is not None else 0

    def body(q_ref, k_ref, v_ref, bg_ref, st_ref, do_ref, dqkv_ref, dbg_ref, dstate):
        n = pl.program_id(0)

        @pl.when(n == 0)
        def _():
            dstate[...] = jnp.zeros_like(dstate)

        q, k, v, do = q_ref[...], k_ref[...], v_ref[...], do_ref[...]
        t = _gdn_chunk(q, k, v, bg_ref[...], H)
        eye, beta, eG, decay, kk, ainv = t["eye"], t["beta"], t["eG"], t["decay"], t["kk"], t["ainv"]
        s0 = st_ref[0]
        ds1 = dstate[...]
        u = t["u_v"] - _bmm(t["w_k"], s0)
        qdec, kdec = q * eG, k * t["ek"]
        d_qdec = _bmm(do, s0, "nt")
        d_attn = _bmm(do, u, "nt")
        du = _bmm(t["attn"], do, "tn") + _bmm(kdec, ds1)
        d_cd = jnp.sum(jnp.sum(ds1 * s0, axis=2, keepdims=True), axis=1, keepdims=True)
        d_kdec = _bmm(u, ds1, "nt")
        d_wk = -_bmm(du, s0, "nt")
        dstate[...] = t["cd"] * ds1 + _bmm(qdec, do, "tn") - _bmm(t["w_k"], du, "tn")
        d_rv = _bmm(ainv, du, "tn")
        d_rk = _bmm(ainv, d_wk, "tn")
        dL = jnp.where(t["strict"], -(_bmm(d_rv, t["u_v"], "nt") + _bmm(d_rk, t["w_k"], "nt")), 0.0)
        rk_k = jnp.sum(d_rk * k, axis=2, keepdims=True)
        d_beta = (jnp.sum(dL * decay * kk, axis=2, keepdims=True) + jnp.sum(d_rv * v, axis=2, keepdims=True)
                  + rk_k * eG)
        d_decay = dL * beta * kk + d_attn * t["qk"]
        d_kk = dL * beta * decay
        d_qk = d_attn * decay
        dqkv_ref[pl.ds(2 * H, H)] = beta * d_rv
        dqkv_ref[pl.ds(0, H)] = _bmm(d_qk, k) + d_qdec * eG
        dqkv_ref[pl.ds(H, H)] = ((beta * eG) * d_rk + _bmm(d_kk, k) + _bmm(d_kk, k, "tn") + _bmm(d_qk, q, "tn")
                       + d_kdec * t["ek"])
        d_eG = rk_k * beta + jnp.sum(d_qdec * q, axis=2, keepdims=True)
        e = jnp.sum(d_kdec * kdec, axis=2, keepdims=True)
        T = d_decay * decay
        dG = d_eG * eG - e + jnp.sum(T, axis=2, keepdims=True) - _to_col(jnp.sum(T, axis=1, keepdims=True), eye)
        dGL = jnp.sum(e, axis=1, keepdims=True) + d_cd * t["cd"]
        row1 = lax.broadcasted_iota(jnp.int32, (H, CHUNK, 1), 1)
        dG = dG + jnp.where(row1 == CHUNK - 1, dGL, 0.0)
        lane = lax.broadcasted_iota(jnp.int32, (CHUNK, LANE), 1)
        out = jnp.zeros((CHUNK, LANE), F32)
        for h in range(H):
            out = out + jnp.where(lane == h, d_beta[h], 0.0) + jnp.where(lane == H + h, dG[h], 0.0)
        dbg_ref[...] = out

    blk = lambda kind: pl.BlockSpec((H, CHUNK, HD), lambda n: (kind, N - 1 - n, 0))
    res = pl.pallas_call(
        _ride(body, 6, 2, rider, *_chunk_steps(N)), name=name, grid=(N,),
        out_shape=[jax.ShapeDtypeStruct((3 * H, S, HD), F32), jax.ShapeDtypeStruct((S, LANE), F32)]
        + (rider.out_shape if rider else []),
        in_specs=[blk(0), blk(1), blk(2), pl.BlockSpec((CHUNK, LANE), lambda n: (N - 1 - n, 0)),
                  pl.BlockSpec((1, H, HD, HD), lambda n: (N - 1 - n, 0, 0, 0)), blk(0)] + [ANY] * extra,
        out_specs=[pl.BlockSpec((3 * H, CHUNK, HD), lambda n: (0, N - 1 - n, 0)),
                   pl.BlockSpec((CHUNK, LANE), lambda n: (N - 1 - n, 0))] + [ANY] * extra,
        scratch_shapes=[pltpu.VMEM((H, HD, HD), F32)] + (rider.scratch if rider else []), compiler_params=_params(),
    )(qkv_hm, qkv_hm, qkv_hm, bg, states, do_hm, *(rider.arrays if rider else []))
    return res[0], res[1], res[2:]


def _gdn_out(o_hm, proj, z_col, gdn_w, name):
    H, S, _ = o_hm.shape
    vw = H * HD
    ts = _tile(S, 256, 8)

    def body(o_ref, z_ref, w_ref, y_ref):
        z = z_ref[...].astype(F32)
        w = w_ref[...]
        parts = []
        for h in range(H):
            o = o_ref[h]
            parts.append(o * _rstd(o) * w)
        y_ref[...] = (jnp.concatenate(parts, axis=1) * _silu(z)).astype(BF16)

    return pl.pallas_call(
        body, name=name, out_shape=jax.ShapeDtypeStruct((S, vw), BF16), grid=(S // ts,),
        in_specs=[pl.BlockSpec((H, ts, HD), lambda i: (0, i, 0)), pl.BlockSpec((ts, vw), lambda i: (i, z_col)),
                  pl.BlockSpec((1, HD), lambda i: (0, 0))],
        out_specs=pl.BlockSpec((ts, vw), lambda i: (i, 0)), compiler_params=_params(),
    )(o_hm, proj, gdn_w)


def _gdn_out_bwd(o_hm, proj, z_col, gdn_w, dy, name):
    H, S, _ = o_hm.shape
    vw = H * HD
    ts = _tile(S, 256, 8)
    n = S // ts

    def body(o_ref, z_ref, w_ref, dy_ref, do_ref, dz_ref, dw_ref, acc):
        i = pl.program_id(0)
        z = z_ref[...].astype(F32)
        dy = dy_ref[...].astype(F32)
        w = w_ref[...]
        gz = dy * _silu(z)
        normed, dwr = [], jnp.zeros((ts, HD), F32)
        for h in range(H):
            o = o_ref[h]
            dxo, dwh = _rms_bwd(o, w, gz[:, HD * h:HD * (h + 1)])
            do_ref[h] = dxo
            dwr = dwr + dwh
            normed.append(o * _rstd(o) * w)
        dz_ref[...] = (dy * jnp.concatenate(normed, axis=1) * _dsilu(z)).astype(BF16)

        @pl.when(i == 0)
        def _():
            acc[...] = jnp.zeros_like(acc)

        acc[...] += _colsum8(dwr)

        @pl.when(i == n - 1)
        def _():
            dw_ref[...] = jnp.sum(acc[...], axis=0, keepdims=True)

    return pl.pallas_call(
        body, name=name, grid=(n,),
        out_shape=(jax.ShapeDtypeStruct((H, S, HD), F32), jax.ShapeDtypeStruct((S, vw), BF16),
                   jax.ShapeDtypeStruct((1, HD), F32)),
        in_specs=[pl.BlockSpec((H, ts, HD), lambda i: (0, i, 0)), pl.BlockSpec((ts, vw), lambda i: (i, z_col)),
                  pl.BlockSpec((1, HD), lambda i: (0, 0)), pl.BlockSpec((ts, vw), lambda i: (i, 0))],
        out_specs=(pl.BlockSpec((H, ts, HD), lambda i: (0, i, 0)), pl.BlockSpec((ts, vw), lambda i: (i, 0)),
                   pl.BlockSpec((1, HD), lambda i: (0, 0))),
        scratch_shapes=[pltpu.VMEM((8, HD), F32)], compiler_params=_params(),
    )(o_hm, proj, gdn_w, dy)


def _pool_trailing(ext, ts, pg, row0):
    outs, inv_cnts = [], []
    t_abs = row0 + lax.broadcasted_iota(jnp.int32, (ts, 1), 0)
    for gi, win in enumerate(POOL_WINDOWS):
        cols = pl.ds(gi * pg, pg)
        cur = ext[pl.ds(HALO, ts), cols]
        acc = cur
        for j in range(1, win):
            acc = acc + ext[pl.ds(HALO - j, ts), cols]
        inv = 1.0 / jnp.minimum(t_abs + 1, win).astype(F32)
        outs.append(acc * inv - cur)
    return outs


def _pool_fwd(proj, p_col, pool_w, pool_scale, pw, name):
    S = proj.shape[0]
    pg = pw // len(POOL_WINDOWS)
    ts = _tile(S, 512, HALO)

    def body(x_ref, h_ref, w_ref, sc_ref, o_ref, ext):
        i = pl.program_id(0)
        _fill_causal(ext, x_ref[...].astype(F32), h_ref[...].astype(F32), i)
        ys = _pool_trailing(ext, ts, pg, i * ts)
        outs = [jnp.dot(ys[gi].astype(BF16), w_ref[gi], preferred_element_type=F32) for gi in range(len(ys))]
        o_ref[...] = (jnp.concatenate(outs, axis=1) * sc_ref[...]).astype(BF16)

    return pl.pallas_call(
        body, name=name, out_shape=jax.ShapeDtypeStruct((S, pw), BF16), grid=(S // ts,),
        in_specs=[pl.BlockSpec((ts, pw), lambda i: (i, p_col)),
                  pl.BlockSpec((HALO, pw), lambda i: (jnp.maximum(i * (ts // HALO) - 1, 0), p_col)),
                  pl.BlockSpec((len(POOL_WINDOWS), pg, pg), lambda i: (0, 0, 0)), pl.BlockSpec((1, pw), lambda i: (0, 0))],
        out_specs=pl.BlockSpec((ts, pw), lambda i: (i, 0)),
        scratch_shapes=[pltpu.VMEM((ts + HALO, pw), F32)], compiler_params=_params(),
    )(proj, proj, pool_w, pool_scale)


def _pool_bwd(proj, p_col, pool_w, pool_scale, dpb, pw, name):
    S = proj.shape[0]
    G = len(POOL_WINDOWS)
    pg = pw // G
    ts = _tile(S, 512, HALO)
    n = S // ts

    def body(x_ref, h_ref, w_ref, sc_ref, d_ref, dn_ref, dp_ref, dw_ref, dsc_ref, ext, zext, wacc, sacc):
        i = pl.program_id(0)
        _fill_causal(ext, x_ref[...].astype(F32), h_ref[...].astype(F32), i)
        ys = _pool_trailing(ext, ts, pg, i * ts)
        d_ext = jnp.concatenate([d_ref[...].astype(F32), jnp.where(i < n - 1, dn_ref[...].astype(F32), 0.0)], axis=0)
        dt = d_ext * sc_ref[...]
        t_abs = i * ts + lax.broadcasted_iota(jnp.int32, (ts + HALO, 1), 0)

        @pl.when(i == 0)
        def _():
            wacc[...] = jnp.zeros_like(wacc)
            sacc[...] = jnp.zeros_like(sacc)

        dps, tfs = [], []
        for gi, win in enumerate(POOL_WINDOWS):
            cols = slice(gi * pg, (gi + 1) * pg)
            w = w_ref[gi]
            dt_g = dt[:, cols].astype(BF16)
            y_g = ys[gi].astype(BF16)
            tfs.append(jnp.dot(y_g, w, preferred_element_type=F32))
            wacc[gi] += lax.dot_general(y_g, dt_g[:ts], (((0,), (0,)), ((), ())), preferred_element_type=F32)
            dyp = lax.dot_general(dt_g, w, (((1,), (1,)), ((), ())), preferred_element_type=F32)
            zext[:, pl.ds(gi * pg, pg)] = dyp * (1.0 / jnp.minimum(t_abs + 1, win).astype(F32))
            acc = -dyp[:ts]
            for j in range(win):
                acc = acc + zext[pl.ds(j, ts), pl.ds(gi * pg, pg)]
            dps.append(acc)
        dp_ref[...] = jnp.concatenate(dps, axis=1).astype(BF16)
        sacc[...] += _colsum8(d_ext[:ts] * jnp.concatenate(tfs, axis=1))

        @pl.when(i == n - 1)
        def _():
            dw_ref[...] = wacc[...]
            dsc_ref[...] = jnp.sum(sacc[...], axis=0, keepdims=True)

    return pl.pallas_call(
        body, name=name, grid=(n,),
        out_shape=(jax.ShapeDtypeStruct((S, pw), BF16), jax.ShapeDtypeStruct((G, pg, pg), F32),
                   jax.ShapeDtypeStruct((1, pw), F32)),
        in_specs=[pl.BlockSpec((ts, pw), lambda i: (i, p_col)),
                  pl.BlockSpec((HALO, pw), lambda i: (jnp.maximum(i * (ts // HALO) - 1, 0), p_col)),
                  pl.BlockSpec((G, pg, pg), lambda i: (0, 0, 0)), pl.BlockSpec((1, pw), lambda i: (0, 0)),
                  pl.BlockSpec((ts, pw), lambda i: (i, 0)),
                  pl.BlockSpec((HALO, pw), lambda i: (jnp.minimum((i + 1) * (ts // HALO), S // HALO - 1), 0))],
        out_specs=(pl.BlockSpec((ts, pw), lambda i: (i, 0)), pl.BlockSpec((G, pg, pg), lambda i: (0, 0, 0)),
                   pl.BlockSpec((1, pw), lambda i: (0, 0))),
        scratch_shapes=[pltpu.VMEM((ts + HALO, pw), F32), pltpu.VMEM((ts + HALO, pw), F32),
                        pltpu.VMEM((G, pg, pg), F32), pltpu.VMEM((8, pw), F32)],
        compiler_params=_params(),
    )(proj, proj, pool_w, pool_scale, dpb, dpb)


def _merge(proj, ga_col, gb_col, ya, yb, name):
    S, d = ya.shape
    ts = _tile(S, 512, 16)

    def body(ga_ref, gb_ref, ya_ref, yb_ref, o_ref):
        o_ref[...] = (_sigmoid(ga_ref[...].astype(F32)) * ya_ref[...].astype(F32)
                      + _sigmoid(gb_ref[...].astype(F32)) * yb_ref[...].astype(F32)).astype(BF16)

    row = pl.BlockSpec((ts, d), lambda i: (i, 0))
    return pl.pallas_call(
        body, name=name, out_shape=jax.ShapeDtypeStruct((S, d), BF16), grid=(S // ts,),
        in_specs=[pl.BlockSpec((ts, d), lambda i: (i, ga_col)), pl.BlockSpec((ts, d), lambda i: (i, gb_col)), row, row],
        out_specs=row, compiler_params=_params(),
    )(proj, proj, ya, yb)


def _merge_bwd(proj, ga_col, gb_col, ya, yb, dm, name):
    S, d = ya.shape
    ts = _tile(S, 512, 16)

    def body(ga_ref, gb_ref, ya_ref, yb_ref, dm_ref, dya_ref, dyb_ref, dga_ref, dgb_ref):
        dmv = dm_ref[...].astype(F32)
        sa, sb = _sigmoid(ga_ref[...].astype(F32)), _sigmoid(gb_ref[...].astype(F32))
        dya_ref[...] = (dmv * sa).astype(BF16)
        dyb_ref[...] = (dmv * sb).astype(BF16)
        dga_ref[...] = (dmv * ya_ref[...].astype(F32) * sa * (1.0 - sa)).astype(BF16)
        dgb_ref[...] = (dmv * yb_ref[...].astype(F32) * sb * (1.0 - sb)).astype(BF16)

    row = pl.BlockSpec((ts, d), lambda i: (i, 0))
    o = jax.ShapeDtypeStruct((S, d), BF16)
    return pl.pallas_call(
        body, name=name, out_shape=(o, o, o, o), grid=(S // ts,),
        in_specs=[pl.BlockSpec((ts, d), lambda i: (i, ga_col)), pl.BlockSpec((ts, d), lambda i: (i, gb_col)), row, row, row],
        out_specs=(row, row, row, row), compiler_params=_params(),
    )(proj, proj, ya, yb, dm)


def _xattn_fwd(q, kv, name):
    S, d = q.shape
    M = kv.shape[0]
    hd = d // XA_HEADS
    ts = _tile(S, 512, 16)
    scale = hd ** -0.5

    def body(q_ref, k_ref, v_ref, o_ref):
        s = lax.dot_general(q_ref[...], k_ref[...], (((1,), (1,)), ((), ())), preferred_element_type=F32) * scale
        p = jnp.exp(s - jnp.max(s, axis=-1, keepdims=True))
        p = p / jnp.sum(p, axis=-1, keepdims=True)
        o_ref[...] = jnp.dot(p.astype(BF16), v_ref[...], preferred_element_type=F32).astype(BF16)

    return pl.pallas_call(
        body, name=name, out_shape=jax.ShapeDtypeStruct((S, d), BF16), grid=(S // ts, XA_HEADS),
        in_specs=[pl.BlockSpec((ts, hd), lambda i, h: (i, h)), pl.BlockSpec((M, hd), lambda i, h: (0, h)),
                  pl.BlockSpec((M, hd), lambda i, h: (0, XA_HEADS + h))],
        out_specs=pl.BlockSpec((ts, hd), lambda i, h: (i, h)), compiler_params=_params(),
    )(q, kv, kv)


def _xattn_bwd(q, kv, do, name):
    S, d = q.shape
    M = kv.shape[0]
    hd = d // XA_HEADS
    ts = _tile(S, 512, 16)
    n = S // ts
    scale = hd ** -0.5

    def body(q_ref, k_ref, v_ref, do_ref, dq_ref, dk_ref, dv_ref, kacc, vacc):
        i = pl.program_id(1)
        qv, kv_, vv, dov = q_ref[...], k_ref[...], v_ref[...], do_ref[...]
        s = lax.dot_general(qv, kv_, (((1,), (1,)), ((), ())), preferred_element_type=F32) * scale
        p = jnp.exp(s - jnp.max(s, axis=-1, keepdims=True))
        p = p / jnp.sum(p, axis=-1, keepdims=True)
        dp = lax.dot_general(dov, vv, (((1,), (1,)), ((), ())), preferred_element_type=F32)
        ds = (p * (dp - jnp.sum(p * dp, axis=-1, keepdims=True)) * scale).astype(BF16)
        dq_ref[...] = jnp.dot(ds, kv_, preferred_element_type=F32).astype(BF16)

        @pl.when(i == 0)
        def _():
            kacc[...] = jnp.zeros_like(kacc)
            vacc[...] = jnp.zeros_like(vacc)

        kacc[...] += lax.dot_general(ds, qv, (((0,), (0,)), ((), ())), preferred_element_type=F32)
        vacc[...] += lax.dot_general(p.astype(BF16), dov, (((0,), (0,)), ((), ())), preferred_element_type=F32)

        @pl.when(i == n - 1)
        def _():
            dk_ref[...] = kacc[...]
            dv_ref[...] = vacc[...]

    dq, dk, dv = pl.pallas_call(
        body, name=name, grid=(XA_HEADS, n),
        out_shape=(jax.ShapeDtypeStruct((S, d), BF16), jax.ShapeDtypeStruct((M, d), F32), jax.ShapeDtypeStruct((M, d), F32)),
        in_specs=[pl.BlockSpec((ts, hd), lambda h, i: (i, h)), pl.BlockSpec((M, hd), lambda h, i: (0, h)),
                  pl.BlockSpec((M, hd), lambda h, i: (0, XA_HEADS + h)), pl.BlockSpec((ts, hd), lambda h, i: (i, h))],
        out_specs=(pl.BlockSpec((ts, hd), lambda h, i: (i, h)), pl.BlockSpec((M, hd), lambda h, i: (0, h)),
                   pl.BlockSpec((M, hd), lambda h, i: (0, h))),
        scratch_shapes=[pltpu.VMEM((M, hd), F32), pltpu.VMEM((M, hd), F32)], compiler_params=_params(),
    )(q, kv, kv, do)
    return dq, jnp.concatenate([dk, dv], axis=1)


def _ffn_act(up, conv_w, bias, name):
    S, F2 = up.shape
    F = F2 // 2
    ts, cw = _tile(S, SEQ_TILE, HALO), _tile(F, 512)
    nb = F // cw

    def body(a_ref, ah_ref, b_ref, bh_ref, wa_ref, wb_ref, ba_ref, bb_ref, o_ref, ea, eb):
        i = pl.program_id(0)
        _fill_causal(ea, a_ref[...].astype(F32), ah_ref[...].astype(F32), i)
        _fill_causal(eb, b_ref[...].astype(F32), bh_ref[...].astype(F32), i)
        wa, wb, bia, bib = wa_ref[...], wb_ref[...], ba_ref[...], bb_ref[...]

        def chunk(r0):
            ua = _filter(wa, _causal_taps(ea, r0, FFN_CONV)) + bia
            ub = _filter(wb, _causal_taps(eb, r0, FFN_CONV)) + bib
            o_ref[pl.ds(r0, RC), :] = (_silu(ua) * ub).astype(BF16)

        _row_chunks(ts, chunk)

    tile = lambda c0: pl.BlockSpec((ts, cw), lambda i, j: (i, j + c0))
    vec = lambda rows, c0: pl.BlockSpec((rows, cw), lambda i, j: (0, j + c0))
    return pl.pallas_call(
        body, name=name, out_shape=jax.ShapeDtypeStruct((S, F), BF16), grid=(S // ts, nb),
        in_specs=[tile(0), _prev_halo_spec(ts, cw), tile(nb), _prev_halo_spec(ts, cw, nb),
                  vec(FFN_CONV, 0), vec(FFN_CONV, nb), vec(1, 0), vec(1, nb)],
        out_specs=pl.BlockSpec((ts, cw), lambda i, j: (i, j)),
        scratch_shapes=[pltpu.VMEM((ts + HALO, cw), F32), pltpu.VMEM((ts + HALO, cw), F32)],
        compiler_params=_params(),
    )(up, up, up, up, conv_w, conv_w, bias, bias)


def _ffn_act_bwd(up, conv_w, bias, dact, name):
    S, F2 = up.shape
    F = F2 // 2
    ts, cw = _tile(S, SEQ_TILE, HALO), _tile(F, 512)
    nb = F // cw
    n = S // ts

    def body(a_ref, ah_ref, b_ref, bh_ref, wa_ref, wb_ref, ba_ref, bb_ref, d_ref,
             du_ref, dwa_ref, dwb_ref, dba_ref, dbb_ref, ea, eb, wacc, bacc):
        i = pl.program_id(1)
        _fill_causal(ea, a_ref[...].astype(F32), ah_ref[...].astype(F32), i)
        _fill_causal(eb, b_ref[...].astype(F32), bh_ref[...].astype(F32), i)
        wa, wb, bia, bib = wa_ref[...], wb_ref[...], ba_ref[...], bb_ref[...]

        @pl.when(i == 0)
        def _():
            wacc[...] = jnp.zeros_like(wacc)
            bacc[...] = jnp.zeros_like(bacc)

        def chunk(r0):
            ta, tb = _causal_taps(ea, r0, FFN_CONV), _causal_taps(eb, r0, FFN_CONV)
            ua, ub = _filter(wa, ta) + bia, _filter(wb, tb) + bib
            d = d_ref[pl.ds(r0, RC), :].astype(F32)
            dua = d * ub * _dsilu(ua)
            dub = d * _silu(ua)
            du_ref[0, pl.ds(r0, RC), :] = dua.astype(BF16)
            du_ref[1, pl.ds(r0, RC), :] = dub.astype(BF16)
            for t in range(FFN_CONV):
                wacc[0, t] += _colsum8(dua * ta[FFN_CONV - 1 - t])
                wacc[1, t] += _colsum8(dub * tb[FFN_CONV - 1 - t])
            bacc[0] += _colsum8(dua)
            bacc[1] += _colsum8(dub)

        _row_chunks(ts, chunk)

        @pl.when(i == n - 1)
        def _():
            dwa_ref[...] = jnp.sum(wacc[0], axis=1)
            dwb_ref[...] = jnp.sum(wacc[1], axis=1)
            dba_ref[...] = jnp.sum(bacc[0], axis=0, keepdims=True)
            dbb_ref[...] = jnp.sum(bacc[1], axis=0, keepdims=True)

    tile = lambda c0: pl.BlockSpec((ts, cw), lambda j, i: (i, j + c0))
    halo = lambda c0: pl.BlockSpec((HALO, cw), lambda j, i: (jnp.maximum(i * (ts // HALO) - 1, 0), j + c0))
    vec = lambda rows, c0: pl.BlockSpec((rows, cw), lambda j, i: (0, j + c0))
    du, dwa, dwb, dba, dbb = pl.pallas_call(
        body, name=name, grid=(nb, n),
        out_shape=(jax.ShapeDtypeStruct((2, S, F), BF16),
                   jax.ShapeDtypeStruct((FFN_CONV, F), F32), jax.ShapeDtypeStruct((FFN_CONV, F), F32),
                   jax.ShapeDtypeStruct((1, F), F32), jax.ShapeDtypeStruct((1, F), F32)),
        in_specs=[tile(0), halo(0), tile(nb), halo(nb), vec(FFN_CONV, 0), vec(FFN_CONV, nb), vec(1, 0), vec(1, nb), tile(0)],
        out_specs=(pl.BlockSpec((2, ts, cw), lambda j, i: (0, i, j)), vec(FFN_CONV, 0), vec(FFN_CONV, 0), vec(1, 0),
                   vec(1, 0)),
        scratch_shapes=[pltpu.VMEM((ts + HALO, cw), F32), pltpu.VMEM((ts + HALO, cw), F32),
                        pltpu.VMEM((2, FFN_CONV, 8, cw), F32), pltpu.VMEM((2, 8, cw), F32)],
        compiler_params=_params(),
    )(up, up, up, up, conv_w, conv_w, bias, bias, dact)
    return du, jnp.concatenate([dwa, dwb], axis=1), jnp.concatenate([dba, dbb], axis=1)


def _adamw(gparts, w, m, v, name):
    R, C = w.shape
    tr = _tile(R, max(16, (256 * 1024) // C), 16)

    def body(g_ref, w_ref, m_ref, v_ref, go_ref, d_ref, mo_ref, vo_ref):
        g = g_ref[0].astype(F32)
        for s in range(1, N_DEV):
            g = g + g_ref[s].astype(F32)
        mn = ADAM_B1 * m_ref[...] + (1.0 - ADAM_B1) * g
        vn = ADAM_B2 * v_ref[...] + (1.0 - ADAM_B2) * (g * g)
        m_hat = mn / (1.0 - ADAM_B1 ** ADAM_STEP)
        v_hat = vn / (1.0 - ADAM_B2 ** ADAM_STEP)
        go_ref[...] = g
        d_ref[...] = -ADAM_LR * (m_hat / (jnp.sqrt(v_hat) + ADAM_EPS) + ADAM_WD * w_ref[...])
        mo_ref[...] = mn
        vo_ref[...] = vn

    row = pl.BlockSpec((tr, C), lambda i: (i, 0))
    o = jax.ShapeDtypeStruct((R, C), F32)
    return pl.pallas_call(
        body, name=name, out_shape=(o, o, o, o), grid=(R // tr,),
        in_specs=[pl.BlockSpec((N_DEV, tr, C), lambda i: (0, i, 0)), row, row, row],
        out_specs=(row, row, row, row), compiler_params=_params(),
    )(gparts, w, m, v)


def _position():
    return lax.axis_index("x"), lax.axis_index("y"), lax.axis_index("c")


class _Copies:
    def __init__(self, arrays):
        self.arrays, self.n = list(arrays), len(arrays)
        self.scratch = [pltpu.SemaphoreType.DMA((7 * self.n,)), pltpu.SemaphoreType.DMA((7 * self.n,)),
                        pltpu.SemaphoreType.DMA((self.n,))]


class _Gather(_Copies):
    def __init__(self, arrays):
        super().__init__(arrays)
        self.out_shape = [jax.ShapeDtypeStruct((N_DEV,) + b.shape, b.dtype) for b in self.arrays]

    def phases(self, x_refs, out_refs, send_sems, recv_sems, local_sems):
        n = self.n
        x, y, c = _position()
        me, sibling = (x, y, c), (x, y, 1 - c)
        chips = [(1 - x, y), (x, 1 - y), (1 - x, 1 - y)]

        def copy(a, k, blk, to, own=False):
            slot = out_refs[a].at[4 * blk[0] + 2 * blk[1] + blk[2]]
            return pltpu.make_async_remote_copy(
                src_ref=x_refs[a] if own else slot, dst_ref=slot,
                send_sem=send_sems.at[7 * a + k], recv_sem=recv_sems.at[7 * a + k], device_id=to, device_id_type=MESH)

        mine = [pltpu.make_async_copy(x_refs[a], out_refs[a].at[4 * x + 2 * y + c], local_sems.at[a]) for a in range(n)]
        first = [copy(a, 0, me, sibling, own=True) for a in range(n)]
        first += [copy(a, 1 + j, me, (*chip, c), own=True) for a in range(n) for j, chip in enumerate(chips)]
        passed = [copy(a, 4 + j, (*chip, c), sibling) for j, chip in enumerate(chips) for a in range(n)]

        def start():
            for cp in mine + first:
                cp.start()

        def hand_on():
            for j, chip in enumerate(chips):
                for a in range(n):
                    copy(a, 1 + j, (*chip, c), me).wait_recv()
                    passed[j * n + a].start()

        def finish():
            for a in range(n):
                copy(a, 0, sibling, me).wait_recv()
            for j, chip in enumerate(chips):
                for a in range(n):
                    copy(a, 4 + j, (*chip, 1 - c), me).wait_recv()
            for cp in first + passed:
                cp.wait_send()
            for cp in mine:
                cp.wait()

        return start, hand_on, finish


class _Exchange(_Copies):
    def __init__(self, arrays):
        super().__init__(arrays)
        self.out_shape = [jax.ShapeDtypeStruct(p.shape, p.dtype) for p in self.arrays]

    def phases(self, p_refs, out_refs, send_sems, recv_sems, local_sems):
        n = self.n
        x, y, c = _position()
        my_slot = 4 * x + 2 * y + c
        mine = [pltpu.make_async_copy(p_refs[a].at[my_slot], out_refs[a].at[my_slot], local_sems.at[a]) for a in range(n)]
        copies = []
        for k in range(1, N_DEV):
            px, py, pc = x ^ (k >> 2), y ^ ((k >> 1) & 1), c ^ (k & 1)
            for a in range(n):
                copies.append(pltpu.make_async_remote_copy(
                    src_ref=p_refs[a].at[4 * px + 2 * py + pc], dst_ref=out_refs[a].at[my_slot],
                    send_sem=send_sems.at[7 * a + k - 1], recv_sem=recv_sems.at[7 * a + k - 1],
                    device_id=(px, py, pc), device_id_type=MESH))

        def start():
            for cp in mine + copies:
                cp.start()

        def finish():
            for cp in copies + mine:
                cp.wait()

        return start, None, finish


def _communicate(copies, name):
    n = copies.n

    def body(*refs):
        start, hand_on, finish = copies.phases(refs[:n], refs[n:2 * n], *refs[2 * n:])
        start()
        if hand_on is not None:
            hand_on()
        finish()

    return pl.pallas_call(body, name=name, out_shape=copies.out_shape, in_specs=[ANY] * n, out_specs=[ANY] * n,
                          scratch_shapes=copies.scratch)(*copies.arrays)


def _col_pieces(col_map, shard_w):
    pieces = []
    for lo, hi, dst in col_map:
        c = lo
        while c < hi:
            j = c // shard_w
            end = min(hi, (j + 1) * shard_w)
            pieces.append((j, c - j * shard_w, end - c, dst + (c - lo)))
            c = end
    return pieces


def _assemble_cols(shards, pieces, width, name):
    _, R, Cs = shards.shape
    tr = _tile(R, 128, 16)

    def body(s_ref, o_ref):
        o_ref[...] = jnp.zeros(o_ref.shape, o_ref.dtype)
        for j, lo, n, dst in pieces:
            o_ref[:, dst:dst + n] = s_ref[j, :, lo:lo + n]

    return pl.pallas_call(
        body, name=name, out_shape=jax.ShapeDtypeStruct((R, width), shards.dtype), grid=(R // tr,),
        in_specs=[pl.BlockSpec((N_DEV, tr, Cs), lambda i: (0, i, 0))],
        out_specs=pl.BlockSpec((tr, width), lambda i: (i, 0)), compiler_params=_params(),
    )(shards)


def _split_cols(full, pieces, shard_w, name):
    R, width = full.shape
    tr = _tile(R, 128, 16)

    def body(f_ref, o_ref):
        for j, lo, n, dst in pieces:
            o_ref[j, :, lo:lo + n] = f_ref[:, dst:dst + n]

    return pl.pallas_call(
        body, name=name, out_shape=jax.ShapeDtypeStruct((N_DEV, R, shard_w), full.dtype), grid=(R // tr,),
        in_specs=[pl.BlockSpec((tr, width), lambda i: (i, 0))],
        out_specs=pl.BlockSpec((N_DEV, tr, shard_w), lambda i: (0, i, 0)), compiler_params=_params(),
    )(full)


def _pack(arrays, row_multiple=8):
    flat, layout, off = [], [], 0
    for a in arrays:
        n = a.size
        padded = -(-n // LANE) * LANE
        f = a.reshape(-1).astype(F32)
        if padded != n:
            f = jnp.pad(f, (0, padded - n))
        flat.append(f)
        layout.append((off, n, a.shape))
        off += padded
    total = -(-off // (LANE * row_multiple)) * (LANE * row_multiple)
    if total != off:
        flat.append(jnp.zeros((total - off,), F32))
    return jnp.concatenate(flat).reshape(total // LANE, LANE), layout


def _unpack(buf, layout):
    flat = buf.reshape(-1)
    return [flat[off:off + n].reshape(shape) for off, n, shape in layout]


def _cols_to_full(g):
    return jnp.transpose(g, (1, 0, 2)).reshape(g.shape[1], N_DEV * g.shape[2])


def _full_to_cols(a):
    return jnp.transpose(a.reshape(a.shape[0], N_DEV, a.shape[1] // N_DEV), (1, 0, 2))


def _rows_to_full(g):
    return g.reshape(N_DEV * g.shape[1], g.shape[2])


def _full_to_rows(a):
    return a.reshape(N_DEV, a.shape[0] // N_DEV, a.shape[1])


def _pad_cols(a, width):
    return a if a.shape[-1] == width else jnp.pad(a, [(0, 0)] * (a.ndim - 1) + [(0, width - a.shape[-1])])


SHARDED = ("w_in", "conv_qkv", "pool_w", "w_branch_a", "w_branch_b", "w_mix_out", "w_xq", "w_xkv", "w_xo", "w_up",
           "ffn_conv_w", "w_down")
REPLICATED = ("mix_pre_norm", "a_log", "dt_bias", "gdn_norm", "pool_scale", "mix_post_norm", "xa_pre_norm", "mem_norm",
              "xa_post_norm", "ffn_pre_norm", "ffn_conv_b", "ffn_post_norm")
WEIGHTS = ("mix_pre_norm", "w_in", "conv_qkv", "a_log", "dt_bias", "gdn_norm", "pool_w", "pool_scale", "w_branch_a",
           "w_branch_b", "w_mix_out", "mix_post_norm", "xa_pre_norm", "mem_norm", "w_xq", "w_xkv", "w_xo", "xa_post_norm",
           "ffn_pre_norm", "w_up", "ffn_conv_w", "ffn_conv_b", "w_down", "ffn_post_norm")
MATMUL_WEIGHTS = ("w_in", "w_branch_a", "w_branch_b", "w_mix_out", "w_xq", "w_xkv", "w_xo", "w_up", "w_down")
COL_SHARDED = ("w_in", "w_branch_b", "w_xkv", "w_up", "conv_qkv", "ffn_conv_w")
ROW_SHARDED = ("w_branch_a", "w_mix_out", "w_xq", "w_xo", "w_down")


class _Layout:
    def __init__(self, D, H, pw, F):
        self.D, self.H, self.pw, self.F = D, H, pw, F
        self.qkv_w, self.vw = 3 * H * HD, H * HD
        self.ba_w = 512 if D >= 2048 else LANE
        self.Fp = -(-F // 512) * 512 if F >= 512 else F
        q, vw = self.qkv_w, self.vw
        self.seg = dict(qkv=(0, q), z=(q, vw), ga=(q + vw, D), gb=(q + vw + D, D), p=(q + vw + 2 * D, pw),
                        ba=(q + vw + 2 * D + pw, self.ba_w))
        self.in_w = q + vw + 2 * D + pw + self.ba_w
        o_z, o_b = q, q + vw
        o_p = o_b + 2 * H
        o_ga = o_p + pw
        o_gb = o_ga + D
        self.d_in = o_gb + D
        self.in_map = [(0, o_z, self.seg["qkv"][0]), (o_z, o_b, self.seg["z"][0]), (o_b, o_p, self.seg["ba"][0]),
                       (o_p, o_ga, self.seg["p"][0]), (o_ga, o_gb, self.seg["ga"][0]), (o_gb, self.d_in, self.seg["gb"][0])]
        self.up_map = [(0, F, 0), (F, 2 * F, self.Fp)]

    def col(self, name, width):
        return self.seg[name][0] // width


def _local_step(x, mem, target, P, L, comm=None):
    D, H, pw, F, Fp = L.D, L.H, L.pw, L.F, L.Fp
    qkv_w, vw, ba_w = L.qkv_w, L.vw, L.ba_w
    col = L.col
    P = dict(P)
    win_p, cw3_p, fb_p = P["win_p"], P["cw3_p"], P["fb_p"]
    conv_qkv, pool_w = P["conv_qkv"], P["pool_w"]
    lanes = lambda vec: jnp.pad(vec.reshape(1, H).astype(F32), ((0, 0), (H, LANE - 2 * H)))
    a_log_l, dt_bias_l = lanes(P["a_log"]), lanes(P["dt_bias"])
    bf = lambda name: P[name]
    vecf = lambda name: P[name].reshape(1, -1).astype(F32)
    g = {}

    def carried(call, name, *args, **kw):
        if comm is not None and name in comm.GATHERS:
            *out, got = call(*args, name, rider=comm.gather(name), **kw)
            P.update(comm.weights_from(name, got))
        elif comm is not None and name in comm.EXCHANGES:
            *out, got = call(*args, name, rider=comm.exchange(name, g), **kw)
            comm.receive(name, got)
        else:
            out = call(*args, name, **kw)
            out = [out] if call is _matmul else list(out[:-1])
        return out[0] if len(out) == 1 else out

    h1 = _prenorm(x, vecf("mix_pre_norm"), "mix_prenorm")
    proj = carried(_matmul, "in_proj", h1, win_p, "nn", BF16, tn=1536)
    qkv_hm = _qkv_conv(proj, conv_qkv, qkv_w, "qkv_conv")
    bg = _gates(proj, col("ba", LANE), a_log_l, dt_bias_l, H, "gates")
    o_hm, states = carried(_gdn_fwd, "gdn_fwd", qkv_hm, bg, H)
    wup_p, wdown_p = P["wup_p"], P["wdown_p"]
    oa = _gdn_out(o_hm, proj, col("z", vw), vecf("gdn_norm"), "gdn_out")
    ya = _matmul(oa, bf("w_branch_a"), "nn", BF16, "branch_a")
    pb = _pool_fwd(proj, col("p", pw), pool_w, vecf("pool_scale"), pw, "pool_fwd")
    yb = _matmul(pb, bf("w_branch_b"), "nn", BF16, "branch_b")
    merged = _merge(proj, col("ga", D), col("gb", D), ya, yb, "merge")
    y1 = _matmul(merged, bf("w_mix_out"), "nn", F32, "mix_out")
    x1, h2 = _post_pre(x, y1, vecf("mix_post_norm"), vecf("xa_pre_norm"), "mix_post")
    mn = _prenorm(mem, vecf("mem_norm"), "mem_norm")
    qx = _matmul(h2, bf("w_xq"), "nn", BF16, "xq")
    kv = _matmul(mn, bf("w_xkv"), "nn", BF16, "xkv")
    ox = _xattn_fwd(qx, kv, "xattn_fwd")
    y2 = _matmul(ox, bf("w_xo"), "nn", F32, "xo")
    x2, h3 = _post_pre(x1, y2, vecf("xa_post_norm"), vecf("ffn_pre_norm"), "xa_post")
    up = _matmul(h3, wup_p, "nn", BF16, "ffn_up")
    act = _ffn_act(up, cw3_p, fb_p, "ffn_act")
    y3 = _matmul(act, wdown_p, "nn", F32, "ffn_down")
    dx3, loss = _post_loss(x2, y3, vecf("ffn_post_norm"), target, "ffn_post_loss")

    dy3, g["ffn_post_norm"] = _post_bwd(y3, vecf("ffn_post_norm"), dx3, "ffn_post_bwd")
    dact = _matmul(dy3, wdown_p, "nt", BF16, "ffn_down_dx")
    g["w_down_p"] = _matmul(act, dy3, "tn", BF16, "ffn_down_dw", tk=4096)
    du, g["ffn_conv_w_p"], g["ffn_conv_b_p"] = _ffn_act_bwd(up, cw3_p, fb_p, dact, "ffn_act_bwd")
    dup = _conv_t(du, jnp.stack([cw3_p[:, :Fp], cw3_p[:, Fp:]]), "ffn_conv_t")
    dh3 = carried(_matmul, "ffn_up_dx", dup, wup_p, "nt", BF16)
    g["w_up_p"] = _matmul(h3, dup, "tn", BF16, "ffn_up_dw", tn=512, tk=4096)
    dx2, g["ffn_pre_norm"] = _pre_bwd(x2, vecf("ffn_pre_norm"), dh3, dx3, "ffn_pre_bwd")
    dy2, g["xa_post_norm"] = _post_bwd(y2, vecf("xa_post_norm"), dx2, "xa_post_bwd")
    dox = _matmul(dy2, bf("w_xo"), "nt", BF16, "xo_dx")
    g["w_xo"] = _matmul(ox, dy2, "tn", BF16, "xo_dw", tn=512, tk=4096)
    dqx, dkv = _xattn_bwd(qx, kv, dox, "xattn_bwd")
    dkv_b = dkv.astype(BF16)
    dh2 = _matmul(dqx, bf("w_xq"), "nt", BF16, "xq_dx")
    g["w_xq"] = _matmul(h2, dqx, "tn", BF16, "xq_dw", tn=512, tk=4096)
    dmn = _matmul(dkv_b, bf("w_xkv"), "nt", F32, "xkv_dx")
    g["w_xkv"] = _matmul(mn, dkv_b, "tn", BF16, "xkv_dw")
    _, g["mem_norm"] = _pre_bwd(mem, vecf("mem_norm"), dmn, jnp.zeros_like(mem), "mem_norm_bwd")
    dx1, g["xa_pre_norm"] = _pre_bwd(x1, vecf("xa_pre_norm"), dh2, dx2, "xa_pre_bwd")
    dy1, g["mix_post_norm"] = _post_bwd(y1, vecf("mix_post_norm"), dx1, "mix_post_bwd")
    dmerged = _matmul(dy1, bf("w_mix_out"), "nt", BF16, "mix_out_dx")
    g["w_mix_out"] = _matmul(merged, dy1, "tn", BF16, "mix_out_dw", tn=512, tk=4096)
    dya, dyb, dga, dgb = _merge_bwd(proj, col("ga", D), col("gb", D), ya, yb, dmerged, "merge_bwd")
    doa = _matmul(dya, bf("w_branch_a"), "nt", BF16, "branch_a_dx")
    g["w_branch_a"] = _matmul(oa, dya, "tn", BF16, "branch_a_dw", tn=512, tk=4096)
    dpb = _matmul(dyb, bf("w_branch_b"), "nt", BF16, "branch_b_dx")
    g["w_branch_b"] = _matmul(pb, dyb, "tn", BF16, "branch_b_dw")
    dp, g["pool_w"], g["pool_scale"] = _pool_bwd(proj, col("p", pw), pool_w, vecf("pool_scale"), dpb, pw, "pool_bwd")
    do_hm, dz, g["gdn_norm"] = _gdn_out_bwd(o_hm, proj, col("z", vw), vecf("gdn_norm"), doa, "gdn_out_bwd")
    dqkv_hm, dbg = carried(_gdn_bwd, "gdn_bwd", qkv_hm, bg, states, do_hm, H)
    dba, dal, ddt = _gates_bwd(proj, col("ba", LANE), ba_w, a_log_l, dt_bias_l, dbg, H, "gates_bwd")
    g["a_log"], g["dt_bias"] = dal[:, H:2 * H], ddt[:, H:2 * H]
    dc, g["conv_qkv"] = _qkv_conv_bwd(proj, conv_qkv, dqkv_hm, qkv_w, "qkv_conv_bwd")
    dqkv = _conv_t(dc[None], conv_qkv[None], "qkv_conv_t")[0]
    dproj = jnp.concatenate([dqkv, dz, dga, dgb, dp, dba], axis=1)
    g["w_in_p"] = carried(_matmul, "in_proj_dw", h1, dproj, "tn", BF16, tn=768, tk=4096)
    dh1 = carried(_matmul, "in_proj_dx", dproj, win_p, "nt", BF16, tk=4608)
    grad_x, g["mix_pre_norm"] = _pre_bwd(x, vecf("mix_pre_norm"), dh1, dx1, "mix_pre_bwd")
    return loss, grad_x, g


def _two_halves(a, F, Fp):
    return jnp.concatenate([_pad_cols(a[..., :F], Fp), _pad_cols(a[..., F:], Fp)], axis=-1)


def _from_halves(a, F, Fp):
    return jnp.concatenate([a[..., :F], a[..., Fp:Fp + F]], axis=-1)


class _StepComm:
    FIRST = ("w_in", "conv_qkv", "pool_w", "ffn_conv_w")
    GATHERS = {"in_proj": ("w_branch_a", "w_branch_b", "w_mix_out", "w_xq", "w_xo"),
               "gdn_fwd": ("w_xkv", "w_up", "w_down")}
    EXCHANGES = {"ffn_up_dx": ("w_down", "ffn_conv_w"),
                 "gdn_bwd": ("pool_w", "w_branch_a", "w_branch_b", "w_mix_out", "w_xq", "w_xkv", "w_xo"),
                 "in_proj_dw": ("w_up",),
                 "in_proj_dx": ("w_in", "conv_qkv")}

    def __init__(self, w, L):
        self.w, self.L = w, L
        self.in_pieces = _col_pieces(L.in_map, w["w_in"].shape[1])
        self.up_pieces = _col_pieces(L.up_map, w["w_up"].shape[1])
        self.received = {}

    def _shard(self, n):
        return self.w[n].astype(BF16) if n in MATMUL_WEIGHTS else self.w[n]

    def first_weights(self):
        L, (g, r, c) = self.L, self.w["pool_w"].shape
        G = dict(zip(self.FIRST, _communicate(_Gather([self._shard(n) for n in self.FIRST]), "gather_first")))
        return {"win_p": _assemble_cols(G["w_in"], self.in_pieces, L.in_w, "assemble_w_in"),
                "conv_qkv": _cols_to_full(G["conv_qkv"]),
                "cw3_p": _two_halves(_cols_to_full(G["ffn_conv_w"]), L.F, L.Fp),
                "pool_w": jnp.transpose(G["pool_w"], (1, 0, 2, 3)).reshape(g, N_DEV * r, c).astype(BF16)}

    def gather(self, call):
        return _Gather([self._shard(n) for n in self.GATHERS[call]])

    def weights_from(self, call, results):
        L, P = self.L, {}
        for n, shards in zip(self.GATHERS[call], results):
            if n == "w_up":
                P["wup_p"] = _assemble_cols(shards, self.up_pieces, 2 * L.Fp, "assemble_w_up")
            elif n == "w_down":
                P["wdown_p"] = jnp.pad(_rows_to_full(shards), ((0, L.Fp - L.F), (0, 0)))
            else:
                P[n] = _rows_to_full(shards) if n in ROW_SHARDED else _cols_to_full(shards)
        return P

    def _slices(self, g, n):
        L, w = self.L, self.w
        if n == "w_in":
            return _split_cols(g["w_in_p"], self.in_pieces, w["w_in"].shape[1], "split_w_in")
        if n == "w_up":
            return _split_cols(g["w_up_p"], self.up_pieces, w["w_up"].shape[1], "split_w_up")
        if n == "w_down":
            return _full_to_rows(g["w_down_p"][:L.F])
        if n == "ffn_conv_w":
            return _full_to_cols(_from_halves(g["ffn_conv_w_p"], L.F, L.Fp))
        if n == "pool_w":
            grp, r, c = w["pool_w"].shape
            return jnp.transpose(g[n].reshape(grp, N_DEV, r, c), (1, 0, 2, 3)).reshape(N_DEV, grp * r, c)
        return _full_to_rows(g[n]) if n in ROW_SHARDED else _full_to_cols(g[n])

    def exchange(self, call, g):
        return _Exchange([self._slices(g, n) for n in self.EXCHANGES[call]])

    def receive(self, call, results):
        self.received.update(zip(self.EXCHANGES[call], results))


def kernel(x, mem, mix_pre_norm, w_in, conv_qkv, a_log, dt_bias, gdn_norm, pool_w, pool_scale, w_branch_a, w_branch_b, w_mix_out, mix_post_norm, xa_pre_norm, mem_norm, w_xq, w_xkv, w_xo, xa_post_norm, ffn_pre_norm, w_up, ffn_conv_w, ffn_conv_b, w_down, ffn_post_norm, loss_target, m_mix_pre_norm, m_w_in, m_conv_qkv, m_a_log, m_dt_bias, m_gdn_norm, m_pool_w, m_pool_scale, m_w_branch_a, m_w_branch_b, m_w_mix_out, m_mix_post_norm, m_xa_pre_norm, m_mem_norm, m_w_xq, m_w_xkv, m_w_xo, m_xa_post_norm, m_ffn_pre_norm, m_w_up, m_ffn_conv_w, m_ffn_conv_b, m_w_down, m_ffn_post_norm, v_mix_pre_norm, v_w_in, v_conv_qkv, v_a_log, v_dt_bias, v_gdn_norm, v_pool_w, v_pool_scale, v_w_branch_a, v_w_branch_b, v_w_mix_out, v_mix_post_norm, v_xa_pre_norm, v_mem_norm, v_w_xq, v_w_xkv, v_w_xo, v_xa_post_norm, v_ffn_pre_norm, v_w_up, v_ffn_conv_w, v_ffn_conv_b, v_w_down, v_ffn_post_norm):
    given = dict(locals())
    w = {n: given[n][0] for n in WEIGHTS}
    m = {n: given["m_" + n][0] for n in WEIGHTS}
    v = {n: given["v_" + n][0] for n in WEIGHTS}
    D = x.shape[-1]
    F = w["w_down"].shape[0] * N_DEV
    L = _Layout(D, w["a_log"].shape[-1], w["pool_scale"].shape[-1], F)
    Fp = L.Fp

    comm = _StepComm(w, L)
    P = {n: w[n] for n in REPLICATED}
    P.update(comm.first_weights())
    P["fb_p"] = _two_halves(w["ffn_conv_b"].reshape(1, 2 * F), F, Fp)
    loss, grad_x, g = _local_step(x[0], mem[0], loss_target[0], P, L, comm)

    received = comm.received
    outs = {}
    for n in SHARDED:
        as2d = lambda a: a.reshape(-1, a.shape[-1])
        res = _adamw(received[n], as2d(w[n]), as2d(m[n]), as2d(v[n]), "adamw_" + n)
        outs[n] = [r.reshape(w[n].shape) for r in res]

    g["ffn_conv_b"] = _from_halves(g["ffn_conv_b_p"], F, Fp)
    rep_parts, rep_layout = _pack([g[n].reshape(w[n].shape) for n in REPLICATED] + [loss])
    rep_all, = _communicate(_Gather([rep_parts]), "gather_small_grads")
    zero_loss = jnp.zeros_like(loss)
    wr, _ = _pack([w[n] for n in REPLICATED] + [zero_loss])
    mr, _ = _pack([m[n] for n in REPLICATED] + [zero_loss])
    vr, _ = _pack([v[n] for n in REPLICATED] + [zero_loss])
    outs_rep = [_unpack(o, rep_layout) for o in _adamw(rep_all, wr, mr, vr, "adamw_replicated")]
    loss_total = outs_rep[0][-1][0, 0]
    for i, n in enumerate(REPLICATED):
        outs[n] = [outs_rep[k][i] for k in range(4)]

    result = [loss_total, grad_x[None]]
    for k in range(4):
        for n in WEIGHTS:
            result.append(outs[n][k][None])
    return tuple(result)
```

```python
import functools

import jax
import jax.numpy as jnp
from jax import lax
from jax.experimental import pallas as pl
from jax.experimental.pallas import tpu as pltpu

F32, BF16 = jnp.float32, jnp.bfloat16
MESH = pl.DeviceIdType.MESH
ANY = pl.BlockSpec(memory_space=pl.ANY)

N_DEV = 8
EPS = 1e-6
CHUNK = 64
HD = 128
GDN_CONV = 4
FFN_CONV = 3
POOL_WINDOWS = (2, 4, 8, 16)
XA_HEADS = 4
HALO = 16
RC = 128
SEQ_TILE = 1024
LANE = 128
VMEM_LIMIT = 48 * 1024 * 1024

ADAM_LR, ADAM_B1, ADAM_B2, ADAM_EPS, ADAM_WD, ADAM_STEP = 0.001, 0.9, 0.999, 1e-08, 0.01, 10


def _tile(n, pref, align=LANE):
    best = None
    t = align
    while t <= min(n, pref):
        if n % t == 0:
            best = t
        t += align
    return best if best is not None else n


def _params(**kw):
    return pltpu.CompilerParams(vmem_limit_bytes=VMEM_LIMIT, **kw)


def _sigmoid(x):
    return 1.0 / (1.0 + jnp.exp(-x))


def _silu(x):
    return x * _sigmoid(x)


def _dsilu(x):
    s = _sigmoid(x)
    return s * (1.0 + x * (1.0 - s))


def _colsum8(t):
    return t.reshape(t.shape[0] // 8, 8, t.shape[1]).sum(axis=0)


def _ride(body, n_in, n_out, rider, first, middle, last):
    if rider is None:
        return body
    n = rider.n

    def wrapped(*refs):
        ins, r_in = refs[:n_in], refs[n_in:n_in + n]
        outs, r_out = refs[n_in + n:n_in + n + n_out], refs[n_in + n + n_out:n_in + 2 * n + n_out]
        rest = refs[n_in + 2 * n + n_out:]
        start, hand_on, finish = rider.phases(r_in, r_out, *rest[-3:])
        pl.when(first())(start)
        body(*ins, *outs, *rest[:-3])
        if hand_on is not None:
            pl.when(middle())(hand_on)
        pl.when(last())(finish)

    return wrapped


def _matmul(a, b, mode, out_dtype, name, tm=1024, tn=1024, tk=2816, rider=None):
    ga = a.shape[0] if (mode == "nt" and a.ndim == 3) else 1
    gb = b.shape[0] if (mode == "tn" and b.ndim == 3) else 1
    if mode == "nn":
        (M, K), (K2, N) = a.shape, b.shape
    elif mode == "nt":
        M, K = a.shape[-2], ga * a.shape[-1]
        N, K2 = b.shape
    else:
        K, M = a.shape
        K2, N = b.shape[-2], gb * b.shape[-1]
    assert K == K2, (name, a.shape, b.shape)
    tm, tn = _tile(M, tm), _tile(N // gb, tn)
    tk = K // ga if K // ga <= tk else _tile(K // ga, tk)
    nk = K // tk
    kpg, npg = K // ga // tk, N // gb // tn
    if mode == "nn":
        a_spec = pl.BlockSpec((tm, tk), lambda i, j, k: (i, k))
        b_spec = pl.BlockSpec((tk, tn), lambda i, j, k: (k, j))
        dims = (((1,), (0,)), ((), ()))
    elif mode == "nt":
        a_spec = (pl.BlockSpec((tm, tk), lambda i, j, k: (i, k)) if a.ndim == 2 else
                  pl.BlockSpec((None, tm, tk), lambda i, j, k: (k // kpg, i, k % kpg)))
        b_spec = pl.BlockSpec((tn, tk), lambda i, j, k: (j, k))
        dims = (((1,), (1,)), ((), ()))
    else:
        a_spec = pl.BlockSpec((tk, tm), lambda i, j, k: (k, i))
        b_spec = (pl.BlockSpec((tk, tn), lambda i, j, k: (k, j)) if b.ndim == 2 else
                  pl.BlockSpec((None, tk, tn), lambda i, j, k: (j // npg, k, j % npg)))
        dims = (((0,), (0,)), ((), ()))

    def body(a_ref, b_ref, o_ref, acc):
        part = lax.dot_general(a_ref[...], b_ref[...], dims, preferred_element_type=F32)
        if nk == 1:
            o_ref[...] = part.astype(o_ref.dtype)
        else:
            k = pl.program_id(2)

            @pl.when(k == 0)
            def _():
                acc[...] = part

            @pl.when(k > 0)
            def _():
                acc[...] += part

            @pl.when(k == nk - 1)
            def _():
                o_ref[...] = acc[...].astype(o_ref.dtype)

    grid = (M // tm, N // tn, nk)
    at = lambda step: lambda: ((pl.program_id(0) == step[0]) & (pl.program_id(1) == step[1])
                               & (pl.program_id(2) == step[2]))
    extra = rider.n if rider is not None else 0
    res = pl.pallas_call(
        _ride(body, 2, 1, rider, at((0, 0, 0)), at((grid[0] // 2, 0, 0)), at((grid[0] - 1, grid[1] - 1, nk - 1))),
        name=name, out_shape=[jax.ShapeDtypeStruct((M, N), out_dtype)] + (rider.out_shape if rider else []),
        grid=grid, in_specs=[a_spec, b_spec] + [ANY] * extra,
        out_specs=[pl.BlockSpec((tm, tn), lambda i, j, k: (i, j))] + [ANY] * extra,
        scratch_shapes=[pltpu.VMEM((tm, tn) if nk > 1 else (8, LANE), F32)] + (rider.scratch if rider else []),
        compiler_params=_params(dimension_semantics=("arbitrary",) * 3 if rider else ("parallel", "parallel", "arbitrary")),
    )(a, b, *(rider.arrays if rider else []))
    return (res[0], res[1:]) if rider else res[0]


def _rstd(xf):
    return lax.rsqrt(jnp.mean(xf * xf, axis=-1, keepdims=True) + EPS)


def _rms_bwd(xf, w, dy):
    r = _rstd(xf)
    g = dy * w
    dx = r * g - xf * (r * r * r) * jnp.mean(g * xf, axis=-1, keepdims=True)
    return dx, dy * xf * r


def _row_tile(rows):
    return _tile(rows, 512, 8)


def _prenorm(x, w, name):
    rows, d = x.shape
    ts = _row_tile(rows)

    def body(x_ref, w_ref, h_ref):
        xf = x_ref[...]
        h_ref[...] = (xf * _rstd(xf) * w_ref[...]).astype(BF16)

    return pl.pallas_call(
        body, name=name, out_shape=jax.ShapeDtypeStruct((rows, d), BF16), grid=(rows // ts,),
        in_specs=[pl.BlockSpec((ts, d), lambda i: (i, 0)), pl.BlockSpec((1, d), lambda i: (0, 0))],
        out_specs=pl.BlockSpec((ts, d), lambda i: (i, 0)), compiler_params=_params(),
    )(x, w)


def _post_pre(xres, y, w_post, w_pre, name):
    rows, d = xres.shape
    ts = _row_tile(rows)

    def body(x_ref, y_ref, wp_ref, wn_ref, xo_ref, h_ref):
        yf = y_ref[...]
        xn = x_ref[...] + yf * _rstd(yf) * wp_ref[...]
        xo_ref[...] = xn
        h_ref[...] = (xn * _rstd(xn) * wn_ref[...]).astype(BF16)

    row = pl.BlockSpec((ts, d), lambda i: (i, 0))
    vec = pl.BlockSpec((1, d), lambda i: (0, 0))
    return pl.pallas_call(
        body, name=name, grid=(rows // ts,),
        out_shape=(jax.ShapeDtypeStruct((rows, d), F32), jax.ShapeDtypeStruct((rows, d), BF16)),
        in_specs=[row, row, vec, vec], out_specs=(row, row), compiler_params=_params(),
    )(xres, y, w_post, w_pre)


def _post_loss(xres, y, w_post, target, name):
    rows, d = xres.shape
    ts = _row_tile(rows)
    n = rows // ts

    def body(x_ref, y_ref, wp_ref, t_ref, dx_ref, loss_ref, acc):
        i = pl.program_id(0)
        yf = y_ref[...]
        diff = x_ref[...] + yf * _rstd(yf) * wp_ref[...] - t_ref[...]
        dx_ref[...] = diff * (1.0 / d)

        @pl.when(i == 0)
        def _():
            acc[...] = jnp.zeros_like(acc)

        acc[...] += _colsum8(diff * diff)

        @pl.when(i == n - 1)
        def _():
            loss_ref[...] = jnp.broadcast_to((0.5 / d) * jnp.sum(acc[...]), loss_ref.shape)

    row = pl.BlockSpec((ts, d), lambda i: (i, 0))
    vec = pl.BlockSpec((1, d), lambda i: (0, 0))
    return pl.pallas_call(
        body, name=name, grid=(n,),
        out_shape=(jax.ShapeDtypeStruct((rows, d), F32), jax.ShapeDtypeStruct((1, LANE), F32)),
        in_specs=[row, row, vec, row], out_specs=(row, pl.BlockSpec((1, LANE), lambda i: (0, 0))),
        scratch_shapes=[pltpu.VMEM((8, d), F32)], compiler_params=_params(),
    )(xres, y, w_post, target)


def _post_bwd(y, w_post, dxn, name):
    rows, d = y.shape
    ts = _row_tile(rows)
    n = rows // ts

    def body(y_ref, w_ref, d_ref, dy_ref, dw_ref, acc):
        i = pl.program_id(0)
        dy, dwr = _rms_bwd(y_ref[...], w_ref[...], d_ref[...])
        dy_ref[...] = dy.astype(BF16)

        @pl.when(i == 0)
        def _():
            acc[...] = jnp.zeros_like(acc)

        acc[...] += _colsum8(dwr)

        @pl.when(i == n - 1)
        def _():
            dw_ref[...] = jnp.sum(acc[...], axis=0, keepdims=True)

    row = pl.BlockSpec((ts, d), lambda i: (i, 0))
    vec = pl.BlockSpec((1, d), lambda i: (0, 0))
    return pl.pallas_call(
        body, name=name, grid=(n,),
        out_shape=(jax.ShapeDtypeStruct((rows, d), BF16), jax.ShapeDtypeStruct((1, d), F32)),
        in_specs=[row, vec, row], out_specs=(row, vec),
        scratch_shapes=[pltpu.VMEM((8, d), F32)], compiler_params=_params(),
    )(y, w_post, dxn)


def _pre_bwd(x, w_pre, dh, dres, name):
    rows, d = x.shape
    ts = _row_tile(rows)
    n = rows // ts

    def body(x_ref, w_ref, dh_ref, dr_ref, dx_ref, dw_ref, acc):
        i = pl.program_id(0)
        dx, dwr = _rms_bwd(x_ref[...], w_ref[...], dh_ref[...].astype(F32))
        dx_ref[...] = dr_ref[...] + dx

        @pl.when(i == 0)
        def _():
            acc[...] = jnp.zeros_like(acc)

        acc[...] += _colsum8(dwr)

        @pl.when(i == n - 1)
        def _():
            dw_ref[...] = jnp.sum(acc[...], axis=0, keepdims=True)

    row = pl.BlockSpec((ts, d), lambda i: (i, 0))
    vec = pl.BlockSpec((1, d), lambda i: (0, 0))
    return pl.pallas_call(
        body, name=name, grid=(n,),
        out_shape=(jax.ShapeDtypeStruct((rows, d), F32), jax.ShapeDtypeStruct((1, d), F32)),
        in_specs=[row, vec, row, row], out_specs=(row, vec),
        scratch_shapes=[pltpu.VMEM((8, d), F32)], compiler_params=_params(),
    )(x, w_pre, dh, dres)


def _prev_halo_spec(ts, cw, col0=0):
    return pl.BlockSpec((HALO, cw), lambda i, j: (jnp.maximum(i * (ts // HALO) - 1, 0), j + col0))


def _fill_causal(ext, tile_f32, halo_f32, i):
    ext[pl.ds(0, HALO), :] = jnp.where(i > 0, halo_f32, 0.0)
    ext[pl.ds(HALO, tile_f32.shape[0]), :] = tile_f32


def _row_chunks(ts, chunk):
    assert ts % RC == 0, (ts, RC)

    def step(c, carry):
        chunk(pl.multiple_of(c * RC, RC))
        return carry

    lax.fori_loop(0, ts // RC, step, 0)


def _causal_taps(ext, r0, K):
    blk = ext[pl.ds(r0 + HALO - 8, RC + 8), :]
    return [blk[8 - j:8 - j + RC] for j in range(K)]


def _advanced_taps(ext, r0, K):
    blk = ext[pl.ds(r0, RC + 8), :]
    return [blk[j:j + RC] for j in range(K)]


def _filter(wv, taps):
    K = len(taps)
    acc = wv[K - 1:K, :] * taps[0]
    for t in range(K - 1):
        acc = acc + wv[t:t + 1, :] * taps[K - 1 - t]
    return acc


def _conv_t(dc, w, name, col_tile=512):
    G, S, C = dc.shape
    K = w.shape[1]
    ts, cw = _tile(S, SEQ_TILE, HALO), _tile(C, col_tile)
    n = S // ts

    def body(d_ref, nx_ref, w_ref, o_ref, ext):
        i = pl.program_id(1)
        ext[pl.ds(0, ts), :] = d_ref[...].astype(F32)
        ext[pl.ds(ts, HALO), :] = jnp.where(i < n - 1, nx_ref[...].astype(F32), 0.0)
        wv = w_ref[...]

        def chunk(r0):
            taps = _advanced_taps(ext, r0, K)
            acc = wv[K - 1:K, :] * taps[0]
            for j in range(K - 1):
                acc = acc + wv[j:j + 1, :] * taps[K - 1 - j]
            o_ref[pl.ds(r0, RC), :] = acc.astype(o_ref.dtype)

        _row_chunks(ts, chunk)

    return pl.pallas_call(
        body, name=name, out_shape=jax.ShapeDtypeStruct((G, S, C), BF16), grid=(G, n, C // cw),
        in_specs=[pl.BlockSpec((None, ts, cw), lambda g, i, j: (g, i, j)),
                  pl.BlockSpec((None, HALO, cw),
                               lambda g, i, j: (g, jnp.minimum((i + 1) * (ts // HALO), S // HALO - 1), j)),
                  pl.BlockSpec((None, K, cw), lambda g, i, j: (g, 0, j))],
        out_specs=pl.BlockSpec((None, ts, cw), lambda g, i, j: (g, i, j)),
        scratch_shapes=[pltpu.VMEM((ts + HALO, cw), F32)], compiler_params=_params(),
    )(dc, dc, w)


def _qkv_conv(proj, conv_w, qkv_w, name):
    S = proj.shape[0]
    H3 = qkv_w // HD
    H = H3 // 3
    hb = 4 if H % 4 == 0 else 1
    cw = hb * HD
    ts = _tile(S, SEQ_TILE, HALO)
    per_kind = H // hb

    def body(x_ref, h_ref, w_ref, o_ref, ext):
        i, j = pl.program_id(0), pl.program_id(1)
        _fill_causal(ext, x_ref[...].astype(F32), h_ref[...].astype(F32), i)
        wv = w_ref[...]
        kind = j // per_kind
        scale = jnp.where(kind == 0, HD ** -0.5, 1.0)

        def chunk(r0):
            s = _silu(_filter(wv, _causal_taps(ext, r0, GDN_CONV)))
            for a in range(hb):
                sa = s[:, HD * a:HD * (a + 1)]
                r = lax.rsqrt(jnp.sum(sa * sa, axis=-1, keepdims=True) + EPS)
                o_ref[a, pl.ds(r0, RC), :] = jnp.where(kind == 2, sa, sa * r * scale)

        _row_chunks(ts, chunk)

    return pl.pallas_call(
        body, name=name, out_shape=jax.ShapeDtypeStruct((H3, S, HD), F32), grid=(S // ts, qkv_w // cw),
        in_specs=[pl.BlockSpec((ts, cw), lambda i, j: (i, j)), _prev_halo_spec(ts, cw),
                  pl.BlockSpec((GDN_CONV, cw), lambda i, j: (0, j))],
        out_specs=pl.BlockSpec((hb, ts, HD), lambda i, j: (j, i, 0)),
        scratch_shapes=[pltpu.VMEM((ts + HALO, cw), F32)], compiler_params=_params(),
    )(proj, proj, conv_w)


def _qkv_conv_bwd(proj, conv_w, dqkv_hm, qkv_w, name):
    S = proj.shape[0]
    H = qkv_w // HD // 3
    hb = 4 if H % 4 == 0 else 1
    cw = hb * HD
    ts = _tile(S, SEQ_TILE, HALO)
    n = S // ts
    per_kind = H // hb

    def body(x_ref, h_ref, w_ref, d_ref, dc_ref, dw_ref, ext, acc):
        j, i = pl.program_id(0), pl.program_id(1)
        _fill_causal(ext, x_ref[...].astype(F32), h_ref[...].astype(F32), i)
        wv = w_ref[...]
        kind = j // per_kind
        scale = jnp.where(kind == 0, HD ** -0.5, 1.0)

        @pl.when(i == 0)
        def _():
            acc[...] = jnp.zeros_like(acc)

        def chunk(r0):
            taps = _causal_taps(ext, r0, GDN_CONV)
            c = _filter(wv, taps)
            s = _silu(c)
            parts = []
            for a in range(hb):
                sa = s[:, HD * a:HD * (a + 1)]
                dy = d_ref[a, pl.ds(r0, RC), :]
                r = lax.rsqrt(jnp.sum(sa * sa, axis=-1, keepdims=True) + EPS)
                dn = scale * (r * dy - sa * (r * r * r) * jnp.sum(dy * sa, axis=-1, keepdims=True))
                parts.append(jnp.where(kind == 2, dy, dn))
            dc = jnp.concatenate(parts, axis=1) * _dsilu(c)
            dc_ref[pl.ds(r0, RC), :] = dc.astype(BF16)
            for t in range(GDN_CONV):
                acc[t] += _colsum8(dc * taps[GDN_CONV - 1 - t])

        _row_chunks(ts, chunk)

        @pl.when(i == n - 1)
        def _():
            dw_ref[...] = jnp.sum(acc[...], axis=1)

    return pl.pallas_call(
        body, name=name, grid=(qkv_w // cw, n),
        out_shape=(jax.ShapeDtypeStruct((S, qkv_w), BF16), jax.ShapeDtypeStruct((GDN_CONV, qkv_w), F32)),
        in_specs=[pl.BlockSpec((ts, cw), lambda j, i: (i, j)),
                  pl.BlockSpec((HALO, cw), lambda j, i: (jnp.maximum(i * (ts // HALO) - 1, 0), j)),
                  pl.BlockSpec((GDN_CONV, cw), lambda j, i: (0, j)),
                  pl.BlockSpec((hb, ts, HD), lambda j, i: (j, i, 0))],
        out_specs=(pl.BlockSpec((ts, cw), lambda j, i: (i, j)), pl.BlockSpec((GDN_CONV, cw), lambda j, i: (0, j))),
        scratch_shapes=[pltpu.VMEM((ts + HALO, cw), F32), pltpu.VMEM((GDN_CONV, 8, cw), F32)],
        compiler_params=_params(),
    )(proj, proj, conv_w, dqkv_hm)


def _chunk_cumsum(x):
    row = lax.broadcasted_iota(jnp.int32, x.shape, 0) & (CHUNK - 1)
    s = 1
    while s < CHUNK:
        x = x + jnp.where(row >= s, pltpu.roll(x, s, axis=0), 0.0)
        s *= 2
    return x


def _chunk_rev_cumsum(x):
    rows = x.shape[0]
    row = lax.broadcasted_iota(jnp.int32, x.shape, 0) & (CHUNK - 1)
    s = 1
    while s < CHUNK:
        x = x + jnp.where(row < CHUNK - s, pltpu.roll(x, rows - s, axis=0), 0.0)
        s *= 2
    return x


def _softplus(x):
    return jnp.maximum(x, 0.0) + jnp.log1p(jnp.exp(-jnp.abs(x)))


def _gates(proj, ba_col, a_log_l, dt_bias_l, H, name):
    S = proj.shape[0]
    ts = _tile(S, 512, CHUNK)

    def body(x_ref, al_ref, dt_ref, o_ref):
        x = x_ref[...].astype(F32)
        lane = lax.broadcasted_iota(jnp.int32, x.shape, 1)
        g = -jnp.exp(al_ref[...]) * _softplus(x + dt_ref[...])
        G = _chunk_cumsum(jnp.where((lane >= H) & (lane < 2 * H), g, 0.0))
        o_ref[...] = jnp.where(lane < H, _sigmoid(x), G)

    return pl.pallas_call(
        body, name=name, out_shape=jax.ShapeDtypeStruct((S, LANE), F32), grid=(S // ts,),
        in_specs=[pl.BlockSpec((ts, LANE), lambda i: (i, ba_col)), pl.BlockSpec((1, LANE), lambda i: (0, 0)),
                  pl.BlockSpec((1, LANE), lambda i: (0, 0))],
        out_specs=pl.BlockSpec((ts, LANE), lambda i: (i, 0)), compiler_params=_params(),
    )(proj, a_log_l, dt_bias_l)


def _gates_bwd(proj, ba_col, ba_w, a_log_l, dt_bias_l, dbg, H, name):
    S = proj.shape[0]
    ts = _tile(S, 512, CHUNK)
    n = S // ts

    def body(x_ref, al_ref, dt_ref, d_ref, o_ref, dal_ref, ddt_ref, acc):
        i = pl.program_id(0)
        x = x_ref[...].astype(F32)
        d = d_ref[...]
        lane = lax.broadcasted_iota(jnp.int32, x.shape, 1)
        is_a = (lane >= H) & (lane < 2 * H)
        beta = _sigmoid(x)
        nea = -jnp.exp(al_ref[...])
        z = x + dt_ref[...]
        dg = _chunk_rev_cumsum(jnp.where(is_a, d, 0.0))
        da_raw = dg * nea * _sigmoid(z)
        o = jnp.where(lane < H, d * beta * (1.0 - beta), jnp.where(is_a, da_raw, 0.0))
        if ba_w > LANE:
            o = jnp.concatenate([o, jnp.zeros((ts, ba_w - LANE), F32)], axis=1)
        o_ref[...] = o.astype(BF16)

        @pl.when(i == 0)
        def _():
            acc[...] = jnp.zeros_like(acc)

        acc[0] += _colsum8(jnp.where(is_a, dg * nea * _softplus(z), 0.0))
        acc[1] += _colsum8(jnp.where(is_a, da_raw, 0.0))

        @pl.when(i == n - 1)
        def _():
            dal_ref[...] = jnp.sum(acc[0], axis=0, keepdims=True)
            ddt_ref[...] = jnp.sum(acc[1], axis=0, keepdims=True)

    vec = pl.BlockSpec((1, LANE), lambda i: (0, 0))
    return pl.pallas_call(
        body, name=name, grid=(n,),
        out_shape=(jax.ShapeDtypeStruct((S, ba_w), BF16), jax.ShapeDtypeStruct((1, LANE), F32),
                   jax.ShapeDtypeStruct((1, LANE), F32)),
        in_specs=[pl.BlockSpec((ts, LANE), lambda i: (i, ba_col)), vec, vec, pl.BlockSpec((ts, LANE), lambda i: (i, 0))],
        out_specs=(pl.BlockSpec((ts, ba_w), lambda i: (i, 0)), vec, vec),
        scratch_shapes=[pltpu.VMEM((2, 8, LANE), F32)], compiler_params=_params(),
    )(proj, a_log_l, dt_bias_l, dbg)


_BMM_FORMS = {"nn": "hik,hkj->hij", "nt": "hik,hjk->hij", "tn": "hki,hkj->hij"}


def _split_bf16(a):
    hi = a.astype(BF16)
    return hi, (a - hi.astype(F32)).astype(BF16)


def _bmm(a, b, form="nn", exact=False):
    e = lambda x, y: jnp.einsum(_BMM_FORMS[form], x, y, preferred_element_type=F32)
    if not exact:
        return e(a.astype(BF16), b.astype(BF16))
    (ah, al), (bh, bl) = _split_bf16(a), _split_bf16(b)
    return (e(ah, bl) + e(al, bh)) + e(ah, bh)


def _unit_lower_inverse(L, r, c):
    eye = (r == c).astype(F32)
    m = jnp.where((r >> 3) == (c >> 3), -L, 0.0)
    m2 = _bmm(m, m, exact=True)
    m4 = _bmm(m2, m2, exact=True)
    x = eye + m
    x = x + _bmm(x, m2, exact=True)
    x = x + _bmm(x, m4, exact=True)
    for sh in (3, 4, 5):
        off = ((r >> (sh + 1)) == (c >> (sh + 1))) & ((r >> sh) != (c >> sh))
        x = x - _bmm(x, _bmm(jnp.where(off, L, 0.0), x))
    return x


def _to_row(col, eye):
    return jnp.sum(jnp.where(eye, jnp.broadcast_to(col, eye.shape), 0.0), axis=1, keepdims=True)


def _to_col(rowv, eye):
    return jnp.sum(jnp.where(eye, jnp.broadcast_to(rowv, eye.shape), 0.0), axis=2, keepdims=True)


def _gdn_chunk(q, k, v, bg, H, P=1):
    shape = (H * P, CHUNK, CHUNK)
    r = lax.broadcasted_iota(jnp.int32, shape, 1)
    c = lax.broadcasted_iota(jnp.int32, shape, 2)
    eye, incl, strict = r == c, r >= c, r > c
    rows = lambda p: slice(p * CHUNK, (p + 1) * CHUNK)
    beta = jnp.stack([bg[rows(p), h:h + 1] for h in range(H) for p in range(P)], axis=0)
    G = jnp.stack([bg[rows(p), H + h:H + h + 1] for h in range(H) for p in range(P)], axis=0)
    gap = jnp.broadcast_to(G, shape) - _to_row(G, eye)
    decay = jnp.where(incl, jnp.exp(jnp.where(incl, gap, 0.0)), 0.0)
    kk = _bmm(k, k, "nt")
    L = jnp.where(strict, beta * decay * kk, 0.0)
    ainv = _unit_lower_inverse(L, r, c)
    eG = jnp.exp(G)
    u_v = _bmm(ainv, beta * v)
    w_k = _bmm(ainv, (beta * eG) * k)
    qk = _bmm(q, k, "nt", exact=True)
    GL = G[:, CHUNK - 1:CHUNK, :]
    ek = jnp.exp(GL - G)
    return dict(eye=eye, strict=strict, r=r, c=c, beta=beta, G=G, decay=decay, kk=kk, ainv=ainv, eG=eG,
                u_v=u_v, w_k=w_k, qk=qk, attn=decay * qk, GL=GL, ek=ek, cd=jnp.exp(GL))


def _chunk_steps(N):
    at = lambda step: lambda: pl.program_id(0) == step
    return at(0), at(N - max(N // 8, 1)), at(N - 1)


def _gdn_fwd(qkv_hm, bg, H, name, rider=None):
    S = qkv_hm.shape[1]
    N = S // CHUNK
    P = 2 if N % 2 == 0 else 1
    extra = rider.n if rider is not None else 0

    def body(q_ref, k_ref, v_ref, bg_ref, o_ref, st_ref, state):
        n = pl.program_id(0)

        @pl.when(n == 0)
        def _():
            state[...] = jnp.zeros_like(state)

        split = lambda x: x.reshape(H * P, CHUNK, HD)
        pick = lambda x, p: x.reshape((H, P) + x.shape[1:])[:, p]
        q, k, v = split(q_ref[...]), split(k_ref[...]), split(v_ref[...])
        t = _gdn_chunk(q, k, v, bg_ref[...], H, P)
        qdec, kdec = q * t["eG"], k * t["ek"]
        s0 = state[...]
        outs = []
        for p in range(P):
            st_ref[p] = s0
            u = pick(t["u_v"], p) - _bmm(pick(t["w_k"], p), s0)
            outs.append(_bmm(pick(qdec, p), s0) + _bmm(pick(t["attn"], p), u))
            s0 = pick(t["cd"], p) * s0 + _bmm(pick(kdec, p), u, "tn")
        state[...] = s0
        o_ref[...] = jnp.concatenate(outs, axis=1)

    blk = lambda kind: pl.BlockSpec((H, P * CHUNK, HD), lambda n: (kind, n, 0))
    res = pl.pallas_call(
        _ride(body, 4, 2, rider, *_chunk_steps(N // P)), name=name, grid=(N // P,),
        out_shape=[jax.ShapeDtypeStruct((H, S, HD), F32), jax.ShapeDtypeStruct((N, H, HD, HD), F32)]
        + (rider.out_shape if rider else []),
        in_specs=[blk(0), blk(1), blk(2), pl.BlockSpec((P * CHUNK, LANE), lambda n: (n, 0))] + [ANY] * extra,
        out_specs=[pl.BlockSpec((H, P * CHUNK, HD), lambda n: (0, n, 0)),
                   pl.BlockSpec((P, H, HD, HD), lambda n: (n, 0, 0, 0))] + [ANY] * extra,
        scratch_shapes=[pltpu.VMEM((H, HD, HD), F32)] + (rider.scratch if rider else []), compiler_params=_params(),
    )(qkv_hm, qkv_hm, qkv_hm, bg, *(rider.arrays if rider else []))
    return res[0], res[1], res[2:]


def _gdn_bwd(qkv_hm, bg, states, do_hm, H, name, rider=None):
    S = qkv_hm.shape[1]
    N = S // CHUNK
    extra = rider.n if rider is not None else 0

    def body(q_ref, k_ref, v_ref, bg_ref, st_ref, do_ref, dqkv_ref, dbg_ref, dstate):
        n = pl.program_id(0)

        @pl.when(n == 0)
        def _():
            dstate[...] = jnp.zeros_like(dstate)

        q, k, v, do = q_ref[...], k_ref[...], v_ref[...], do_ref[...]
        t = _gdn_chunk(q, k, v, bg_ref[...], H)
        eye, beta, eG, decay, kk, ainv = t["eye"], t["beta"], t["eG"], t["decay"], t["kk"], t["ainv"]
        s0 = st_ref[0]
        ds1 = dstate[...]
        u = t["u_v"] - _bmm(t["w_k"], s0)
        qdec, kdec = q * eG, k * t["ek"]
        d_qdec = _bmm(do, s0, "nt")
        d_attn = _bmm(do, u, "nt")
        du = _bmm(t["attn"], do, "tn") + _bmm(kdec, ds1)
        d_cd = jnp.sum(jnp.sum(ds1 * s0, axis=2, keepdims=True), axis=1, keepdims=True)
        d_kdec = _bmm(u, ds1, "nt")
        d_wk = -_bmm(du, s0, "nt")
        dstate[...] = t["cd"] * ds1 + _bmm(qdec, do, "tn") - _bmm(t["w_k"], du, "tn")
        d_rv = _bmm(ainv, du, "tn")
        d_rk = _bmm(ainv, d_wk, "tn")
        dL = jnp.where(t["strict"], -(_bmm(d_rv, t["u_v"], "nt") + _bmm(d_rk, t["w_k"], "nt")), 0.0)
        rk_k = jnp.sum(d_rk * k, axis=2, keepdims=True)
        d_beta = (jnp.sum(dL * decay * kk, axis=2, keepdims=True) + jnp.sum(d_rv * v, axis=2, keepdims=True)
                  + rk_k * eG)
        d_decay = dL * beta * kk + d_attn * t["qk"]
        d_kk = dL * beta * decay
        d_qk = d_attn * decay
        dqkv_ref[pl.ds(2 * H, H)] = beta * d_rv
        dqkv_ref[pl.ds(0, H)] = _bmm(d_qk, k) + d_qdec * eG
        dqkv_ref[pl.ds(H, H)] = ((beta * eG) * d_rk + _bmm(d_kk, k) + _bmm(d_kk, k, "tn") + _bmm(d_qk, q, "tn")
                       + d_kdec * t["ek"])
        d_eG = rk_k * beta + jnp.sum(d_qdec * q, axis=2, keepdims=True)
        e = jnp.sum(d_kdec * kdec, axis=2, keepdims=True)
        T = d_decay * decay
        dG = d_eG * eG - e + jnp.sum(T, axis=2, keepdims=True) - _to_col(jnp.sum(T, axis=1, keepdims=True), eye)
        dGL = jnp.sum(e, axis=1, keepdims=True) + d_cd * t["cd"]
        row1 = lax.broadcasted_iota(jnp.int32, (H, CHUNK, 1), 1)
        dG = dG + jnp.where(row1 == CHUNK - 1, dGL, 0.0)
        lane = lax.broadcasted_iota(jnp.int32, (CHUNK, LANE), 1)
        out = jnp.zeros((CHUNK, LANE), F32)
        for h in range(H):
            out = out + jnp.where(lane == h, d_beta[h], 0.0) + jnp.where(lane == H + h, dG[h], 0.0)
        dbg_ref[...] = out

    blk = lambda kind: pl.BlockSpec((H, CHUNK, HD), lambda n: (kind, N - 1 - n, 0))
    res = pl.pallas_call(
        _ride(body, 6, 2, rider, *_chunk_steps(N)), name=name, grid=(N,),
        out_shape=[jax.ShapeDtypeStruct((3 * H, S, HD), F32), jax.ShapeDtypeStruct((S, LANE), F32)]
        + (rider.out_shape if rider else []),
        in_specs=[blk(0), blk(1), blk(2), pl.BlockSpec((CHUNK, LANE), lambda n: (N - 1 - n, 0)),
                  pl.BlockSpec((1, H, HD, HD), lambda n: (N - 1 - n, 0, 0, 0)), blk(0)] + [ANY] * extra,
        out_specs=[pl.BlockSpec((3 * H, CHUNK, HD), lambda n: (0, N - 1 - n, 0)),
                   pl.BlockSpec((CHUNK, LANE), lambda n: (N - 1 - n, 0))] + [ANY] * extra,
        scratch_shapes=[pltpu.VMEM((H, HD, HD), F32)] + (rider.scratch if rider else []), compiler_params=_params(),
    )(qkv_hm, qkv_hm, qkv_hm, bg, states, do_hm, *(rider.arrays if rider else []))
    return res[0], res[1], res[2:]


def _gdn_out(o_hm, proj, z_col, gdn_w, name):
    H, S, _ = o_hm.shape
    vw = H * HD
    ts = _tile(S, 256, 8)

    def body(o_ref, z_ref, w_ref, y_ref):
        z = z_ref[...].astype(F32)
        w = w_ref[...]
        parts = []
        for h in range(H):
            o = o_ref[h]
            parts.append(o * _rstd(o) * w)
        y_ref[...] = (jnp.concatenate(parts, axis=1) * _silu(z)).astype(BF16)

    return pl.pallas_call(
        body, name=name, out_shape=jax.ShapeDtypeStruct((S, vw), BF16), grid=(S // ts,),
        in_specs=[pl.BlockSpec((H, ts, HD), lambda i: (0, i, 0)), pl.BlockSpec((ts, vw), lambda i: (i, z_col)),
                  pl.BlockSpec((1, HD), lambda i: (0, 0))],
        out_specs=pl.BlockSpec((ts, vw), lambda i: (i, 0)), compiler_params=_params(),
    )(o_hm, proj, gdn_w)


def _gdn_out_bwd(o_hm, proj, z_col, gdn_w, dy, name):
    H, S, _ = o_hm.shape
    vw = H * HD
    ts = _tile(S, 256, 8)
    n = S // ts

    def body(o_ref, z_ref, w_ref, dy_ref, do_ref, dz_ref, dw_ref, acc):
        i = pl.program_id(0)
        z = z_ref[...].astype(F32)
        dy = dy_ref[...].astype(F32)
        w = w_ref[...]
        gz = dy * _silu(z)
        normed, dwr = [], jnp.zeros((ts, HD), F32)
        for h in range(H):
            o = o_ref[h]
            dxo, dwh = _rms_bwd(o, w, gz[:, HD * h:HD * (h + 1)])
            do_ref[h] = dxo
            dwr = dwr + dwh
            normed.append(o * _rstd(o) * w)
        dz_ref[...] = (dy * jnp.concatenate(normed, axis=1) * _dsilu(z)).astype(BF16)

        @pl.when(i == 0)
        def _():
            acc[...] = jnp.zeros_like(acc)

        acc[...] += _colsum8(dwr)

        @pl.when(i == n - 1)
        def _():
            dw_ref[...] = jnp.sum(acc[...], axis=0, keepdims=True)

    return pl.pallas_call(
        body, name=name, grid=(n,),
        out_shape=(jax.ShapeDtypeStruct((H, S, HD), F32), jax.ShapeDtypeStruct((S, vw), BF16),
                   jax.ShapeDtypeStruct((1, HD), F32)),
        in_specs=[pl.BlockSpec((H, ts, HD), lambda i: (0, i, 0)), pl.BlockSpec((ts, vw), lambda i: (i, z_col)),
                  pl.BlockSpec((1, HD), lambda i: (0, 0)), pl.BlockSpec((ts, vw), lambda i: (i, 0))],
        out_specs=(pl.BlockSpec((H, ts, HD), lambda i: (0, i, 0)), pl.BlockSpec((ts, vw), lambda i: (i, 0)),
                   pl.BlockSpec((1, HD), lambda i: (0, 0))),
        scratch_shapes=[pltpu.VMEM((8, HD), F32)], compiler_params=_params(),
    )(o_hm, proj, gdn_w, dy)


def _pool_trailing(ext, ts, pg, row0):
    outs, inv_cnts = [], []
    t_abs = row0 + lax.broadcasted_iota(jnp.int32, (ts, 1), 0)
    for gi, win in enumerate(POOL_WINDOWS):
        cols = pl.ds(gi * pg, pg)
        cur = ext[pl.ds(HALO, ts), cols]
        acc = cur
        for j in range(1, win):
            acc = acc + ext[pl.ds(HALO - j, ts), cols]
        inv = 1.0 / jnp.minimum(t_abs + 1, win).astype(F32)
        outs.append(acc * inv - cur)
    return outs


def _pool_fwd(proj, p_col, pool_w, pool_scale, pw, name):
    S = proj.shape[0]
    pg = pw // len(POOL_WINDOWS)
    ts = _tile(S, 512, HALO)

    def body(x_ref, h_ref, w_ref, sc_ref, o_ref, ext):
        i = pl.program_id(0)
        _fill_causal(ext, x_ref[...].astype(F32), h_ref[...].astype(F32), i)
        ys = _pool_trailing(ext, ts, pg, i * ts)
        outs = [jnp.dot(ys[gi].astype(BF16), w_ref[gi], preferred_element_type=F32) for gi in range(len(ys))]
        o_ref[...] = (jnp.concatenate(outs, axis=1) * sc_ref[...]).astype(BF16)

    return pl.pallas_call(
        body, name=name, out_shape=jax.ShapeDtypeStruct((S, pw), BF16), grid=(S // ts,),
        in_specs=[pl.BlockSpec((ts, pw), lambda i: (i, p_col)),
                  pl.BlockSpec((HALO, pw), lambda i: (jnp.maximum(i * (ts // HALO) - 1, 0), p_col)),
                  pl.BlockSpec((len(POOL_WINDOWS), pg, pg), lambda i: (0, 0, 0)), pl.BlockSpec((1, pw), lambda i: (0, 0))],
        out_specs=pl.BlockSpec((ts, pw), lambda i: (i, 0)),
        scratch_shapes=[pltpu.VMEM((ts + HALO, pw), F32)], compiler_params=_params(),
    )(proj, proj, pool_w, pool_scale)


def _pool_bwd(proj, p_col, pool_w, pool_scale, dpb, pw, name):
    S = proj.shape[0]
    G = len(POOL_WINDOWS)
    pg = pw // G
    ts = _tile(S, 512, HALO)
    n = S // ts

    def body(x_ref, h_ref, w_ref, sc_ref, d_ref, dn_ref, dp_ref, dw_ref, dsc_ref, ext, zext, wacc, sacc):
        i = pl.program_id(0)
        _fill_causal(ext, x_ref[...].astype(F32), h_ref[...].astype(F32), i)
        ys = _pool_trailing(ext, ts, pg, i * ts)
        d_ext = jnp.concatenate([d_ref[...].astype(F32), jnp.where(i < n - 1, dn_ref[...].astype(F32), 0.0)], axis=0)
        dt = d_ext * sc_ref[...]
        t_abs = i * ts + lax.broadcasted_iota(jnp.int32, (ts + HALO, 1), 0)

        @pl.when(i == 0)
        def _():
            wacc[...] = jnp.zeros_like(wacc)
            sacc[...] = jnp.zeros_like(sacc)

        dps, tfs = [], []
        for gi, win in enumerate(POOL_WINDOWS):
            cols = slice(gi * pg, (gi + 1) * pg)
            w = w_ref[gi]
            dt_g = dt[:, cols].astype(BF16)
            y_g = ys[gi].astype(BF16)
            tfs.append(jnp.dot(y_g, w, preferred_element_type=F32))
            wacc[gi] += lax.dot_general(y_g, dt_g[:ts], (((0,), (0,)), ((), ())), preferred_element_type=F32)
            dyp = lax.dot_general(dt_g, w, (((1,), (1,)), ((), ())), preferred_element_type=F32)
            zext[:, pl.ds(gi * pg, pg)] = dyp * (1.0 / jnp.minimum(t_abs + 1, win).astype(F32))
            acc = -dyp[:ts]
            for j in range(win):
                acc = acc + zext[pl.ds(j, ts), pl.ds(gi * pg, pg)]
            dps.append(acc)
        dp_ref[...] = jnp.concatenate(dps, axis=1).astype(BF16)
        sacc[...] += _colsum8(d_ext[:ts] * jnp.concatenate(tfs, axis=1))

        @pl.when(i == n - 1)
        def _():
            dw_ref[...] = wacc[...]
            dsc_ref[...] = jnp.sum(sacc[...], axis=0, keepdims=True)

    return pl.pallas_call(
        body, name=name, grid=(n,),
        out_shape=(jax.ShapeDtypeStruct((S, pw), BF16), jax.ShapeDtypeStruct((G, pg, pg), F32),
                   jax.ShapeDtypeStruct((1, pw), F32)),
        in_specs=[pl.BlockSpec((ts, pw), lambda i: (i, p_col)),
                  pl.BlockSpec((HALO, pw), lambda i: (jnp.maximum(i * (ts // HALO) - 1, 0), p_col)),
                  pl.BlockSpec((G, pg, pg), lambda i: (0, 0, 0)), pl.BlockSpec((1, pw), lambda i: (0, 0)),
                  pl.BlockSpec((ts, pw), lambda i: (i, 0)),
                  pl.BlockSpec((HALO, pw), lambda i: (jnp.minimum((i + 1) * (ts // HALO), S // HALO - 1), 0))],
        out_specs=(pl.BlockSpec((ts, pw), lambda i: (i, 0)), pl.BlockSpec((G, pg, pg), lambda i: (0, 0, 0)),
                   pl.BlockSpec((1, pw), lambda i: (0, 0))),
        scratch_shapes=[pltpu.VMEM((ts + HALO, pw), F32), pltpu.VMEM((ts + HALO, pw), F32),
                        pltpu.VMEM((G, pg, pg), F32), pltpu.VMEM((8, pw), F32)],
        compiler_params=_params(),
    )(proj, proj, pool_w, pool_scale, dpb, dpb)


def _merge(proj, ga_col, gb_col, ya, yb, name):
    S, d = ya.shape
    ts = _tile(S, 512, 16)

    def body(ga_ref, gb_ref, ya_ref, yb_ref, o_ref):
        o_ref[...] = (_sigmoid(ga_ref[...].astype(F32)) * ya_ref[...].astype(F32)
                      + _sigmoid(gb_ref[...].astype(F32)) * yb_ref[...].astype(F32)).astype(BF16)

    row = pl.BlockSpec((ts, d), lambda i: (i, 0))
    return pl.pallas_call(
        body, name=name, out_shape=jax.ShapeDtypeStruct((S, d), BF16), grid=(S // ts,),
        in_specs=[pl.BlockSpec((ts, d), lambda i: (i, ga_col)), pl.BlockSpec((ts, d), lambda i: (i, gb_col)), row, row],
        out_specs=row, compiler_params=_params(),
    )(proj, proj, ya, yb)


def _merge_bwd(proj, ga_col, gb_col, ya, yb, dm, name):
    S, d = ya.shape
    ts = _tile(S, 512, 16)

    def body(ga_ref, gb_ref, ya_ref, yb_ref, dm_ref, dya_ref, dyb_ref, dga_ref, dgb_ref):
        dmv = dm_ref[...].astype(F32)
        sa, sb = _sigmoid(ga_ref[...].astype(F32)), _sigmoid(gb_ref[...].astype(F32))
        dya_ref[...] = (dmv * sa).astype(BF16)
        dyb_ref[...] = (dmv * sb).astype(BF16)
        dga_ref[...] = (dmv * ya_ref[...].astype(F32) * sa * (1.0 - sa)).astype(BF16)
        dgb_ref[...] = (dmv * yb_ref[...].astype(F32) * sb * (1.0 - sb)).astype(BF16)

    row = pl.BlockSpec((ts, d), lambda i: (i, 0))
    o = jax.ShapeDtypeStruct((S, d), BF16)
    return pl.pallas_call(
        body, name=name, out_shape=(o, o, o, o), grid=(S // ts,),
        in_specs=[pl.BlockSpec((ts, d), lambda i: (i, ga_col)), pl.BlockSpec((ts, d), lambda i: (i, gb_col)), row, row, row],
        out_specs=(row, row, row, row), compiler_params=_params(),
    )(proj, proj, ya, yb, dm)


def _xattn_fwd(q, kv, name):
    S, d = q.shape
    M = kv.shape[0]
    hd = d // XA_HEADS
    ts = _tile(S, 512, 16)
    scale = hd ** -0.5

    def body(q_ref, k_ref, v_ref, o_ref):
        s = lax.dot_general(q_ref[...], k_ref[...], (((1,), (1,)), ((), ())), preferred_element_type=F32) * scale
        p = jnp.exp(s - jnp.max(s, axis=-1, keepdims=True))
        p = p / jnp.sum(p, axis=-1, keepdims=True)
        o_ref[...] = jnp.dot(p.astype(BF16), v_ref[...], preferred_element_type=F32).astype(BF16)

    return pl.pallas_call(
        body, name=name, out_shape=jax.ShapeDtypeStruct((S, d), BF16), grid=(S // ts, XA_HEADS),
        in_specs=[pl.BlockSpec((ts, hd), lambda i, h: (i, h)), pl.BlockSpec((M, hd), lambda i, h: (0, h)),
                  pl.BlockSpec((M, hd), lambda i, h: (0, XA_HEADS + h))],
        out_specs=pl.BlockSpec((ts, hd), lambda i, h: (i, h)), compiler_params=_params(),
    )(q, kv, kv)


def _xattn_bwd(q, kv, do, name):
    S, d = q.shape
    M = kv.shape[0]
    hd = d // XA_HEADS
    ts = _tile(S, 512, 16)
    n = S // ts
    scale = hd ** -0.5

    def body(q_ref, k_ref, v_ref, do_ref, dq_ref, dk_ref, dv_ref, kacc, vacc):
        i = pl.program_id(1)
        qv, kv_, vv, dov = q_ref[...], k_ref[...], v_ref[...], do_ref[...]
        s = lax.dot_general(qv, kv_, (((1,), (1,)), ((), ())), preferred_element_type=F32) * scale
        p = jnp.exp(s - jnp.max(s, axis=-1, keepdims=True))
        p = p / jnp.sum(p, axis=-1, keepdims=True)
        dp = lax.dot_general(dov, vv, (((1,), (1,)), ((), ())), preferred_element_type=F32)
        ds = (p * (dp - jnp.sum(p * dp, axis=-1, keepdims=True)) * scale).astype(BF16)
        dq_ref[...] = jnp.dot(ds, kv_, preferred_element_type=F32).astype(BF16)

        @pl.when(i == 0)
        def _():
            kacc[...] = jnp.zeros_like(kacc)
            vacc[...] = jnp.zeros_like(vacc)

        kacc[...] += lax.dot_general(ds, qv, (((0,), (0,)), ((), ())), preferred_element_type=F32)
        vacc[...] += lax.dot_general(p.astype(BF16), dov, (((0,), (0,)), ((), ())), preferred_element_type=F32)

        @pl.when(i == n - 1)
        def _():
            dk_ref[...] = kacc[...]
            dv_ref[...] = vacc[...]

    dq, dk, dv = pl.pallas_call(
        body, name=name, grid=(XA_HEADS, n),
        out_shape=(jax.ShapeDtypeStruct((S, d), BF16), jax.ShapeDtypeStruct((M, d), F32), jax.ShapeDtypeStruct((M, d), F32)),
        in_specs=[pl.BlockSpec((ts, hd), lambda h, i: (i, h)), pl.BlockSpec((M, hd), lambda h, i: (0, h)),
                  pl.BlockSpec((M, hd), lambda h, i: (0, XA_HEADS + h)), pl.BlockSpec((ts, hd), lambda h, i: (i, h))],
        out_specs=(pl.BlockSpec((ts, hd), lambda h, i: (i, h)), pl.BlockSpec((M, hd), lambda h, i: (0, h)),
                   pl.BlockSpec((M, hd), lambda h, i: (0, h))),
        scratch_shapes=[pltpu.VMEM((M, hd), F32), pltpu.VMEM((M, hd), F32)], compiler_params=_params(),
    )(q, kv, kv, do)
    return dq, jnp.concatenate([dk, dv], axis=1)


def _ffn_act(up, conv_w, bias, name):
    S, F2 = up.shape
    F = F2 // 2
    ts, cw = _tile(S, SEQ_TILE, HALO), _tile(F, 512)
    nb = F // cw

    def body(a_ref, ah_ref, b_ref, bh_ref, wa_ref, wb_ref, ba_ref, bb_ref, o_ref, ea, eb):
        i = pl.program_id(0)
        _fill_causal(ea, a_ref[...].astype(F32), ah_ref[...].astype(F32), i)
        _fill_causal(eb, b_ref[...].astype(F32), bh_ref[...].astype(F32), i)
        wa, wb, bia, bib = wa_ref[...], wb_ref[...], ba_ref[...], bb_ref[...]

        def chunk(r0):
            ua = _filter(wa, _causal_taps(ea, r0, FFN_CONV)) + bia
            ub = _filter(wb, _causal_taps(eb, r0, FFN_CONV)) + bib
            o_ref[pl.ds(r0, RC), :] = (_silu(ua) * ub).astype(BF16)

        _row_chunks(ts, chunk)

    tile = lambda c0: pl.BlockSpec((ts, cw), lambda i, j: (i, j + c0))
    vec = lambda rows, c0: pl.BlockSpec((rows, cw), lambda i, j: (0, j + c0))
    return pl.pallas_call(
        body, name=name, out_shape=jax.ShapeDtypeStruct((S, F), BF16), grid=(S // ts, nb),
        in_specs=[tile(0), _prev_halo_spec(ts, cw), tile(nb), _prev_halo_spec(ts, cw, nb),
                  vec(FFN_CONV, 0), vec(FFN_CONV, nb), vec(1, 0), vec(1, nb)],
        out_specs=pl.BlockSpec((ts, cw), lambda i, j: (i, j)),
        scratch_shapes=[pltpu.VMEM((ts + HALO, cw), F32), pltpu.VMEM((ts + HALO, cw), F32)],
        compiler_params=_params(),
    )(up, up, up, up, conv_w, conv_w, bias, bias)


def _ffn_act_bwd(up, conv_w, bias, dact, name):
    S, F2 = up.shape
    F = F2 // 2
    ts, cw = _tile(S, SEQ_TILE, HALO), _tile(F, 512)
    nb = F // cw
    n = S // ts

    def body(a_ref, ah_ref, b_ref, bh_ref, wa_ref, wb_ref, ba_ref, bb_ref, d_ref,
             du_ref, dwa_ref, dwb_ref, dba_ref, dbb_ref, ea, eb, wacc, bacc):
        i = pl.program_id(1)
        _fill_causal(ea, a_ref[...].astype(F32), ah_ref[...].astype(F32), i)
        _fill_causal(eb, b_ref[...].astype(F32), bh_ref[...].astype(F32), i)
        wa, wb, bia, bib = wa_ref[...], wb_ref[...], ba_ref[...], bb_ref[...]

        @pl.when(i == 0)
        def _():
            wacc[...] = jnp.zeros_like(wacc)
            bacc[...] = jnp.zeros_like(bacc)

        def chunk(r0):
            ta, tb = _causal_taps(ea, r0, FFN_CONV), _causal_taps(eb, r0, FFN_CONV)
            ua, ub = _filter(wa, ta) + bia, _filter(wb, tb) + bib
            d = d_ref[pl.ds(r0, RC), :].astype(F32)
            dua = d * ub * _dsilu(ua)
            dub = d * _silu(ua)
            du_ref[0, pl.ds(r0, RC), :] = dua.astype(BF16)
            du_ref[1, pl.ds(r0, RC), :] = dub.astype(BF16)
            for t in range(FFN_CONV):
                wacc[0, t] += _colsum8(dua * ta[FFN_CONV - 1 - t])
                wacc[1, t] += _colsum8(dub * tb[FFN_CONV - 1 - t])
            bacc[0] += _colsum8(dua)
            bacc[1] += _colsum8(dub)

        _row_chunks(ts, chunk)

        @pl.when(i == n - 1)
        def _():
            dwa_ref[...] = jnp.sum(wacc[0], axis=1)
            dwb_ref[...] = jnp.sum(wacc[1], axis=1)
            dba_ref[...] = jnp.sum(bacc[0], axis=0, keepdims=True)
            dbb_ref[...] = jnp.sum(bacc[1], axis=0, keepdims=True)

    tile = lambda c0: pl.BlockSpec((ts, cw), lambda j, i: (i, j + c0))
    halo = lambda c0: pl.BlockSpec((HALO, cw), lambda j, i: (jnp.maximum(i * (ts // HALO) - 1, 0), j + c0))
    vec = lambda rows, c0: pl.BlockSpec((rows, cw), lambda j, i: (0, j + c0))
    du, dwa, dwb, dba, dbb = pl.pallas_call(
        body, name=name, grid=(nb, n),
        out_shape=(jax.ShapeDtypeStruct((2, S, F), BF16),
                   jax.ShapeDtypeStruct((FFN_CONV, F), F32), jax.ShapeDtypeStruct((FFN_CONV, F), F32),
                   jax.ShapeDtypeStruct((1, F), F32), jax.ShapeDtypeStruct((1, F), F32)),
        in_specs=[tile(0), halo(0), tile(nb), halo(nb), vec(FFN_CONV, 0), vec(FFN_CONV, nb), vec(1, 0), vec(1, nb), tile(0)],
        out_specs=(pl.BlockSpec((2, ts, cw), lambda j, i: (0, i, j)), vec(FFN_CONV, 0), vec(FFN_CONV, 0), vec(1, 0),
                   vec(1, 0)),
        scratch_shapes=[pltpu.VMEM((ts + HALO, cw), F32), pltpu.VMEM((ts + HALO, cw), F32),
                        pltpu.VMEM((2, FFN_CONV, 8, cw), F32), pltpu.VMEM((2, 8, cw), F32)],
        compiler_params=_params(),
    )(up, up, up, up, conv_w, conv_w, bias, bias, dact)
    return du, jnp.concatenate([dwa, dwb], axis=1), jnp.concatenate([dba, dbb], axis=1)


def _adamw(gparts, w, m, v, name):
    R, C = w.shape
    tr = _tile(R, max(16, (256 * 1024) // C), 16)

    def body(g_ref, w_ref, m_ref, v_ref, go_ref, d_ref, mo_ref, vo_ref):
        g = g_ref[0].astype(F32)
        for s in range(1, N_DEV):
            g = g + g_ref[s].astype(F32)
        mn = ADAM_B1 * m_ref[...] + (1.0 - ADAM_B1) * g
        vn = ADAM_B2 * v_ref[...] + (1.0 - ADAM_B2) * (g * g)
        m_hat = mn / (1.0 - ADAM_B1 ** ADAM_STEP)
        v_hat = vn / (1.0 - ADAM_B2 ** ADAM_STEP)
        go_ref[...] = g
        d_ref[...] = -ADAM_LR * (m_hat / (jnp.sqrt(v_hat) + ADAM_EPS) + ADAM_WD * w_ref[...])
        mo_ref[...] = mn
        vo_ref[...] = vn

    row = pl.BlockSpec((tr, C), lambda i: (i, 0))
    o = jax.ShapeDtypeStruct((R, C), F32)
    return pl.pallas_call(
        body, name=name, out_shape=(o, o, o, o), grid=(R // tr,),
        in_specs=[pl.BlockSpec((N_DEV, tr, C), lambda i: (0, i, 0)), row, row, row],
        out_specs=(row, row, row, row), compiler_params=_params(),
    )(gparts, w, m, v)


def _position():
    return lax.axis_index("x"), lax.axis_index("y"), lax.axis_index("c")


class _Copies:
    def __init__(self, arrays):
        self.arrays, self.n = list(arrays), len(arrays)
        self.scratch = [pltpu.SemaphoreType.DMA((7 * self.n,)), pltpu.SemaphoreType.DMA((7 * self.n,)),
                        pltpu.SemaphoreType.DMA((self.n,))]


class _Gather(_Copies):
    def __init__(self, arrays):
        super().__init__(arrays)
        self.out_shape = [jax.ShapeDtypeStruct((N_DEV,) + b.shape, b.dtype) for b in self.arrays]

    def phases(self, x_refs, out_refs, send_sems, recv_sems, local_sems):
        n = self.n
        x, y, c = _position()
        me, sibling = (x, y, c), (x, y, 1 - c)
        chips = [(1 - x, y), (x, 1 - y), (1 - x, 1 - y)]

        def copy(a, k, blk, to, own=False):
            slot = out_refs[a].at[4 * blk[0] + 2 * blk[1] + blk[2]]
            return pltpu.make_async_remote_copy(
                src_ref=x_refs[a] if own else slot, dst_ref=slot,
                send_sem=send_sems.at[7 * a + k], recv_sem=recv_sems.at[7 * a + k], device_id=to, device_id_type=MESH)

        mine = [pltpu.make_async_copy(x_refs[a], out_refs[a].at[4 * x + 2 * y + c], local_sems.at[a]) for a in range(n)]
        first = [copy(a, 0, me, sibling, own=True) for a in range(n)]
        first += [copy(a, 1 + j, me, (*chip, c), own=True) for a in range(n) for j, chip in enumerate(chips)]
        passed = [copy(a, 4 + j, (*chip, c), sibling) for j, chip in enumerate(chips) for a in range(n)]

        def start():
            for cp in mine + first:
                cp.start()

        def hand_on():
            for j, chip in enumerate(chips):
                for a in range(n):
                    copy(a, 1 + j, (*chip, c), me).wait_recv()
                    passed[j * n + a].start()

        def finish():
            for a in range(n):
                copy(a, 0, sibling, me).wait_recv()
            for j, chip in enumerate(chips):
                for a in range(n):
                    copy(a, 4 + j, (*chip, 1 - c), me).wait_recv()
            for cp in first + passed:
                cp.wait_send()
            for cp in mine:
                cp.wait()

        return start, hand_on, finish


class _Exchange(_Copies):
    def __init__(self, arrays):
        super().__init__(arrays)
        self.out_shape = [jax.ShapeDtypeStruct(p.shape, p.dtype) for p in self.arrays]

    def phases(self, p_refs, out_refs, send_sems, recv_sems, local_sems):
        n = self.n
        x, y, c = _position()
        my_slot = 4 * x + 2 * y + c
        mine = [pltpu.make_async_copy(p_refs[a].at[my_slot], out_refs[a].at[my_slot], local_sems.at[a]) for a in range(n)]
        copies = []
        for k in range(1, N_DEV):
            px, py, pc = x ^ (k >> 2), y ^ ((k >> 1) & 1), c ^ (k & 1)
            for a in range(n):
                copies.append(pltpu.make_async_remote_copy(
                    src_ref=p_refs[a].at[4 * px + 2 * py + pc], dst_ref=out_refs[a].at[my_slot],
                    send_sem=send_sems.at[7 * a + k - 1], recv_sem=recv_sems.at[7 * a + k - 1],
                    device_id=(px, py, pc), device_id_type=MESH))

        def start():
            for cp in mine + copies:
                cp.start()

        def finish():
            for cp in copies + mine:
                cp.wait()

        return start, None, finish


def _communicate(copies, name):
    n = copies.n

    def body(*refs):
        start, hand_on, finish = copies.phases(refs[:n], refs[n:2 * n], *refs[2 * n:])
        start()
        if hand_on is not None:
            hand_on()
        finish()

    return pl.pallas_call(body, name=name, out_shape=copies.out_shape, in_specs=[ANY] * n, out_specs=[ANY] * n,
                          scratch_shapes=copies.scratch)(*copies.arrays)


def _col_pieces(col_map, shard_w):
    pieces = []
    for lo, hi, dst in col_map:
        c = lo
        while c < hi:
            j = c // shard_w
            end = min(hi, (j + 1) * shard_w)
            pieces.append((j, c - j * shard_w, end - c, dst + (c - lo)))
            c = end
    return pieces


def _assemble_cols(shards, pieces, width, name):
    _, R, Cs = shards.shape
    tr = _tile(R, 128, 16)

    def body(s_ref, o_ref):
        o_ref[...] = jnp.zeros(o_ref.shape, o_ref.dtype)
        for j, lo, n, dst in pieces:
            o_ref[:, dst:dst + n] = s_ref[j, :, lo:lo + n]

    return pl.pallas_call(
        body, name=name, out_shape=jax.ShapeDtypeStruct((R, width), shards.dtype), grid=(R // tr,),
        in_specs=[pl.BlockSpec((N_DEV, tr, Cs), lambda i: (0, i, 0))],
        out_specs=pl.BlockSpec((tr, width), lambda i: (i, 0)), compiler_params=_params(),
    )(shards)


def _split_cols(full, pieces, shard_w, name):
    R, width = full.shape
    tr = _tile(R, 128, 16)

    def body(f_ref, o_ref):
        for j, lo, n, dst in pieces:
            o_ref[j, :, lo:lo + n] = f_ref[:, dst:dst + n]

    return pl.pallas_call(
        body, name=name, out_shape=jax.ShapeDtypeStruct((N_DEV, R, shard_w), full.dtype), grid=(R // tr,),
        in_specs=[pl.BlockSpec((tr, width), lambda i: (i, 0))],
        out_specs=pl.BlockSpec((N_DEV, tr, shard_w), lambda i: (0, i, 0)), compiler_params=_params(),
    )(full)


def _pack(arrays, row_multiple=8):
    flat, layout, off = [], [], 0
    for a in arrays:
        n = a.size
        padded = -(-n // LANE) * LANE
        f = a.reshape(-1).astype(F32)
        if padded != n:
            f = jnp.pad(f, (0, padded - n))
        flat.append(f)
        layout.append((off, n, a.shape))
        off += padded
    total = -(-off // (LANE * row_multiple)) * (LANE * row_multiple)
    if total != off:
        flat.append(jnp.zeros((total - off,), F32))
    return jnp.concatenate(flat).reshape(total // LANE, LANE), layout


def _unpack(buf, layout):
    flat = buf.reshape(-1)
    return [flat[off:off + n].reshape(shape) for off, n, shape in layout]


def _cols_to_full(g):
    return jnp.transpose(g, (1, 0, 2)).reshape(g.shape[1], N_DEV * g.shape[2])


def _full_to_cols(a):
    return jnp.transpose(a.reshape(a.shape[0], N_DEV, a.shape[1] // N_DEV), (1, 0, 2))


def _rows_to_full(g):
    return g.reshape(N_DEV * g.shape[1], g.shape[2])


def _full_to_rows(a):
    return a.reshape(N_DEV, a.shape[0] // N_DEV, a.shape[1])


def _pad_cols(a, width):
    return a if a.shape[-1] == width else jnp.pad(a, [(0, 0)] * (a.ndim - 1) + [(0, width - a.shape[-1])])


SHARDED = ("w_in", "conv_qkv", "pool_w", "w_branch_a", "w_branch_b", "w_mix_out", "w_xq", "w_xkv", "w_xo", "w_up",
           "ffn_conv_w", "w_down")
REPLICATED = ("mix_pre_norm", "a_log", "dt_bias", "gdn_norm", "pool_scale", "mix_post_norm", "xa_pre_norm", "mem_norm",
              "xa_post_norm", "ffn_pre_norm", "ffn_conv_b", "ffn_post_norm")
WEIGHTS = ("mix_pre_norm", "w_in", "conv_qkv", "a_log", "dt_bias", "gdn_norm", "pool_w", "pool_scale", "w_branch_a",
           "w_branch_b", "w_mix_out", "mix_post_norm", "xa_pre_norm", "mem_norm", "w_xq", "w_xkv", "w_xo", "xa_post_norm",
           "ffn_pre_norm", "w_up", "ffn_conv_w", "ffn_conv_b", "w_down", "ffn_post_norm")
MATMUL_WEIGHTS = ("w_in", "w_branch_a", "w_branch_b", "w_mix_out", "w_xq", "w_xkv", "w_xo", "w_up", "w_down")
COL_SHARDED = ("w_in", "w_branch_b", "w_xkv", "w_up", "conv_qkv", "ffn_conv_w")
ROW_SHARDED = ("w_branch_a", "w_mix_out", "w_xq", "w_xo", "w_down")


class _Layout:
    def __init__(self, D, H, pw, F):
        self.D, self.H, self.pw, self.F = D, H, pw, F
        self.qkv_w, self.vw = 3 * H * HD, H * HD
        self.ba_w = 512 if D >= 2048 else LANE
        self.Fp = -(-F // 512) * 512 if F >= 512 else F
        q, vw = self.qkv_w, self.vw
        self.seg = dict(qkv=(0, q), z=(q, vw), ga=(q + vw, D), gb=(q + vw + D, D), p=(q + vw + 2 * D, pw),
                        ba=(q + vw + 2 * D + pw, self.ba_w))
        self.in_w = q + vw + 2 * D + pw + self.ba_w
        o_z, o_b = q, q + vw
        o_p = o_b + 2 * H
        o_ga = o_p + pw
        o_gb = o_ga + D
        self.d_in = o_gb + D
        self.in_map = [(0, o_z, self.seg["qkv"][0]), (o_z, o_b, self.seg["z"][0]), (o_b, o_p, self.seg["ba"][0]),
                       (o_p, o_ga, self.seg["p"][0]), (o_ga, o_gb, self.seg["ga"][0]), (o_gb, self.d_in, self.seg["gb"][0])]
        self.up_map = [(0, F, 0), (F, 2 * F, self.Fp)]

    def col(self, name, width):
        return self.seg[name][0] // width


def _local_step(x, mem, target, P, L, comm=None):
    D, H, pw, F, Fp = L.D, L.H, L.pw, L.F, L.Fp
    qkv_w, vw, ba_w = L.qkv_w, L.vw, L.ba_w
    col = L.col
    P = dict(P)
    win_p, cw3_p, fb_p = P["win_p"], P["cw3_p"], P["fb_p"]
    conv_qkv, pool_w = P["conv_qkv"], P["pool_w"]
    lanes = lambda vec: jnp.pad(vec.reshape(1, H).astype(F32), ((0, 0), (H, LANE - 2 * H)))
    a_log_l, dt_bias_l = lanes(P["a_log"]), lanes(P["dt_bias"])
    bf = lambda name: P[name]
    vecf = lambda name: P[name].reshape(1, -1).astype(F32)
    g = {}

    def carried(call, name, *args, **kw):
        if comm is not None and name in comm.GATHERS:
            *out, got = call(*args, name, rider=comm.gather(name), **kw)
            P.update(comm.weights_from(name, got))
        elif comm is not None and name in comm.EXCHANGES:
            *out, got = call(*args, name, rider=comm.exchange(name, g), **kw)
            comm.receive(name, got)
        else:
            out = call(*args, name, **kw)
            out = [out] if call is _matmul else list(out[:-1])
        return out[0] if len(out) == 1 else out

    h1 = _prenorm(x, vecf("mix_pre_norm"), "mix_prenorm")
    proj = carried(_matmul, "in_proj", h1, win_p, "nn", BF16, tn=1536)
    qkv_hm = _qkv_conv(proj, conv_qkv, qkv_w, "qkv_conv")
    bg = _gates(proj, col("ba", LANE), a_log_l, dt_bias_l, H, "gates")
    o_hm, states = carried(_gdn_fwd, "gdn_fwd", qkv_hm, bg, H)
    wup_p, wdown_p = P["wup_p"], P["wdown_p"]
    oa = _gdn_out(o_hm, proj, col("z", vw), vecf("gdn_norm"), "gdn_out")
    ya = _matmul(oa, bf("w_branch_a"), "nn", BF16, "branch_a")
    pb = _pool_fwd(proj, col("p", pw), pool_w, vecf("pool_scale"), pw, "pool_fwd")
    yb = _matmul(pb, bf("w_branch_b"), "nn", BF16, "branch_b")
    merged = _merge(proj, col("ga", D), col("gb", D), ya, yb, "merge")
    y1 = _matmul(merged, bf("w_mix_out"), "nn", F32, "mix_out")
    x1, h2 = _post_pre(x, y1, vecf("mix_post_norm"), vecf("xa_pre_norm"), "mix_post")
    mn = _prenorm(mem, vecf("mem_norm"), "mem_norm")
    qx = _matmul(h2, bf("w_xq"), "nn", BF16, "xq")
    kv = _matmul(mn, bf("w_xkv"), "nn", BF16, "xkv")
    ox = _xattn_fwd(qx, kv, "xattn_fwd")
    y2 = _matmul(ox, bf("w_xo"), "nn", F32, "xo")
    x2, h3 = _post_pre(x1, y2, vecf("xa_post_norm"), vecf("ffn_pre_norm"), "xa_post")
    up = _matmul(h3, wup_p, "nn", BF16, "ffn_up")
    act = _ffn_act(up, cw3_p, fb_p, "ffn_act")
    y3 = _matmul(act, wdown_p, "nn", F32, "ffn_down")
    dx3, loss = _post_loss(x2, y3, vecf("ffn_post_norm"), target, "ffn_post_loss")

    dy3, g["ffn_post_norm"] = _post_bwd(y3, vecf("ffn_post_norm"), dx3, "ffn_post_bwd")
    dact = _matmul(dy3, wdown_p, "nt", BF16, "ffn_down_dx")
    g["w_down_p"] = _matmul(act, dy3, "tn", BF16, "ffn_down_dw", tk=4096)
    du, g["ffn_conv_w_p"], g["ffn_conv_b_p"] = _ffn_act_bwd(up, cw3_p, fb_p, dact, "ffn_act_bwd")
    dup = _conv_t(du, jnp.stack([cw3_p[:, :Fp], cw3_p[:, Fp:]]), "ffn_conv_t")
    dh3 = carried(_matmul, "ffn_up_dx", dup, wup_p, "nt", BF16)
    g["w_up_p"] = _matmul(h3, dup, "tn", BF16, "ffn_up_dw", tn=512, tk=4096)
    dx2, g["ffn_pre_norm"] = _pre_bwd(x2, vecf("ffn_pre_norm"), dh3, dx3, "ffn_pre_bwd")
    dy2, g["xa_post_norm"] = _post_bwd(y2, vecf("xa_post_norm"), dx2, "xa_post_bwd")
    dox = _matmul(dy2, bf("w_xo"), "nt", BF16, "xo_dx")
    g["w_xo"] = _matmul(ox, dy2, "tn", BF16, "xo_dw", tn=512, tk=4096)
    dqx, dkv = _xattn_bwd(qx, kv, dox, "xattn_bwd")
    dkv_b = dkv.astype(BF16)
    dh2 = _matmul(dqx, bf("w_xq"), "nt", BF16, "xq_dx")
    g["w_xq"] = _matmul(h2, dqx, "tn", BF16, "xq_dw", tn=512, tk=4096)
    dmn = _matmul(dkv_b, bf("w_xkv"), "nt", F32, "xkv_dx")
    g["w_xkv"] = _matmul(mn, dkv_b, "tn", BF16, "xkv_dw")
    _, g["mem_norm"] = _pre_bwd(mem, vecf("mem_norm"), dmn, jnp.zeros_like(mem), "mem_norm_bwd")
    dx1, g["xa_pre_norm"] = _pre_bwd(x1, vecf("xa_pre_norm"), dh2, dx2, "xa_pre_bwd")
    dy1, g["mix_post_norm"] = _post_bwd(y1, vecf("mix_post_norm"), dx1, "mix_post_bwd")
    dmerged = _matmul(dy1, bf("w_mix_out"), "nt", BF16, "mix_out_dx")
    g["w_mix_out"] = _matmul(merged, dy1, "tn", BF16, "mix_out_dw", tn=512, tk=4096)
    dya, dyb, dga, dgb = _merge_bwd(proj, col("ga", D), col("gb", D), ya, yb, dmerged, "merge_bwd")
    doa = _matmul(dya, bf("w_branch_a"), "nt", BF16, "branch_a_dx")
    g["w_branch_a"] = _matmul(oa, dya, "tn", BF16, "branch_a_dw", tn=512, tk=4096)
    dpb = _matmul(dyb, bf("w_branch_b"), "nt", BF16, "branch_b_dx")
    g["w_branch_b"] = _matmul(pb, dyb, "tn", BF16, "branch_b_dw")
    dp, g["pool_w"], g["pool_scale"] = _pool_bwd(proj, col("p", pw), pool_w, vecf("pool_scale"), dpb, pw, "pool_bwd")
    do_hm, dz, g["gdn_norm"] = _gdn_out_bwd(o_hm, proj, col("z", vw), vecf("gdn_norm"), doa, "gdn_out_bwd")
    dqkv_hm, dbg = carried(_gdn_bwd, "gdn_bwd", qkv_hm, bg, states, do_hm, H)
    dba, dal, ddt = _gates_bwd(proj, col("ba", LANE), ba_w, a_log_l, dt_bias_l, dbg, H, "gates_bwd")
    g["a_log"], g["dt_bias"] = dal[:, H:2 * H], ddt[:, H:2 * H]
    dc, g["conv_qkv"] = _qkv_conv_bwd(proj, conv_qkv, dqkv_hm, qkv_w, "qkv_conv_bwd")
    dqkv = _conv_t(dc[None], conv_qkv[None], "qkv_conv_t")[0]
    dproj = jnp.concatenate([dqkv, dz, dga, dgb, dp, dba], axis=1)
    g["w_in_p"] = carried(_matmul, "in_proj_dw", h1, dproj, "tn", BF16, tn=768, tk=4096)
    dh1 = carried(_matmul, "in_proj_dx", dproj, win_p, "nt", BF16, tk=4608)
    grad_x, g["mix_pre_norm"] = _pre_bwd(x, vecf("mix_pre_norm"), dh1, dx1, "mix_pre_bwd")
    return loss, grad_x, g


def _two_halves(a, F, Fp):
    return jnp.concatenate([_pad_cols(a[..., :F], Fp), _pad_cols(a[..., F:], Fp)], axis=-1)


def _from_halves(a, F, Fp):
    return jnp.concatenate([a[..., :F], a[..., Fp:Fp + F]], axis=-1)


class _StepComm:
    FIRST = ("w_in", "conv_qkv", "pool_w", "ffn_conv_w")
    GATHERS = {"in_proj": ("w_branch_a", "w_branch_b", "w_mix_out", "w_xq", "w_xo", "w_down"),
               "gdn_fwd": ("w_xkv", "w_up")}
    EXCHANGES = {"ffn_up_dx": ("w_down", "ffn_conv_w"),
                 "gdn_bwd": ("pool_w", "w_branch_a", "w_branch_b", "w_mix_out", "w_xq", "w_xkv", "w_xo"),
                 "in_proj_dw": ("w_up",),
                 "in_proj_dx": ("w_in", "conv_qkv")}

    def __init__(self, w, L):
        self.w, self.L = w, L
        self.in_pieces = _col_pieces(L.in_map, w["w_in"].shape[1])
        self.up_pieces = _col_pieces(L.up_map, w["w_up"].shape[1])
        self.received = {}

    def _shard(self, n):
        return self.w[n].astype(BF16) if n in MATMUL_WEIGHTS else self.w[n]

    def first_weights(self):
        L, (g, r, c) = self.L, self.w["pool_w"].shape
        G = dict(zip(self.FIRST, _communicate(_Gather([self._shard(n) for n in self.FIRST]), "gather_first")))
        return {"win_p": _assemble_cols(G["w_in"], self.in_pieces, L.in_w, "assemble_w_in"),
                "conv_qkv": _cols_to_full(G["conv_qkv"]),
                "cw3_p": _two_halves(_cols_to_full(G["ffn_conv_w"]), L.F, L.Fp),
                "pool_w": jnp.transpose(G["pool_w"], (1, 0, 2, 3)).reshape(g, N_DEV * r, c).astype(BF16)}

    def gather(self, call):
        return _Gather([self._shard(n) for n in self.GATHERS[call]])

    def weights_from(self, call, results):
        L, P = self.L, {}
        for n, shards in zip(self.GATHERS[call], results):
            if n == "w_up":
                P["wup_p"] = _assemble_cols(shards, self.up_pieces, 2 * L.Fp, "assemble_w_up")
            elif n == "w_down":
                P["wdown_p"] = jnp.pad(_rows_to_full(shards), ((0, L.Fp - L.F), (0, 0)))
            else:
                P[n] = _rows_to_full(shards) if n in ROW_SHARDED else _cols_to_full(shards)
        return P

    def _slices(self, g, n):
        L, w = self.L, self.w
        if n == "w_in":
            return _split_cols(g["w_in_p"], self.in_pieces, w["w_in"].shape[1], "split_w_in")
        if n == "w_up":
            return _split_cols(g["w_up_p"], self.up_pieces, w["w_up"].shape[1], "split_w_up")
        if n == "w_down":
            return _full_to_rows(g["w_down_p"][:L.F])
        if n == "ffn_conv_w":
            return _full_to_cols(_from_halves(g["ffn_conv_w_p"], L.F, L.Fp))
        if n == "pool_w":
            grp, r, c = w["pool_w"].shape
            return jnp.transpose(g[n].reshape(grp, N_DEV, r, c), (1, 0, 2, 3)).reshape(N_DEV, grp * r, c)
        return _full_to_rows(g[n]) if n in ROW_SHARDED else _full_to_cols(g[n])

    def exchange(self, call, g):
        return _Exchange([self._slices(g, n) for n in self.EXCHANGES[call]])

    def receive(self, call, results):
        self.received.update(zip(self.EXCHANGES[call], results))


def kernel(x, mem, mix_pre_norm, w_in, conv_qkv, a_log, dt_bias, gdn_norm, pool_w, pool_scale, w_branch_a, w_branch_b, w_mix_out, mix_post_norm, xa_pre_norm, mem_norm, w_xq, w_xkv, w_xo, xa_post_norm, ffn_pre_norm, w_up, ffn_conv_w, ffn_conv_b, w_down, ffn_post_norm, loss_target, m_mix_pre_norm, m_w_in, m_conv_qkv, m_a_log, m_dt_bias, m_gdn_norm, m_pool_w, m_pool_scale, m_w_branch_a, m_w_branch_b, m_w_mix_out, m_mix_post_norm, m_xa_pre_norm, m_mem_norm, m_w_xq, m_w_xkv, m_w_xo, m_xa_post_norm, m_ffn_pre_norm, m_w_up, m_ffn_conv_w, m_ffn_conv_b, m_w_down, m_ffn_post_norm, v_mix_pre_norm, v_w_in, v_conv_qkv, v_a_log, v_dt_bias, v_gdn_norm, v_pool_w, v_pool_scale, v_w_branch_a, v_w_branch_b, v_w_mix_out, v_mix_post_norm, v_xa_pre_norm, v_mem_norm, v_w_xq, v_w_xkv, v_w_xo, v_xa_post_norm, v_ffn_pre_norm, v_w_up, v_ffn_conv_w, v_ffn_conv_b, v_w_down, v_ffn_post_norm):
    given = dict(locals())
    w = {n: given[n][0] for n in WEIGHTS}
    m = {n: given["m_" + n][0] for n in WEIGHTS}
    v = {n: given["v_" + n][0] for n in WEIGHTS}
    D = x.shape[-1]
    F = w["w_down"].shape[0] * N_DEV
    L = _Layout(D, w["a_log"].shape[-1], w["pool_scale"].shape[-1], F)
    Fp = L.Fp

    comm = _StepComm(w, L)
    P = {n: w[n] for n in REPLICATED}
    P.update(comm.first_weights())
    P["fb_p"] = _two_halves(w["ffn_conv_b"].reshape(1, 2 * F), F, Fp)
    loss, grad_x, g = _local_step(x[0], mem[0], loss_target[0], P, L, comm)

    received = comm.received
    outs = {}
    for n in SHARDED:
        as2d = lambda a: a.reshape(-1, a.shape[-1])
        res = _adamw(received[n], as2d(w[n]), as2d(m[n]), as2d(v[n]), "adamw_" + n)
        outs[n] = [r.reshape(w[n].shape) for r in res]

    g["ffn_conv_b"] = _from_halves(g["ffn_conv_b_p"], F, Fp)
    rep_parts, rep_layout = _pack([g[n].reshape(w[n].shape) for n in REPLICATED] + [loss])
    rep_all, = _communicate(_Gather([rep_parts]), "gather_small_grads")
    zero_loss = jnp.zeros_like(loss)
    wr, _ = _pack([w[n] for n in REPLICATED] + [zero_loss])
    mr, _ = _pack([m[n] for n in REPLICATED] + [zero_loss])
    vr, _ = _pack([v[n] for n in REPLICATED] + [zero_loss])
    outs_rep = [_unpack(o, rep_layout) for o in _adamw(rep_all, wr, mr, vr, "adamw_replicated")]
    loss_total = outs_rep[0][-1][0, 0]
    for i, n in enumerate(REPLICATED):
        outs[n] = [outs_rep[k][i] for k in range(4)]

    result = [loss_total, grad_x[None]]
    for k in range(4):
        for n in WEIGHTS:
            result.append(outs[n][k][None])
    return tuple(result)
```
